```python
import math
import jax, jax.numpy as jnp
from jax import lax
import numpy as np

D_MODEL = 1024
BATCH = 8
SEQ = 2048
DEPTH = 2

CHUNK = 64
Q_BLOCK = 128
N_MIXERS = 2
DIFF_HEADS = 8
DIFF_HEAD_DIM = D_MODEL // (2 * DIFF_HEADS)
SB_HEADS = 16
SB_HEAD_DIM = D_MODEL // SB_HEADS
N_GROUPS = 4
EXPERTS_PER_GROUP = 8
N_EXPERTS = N_GROUPS * EXPERTS_PER_GROUP
TOP_K = 2
EXPERT_HIDDEN = D_MODEL // 2
MOE_BLOCK = 128
RMS_EPS = 1e-6
SUBLN_EPS = 1e-5

kernel_name = "hybrid_diffattn_stickbreak_hmoe_adaln"


def rms_norm(x, g, eps=RMS_EPS):
    xf = x.astype(jnp.float32)
    y = xf * lax.rsqrt(jnp.mean(xf * xf, axis=-1, keepdims=True) + eps)
    return (y * g.astype(jnp.float32)).astype(x.dtype)


def alibi_slopes(n_heads):
    return jnp.exp2(-8.0 * jnp.arange(1, n_heads + 1, dtype=jnp.float32) / n_heads)


def diff_attention(h, w_in, w_out, lq1, lk1, lq2, lk2, subln_g, lambda_init):
    B, S, D = h.shape
    H, d = DIFF_HEADS, DIFF_HEAD_DIM
    q, k, v = jnp.split(h @ w_in, 3, axis=-1)
    q = q.reshape(B, S, H, 2, d)
    k = k.reshape(B, S, H, 2, d)
    v = v.reshape(B, S, H, 2 * d)
    lam = (jnp.exp(jnp.sum(lq1.astype(jnp.float32) * lk1.astype(jnp.float32)))
           - jnp.exp(jnp.sum(lq2.astype(jnp.float32) * lk2.astype(jnp.float32)))
           + lambda_init)
    slopes = alibi_slopes(H)[None, :, None, None, None]
    scale = 1.0 / math.sqrt(d)
    pos = jnp.arange(S, dtype=jnp.int32)
    outs = []
    for i in range(S // Q_BLOCK):
        q0, q1 = i * Q_BLOCK, (i + 1) * Q_BLOCK
        qb, kb, vb = q[:, q0:q1], k[:, :q1], v[:, :q1]
        tq, tk = pos[q0:q1], pos[:q1]
        s = jnp.einsum('bqhmd,bkhmd->bhmqk', qb, kb).astype(jnp.float32) * scale
        dist = jnp.abs(tq[:, None] - tk[None, :]).astype(jnp.float32)
        allowed = (tk[None, :] // CHUNK) <= (tq[:, None] // CHUNK)
        s = jnp.where(allowed, s - slopes * dist, -jnp.inf)
        p = jax.nn.softmax(s, axis=-1)
        a = p[:, :, 0] - lam * p[:, :, 1]
        outs.append(jnp.einsum('bhqk,bkhe->bqhe', a.astype(vb.dtype), vb))
    o = jnp.concatenate(outs, axis=1)
    o = rms_norm(o, subln_g, SUBLN_EPS) * (1.0 - lambda_init)
    return o.reshape(B, S, D) @ w_out


def stick_breaking_attention(h, w_in, w_out):
    B, S, D = h.shape
    H, d = SB_HEADS, SB_HEAD_DIM
    q, k, v = jnp.split(h @ w_in, 3, axis=-1)
    q = q.reshape(B, S, H, d)
    k = k.reshape(B, S, H, d)
    v = v.reshape(B, S, H, d)
    scale = 1.0 / math.sqrt(d)
    pos = jnp.arange(S, dtype=jnp.int32)
    outs = []
    for i in range(S // Q_BLOCK):
        q0, q1 = i * Q_BLOCK, (i + 1) * Q_BLOCK
        qb, kb, vb = q[:, q0:q1], k[:, :q1], v[:, :q1]
        tq, tk = pos[q0:q1], pos[:q1]
        z = jnp.einsum('bqhd,bkhd->bhqk', qb, kb).astype(jnp.float32) * scale
        strict = tk[None, :] < tq[:, None]
        log_fail = jnp.where(strict, jax.nn.log_sigmoid(-z), 0.0)
        between = lax.cumsum(log_fail, axis=log_fail.ndim - 1, reverse=True) - log_fail
        a = jnp.where(strict, jnp.exp(jax.nn.log_sigmoid(z) + between), 0.0)
        outs.append(jnp.einsum('bhqk,bkhd->bqhd', a.astype(vb.dtype), vb))
    o = jnp.concatenate(outs, axis=1)
    return o.reshape(B, S, D) @ w_out


def expert_dispatch(xf, expert_ids, weights, w_gate, w_up, w_down):
    N, D = xf.shape
    NK = expert_ids.shape[0]
    n_slots = NK + N_EXPERTS * MOE_BLOCK
    n_blocks = n_slots // MOE_BLOCK
    token_ids = jnp.arange(NK, dtype=jnp.int32) // TOP_K
    order = jnp.argsort(expert_ids)
    e_sorted = expert_ids[order]
    counts = jnp.zeros((N_EXPERTS,), jnp.int32).at[expert_ids].add(1)
    padded = (counts + MOE_BLOCK - 1) // MOE_BLOCK * MOE_BLOCK
    starts = jnp.cumsum(counts) - counts
    pad_ends = jnp.cumsum(padded)
    pad_starts = pad_ends - padded
    rank = jnp.arange(NK, dtype=jnp.int32) - starts[e_sorted]
    dest = pad_starts[e_sorted] + rank
    slot_tok = jnp.full((n_slots,), N, jnp.int32).at[dest].set(token_ids[order])
    slot_w = jnp.zeros((n_slots,), jnp.float32).at[dest].set(weights[order])
    block_start = jnp.arange(n_blocks, dtype=jnp.int32) * MOE_BLOCK
    block_expert = jnp.minimum(jnp.searchsorted(pad_ends, block_start, side='right'),
                               N_EXPERTS - 1).astype(jnp.int32)
    x_pad = jnp.concatenate([xf, jnp.zeros((1, D), xf.dtype)], axis=0)
    xs = x_pad[slot_tok].reshape(n_blocks, MOE_BLOCK, D)

    def run_block(args):
        xb, e = args
        hid = jax.nn.silu(xb @ w_gate[e]) * (xb @ w_up[e])
        return hid @ w_down[e]

    ys = lax.map(run_block, (xs, block_expert)).reshape(n_slots, D)
    ys = ys * slot_w[:, None].astype(ys.dtype)
    out = jnp.zeros((N + 1, D), ys.dtype).at[slot_tok].add(ys)
    return out[:N]


def hierarchical_moe(h, w_group, b_group, w_expert, b_expert, w_gate, w_up, w_down):
    B, S, D = h.shape
    N = B * S
    xf = h.reshape(N, D)
    rows = jnp.arange(N, dtype=jnp.int32)
    g_logits = (xf @ w_group).astype(jnp.float32) + b_group.astype(jnp.float32)
    g_prob = jax.nn.softmax(g_logits, axis=-1)
    g_idx = jnp.argmax(g_logits, axis=-1).astype(jnp.int32)
    g_w = g_prob[rows, g_idx][:, None]
    e_all = ((xf @ w_expert).astype(jnp.float32).reshape(N, N_GROUPS, EXPERTS_PER_GROUP)
             + b_expert.astype(jnp.float32))
    e_logits = e_all[rows, g_idx]
    top_v, top_i = lax.top_k(e_logits, TOP_K)
    e_w = jax.nn.softmax(top_v, axis=-1)
    weights = (g_w * e_w).reshape(-1)
    expert_ids = (g_idx[:, None] * EXPERTS_PER_GROUP + top_i.astype(jnp.int32)).reshape(-1)
    y = expert_dispatch(xf, expert_ids, weights, w_gate, w_up, w_down)
    return y.reshape(B, S, D)


def setup_inputs(seed: int = 0) -> dict:
    key = jax.random.key(seed)
    ks = jax.random.split(key, 24)
    D, F, E = D_MODEL, EXPERT_HIDDEN, N_EXPERTS
    nA = (DEPTH + 1) // 2
    nB = DEPTH // 2
    dA = DIFF_HEAD_DIM
    nrm = jax.random.normal
    f32 = jnp.float32
    inv = D ** -0.5
    return {
        "x": nrm(ks[0], (BATCH, SEQ, D), f32),
        "c": nrm(ks[1], (BATCH, D), f32),
        "norm1_g": 1.0 + 0.02 * nrm(ks[2], (DEPTH, D), f32),
        "norm2_g": 1.0 + 0.02 * nrm(ks[3], (DEPTH, D), f32),
        "ada_w": 0.5 * inv * nrm(ks[4], (DEPTH, D, 6 * D), f32),
        "ada_b": 0.01 * nrm(ks[5], (DEPTH, 6 * D), f32),
        "diff_w_in": inv * nrm(ks[6], (nA, D, 3 * D), f32),
        "diff_w_out": inv * nrm(ks[7], (nA, D, D), f32),
        "diff_lambda_q1": 0.1 * nrm(ks[8], (nA, dA), f32),
        "diff_lambda_k1": 0.1 * nrm(ks[9], (nA, dA), f32),
        "diff_lambda_q2": 0.1 * nrm(ks[10], (nA, dA), f32),
        "diff_lambda_k2": 0.1 * nrm(ks[11], (nA, dA), f32),
        "diff_subln_g": 1.0 + 0.02 * nrm(ks[12], (nA, 2 * dA), f32),
        "sb_w_in": inv * nrm(ks[13], (nB, D, 3 * D), f32),
        "sb_w_out": inv * nrm(ks[14], (nB, D, D), f32),
        "router_group_w": inv * nrm(ks[15], (DEPTH, D, N_GROUPS), f32),
        "router_group_b": 0.01 * nrm(ks[16], (DEPTH, N_GROUPS), f32),
        "router_expert_w": inv * nrm(ks[17], (DEPTH, D, N_GROUPS * EXPERTS_PER_GROUP), f32),
        "router_expert_b": 0.01 * nrm(ks[18], (DEPTH, N_GROUPS, EXPERTS_PER_GROUP), f32),
        "expert_w_gate": inv * nrm(ks[19], (DEPTH, E, D, F), f32),
        "expert_w_up": inv * nrm(ks[20], (DEPTH, E, D, F), f32),
        "expert_w_down": (F ** -0.5) * nrm(ks[21], (DEPTH, E, F, D), f32),
        "final_norm_g": 1.0 + 0.02 * nrm(ks[22], (D,), f32),
    }


def reference(x, c, norm1_g, norm2_g, ada_w, ada_b, diff_w_in, diff_w_out,
              diff_lambda_q1, diff_lambda_k1, diff_lambda_q2, diff_lambda_k2, diff_subln_g,
              sb_w_in, sb_w_out, router_group_w, router_group_b, router_expert_w,
              router_expert_b, expert_w_gate, expert_w_up, expert_w_down, final_norm_g):
    cond = jax.nn.silu(c)
    for i in range(DEPTH):
        mod = cond @ ada_w[i] + ada_b[i]
        sh1, sc1, g1, sh2, sc2, g2 = jnp.split(mod[:, None, :], 6, axis=-1)
        h = rms_norm(x, norm1_g[i]) * (1.0 + sc1) + sh1
        j = i // N_MIXERS
        if i % N_MIXERS == 0:
            lambda_init = 0.8 - 0.6 * math.exp(-0.3 * i)
            y = diff_attention(h, diff_w_in[j], diff_w_out[j], diff_lambda_q1[j],
                               diff_lambda_k1[j], diff_lambda_q2[j], diff_lambda_k2[j],
                               diff_subln_g[j], lambda_init)
        else:
            y = stick_breaking_attention(h, sb_w_in[j], sb_w_out[j])
        x = x + g1 * y
        h = rms_norm(x, norm2_g[i]) * (1.0 + sc2) + sh2
        y = hierarchical_moe(h, router_group_w[i], router_group_b[i], router_expert_w[i],
                             router_expert_b[i], expert_w_gate[i], expert_w_up[i],
                             expert_w_down[i])
        x = x + g2 * y
    return rms_norm(x, final_norm_g)
```

```python
import functools
import math

import jax
import jax.numpy as jnp
from jax import lax
from jax.experimental import pallas as pl
from jax.experimental.pallas import tpu as pltpu

D_MODEL = 1024
BATCH = 8
SEQ = 2048
DEPTH = 2
N_TOK = BATCH * SEQ

CHUNK = 64
DIFF_HEADS = 8
DIFF_HEAD_DIM = D_MODEL // (2 * DIFF_HEADS)
SB_HEADS = 16
SB_HEAD_DIM = D_MODEL // SB_HEADS
N_GROUPS = 4
EXPERTS_PER_GROUP = 8
N_EXPERTS = N_GROUPS * EXPERTS_PER_GROUP
TOP_K = 2
EXPERT_HIDDEN = D_MODEL // 2
RMS_EPS = 1e-6
SUBLN_EPS = 1e-5

LANES = 128
EXPERT_LANE0 = 32
EXP_UNDERFLOW = -104.0

ADA_TN = 1536
QKV_TM = 512
ATT_TQ = 256
ATT_TK = 256
OUT_TM = 512
DISP_TM = 1024
MOE_BLK = 256
N_SLOTS = N_TOK * TOP_K + N_EXPERTS * MOE_BLK
N_BLOCKS = N_SLOTS // MOE_BLK
COMB_TM = 512

VMEM_LIMIT = 56 * 1024 * 1024

F32 = jnp.float32
BF16 = jnp.bfloat16


def _cparams(sem):
    return pltpu.CompilerParams(dimension_semantics=sem, vmem_limit_bytes=VMEM_LIMIT)


def _ada_kernel(c_ref, w_ref, b_ref, o_ref):
    c = c_ref[...]
    cond = c * jax.nn.sigmoid(c)
    o_ref[0] = jnp.dot(cond, w_ref[0], preferred_element_type=F32,
                       precision=lax.Precision.HIGHEST) + b_ref[0]


def _ada(c, ada_w, ada_b):
    six_d = ada_w.shape[-1]
    return pl.pallas_call(
        _ada_kernel,
        grid=(DEPTH, six_d // ADA_TN),
        in_specs=[
            pl.BlockSpec((BATCH, D_MODEL), lambda l, n: (0, 0)),
            pl.BlockSpec((1, D_MODEL, ADA_TN), lambda l, n: (l, 0, n)),
            pl.BlockSpec((1, 1, ADA_TN), lambda l, n: (l, 0, n)),
        ],
        out_specs=pl.BlockSpec((1, BATCH, ADA_TN), lambda l, n: (l, 0, n)),
        out_shape=jax.ShapeDtypeStruct((DEPTH, BATCH, six_d), F32),
        compiler_params=_cparams(("arbitrary", "arbitrary")),
        name="ada",
    )(c, ada_w, ada_b.reshape(DEPTH, 1, six_d))


def _modulated_norm(x, g, scale, shift):
    ms = jnp.mean(x * x, axis=-1, keepdims=True)
    return x * lax.rsqrt(ms + RMS_EPS) * (g * (1.0 + scale)) + shift


def _qkv_kernel(x_ref, mod_ref, g_ref, w_ref, o_ref):
    m = mod_ref[0]
    h = _modulated_norm(x_ref[...], g_ref[...], m[1:2], m[0:1]).astype(BF16)
    for n in range(3):
        cols = slice(n * D_MODEL, (n + 1) * D_MODEL)
        o_ref[:, cols] = jnp.dot(h, w_ref[:, cols], preferred_element_type=F32).astype(BF16)


def _qkv(x, mod, g, w):
    tiles_per_batch = SEQ // QKV_TM
    return pl.pallas_call(
        _qkv_kernel,
        grid=(N_TOK // QKV_TM,),
        in_specs=[
            pl.BlockSpec((QKV_TM, D_MODEL), lambda i: (i, 0)),
            pl.BlockSpec((1, 6, D_MODEL), lambda i: (i // tiles_per_batch, 0, 0)),
            pl.BlockSpec((1, D_MODEL), lambda i: (0, 0)),
            pl.BlockSpec((D_MODEL, 3 * D_MODEL), lambda i: (0, 0)),
        ],
        out_specs=pl.BlockSpec((QKV_TM, 3 * D_MODEL), lambda i: (i, 0)),
        out_shape=jax.ShapeDtypeStruct((N_TOK, 3 * D_MODEL), BF16),
        compiler_params=_cparams(("arbitrary",)),
        name="qkv",
    )(x, mod, g, w)


def _half_masked(q, upper):
    lane = lax.broadcasted_iota(jnp.int32, q.shape, 1)
    keep = (lane >= LANES // 2) if upper else (lane < LANES // 2)
    return jnp.where(keep, q, jnp.zeros_like(q))


def _qk(q, k):
    return lax.dot_general(q, k, (((1,), (1,)), ((), ())), preferred_element_type=F32)


def _diff_attn_kernel(slopes_ref, lam_ref, q_ref, k_ref, v_ref, g_ref, o_ref, *, lambda_init):
    h = pl.program_id(1)
    qi = pl.program_id(2)
    slope = slopes_ref[h]
    q = q_ref[...]
    qm = (_half_masked(q, False), _half_masked(q, True))

    row = lax.broadcasted_iota(jnp.int32, (ATT_TQ, ATT_TK), 0)
    col = lax.broadcasted_iota(jnp.int32, (ATT_TQ, ATT_TK), 1)
    rel = (row - col).astype(F32) * slope
    allowed = (col // CHUNK) <= (row // CHUNK)
    diag_bias = jnp.where(allowed, jnp.abs(rel), jnp.inf)

    k = k_ref[pl.ds(qi * ATT_TK, ATT_TK), :]
    v = v_ref[pl.ds(qi * ATT_TK, ATT_TK), :]
    carry = []
    for mi in range(2):
        s = _qk(qm[mi], k) - diag_bias
        m = jnp.max(s, axis=-1, keepdims=True)
        p = jnp.exp(s - m)
        l = jnp.sum(p, axis=-1, keepdims=True)
        acc = jnp.dot(p.astype(BF16), v, preferred_element_type=F32)
        carry += [m, l, acc]

    def body(j, carry):
        k = k_ref[pl.ds(j * ATT_TK, ATT_TK), :]
        v = v_ref[pl.ds(j * ATT_TK, ATT_TK), :]
        shift = slope * ((qi - j) * ATT_TK).astype(F32)
        out = []
        for mi in range(2):
            m, l, acc = carry[3 * mi:3 * mi + 3]
            s = _qk(qm[mi], k) - rel
            m_new = jnp.maximum(m, jnp.max(s, axis=-1, keepdims=True) - shift)
            alpha = jnp.exp(m - m_new)
            p = jnp.exp(s - (m_new + shift))
            l = alpha * l + jnp.sum(p, axis=-1, keepdims=True)
            acc = alpha * acc + jnp.dot(p.astype(BF16), v, preferred_element_type=F32)
            out += [m_new, l, acc]
        return tuple(out)

    carry = lax.fori_loop(0, qi, body, tuple(carry))
    _, l0, acc0, _, l1, acc1 = carry
    lam = lam_ref[...]
    o = acc0 / l0 - lam * (acc1 / l1)
    ms = jnp.mean(o * o, axis=-1, keepdims=True)
    o = o * lax.rsqrt(ms + SUBLN_EPS) * (g_ref[...] * (1.0 - lambda_init))
    o_ref[...] = o.astype(BF16)


def _diff_attn(qkv, slopes, lam, subln_g, lambda_init):
    nq = SEQ // ATT_TQ
    return pl.pallas_call(
        functools.partial(_diff_attn_kernel, lambda_init=lambda_init),
        grid=(BATCH, DIFF_HEADS, nq),
        in_specs=[
            pl.BlockSpec(memory_space=pltpu.SMEM),
            pl.BlockSpec((1, 1), lambda b, h, i: (0, 0)),
            pl.BlockSpec((ATT_TQ, LANES), lambda b, h, i: (b * nq + i, h)),
            pl.BlockSpec((SEQ, LANES), lambda b, h, i: (b, DIFF_HEADS + h)),
            pl.BlockSpec((SEQ, LANES), lambda b, h, i: (b, 2 * DIFF_HEADS + h)),
            pl.BlockSpec((1, LANES), lambda b, h, i: (0, 0)),
        ],
        out_specs=pl.BlockSpec((ATT_TQ, LANES), lambda b, h, i: (b * nq + i, h)),
        out_shape=jax.ShapeDtypeStruct((N_TOK, D_MODEL), BF16),
        compiler_params=_cparams(("arbitrary", "arbitrary", "arbitrary")),
        name="diff_attn",
    )(slopes, lam, qkv, qkv, qkv, subln_g)


def _sb_attn_kernel(q_ref, k_ref, v_ref, o_ref):
    qi = pl.program_id(2)
    q = q_ref[...]
    qh = (_half_masked(q, False), _half_masked(q, True))

    row = lax.broadcasted_iota(jnp.int32, (ATT_TQ, ATT_TK), 0)
    col = lax.broadcasted_iota(jnp.int32, (ATT_TQ, ATT_TK), 1)
    strict = col < row
    later = jnp.where(row > col, 1.0, 0.0).astype(BF16)

    def block(j, qhead, tail, acc, mask):
        k = k_ref[pl.ds(j * ATT_TK, ATT_TK), :]
        v = v_ref[pl.ds(j * ATT_TK, ATT_TK), :]
        z = _qk(qhead, k)
        lf = -(jnp.maximum(z, 0.0) + jnp.log1p(jnp.exp(-jnp.abs(z))))
        if mask is not None:
            lf = jnp.where(mask, lf, 0.0)
        hi = lf.astype(BF16)
        lo = (lf - hi.astype(F32)).astype(BF16)
        between = (jnp.dot(hi, later, preferred_element_type=F32)
                   + jnp.dot(lo, later, preferred_element_type=F32))
        a = jnp.exp(z + lf + between + tail)
        if mask is not None:
            a = jnp.where(mask, a, 0.0)
        acc = acc + jnp.dot(a.astype(BF16), v, preferred_element_type=F32)
        tail = tail + jnp.sum(lf, axis=-1, keepdims=True)
        return tail, acc

    zero_tail = jnp.zeros((ATT_TQ, 1), F32)
    zero_acc = jnp.zeros((ATT_TQ, LANES), F32)
    carry = []
    for hh in range(2):
        carry += list(block(qi, qh[hh], zero_tail, zero_acc, strict))

    def body(jj, carry):
        j = qi - 1 - jj
        out = []
        for hh in range(2):
            out += list(block(j, qh[hh], carry[2 * hh], carry[2 * hh + 1], None))
        return tuple(out)

    carry = lax.fori_loop(0, qi, body, tuple(carry))
    lane = lax.broadcasted_iota(jnp.int32, (ATT_TQ, LANES), 1)
    o_ref[...] = jnp.where(lane < LANES // 2, carry[1], carry[3]).astype(BF16)


def _sb_attn(qkv):
    nq = SEQ // ATT_TQ
    pairs = SB_HEADS // 2
    return pl.pallas_call(
        _sb_attn_kernel,
        grid=(BATCH, pairs, nq),
        in_specs=[
            pl.BlockSpec((ATT_TQ, LANES), lambda b, h, i: (b * nq + i, h)),
            pl.BlockSpec((SEQ, LANES), lambda b, h, i: (b, pairs + h)),
            pl.BlockSpec((SEQ, LANES), lambda b, h, i: (b, 2 * pairs + h)),
        ],
        out_specs=pl.BlockSpec((ATT_TQ, LANES), lambda b, h, i: (b * nq + i, h)),
        out_shape=jax.ShapeDtypeStruct((N_TOK, D_MODEL), BF16),
        compiler_params=_cparams(("arbitrary", "arbitrary", "arbitrary")),
        name="sb_attn",
    )(qkv, qkv, qkv)


def _out_router_kernel(o_ref, x_ref, mod_ref, g_ref, w_ref, rw_ref, rb_ref,
                       x1_ref, h2_ref, idr_ref, wts_ref, cnt_ref, tri_ref, base_ref):
    i = pl.program_id(0)

    @pl.when(i == 0)
    def _():
        r = lax.broadcasted_iota(jnp.int32, (OUT_TM, OUT_TM), 0)
        c = lax.broadcasted_iota(jnp.int32, (OUT_TM, OUT_TM), 1)
        tri_ref[...] = jnp.where(c < r, 1.0, 0.0).astype(BF16)
        base_ref[...] = jnp.zeros_like(base_ref)

    m = mod_ref[0]
    y = jnp.dot(o_ref[...], w_ref[...], preferred_element_type=F32)
    x1 = x_ref[...] + m[2:3] * y
    x1_ref[...] = x1
    h2 = _modulated_norm(x1, g_ref[...], m[4:5], m[3:4])
    h2_ref[...] = h2

    logits = jnp.dot(h2.astype(BF16), rw_ref[...], preferred_element_type=F32) + rb_ref[...]
    lane = lax.broadcasted_iota(jnp.int32, logits.shape, 1).astype(F32)
    neg_inf = jnp.float32(-jnp.inf)
    big = jnp.float32(1e9)

    is_group = lane < N_GROUPS
    gl = jnp.where(is_group, logits, neg_inf)
    gmax = jnp.max(gl, axis=-1, keepdims=True)
    gidx = jnp.min(jnp.where(gl == gmax, lane, big), axis=-1, keepdims=True)
    gsum = jnp.sum(jnp.where(is_group, jnp.exp(logits - gmax), 0.0), axis=-1, keepdims=True)
    g_w = 1.0 / gsum

    lo = EXPERT_LANE0 + EXPERTS_PER_GROUP * gidx
    in_group = (lane >= lo) & (lane < lo + EXPERTS_PER_GROUP)
    el = jnp.where(in_group, logits, neg_inf)
    v0 = jnp.max(el, axis=-1, keepdims=True)
    i0 = jnp.min(jnp.where(el == v0, lane, big), axis=-1, keepdims=True)
    el = jnp.where(lane == i0, neg_inf, el)
    v1 = jnp.max(el, axis=-1, keepdims=True)
    i1 = jnp.min(jnp.where(el == v1, lane, big), axis=-1, keepdims=True)
    t = jnp.exp(v1 - v0)
    w0 = g_w / (1.0 + t)
    w1 = g_w * t / (1.0 + t)

    oh0 = jnp.where(lane == i0, 1.0, 0.0)
    oh1 = jnp.where(lane == i1, 1.0, 0.0)
    both = oh0 + oh1
    before = jnp.dot(tri_ref[...], both.astype(BF16), preferred_element_type=F32) + base_ref[...]
    r0 = jnp.sum(before * oh0, axis=-1, keepdims=True)
    r1 = jnp.sum(before * oh1, axis=-1, keepdims=True)
    base_ref[...] = base_ref[...] + jnp.sum(both, axis=0, keepdims=True)
    cnt_ref[...] = base_ref[...]

    e0 = i0 - EXPERT_LANE0
    e1 = i1 - EXPERT_LANE0
    idr = jnp.where(lane == 0, e0, jnp.where(lane == 1, e1, jnp.where(lane == 2, r0, r1)))
    idr_ref[...] = idr[:, :4].astype(jnp.int32)
    wts_ref[...] = jnp.where(lane == 0, w0, w1)[:, :2]


def _out_router(o, x, mod, g, w, rw, rb):
    tiles_per_batch = SEQ // OUT_TM
    return pl.pallas_call(
        _out_router_kernel,
        grid=(N_TOK // OUT_TM,),
        in_specs=[
            pl.BlockSpec((OUT_TM, D_MODEL), lambda i: (i, 0)),
            pl.BlockSpec((OUT_TM, D_MODEL), lambda i: (i, 0)),
            pl.BlockSpec((1, 6, D_MODEL), lambda i: (i // tiles_per_batch, 0, 0)),
            pl.BlockSpec((1, D_MODEL), lambda i: (0, 0)),
            pl.BlockSpec((D_MODEL, D_MODEL), lambda i: (0, 0)),
            pl.BlockSpec((D_MODEL, LANES), lambda i: (0, 0)),
            pl.BlockSpec((1, LANES), lambda i: (0, 0)),
        ],
        out_specs=[
            pl.BlockSpec((OUT_TM, D_MODEL), lambda i: (i, 0)),
            pl.BlockSpec((OUT_TM, D_MODEL), lambda i: (i, 0)),
            pl.BlockSpec((OUT_TM, 4), lambda i: (i, 0)),
            pl.BlockSpec((OUT_TM, 2), lambda i: (i, 0)),
            pl.BlockSpec((1, LANES), lambda i: (0, 0)),
        ],
        out_shape=[
            jax.ShapeDtypeStruct((N_TOK, D_MODEL), F32),
            jax.ShapeDtypeStruct((N_TOK, D_MODEL), F32),
            jax.ShapeDtypeStruct((N_TOK, 4), jnp.int32),
            jax.ShapeDtypeStruct((N_TOK, 2), F32),
            jax.ShapeDtypeStruct((1, LANES), F32),
        ],
        scratch_shapes=[
            pltpu.VMEM((OUT_TM, OUT_TM), BF16),
            pltpu.VMEM((1, LANES), F32),
        ],
        compiler_params=_cparams(("arbitrary",)),
        name="out_router",
    )(o, x, mod, g, w, rw, rb)


def _dispatch_kernel(dest_ref, h_ref, xs_in_ref, xs_ref, sem):
    del xs_in_ref

    def row_copy(r, d):
        return pltpu.make_async_copy(h_ref.at[pl.ds(r, 1)], xs_ref.at[pl.ds(d, 1)], sem)

    def issue(r, _):
        for k in range(TOP_K):
            row_copy(r, dest_ref[0, 0, TOP_K * r + k]).start()
        return 0

    def drain(r, _):
        for k in range(TOP_K):
            row_copy(0, 0).wait()
        return 0

    lax.fori_loop(0, DISP_TM, issue, 0)
    lax.fori_loop(0, DISP_TM, drain, 0)


def _dispatch(dest, h2, xs_init):
    nt = N_TOK // DISP_TM
    return pl.pallas_call(
        _dispatch_kernel,
        grid=(nt,),
        in_specs=[
            pl.BlockSpec((1, 1, TOP_K * DISP_TM), lambda i: (i, 0, 0), memory_space=pltpu.SMEM),
            pl.BlockSpec((DISP_TM, D_MODEL), lambda i: (i, 0)),
            pl.BlockSpec(memory_space=pl.ANY),
        ],
        out_specs=pl.BlockSpec(memory_space=pl.ANY),
        out_shape=jax.ShapeDtypeStruct((N_SLOTS, D_MODEL), F32),
        scratch_shapes=[pltpu.SemaphoreType.DMA(())],
        input_output_aliases={2: 0},
        compiler_params=_cparams(("arbitrary",)),
        name="dispatch",
    )(dest.reshape(nt, 1, TOP_K * DISP_TM), h2, xs_init)


def _expert_kernel(bexp_ref, nused_ref, xs_ref, wg_ref, wu_ref, wd_ref, ys_ref,
                   wg_bf, wu_bf, wd_bf):
    i = pl.program_id(0)

    @pl.when(i < nused_ref[0])
    def _():
        e = bexp_ref[i]
        prev = bexp_ref[jnp.maximum(i - 1, 0)]

        @pl.when((i == 0) | (e != prev))
        def _():
            wg_bf[...] = wg_ref[0].astype(BF16)
            wu_bf[...] = wu_ref[0].astype(BF16)
            wd_bf[...] = wd_ref[0].astype(BF16)

        x = xs_ref[...].astype(BF16)
        g = jnp.dot(x, wg_bf[...], preferred_element_type=F32)
        u = jnp.dot(x, wu_bf[...], preferred_element_type=F32)
        hid = (g * jax.nn.sigmoid(g)) * u
        ys_ref[...] = jnp.dot(hid.astype(BF16), wd_bf[...], preferred_element_type=F32)

    @pl.when(i >= nused_ref[0])
    def _():
        ys_ref[...] = jnp.zeros_like(ys_ref)


def _experts(bexp, nused, xs, wg, wu, wd):
    def row_map(i, bexp, nused):
        return (i, 0)

    def w_map(i, bexp, nused):
        return (bexp[jnp.minimum(i, nused[0] - 1)], 0, 0)

    return pl.pallas_call(
        _expert_kernel,
        grid_spec=pltpu.PrefetchScalarGridSpec(
            num_scalar_prefetch=2,
            grid=(N_BLOCKS,),
            in_specs=[
                pl.BlockSpec((MOE_BLK, D_MODEL), row_map),
                pl.BlockSpec((1, D_MODEL, EXPERT_HIDDEN), w_map),
                pl.BlockSpec((1, D_MODEL, EXPERT_HIDDEN), w_map),
                pl.BlockSpec((1, EXPERT_HIDDEN, D_MODEL), w_map),
            ],
            out_specs=pl.BlockSpec((MOE_BLK, D_MODEL), row_map),
            scratch_shapes=[
                pltpu.VMEM((D_MODEL, EXPERT_HIDDEN), BF16),
                pltpu.VMEM((D_MODEL, EXPERT_HIDDEN), BF16),
                pltpu.VMEM((EXPERT_HIDDEN, D_MODEL), BF16),
            ],
        ),
        out_shape=jax.ShapeDtypeStruct((N_SLOTS, D_MODEL), F32),
        compiler_params=_cparams(("arbitrary",)),
        name="experts",
    )(bexp, nused, xs, wg, wu, wd)


def _combine_kernel(dest_ref, x_ref, wts_ref, mod_ref, fg_ref, ys_ref, o_ref, buf, sem, *, final):
    def row_copy(d, k, r):
        return pltpu.make_async_copy(ys_ref.at[pl.ds(d, 1)], buf.at[k, pl.ds(r, 1)], sem)

    def issue(r, _):
        for k in range(TOP_K):
            row_copy(dest_ref[0, 0, TOP_K * r + k], k, r).start()
        return 0

    def drain(r, _):
        for k in range(TOP_K):
            row_copy(0, k, 0).wait()
        return 0

    lax.fori_loop(0, COMB_TM, issue, 0)
    lax.fori_loop(0, COMB_TM, drain, 0)

    w = wts_ref[...]
    y = w[:, 0:1] * buf[0] + w[:, 1:2] * buf[1]
    out = x_ref[...] + mod_ref[0][5:6] * y
    if final:
        ms = jnp.mean(out * out, axis=-1, keepdims=True)
        out = out * lax.rsqrt(ms + RMS_EPS) * fg_ref[...]
    o_ref[...] = out


def _combine(dest, x1, wts, mod, fg, ys, final):
    nt = N_TOK // COMB_TM
    tiles_per_batch = SEQ // COMB_TM
    return pl.pallas_call(
        functools.partial(_combine_kernel, final=final),
        grid=(nt,),
        in_specs=[
            pl.BlockSpec((1, 1, TOP_K * COMB_TM), lambda i: (i, 0, 0), memory_space=pltpu.SMEM),
            pl.BlockSpec((COMB_TM, D_MODEL), lambda i: (i, 0)),
            pl.BlockSpec((COMB_TM, TOP_K), lambda i: (i, 0)),
            pl.BlockSpec((1, 6, D_MODEL), lambda i: (i // tiles_per_batch, 0, 0)),
            pl.BlockSpec((1, D_MODEL), lambda i: (0, 0)),
            pl.BlockSpec(memory_space=pl.ANY),
        ],
        out_specs=pl.BlockSpec((COMB_TM, D_MODEL), lambda i: (i, 0)),
        out_shape=jax.ShapeDtypeStruct((N_TOK, D_MODEL), F32),
        scratch_shapes=[
            pltpu.VMEM((TOP_K, COMB_TM, D_MODEL), F32),
            pltpu.SemaphoreType.DMA(()),
        ],
        compiler_params=_cparams(("arbitrary",)),
        name="combine",
    )(dest.reshape(nt, 1, TOP_K * COMB_TM), x1, wts, mod, fg, ys)


def _routing_tables(idr, cnt):
    counts = cnt[0, EXPERT_LANE0:EXPERT_LANE0 + N_EXPERTS].astype(jnp.int32)
    padded = (counts + MOE_BLK - 1) // MOE_BLK * MOE_BLK
    pad_ends = jnp.cumsum(padded)
    pad_starts = pad_ends - padded
    dest = pad_starts[idr[:, 0:2]] + idr[:, 2:4]
    block_start = jnp.arange(N_BLOCKS, dtype=jnp.int32) * MOE_BLK
    bexp = jnp.minimum(jnp.searchsorted(pad_ends, block_start, side="right"),
                       N_EXPERTS - 1).astype(jnp.int32)
    nused = (pad_ends[-1:] // MOE_BLK).astype(jnp.int32)
    return dest.reshape(-1).astype(jnp.int32), bexp, nused


def _router_weights(w_group, b_group, w_expert, b_expert):
    rw = jnp.zeros((D_MODEL, LANES), F32)
    rw = rw.at[:, :N_GROUPS].set(w_group)
    rw = rw.at[:, EXPERT_LANE0:EXPERT_LANE0 + N_EXPERTS].set(w_expert)
    rb = jnp.zeros((1, LANES), F32)
    rb = rb.at[0, :N_GROUPS].set(b_group)
    rb = rb.at[0, EXPERT_LANE0:EXPERT_LANE0 + N_EXPERTS].set(b_expert.reshape(-1))
    return rw.astype(BF16), rb


def _scaled_qkv_weight(w_in, head_dim):
    scale = jnp.concatenate([jnp.full((D_MODEL,), 1.0 / math.sqrt(head_dim), F32),
                             jnp.ones((2 * D_MODEL,), F32)])
    return (w_in * scale).astype(BF16)


def kernel(x, c, norm1_g, norm2_g, ada_w, ada_b, diff_w_in, diff_w_out, diff_lambda_q1, diff_lambda_k1, diff_lambda_q2, diff_lambda_k2, diff_subln_g, sb_w_in, sb_w_out, router_group_w, router_group_b, router_expert_w, router_expert_b, expert_w_gate, expert_w_up, expert_w_down, final_norm_g):
    xf = x.reshape(N_TOK, D_MODEL)
    mod_all = _ada(c, ada_w, ada_b)
    slopes = jnp.exp2(-8.0 * jnp.arange(1, DIFF_HEADS + 1, dtype=F32) / DIFF_HEADS)
    xs_init = jnp.zeros((N_SLOTS, D_MODEL), F32)

    for i in range(DEPTH):
        mod = mod_all[i].reshape(BATCH, 6, D_MODEL)
        j = i // 2
        if i % 2 == 0:
            lambda_init = 0.8 - 0.6 * math.exp(-0.3 * i)
            lam = (jnp.exp(jnp.sum(diff_lambda_q1[j] * diff_lambda_k1[j]))
                   - jnp.exp(jnp.sum(diff_lambda_q2[j] * diff_lambda_k2[j]))
                   + lambda_init).reshape(1, 1)
            qkv = _qkv(xf, mod, norm1_g[i].reshape(1, D_MODEL),
                       _scaled_qkv_weight(diff_w_in[j], DIFF_HEAD_DIM))
            o = _diff_attn(qkv, slopes, lam, diff_subln_g[j].reshape(1, LANES), lambda_init)
            w_out = diff_w_out[j]
        else:
            qkv = _qkv(xf, mod, norm1_g[i].reshape(1, D_MODEL),
                       _scaled_qkv_weight(sb_w_in[j], SB_HEAD_DIM))
            o = _sb_attn(qkv)
            w_out = sb_w_out[j]
        rw, rb = _router_weights(router_group_w[i], router_group_b[i],
                                 router_expert_w[i], router_expert_b[i])
        x1, h2, idr, wts, cnt = _out_router(o, xf, mod, norm2_g[i].reshape(1, D_MODEL),
                                            w_out.astype(BF16), rw, rb)
        dest, bexp, nused = _routing_tables(idr, cnt)
        xs = _dispatch(dest, h2, xs_init)
        ys = _experts(bexp, nused, xs, expert_w_gate[i], expert_w_up[i], expert_w_down[i])
        xf = _combine(dest, x1, wts, mod, final_norm_g.reshape(1, D_MODEL), ys,
                      final=(i == DEPTH - 1))
    return xf.reshape(BATCH, SEQ, D_MODEL)
```

```python
import functools
import math

import jax
import jax.numpy as jnp
from jax import lax
from jax.experimental import pallas as pl
from jax.experimental.pallas import tpu as pltpu

D_MODEL = 1024
BATCH = 8
SEQ = 2048
DEPTH = 2
N_TOK = BATCH * SEQ

CHUNK = 64
DIFF_HEADS = 8
DIFF_HEAD_DIM = D_MODEL // (2 * DIFF_HEADS)
SB_HEADS = 16
SB_HEAD_DIM = D_MODEL // SB_HEADS
N_GROUPS = 4
EXPERTS_PER_GROUP = 8
N_EXPERTS = N_GROUPS * EXPERTS_PER_GROUP
TOP_K = 2
EXPERT_HIDDEN = D_MODEL // 2
RMS_EPS = 1e-6
SUBLN_EPS = 1e-5

LANES = 128
EXPERT_LANE0 = 32
EXP_UNDERFLOW = -104.0

ADA_TN = 1536
QKV_TM = 512
DIFF_T = 512
SB_T = 256
OUT_TM = 512
DISP_TM = 1024
MOE_BLK = 256
N_SLOTS = N_TOK * TOP_K + N_EXPERTS * MOE_BLK
N_BLOCKS = N_SLOTS // MOE_BLK
COMB_TM = 512
DMA_UNROLL = 8

VMEM_LIMIT = 56 * 1024 * 1024

F32 = jnp.float32
BF16 = jnp.bfloat16


def _cparams(sem):
    return pltpu.CompilerParams(dimension_semantics=sem, vmem_limit_bytes=VMEM_LIMIT)


def _ada_kernel(c_ref, w_ref, b_ref, o_ref):
    c = c_ref[...]
    cond = c * jax.nn.sigmoid(c)
    o_ref[0] = jnp.dot(cond, w_ref[0], preferred_element_type=F32,
                       precision=lax.Precision.HIGHEST) + b_ref[0]


def _ada(c, ada_w, ada_b):
    six_d = ada_w.shape[-1]
    return pl.pallas_call(
        _ada_kernel,
        grid=(DEPTH, six_d // ADA_TN),
        in_specs=[
            pl.BlockSpec((BATCH, D_MODEL), lambda l, n: (0, 0)),
            pl.BlockSpec((1, D_MODEL, ADA_TN), lambda l, n: (l, 0, n)),
            pl.BlockSpec((1, 1, ADA_TN), lambda l, n: (l, 0, n)),
        ],
        out_specs=pl.BlockSpec((1, BATCH, ADA_TN), lambda l, n: (l, 0, n)),
        out_shape=jax.ShapeDtypeStruct((DEPTH, BATCH, six_d), F32),
        compiler_params=_cparams(("arbitrary", "arbitrary")),
        name="ada",
    )(c, ada_w, ada_b.reshape(DEPTH, 1, six_d))


def _modulated_norm(x, g, scale, shift):
    ms = jnp.mean(x * x, axis=-1, keepdims=True)
    return x * lax.rsqrt(ms + RMS_EPS) * (g * (1.0 + scale)) + shift


def _qkv_kernel(x_ref, mod_ref, g_ref, w_ref, o_ref):
    m = mod_ref[0]
    h = _modulated_norm(x_ref[...], g_ref[...], m[1:2], m[0:1]).astype(BF16)
    for n in range(3):
        cols = slice(n * D_MODEL, (n + 1) * D_MODEL)
        o_ref[:, cols] = jnp.dot(h, w_ref[:, cols], preferred_element_type=F32).astype(BF16)


def _qkv(x, mod, g, w):
    tiles_per_batch = SEQ // QKV_TM
    return pl.pallas_call(
        _qkv_kernel,
        grid=(N_TOK // QKV_TM,),
        in_specs=[
            pl.BlockSpec((QKV_TM, D_MODEL), lambda i: (i, 0)),
            pl.BlockSpec((1, 6, D_MODEL), lambda i: (i // tiles_per_batch, 0, 0)),
            pl.BlockSpec((1, D_MODEL), lambda i: (0, 0)),
            pl.BlockSpec((D_MODEL, 3 * D_MODEL), lambda i: (0, 0)),
        ],
        out_specs=pl.BlockSpec((QKV_TM, 3 * D_MODEL), lambda i: (i, 0)),
        out_shape=jax.ShapeDtypeStruct((N_TOK, 3 * D_MODEL), BF16),
        compiler_params=_cparams(("arbitrary",)),
        name="qkv",
    )(x, mod, g, w)


def _half_masked(q, upper):
    lane = lax.broadcasted_iota(jnp.int32, q.shape, 1)
    keep = (lane >= LANES // 2) if upper else (lane < LANES // 2)
    return jnp.where(keep, q, jnp.zeros_like(q))


def _lane_tile(x, n):
    return jnp.concatenate([x] * n, axis=1)


def _qk(q, k):
    return lax.dot_general(q, k, (((1,), (1,)), ((), ())), preferred_element_type=F32)


def _diff_attn_kernel(slopes_ref, lam_ref, q_ref, k_ref, v_ref, g_ref, o_ref,
                      m0_ref, m1_ref, acc0_ref, acc1_ref, *, lambda_init):
    h = pl.program_id(1)
    qi = pl.program_id(2)
    slope = slopes_ref[h]
    q = q_ref[...]
    qm = (_half_masked(q, False), _half_masked(q, True))
    m_refs = (m0_ref, m1_ref)
    acc_refs = (acc0_ref, acc1_ref)
    ones = jnp.ones((DIFF_T, LANES), BF16)

    row = lax.broadcasted_iota(jnp.int32, (DIFF_T, DIFF_T), 0)
    col = lax.broadcasted_iota(jnp.int32, (DIFF_T, DIFF_T), 1)
    rel = (row - col).astype(F32) * slope

    allowed = (col // CHUNK) <= (row // CHUNK)
    diag_bias = jnp.where(allowed, jnp.abs(rel), jnp.inf)
    k = k_ref[pl.ds(qi * DIFF_T, DIFF_T), :]
    v1 = jnp.concatenate([v_ref[pl.ds(qi * DIFF_T, DIFF_T), :], ones], axis=1)
    s_maps = [_qk(qm[mi], k) - diag_bias for mi in range(2)]
    for mi in range(2):
        m = jnp.max(s_maps[mi], axis=-1, keepdims=True)
        p = jnp.exp(s_maps[mi] - m)
        m_refs[mi][...] = jnp.broadcast_to(m, (DIFF_T, LANES))
        acc_refs[mi][...] = jnp.dot(p.astype(BF16), v1, preferred_element_type=F32)

    def body(j, _):
        k = k_ref[pl.ds(j * DIFF_T, DIFF_T), :]
        v1 = jnp.concatenate([v_ref[pl.ds(j * DIFF_T, DIFF_T), :], ones], axis=1)
        shift = slope * ((qi - j) * DIFF_T).astype(F32)
        s_maps = [_qk(qm[mi], k) - rel for mi in range(2)]
        for mi in range(2):
            m = m_refs[mi][...]
            m_new = jnp.maximum(m, jnp.max(s_maps[mi], axis=-1, keepdims=True) - shift)
            alpha = jnp.exp(m - m_new)
            p = jnp.exp(s_maps[mi] - _lane_tile(m_new + shift, DIFF_T // LANES))
            m_refs[mi][...] = m_new
            acc_refs[mi][...] = _lane_tile(alpha, 2) * acc_refs[mi][...] + jnp.dot(
                p.astype(BF16), v1, preferred_element_type=F32)
        return 0

    lax.fori_loop(0, qi, body, 0)
    a0 = acc0_ref[...]
    a1 = acc1_ref[...]
    o = a0[:, :LANES] / a0[:, LANES:] - lam_ref[...] * (a1[:, :LANES] / a1[:, LANES:])
    ms = jnp.mean(o * o, axis=-1, keepdims=True)
    o = o * lax.rsqrt(ms + SUBLN_EPS) * (g_ref[...] * (1.0 - lambda_init))
    o_ref[...] = o.astype(BF16)


def _diff_attn(qkv, slopes, lam, subln_g, lambda_init):
    nq = SEQ // DIFF_T
    return pl.pallas_call(
        functools.partial(_diff_attn_kernel, lambda_init=lambda_init),
        grid=(BATCH, DIFF_HEADS, nq),
        in_specs=[
            pl.BlockSpec(memory_space=pltpu.SMEM),
            pl.BlockSpec((1, 1), lambda b, h, i: (0, 0)),
            pl.BlockSpec((DIFF_T, LANES), lambda b, h, i: (b * nq + i, h)),
            pl.BlockSpec((SEQ, LANES), lambda b, h, i: (b, DIFF_HEADS + h)),
            pl.BlockSpec((SEQ, LANES), lambda b, h, i: (b, 2 * DIFF_HEADS + h)),
            pl.BlockSpec((1, LANES), lambda b, h, i: (0, 0)),
        ],
        out_specs=pl.BlockSpec((DIFF_T, LANES), lambda b, h, i: (b * nq + i, h)),
        out_shape=jax.ShapeDtypeStruct((N_TOK, D_MODEL), BF16),
        scratch_shapes=[
            pltpu.VMEM((DIFF_T, LANES), F32),
            pltpu.VMEM((DIFF_T, LANES), F32),
            pltpu.VMEM((DIFF_T, 2 * LANES), F32),
            pltpu.VMEM((DIFF_T, 2 * LANES), F32),
        ],
        compiler_params=_cparams(("arbitrary", "arbitrary", "arbitrary")),
        name="diff_attn",
    )(slopes, lam, qkv, qkv, qkv, subln_g)


def _sb_attn_kernel(q_ref, k_ref, v_ref, o_ref, tail0_ref, tail1_ref, acc0_ref, acc1_ref):
    qi = pl.program_id(2)
    q = q_ref[...]
    qh = (_half_masked(q, False), _half_masked(q, True))
    tail_refs = (tail0_ref, tail1_ref)
    acc_refs = (acc0_ref, acc1_ref)

    row = lax.broadcasted_iota(jnp.int32, (SB_T, SB_T), 0)
    col = lax.broadcasted_iota(jnp.int32, (SB_T, SB_T), 1)
    strict = col < row
    neg_from = jnp.where(row >= col, -1.0, 0.0).astype(BF16)

    def scores(j, hh, mask):
        z = _qk(qh[hh], k_ref[pl.ds(j * SB_T, SB_T), :])
        sp = jnp.maximum(z, 0.0) + jnp.log(1.0 + jnp.exp(-jnp.abs(z)))
        if mask is not None:
            sp = jnp.where(mask, sp, 0.0)
        return z, sp

    def weighted(j, z, sp, tail, mask):
        log_a = (z + jnp.dot(sp.astype(BF16), neg_from, preferred_element_type=F32)
                 + _lane_tile(tail, SB_T // LANES))
        a = jnp.exp(log_a)
        if mask is not None:
            a = jnp.where(mask, a, 0.0)
        return jnp.dot(a.astype(BF16), v_ref[pl.ds(j * SB_T, SB_T), :],
                       preferred_element_type=F32)

    def row_sum(sp):
        return jnp.broadcast_to(jnp.sum(sp, axis=-1, keepdims=True), (SB_T, LANES))

    zero = jnp.zeros((SB_T, LANES), F32)

    @pl.when(qi == 0)
    def _():
        for hh in range(2):
            z, sp = scores(0, hh, strict)
            acc_refs[hh][...] = weighted(0, z, sp, zero, strict)
            tail_refs[hh][...] = -row_sum(sp)

    @pl.when(qi > 0)
    def _():
        zs = [(scores(qi, hh, strict), scores(qi - 1, hh, None)) for hh in range(2)]
        for hh in range(2):
            (z0, sp0), (z1, sp1) = zs[hh]
            tail0 = -row_sum(sp0)
            acc_refs[hh][...] = (weighted(qi, z0, sp0, zero, strict)
                                 + weighted(qi - 1, z1, sp1, tail0, None))
            tail_refs[hh][...] = tail0 - row_sum(sp1)

    def cond(state):
        j, live = state
        return (j >= 0) & (live > EXP_UNDERFLOW)

    def body(state):
        j, _ = state
        live = jnp.float32(-jnp.inf)
        for hh in range(2):
            z, sp = scores(j, hh, None)
            tail = tail_refs[hh][...]
            acc_refs[hh][...] += weighted(j, z, sp, tail, None)
            tail = tail - row_sum(sp)
            tail_refs[hh][...] = tail
            live = jnp.maximum(live, jnp.max(tail))
        return j - 1, live

    live = jnp.maximum(jnp.max(tail0_ref[...]), jnp.max(tail1_ref[...]))
    lax.while_loop(cond, body, (qi - 2, live))
    lane = lax.broadcasted_iota(jnp.int32, (SB_T, LANES), 1)
    o_ref[...] = jnp.where(lane < LANES // 2, acc0_ref[...], acc1_ref[...]).astype(BF16)


def _sb_attn(qkv):
    nq = SEQ // SB_T
    pairs = SB_HEADS // 2
    return pl.pallas_call(
        _sb_attn_kernel,
        grid=(BATCH, pairs, nq),
        in_specs=[
            pl.BlockSpec((SB_T, LANES), lambda b, h, i: (b * nq + i, h)),
            pl.BlockSpec((SEQ, LANES), lambda b, h, i: (b, pairs + h)),
            pl.BlockSpec((SEQ, LANES), lambda b, h, i: (b, 2 * pairs + h)),
        ],
        out_specs=pl.BlockSpec((SB_T, LANES), lambda b, h, i: (b * nq + i, h)),
        out_shape=jax.ShapeDtypeStruct((N_TOK, D_MODEL), BF16),
        scratch_shapes=[
            pltpu.VMEM((SB_T, LANES), F32),
            pltpu.VMEM((SB_T, LANES), F32),
            pltpu.VMEM((SB_T, LANES), F32),
            pltpu.VMEM((SB_T, LANES), F32),
        ],
        compiler_params=_cparams(("arbitrary", "arbitrary", "arbitrary")),
        name="sb_attn",
    )(qkv, qkv, qkv)


def _out_router_kernel(o_ref, x_ref, mod_ref, g_ref, w_ref, rw_ref, rb_ref,
                       x1_ref, h2_ref, idr_ref, wts_ref, cnt_ref, tri_ref, base_ref):
    i = pl.program_id(0)

    @pl.when(i == 0)
    def _():
        r = lax.broadcasted_iota(jnp.int32, (OUT_TM, OUT_TM), 0)
        c = lax.broadcasted_iota(jnp.int32, (OUT_TM, OUT_TM), 1)
        tri_ref[...] = jnp.where(c < r, 1.0, 0.0).astype(BF16)
        base_ref[...] = jnp.zeros_like(base_ref)

    m = mod_ref[0]
    y = jnp.dot(o_ref[...], w_ref[...], preferred_element_type=F32)
    x1 = x_ref[...] + m[2:3] * y
    x1_ref[...] = x1
    h2 = _modulated_norm(x1, g_ref[...], m[4:5], m[3:4])
    h2_ref[...] = h2

    logits = jnp.dot(h2.astype(BF16), rw_ref[...], preferred_element_type=F32) + rb_ref[...]
    lane = lax.broadcasted_iota(jnp.int32, logits.shape, 1).astype(F32)
    neg_inf = jnp.float32(-jnp.inf)
    big = jnp.float32(1e9)

    is_group = lane < N_GROUPS
    gl = jnp.where(is_group, logits, neg_inf)
    gmax = jnp.max(gl, axis=-1, keepdims=True)
    gidx = jnp.min(jnp.where(gl == gmax, lane, big), axis=-1, keepdims=True)
    gsum = jnp.sum(jnp.where(is_group, jnp.exp(logits - gmax), 0.0), axis=-1, keepdims=True)
    g_w = 1.0 / gsum

    lo = EXPERT_LANE0 + EXPERTS_PER_GROUP * gidx
    in_group = (lane >= lo) & (lane < lo + EXPERTS_PER_GROUP)
    el = jnp.where(in_group, logits, neg_inf)
    v0 = jnp.max(el, axis=-1, keepdims=True)
    i0 = jnp.min(jnp.where(el == v0, lane, big), axis=-1, keepdims=True)
    el = jnp.where(lane == i0, neg_inf, el)
    v1 = jnp.max(el, axis=-1, keepdims=True)
    i1 = jnp.min(jnp.where(el == v1, lane, big), axis=-1, keepdims=True)
    t = jnp.exp(v1 - v0)
    w0 = g_w / (1.0 + t)
    w1 = g_w * t / (1.0 + t)

    oh0 = jnp.where(lane == i0, 1.0, 0.0)
    oh1 = jnp.where(lane == i1, 1.0, 0.0)
    both = oh0 + oh1
    before = jnp.dot(tri_ref[...], both.astype(BF16), preferred_element_type=F32) + base_ref[...]
    r0 = jnp.sum(before * oh0, axis=-1, keepdims=True)
    r1 = jnp.sum(before * oh1, axis=-1, keepdims=True)
    base_ref[...] = base_ref[...] + jnp.sum(both, axis=0, keepdims=True)
    cnt_ref[...] = base_ref[...]

    e0 = i0 - EXPERT_LANE0
    e1 = i1 - EXPERT_LANE0
    idr = jnp.where(lane == 0, e0, jnp.where(lane == 1, e1, jnp.where(lane == 2, r0, r1)))
    idr_ref[...] = idr[:, :4].astype(jnp.int32)
    wts_ref[...] = jnp.where(lane == 0, w0, w1)[:, :2]


def _out_router(o, x, mod, g, w, rw, rb):
    tiles_per_batch = SEQ // OUT_TM
    return pl.pallas_call(
        _out_router_kernel,
        grid=(N_TOK // OUT_TM,),
        in_specs=[
            pl.BlockSpec((OUT_TM, D_MODEL), lambda i: (i, 0)),
            pl.BlockSpec((OUT_TM, D_MODEL), lambda i: (i, 0)),
            pl.BlockSpec((1, 6, D_MODEL), lambda i: (i // tiles_per_batch, 0, 0)),
            pl.BlockSpec((1, D_MODEL), lambda i: (0, 0)),
            pl.BlockSpec((D_MODEL, D_MODEL), lambda i: (0, 0)),
            pl.BlockSpec((D_MODEL, LANES), lambda i: (0, 0)),
            pl.BlockSpec((1, LANES), lambda i: (0, 0)),
        ],
        out_specs=[
            pl.BlockSpec((OUT_TM, D_MODEL), lambda i: (i, 0)),
            pl.BlockSpec((OUT_TM, D_MODEL), lambda i: (i, 0)),
            pl.BlockSpec((OUT_TM, 4), lambda i: (i, 0)),
            pl.BlockSpec((OUT_TM, 2), lambda i: (i, 0)),
            pl.BlockSpec((1, LANES), lambda i: (0, 0)),
        ],
        out_shape=[
            jax.ShapeDtypeStruct((N_TOK, D_MODEL), F32),
            jax.ShapeDtypeStruct((N_TOK, D_MODEL), F32),
            jax.ShapeDtypeStruct((N_TOK, 4), jnp.int32),
            jax.ShapeDtypeStruct((N_TOK, 2), F32),
            jax.ShapeDtypeStruct((1, LANES), F32),
        ],
        scratch_shapes=[
            pltpu.VMEM((OUT_TM, OUT_TM), BF16),
            pltpu.VMEM((1, LANES), F32),
        ],
        compiler_params=_cparams(("arbitrary",)),
        name="out_router",
    )(o, x, mod, g, w, rw, rb)


def _dispatch_kernel(dest_ref, h_ref, xs_in_ref, xs_ref, sem):
    del xs_in_ref

    def row_copy(r, d):
        return pltpu.make_async_copy(h_ref.at[pl.ds(r, 1)], xs_ref.at[pl.ds(d, 1)], sem)

    def issue(r, _):
        for k in range(TOP_K):
            row_copy(r, dest_ref[0, 0, TOP_K * r + k]).start()
        return 0

    def drain(r, _):
        for k in range(TOP_K):
            row_copy(0, 0).wait()
        return 0

    lax.fori_loop(0, DISP_TM, issue, 0, unroll=DMA_UNROLL)
    lax.fori_loop(0, DISP_TM, drain, 0, unroll=DMA_UNROLL)


def _dispatch(dest, h2, xs_init):
    nt = N_TOK // DISP_TM
    return pl.pallas_call(
        _dispatch_kernel,
        grid=(nt,),
        in_specs=[
            pl.BlockSpec((1, 1, TOP_K * DISP_TM), lambda i: (i, 0, 0), memory_space=pltpu.SMEM),
            pl.BlockSpec((DISP_TM, D_MODEL), lambda i: (i, 0)),
            pl.BlockSpec(memory_space=pl.ANY),
        ],
        out_specs=pl.BlockSpec(memory_space=pl.ANY),
        out_shape=jax.ShapeDtypeStruct((N_SLOTS, D_MODEL), F32),
        scratch_shapes=[pltpu.SemaphoreType.DMA(())],
        input_output_aliases={2: 0},
        compiler_params=_cparams(("arbitrary",)),
        name="dispatch",
    )(dest.reshape(nt, 1, TOP_K * DISP_TM), h2, xs_init)


def _expert_kernel(bexp_ref, nused_ref, xs_ref, wg_ref, wu_ref, wd_ref, ys_ref,
                   wg_bf, wu_bf, wd_bf):
    i = pl.program_id(0)

    @pl.when(i < nused_ref[0])
    def _():
        e = bexp_ref[i]
        prev = bexp_ref[jnp.maximum(i - 1, 0)]

        @pl.when((i == 0) | (e != prev))
        def _():
            wg_bf[...] = wg_ref[0, 0].astype(BF16)
            wu_bf[...] = wu_ref[0, 0].astype(BF16)
            wd_bf[...] = wd_ref[0, 0].astype(BF16)

        x = xs_ref[...].astype(BF16)
        g = jnp.dot(x, wg_bf[...], preferred_element_type=F32)
        u = jnp.dot(x, wu_bf[...], preferred_element_type=F32)
        hid = (g * jax.nn.sigmoid(g)) * u
        ys_ref[...] = jnp.dot(hid.astype(BF16), wd_bf[...], preferred_element_type=F32)

    @pl.when(i >= nused_ref[0])
    def _():
        ys_ref[...] = jnp.zeros_like(ys_ref)


def _experts(layer, bexp, nused, xs, wg, wu, wd):
    def row_map(i, bexp, nused):
        return (i, 0)

    def w_map(i, bexp, nused):
        return (layer, bexp[jnp.minimum(i, nused[0] - 1)], 0, 0)

    return pl.pallas_call(
        _expert_kernel,
        grid_spec=pltpu.PrefetchScalarGridSpec(
            num_scalar_prefetch=2,
            grid=(N_BLOCKS,),
            in_specs=[
                pl.BlockSpec((MOE_BLK, D_MODEL), row_map),
                pl.BlockSpec((1, 1, D_MODEL, EXPERT_HIDDEN), w_map),
                pl.BlockSpec((1, 1, D_MODEL, EXPERT_HIDDEN), w_map),
                pl.BlockSpec((1, 1, EXPERT_HIDDEN, D_MODEL), w_map),
            ],
            out_specs=pl.BlockSpec((MOE_BLK, D_MODEL), row_map),
            scratch_shapes=[
                pltpu.VMEM((D_MODEL, EXPERT_HIDDEN), BF16),
                pltpu.VMEM((D_MODEL, EXPERT_HIDDEN), BF16),
                pltpu.VMEM((EXPERT_HIDDEN, D_MODEL), BF16),
            ],
        ),
        out_shape=jax.ShapeDtypeStruct((N_SLOTS, D_MODEL), F32),
        compiler_params=_cparams(("arbitrary",)),
        name="experts",
    )(bexp, nused, xs, wg, wu, wd)


def _combine_kernel(dest_ref, x_ref, wts_ref, mod_ref, fg_ref, ys_ref, o_ref, buf, sem, *, final):
    def row_copy(d, k, r):
        return pltpu.make_async_copy(ys_ref.at[pl.ds(d, 1)], buf.at[k, pl.ds(r, 1)], sem)

    def issue(r, _):
        for k in range(TOP_K):
            row_copy(dest_ref[0, 0, TOP_K * r + k], k, r).start()
        return 0

    def drain(r, _):
        for k in range(TOP_K):
            row_copy(0, k, 0).wait()
        return 0

    lax.fori_loop(0, COMB_TM, issue, 0, unroll=DMA_UNROLL)
    lax.fori_loop(0, COMB_TM, drain, 0, unroll=DMA_UNROLL)

    w = wts_ref[...]
    y = w[:, 0:1] * buf[0] + w[:, 1:2] * buf[1]
    out = x_ref[...] + mod_ref[0][5:6] * y
    if final:
        ms = jnp.mean(out * out, axis=-1, keepdims=True)
        out = out * lax.rsqrt(ms + RMS_EPS) * fg_ref[...]
    o_ref[...] = out


def _combine(dest, x1, wts, mod, fg, ys, final):
    nt = N_TOK // COMB_TM
    tiles_per_batch = SEQ // COMB_TM
    return pl.pallas_call(
        functools.partial(_combine_kernel, final=final),
        grid=(nt,),
        in_specs=[
            pl.BlockSpec((1, 1, TOP_K * COMB_TM), lambda i: (i, 0, 0), memory_space=pltpu.SMEM),
            pl.BlockSpec((COMB_TM, D_MODEL), lambda i: (i, 0)),
            pl.BlockSpec((COMB_TM, TOP_K), lambda i: (i, 0)),
            pl.BlockSpec((1, 6, D_MODEL), lambda i: (i // tiles_per_batch, 0, 0)),
            pl.BlockSpec((1, D_MODEL), lambda i: (0, 0)),
            pl.BlockSpec(memory_space=pl.ANY),
        ],
        out_specs=pl.BlockSpec((COMB_TM, D_MODEL), lambda i: (i, 0)),
        out_shape=jax.ShapeDtypeStruct((N_TOK, D_MODEL), F32),
        scratch_shapes=[
            pltpu.VMEM((TOP_K, COMB_TM, D_MODEL), F32),
            pltpu.SemaphoreType.DMA(()),
        ],
        compiler_params=_cparams(("arbitrary",)),
        name="combine",
    )(dest.reshape(nt, 1, TOP_K * COMB_TM), x1, wts, mod, fg, ys)


def _routing_tables(idr, cnt):
    counts = cnt[0, EXPERT_LANE0:EXPERT_LANE0 + N_EXPERTS].astype(jnp.int32)
    padded = (counts + MOE_BLK - 1) // MOE_BLK * MOE_BLK
    pad_ends = jnp.cumsum(padded)
    pad_starts = pad_ends - padded
    dest = pad_starts[idr[:, 0:2]] + idr[:, 2:4]
    block_start = jnp.arange(N_BLOCKS, dtype=jnp.int32) * MOE_BLK
    bexp = jnp.minimum(jnp.sum(block_start[:, None] >= pad_ends[None, :], axis=1),
                       N_EXPERTS - 1).astype(jnp.int32)
    nused = (pad_ends[-1:] // MOE_BLK).astype(jnp.int32)
    return dest.reshape(-1).astype(jnp.int32), bexp, nused


def _router_weights(w_group, b_group, w_expert, b_expert):
    rw = jnp.zeros((D_MODEL, LANES), F32)
    rw = rw.at[:, :N_GROUPS].set(w_group)
    rw = rw.at[:, EXPERT_LANE0:EXPERT_LANE0 + N_EXPERTS].set(w_expert)
    rb = jnp.zeros((1, LANES), F32)
    rb = rb.at[0, :N_GROUPS].set(b_group)
    rb = rb.at[0, EXPERT_LANE0:EXPERT_LANE0 + N_EXPERTS].set(b_expert.reshape(-1))
    return rw.astype(BF16), rb


def _scaled_qkv_weight(w_in, head_dim):
    scale = jnp.concatenate([jnp.full((D_MODEL,), 1.0 / math.sqrt(head_dim), F32),
                             jnp.ones((2 * D_MODEL,), F32)])
    return (w_in * scale).astype(BF16)


def kernel(x, c, norm1_g, norm2_g, ada_w, ada_b, diff_w_in, diff_w_out, diff_lambda_q1, diff_lambda_k1, diff_lambda_q2, diff_lambda_k2, diff_subln_g, sb_w_in, sb_w_out, router_group_w, router_group_b, router_expert_w, router_expert_b, expert_w_gate, expert_w_up, expert_w_down, final_norm_g):
    xf = x.reshape(N_TOK, D_MODEL)
    mod_all = _ada(c, ada_w, ada_b)
    slopes = jnp.exp2(-8.0 * jnp.arange(1, DIFF_HEADS + 1, dtype=F32) / DIFF_HEADS)
    xs_init = jnp.zeros((N_SLOTS, D_MODEL), F32)

    for i in range(DEPTH):
        mod = mod_all[i].reshape(BATCH, 6, D_MODEL)
        j = i // 2
        if i % 2 == 0:
            lambda_init = 0.8 - 0.6 * math.exp(-0.3 * i)
            lam = (jnp.exp(jnp.sum(diff_lambda_q1[j] * diff_lambda_k1[j]))
                   - jnp.exp(jnp.sum(diff_lambda_q2[j] * diff_lambda_k2[j]))
                   + lambda_init).reshape(1, 1)
            qkv = _qkv(xf, mod, norm1_g[i].reshape(1, D_MODEL),
                       _scaled_qkv_weight(diff_w_in[j], DIFF_HEAD_DIM))
            o = _diff_attn(qkv, slopes, lam, diff_subln_g[j].reshape(1, LANES), lambda_init)
            w_out = diff_w_out[j]
        else:
            qkv = _qkv(xf, mod, norm1_g[i].reshape(1, D_MODEL),
                       _scaled_qkv_weight(sb_w_in[j], SB_HEAD_DIM))
            o = _sb_attn(qkv)
            w_out = sb_w_out[j]
        rw, rb = _router_weights(router_group_w[i], router_group_b[i],
                                 router_expert_w[i], router_expert_b[i])
        x1, h2, idr, wts, cnt = _out_router(o, xf, mod, norm2_g[i].reshape(1, D_MODEL),
                                            w_out.astype(BF16), rw, rb)
        dest, bexp, nused = _routing_tables(idr, cnt)
        xs = _dispatch(dest, h2, xs_init)
        ys = _experts(i, bexp, nused, xs, expert_w_gate, expert_w_up, expert_w_down)
        xf = _combine(dest, x1, wts, mod, final_norm_g.reshape(1, D_MODEL), ys,
                      final=(i == DEPTH - 1))
    return xf.reshape(BATCH, SEQ, D_MODEL)
```

```python
import functools
import math

import jax
import jax.numpy as jnp
from jax import lax
from jax.experimental import pallas as pl
from jax.experimental.pallas import tpu as pltpu

D_MODEL = 1024
BATCH = 8
SEQ = 2048
DEPTH = 2
N_TOK = BATCH * SEQ

CHUNK = 64
DIFF_HEADS = 8
DIFF_HEAD_DIM = D_MODEL // (2 * DIFF_HEADS)
SB_HEADS = 16
SB_HEAD_DIM = D_MODEL // SB_HEADS
N_GROUPS = 4
EXPERTS_PER_GROUP = 8
N_EXPERTS = N_GROUPS * EXPERTS_PER_GROUP
TOP_K = 2
EXPERT_HIDDEN = D_MODEL // 2
RMS_EPS = 1e-6
SUBLN_EPS = 1e-5

LANES = 128
EXPERT_LANE0 = 32
EXP_UNDERFLOW = -104.0

ADA_TN = 1536
QKV_TM = 512
DIFF_T = 512
SB_T = 256
OUT_TM = 512
DISP_TM = 1024
MOE_BLK = 256
N_SLOTS = N_TOK * TOP_K + N_EXPERTS * MOE_BLK
N_BLOCKS = N_SLOTS // MOE_BLK
COMB_TM = 512
DMA_UNROLL = 8
ROW_WORDS = D_MODEL // 2

VMEM_LIMIT = 56 * 1024 * 1024

F32 = jnp.float32
BF16 = jnp.bfloat16


def _cparams(sem):
    return pltpu.CompilerParams(dimension_semantics=sem, vmem_limit_bytes=VMEM_LIMIT)


def _ada_kernel(c_ref, w_ref, b_ref, o_ref):
    c = c_ref[...]
    cond = c * jax.nn.sigmoid(c)
    o_ref[0] = jnp.dot(cond, w_ref[0], preferred_element_type=F32,
                       precision=lax.Precision.HIGHEST) + b_ref[0]


def _ada(c, ada_w, ada_b):
    six_d = ada_w.shape[-1]
    return pl.pallas_call(
        _ada_kernel,
        grid=(DEPTH, six_d // ADA_TN),
        in_specs=[
            pl.BlockSpec((BATCH, D_MODEL), lambda l, n: (0, 0)),
            pl.BlockSpec((1, D_MODEL, ADA_TN), lambda l, n: (l, 0, n)),
            pl.BlockSpec((1, 1, ADA_TN), lambda l, n: (l, 0, n)),
        ],
        out_specs=pl.BlockSpec((1, BATCH, ADA_TN), lambda l, n: (l, 0, n)),
        out_shape=jax.ShapeDtypeStruct((DEPTH, BATCH, six_d), F32),
        compiler_params=_cparams(("arbitrary", "arbitrary")),
        name="ada",
    )(c, ada_w, ada_b.reshape(DEPTH, 1, six_d))


def _modulated_norm(x, g, scale, shift):
    ms = jnp.mean(x * x, axis=-1, keepdims=True)
    return x * lax.rsqrt(ms + RMS_EPS) * (g * (1.0 + scale)) + shift


def _qkv_kernel(x_ref, mod_ref, g_ref, w_ref, o_ref):
    m = mod_ref[0]
    h = _modulated_norm(x_ref[...], g_ref[...], m[1:2], m[0:1]).astype(BF16)
    for n in range(3):
        cols = slice(n * D_MODEL, (n + 1) * D_MODEL)
        o_ref[:, cols] = jnp.dot(h, w_ref[:, cols], preferred_element_type=F32).astype(BF16)


def _qkv(x, mod, g, w):
    tiles_per_batch = SEQ // QKV_TM
    return pl.pallas_call(
        _qkv_kernel,
        grid=(N_TOK // QKV_TM,),
        in_specs=[
            pl.BlockSpec((QKV_TM, D_MODEL), lambda i: (i, 0)),
            pl.BlockSpec((1, 6, D_MODEL), lambda i: (i // tiles_per_batch, 0, 0)),
            pl.BlockSpec((1, D_MODEL), lambda i: (0, 0)),
            pl.BlockSpec((D_MODEL, 3 * D_MODEL), lambda i: (0, 0)),
        ],
        out_specs=pl.BlockSpec((QKV_TM, 3 * D_MODEL), lambda i: (i, 0)),
        out_shape=jax.ShapeDtypeStruct((N_TOK, 3 * D_MODEL), BF16),
        compiler_params=_cparams(("arbitrary",)),
        name="qkv",
    )(x, mod, g, w)


def _half_masked(q, upper):
    lane = lax.broadcasted_iota(jnp.int32, q.shape, 1)
    keep = (lane >= LANES // 2) if upper else (lane < LANES // 2)
    return jnp.where(keep, q, jnp.zeros_like(q))


def _pack_bf16_pairs(x):
    half = x.shape[1] // 2
    hi = lax.bitcast_convert_type(x[:, :half].astype(BF16).astype(F32), jnp.uint32)
    lo = lax.bitcast_convert_type(x[:, half:].astype(BF16).astype(F32), jnp.uint32)
    return hi | (lo >> 16)


def _unpack_bf16_pairs(u):
    hi = lax.bitcast_convert_type(u & jnp.uint32(0xFFFF0000), F32)
    lo = lax.bitcast_convert_type(u << 16, F32)
    return jnp.concatenate([hi, lo], axis=1)


def _lane_tile(x, n):
    return jnp.concatenate([x] * n, axis=1)


def _qk(q, k):
    return lax.dot_general(q, k, (((1,), (1,)), ((), ())), preferred_element_type=F32)


def _diff_attn_kernel(slopes_ref, lam_ref, q_ref, k_ref, v_ref, g_ref, o_ref,
                      m0_ref, m1_ref, acc0_ref, acc1_ref, *, lambda_init):
    slope = slopes_ref[pl.program_id(1)]
    m_refs = (m0_ref, m1_ref)
    acc_refs = (acc0_ref, acc1_ref)
    ones = jnp.ones((DIFF_T, LANES), BF16)

    row = lax.broadcasted_iota(jnp.int32, (DIFF_T, DIFF_T), 0)
    col = lax.broadcasted_iota(jnp.int32, (DIFF_T, DIFF_T), 1)
    rel = (row - col).astype(F32) * slope
    allowed = (col // CHUNK) <= (row // CHUNK)
    diag_bias = jnp.where(allowed, jnp.abs(rel), jnp.inf)
    out_gain = g_ref[...] * (1.0 - lambda_init)

    def query_tile(qi, _):
        q = q_ref[pl.ds(qi * DIFF_T, DIFF_T), :]
        qm = (_half_masked(q, False), _half_masked(q, True))

        k = k_ref[pl.ds(qi * DIFF_T, DIFF_T), :]
        v1 = jnp.concatenate([v_ref[pl.ds(qi * DIFF_T, DIFF_T), :], ones], axis=1)
        s_maps = [_qk(qm[mi], k) - diag_bias for mi in range(2)]
        for mi in range(2):
            m = jnp.max(s_maps[mi], axis=-1, keepdims=True)
            p = jnp.exp(s_maps[mi] - m)
            m_refs[mi][...] = jnp.broadcast_to(m, (DIFF_T, LANES))
            acc_refs[mi][...] = jnp.dot(p.astype(BF16), v1, preferred_element_type=F32)

        def body(j, _):
            k = k_ref[pl.ds(j * DIFF_T, DIFF_T), :]
            v1 = jnp.concatenate([v_ref[pl.ds(j * DIFF_T, DIFF_T), :], ones], axis=1)
            shift = slope * jnp.asarray((qi - j) * DIFF_T).astype(F32)
            s_maps = [_qk(qm[mi], k) - rel for mi in range(2)]
            for mi in range(2):
                m = m_refs[mi][...]
                m_new = jnp.maximum(m, jnp.max(s_maps[mi], axis=-1, keepdims=True) - shift)
                alpha = jnp.exp(m - m_new)
                p = jnp.exp(s_maps[mi] - _lane_tile(m_new + shift, DIFF_T // LANES))
                m_refs[mi][...] = m_new
                acc_refs[mi][...] = _lane_tile(alpha, 2) * acc_refs[mi][...] + jnp.dot(
                    p.astype(BF16), v1, preferred_element_type=F32)
            return 0

        lax.fori_loop(0, qi, body, 0)
        a0 = acc0_ref[...]
        a1 = acc1_ref[...]
        o = a0[:, :LANES] / a0[:, LANES:] - lam_ref[...] * (a1[:, :LANES] / a1[:, LANES:])
        ms = jnp.mean(o * o, axis=-1, keepdims=True)
        o = o * lax.rsqrt(ms + SUBLN_EPS) * out_gain
        o_ref[pl.ds(qi * DIFF_T, DIFF_T), :] = o.astype(BF16)
        return 0

    lax.fori_loop(0, SEQ // DIFF_T, query_tile, 0)


def _diff_attn(qkv, slopes, lam, subln_g, lambda_init):
    return pl.pallas_call(
        functools.partial(_diff_attn_kernel, lambda_init=lambda_init),
        grid=(BATCH, DIFF_HEADS),
        in_specs=[
            pl.BlockSpec(memory_space=pltpu.SMEM),
            pl.BlockSpec((1, 1), lambda b, h: (0, 0)),
            pl.BlockSpec((SEQ, LANES), lambda b, h: (b, h)),
            pl.BlockSpec((SEQ, LANES), lambda b, h: (b, DIFF_HEADS + h)),
            pl.BlockSpec((SEQ, LANES), lambda b, h: (b, 2 * DIFF_HEADS + h)),
            pl.BlockSpec((1, LANES), lambda b, h: (0, 0)),
        ],
        out_specs=pl.BlockSpec((SEQ, LANES), lambda b, h: (b, h)),
        out_shape=jax.ShapeDtypeStruct((N_TOK, D_MODEL), BF16),
        scratch_shapes=[
            pltpu.VMEM((DIFF_T, LANES), F32),
            pltpu.VMEM((DIFF_T, LANES), F32),
            pltpu.VMEM((DIFF_T, 2 * LANES), F32),
            pltpu.VMEM((DIFF_T, 2 * LANES), F32),
        ],
        compiler_params=_cparams(("arbitrary", "arbitrary")),
        name="diff_attn",
    )(slopes, lam, qkv, qkv, qkv, subln_g)


def _sb_attn_kernel(q_ref, k_ref, v_ref, o_ref, tail0_ref, tail1_ref, acc0_ref, acc1_ref):
    tail_refs = (tail0_ref, tail1_ref)
    acc_refs = (acc0_ref, acc1_ref)

    row = lax.broadcasted_iota(jnp.int32, (SB_T, SB_T), 0)
    col = lax.broadcasted_iota(jnp.int32, (SB_T, SB_T), 1)
    strict = col < row
    neg_from = jnp.where(row >= col, -1.0, 0.0).astype(BF16)

    def scores(j, qhead, mask):
        z = _qk(qhead, k_ref[pl.ds(j * SB_T, SB_T), :])
        sp = jnp.maximum(z, 0.0) + jnp.log(1.0 + jnp.exp(-jnp.abs(z)))
        if mask is not None:
            sp = jnp.where(mask, sp, 0.0)
        return z, sp

    def weighted(j, z, sp, tail, mask):
        log_a = (z + jnp.dot(sp.astype(BF16), neg_from, preferred_element_type=F32)
                 + _lane_tile(tail, SB_T // LANES))
        a = jnp.exp(log_a)
        if mask is not None:
            a = jnp.where(mask, a, 0.0)
        return jnp.dot(a.astype(BF16), v_ref[pl.ds(j * SB_T, SB_T), :],
                       preferred_element_type=F32)

    def row_sum(sp):
        return jnp.broadcast_to(jnp.sum(sp, axis=-1, keepdims=True), (SB_T, LANES))

    zero = jnp.zeros((SB_T, LANES), F32)
    lane = lax.broadcasted_iota(jnp.int32, (SB_T, LANES), 1)

    def query_heads(qi):
        q = q_ref[pl.ds(qi * SB_T, SB_T), :]
        return (_half_masked(q, False), _half_masked(q, True))

    def finish(qi):
        o_ref[pl.ds(qi * SB_T, SB_T), :] = jnp.where(
            lane < LANES // 2, acc0_ref[...], acc1_ref[...]).astype(BF16)

    qh = query_heads(0)
    for hh in range(2):
        z, sp = scores(0, qh[hh], strict)
        acc_refs[hh][...] = weighted(0, z, sp, zero, strict)
    finish(0)

    def query_tile(qi, _):
        qh = query_heads(qi)
        zs = [(scores(qi, qh[hh], strict), scores(qi - 1, qh[hh], None)) for hh in range(2)]
        for hh in range(2):
            (z0, sp0), (z1, sp1) = zs[hh]
            tail0 = -row_sum(sp0)
            acc_refs[hh][...] = (weighted(qi, z0, sp0, zero, strict)
                                 + weighted(qi - 1, z1, sp1, tail0, None))
            tail_refs[hh][...] = tail0 - row_sum(sp1)

        def cond(state):
            j, live = state
            return (j >= 0) & (live > EXP_UNDERFLOW)

        def body(state):
            j, _ = state
            live = jnp.float32(-jnp.inf)
            for hh in range(2):
                z, sp = scores(j, qh[hh], None)
                tail = tail_refs[hh][...]
                acc_refs[hh][...] += weighted(j, z, sp, tail, None)
                tail = tail - row_sum(sp)
                tail_refs[hh][...] = tail
                live = jnp.maximum(live, jnp.max(tail))
            return j - 1, live

        live = jnp.maximum(jnp.max(tail0_ref[...]), jnp.max(tail1_ref[...]))
        lax.while_loop(cond, body, (qi - 2, live))
        finish(qi)
        return 0

    lax.fori_loop(1, SEQ // SB_T, query_tile, 0)


def _sb_attn(qkv):
    pairs = SB_HEADS // 2
    return pl.pallas_call(
        _sb_attn_kernel,
        grid=(BATCH, pairs),
        in_specs=[
            pl.BlockSpec((SEQ, LANES), lambda b, h: (b, h)),
            pl.BlockSpec((SEQ, LANES), lambda b, h: (b, pairs + h)),
            pl.BlockSpec((SEQ, LANES), lambda b, h: (b, 2 * pairs + h)),
        ],
        out_specs=pl.BlockSpec((SEQ, LANES), lambda b, h: (b, h)),
        out_shape=jax.ShapeDtypeStruct((N_TOK, D_MODEL), BF16),
        scratch_shapes=[
            pltpu.VMEM((SB_T, LANES), F32),
            pltpu.VMEM((SB_T, LANES), F32),
            pltpu.VMEM((SB_T, LANES), F32),
            pltpu.VMEM((SB_T, LANES), F32),
        ],
        compiler_params=_cparams(("arbitrary", "arbitrary")),
        name="sb_attn",
    )(qkv, qkv, qkv)


def _out_router_kernel(o_ref, x_ref, mod_ref, g_ref, w_ref, rw_ref, rb_ref,
                       x1_ref, h2_ref, idr_ref, wts_ref, cnt_ref, tri_ref, base_ref):
    i = pl.program_id(0)

    @pl.when(i == 0)
    def _():
        r = lax.broadcasted_iota(jnp.int32, (OUT_TM, OUT_TM), 0)
        c = lax.broadcasted_iota(jnp.int32, (OUT_TM, OUT_TM), 1)
        tri_ref[...] = jnp.where(c < r, 1.0, 0.0).astype(BF16)
        base_ref[...] = jnp.zeros_like(base_ref)

    m = mod_ref[0]
    y = jnp.dot(o_ref[...], w_ref[...], preferred_element_type=F32)
    x1 = x_ref[...] + m[2:3] * y
    x1_ref[...] = x1
    h2 = _modulated_norm(x1, g_ref[...], m[4:5], m[3:4])
    h2_ref[...] = _pack_bf16_pairs(h2)

    logits = jnp.dot(h2.astype(BF16), rw_ref[...], preferred_element_type=F32) + rb_ref[...]
    lane = lax.broadcasted_iota(jnp.int32, logits.shape, 1).astype(F32)
    neg_inf = jnp.float32(-jnp.inf)
    big = jnp.float32(1e9)

    is_group = lane < N_GROUPS
    gl = jnp.where(is_group, logits, neg_inf)
    gmax = jnp.max(gl, axis=-1, keepdims=True)
    gidx = jnp.min(jnp.where(gl == gmax, lane, big), axis=-1, keepdims=True)
    gsum = jnp.sum(jnp.where(is_group, jnp.exp(logits - gmax), 0.0), axis=-1, keepdims=True)
    g_w = 1.0 / gsum

    lo = EXPERT_LANE0 + EXPERTS_PER_GROUP * gidx
    in_group = (lane >= lo) & (lane < lo + EXPERTS_PER_GROUP)
    el = jnp.where(in_group, logits, neg_inf)
    v0 = jnp.max(el, axis=-1, keepdims=True)
    i0 = jnp.min(jnp.where(el == v0, lane, big), axis=-1, keepdims=True)
    el = jnp.where(lane == i0, neg_inf, el)
    v1 = jnp.max(el, axis=-1, keepdims=True)
    i1 = jnp.min(jnp.where(el == v1, lane, big), axis=-1, keepdims=True)
    t = jnp.exp(v1 - v0)
    w0 = g_w / (1.0 + t)
    w1 = g_w * t / (1.0 + t)

    oh0 = jnp.where(lane == i0, 1.0, 0.0)
    oh1 = jnp.where(lane == i1, 1.0, 0.0)
    both = oh0 + oh1
    before = jnp.dot(tri_ref[...], both.astype(BF16), preferred_element_type=F32) + base_ref[...]
    r0 = jnp.sum(before * oh0, axis=-1, keepdims=True)
    r1 = jnp.sum(before * oh1, axis=-1, keepdims=True)
    base_ref[...] = base_ref[...] + jnp.sum(both, axis=0, keepdims=True)
    cnt_ref[...] = base_ref[...]

    e0 = i0 - EXPERT_LANE0
    e1 = i1 - EXPERT_LANE0
    idr = jnp.where(lane == 0, e0, jnp.where(lane == 1, e1, jnp.where(lane == 2, r0, r1)))
    idr_ref[...] = idr[:, :4].astype(jnp.int32)
    wts_ref[...] = jnp.where(lane == 0, w0, w1)[:, :2]


def _out_router(o, x, mod, g, w, rw, rb):
    tiles_per_batch = SEQ // OUT_TM
    return pl.pallas_call(
        _out_router_kernel,
        grid=(N_TOK // OUT_TM,),
        in_specs=[
            pl.BlockSpec((OUT_TM, D_MODEL), lambda i: (i, 0)),
            pl.BlockSpec((OUT_TM, D_MODEL), lambda i: (i, 0)),
            pl.BlockSpec((1, 6, D_MODEL), lambda i: (i // tiles_per_batch, 0, 0)),
            pl.BlockSpec((1, D_MODEL), lambda i: (0, 0)),
            pl.BlockSpec((D_MODEL, D_MODEL), lambda i: (0, 0)),
            pl.BlockSpec((D_MODEL, LANES), lambda i: (0, 0)),
            pl.BlockSpec((1, LANES), lambda i: (0, 0)),
        ],
        out_specs=[
            pl.BlockSpec((OUT_TM, D_MODEL), lambda i: (i, 0)),
            pl.BlockSpec((OUT_TM, ROW_WORDS), lambda i: (i, 0)),
            pl.BlockSpec((OUT_TM, 4), lambda i: (i, 0)),
            pl.BlockSpec((OUT_TM, 2), lambda i: (i, 0)),
            pl.BlockSpec((1, LANES), lambda i: (0, 0)),
        ],
        out_shape=[
            jax.ShapeDtypeStruct((N_TOK, D_MODEL), F32),
            jax.ShapeDtypeStruct((N_TOK, ROW_WORDS), jnp.uint32),
            jax.ShapeDtypeStruct((N_TOK, 4), jnp.int32),
            jax.ShapeDtypeStruct((N_TOK, 2), F32),
            jax.ShapeDtypeStruct((1, LANES), F32),
        ],
        scratch_shapes=[
            pltpu.VMEM((OUT_TM, OUT_TM), BF16),
            pltpu.VMEM((1, LANES), F32),
        ],
        compiler_params=_cparams(("arbitrary",)),
        name="out_router",
    )(o, x, mod, g, w, rw, rb)


def _dispatch_kernel(dest_ref, pad_end_ref, padded_ref, h_ref, xs_ref, zeros_ref, sem, zsem):
    @pl.when(pl.program_id(0) == 0)
    def _():
        zeros_ref[...] = jnp.zeros_like(zeros_ref)

        def tail_copy(e):
            start = pl.multiple_of(jnp.maximum(pad_end_ref[e] - MOE_BLK, 0), MOE_BLK)
            return pltpu.make_async_copy(zeros_ref, xs_ref.at[pl.ds(start, MOE_BLK)], zsem)

        def fill(e, _):
            @pl.when(padded_ref[e] > 0)
            def _():
                tail_copy(e).start()
            return 0

        def fill_done(e, _):
            @pl.when(padded_ref[e] > 0)
            def _():
                tail_copy(e).wait()
            return 0

        def block_copy(b):
            start = pl.multiple_of(b * MOE_BLK, MOE_BLK)
            return pltpu.make_async_copy(zeros_ref, xs_ref.at[pl.ds(start, MOE_BLK)], zsem)

        def fill_unused(b, _):
            block_copy(b).start()
            return 0

        def fill_unused_done(b, _):
            block_copy(b).wait()
            return 0

        first_unused = pad_end_ref[N_EXPERTS - 1] // MOE_BLK
        lax.fori_loop(0, N_EXPERTS, fill, 0)
        lax.fori_loop(first_unused, N_BLOCKS, fill_unused, 0)
        lax.fori_loop(0, N_EXPERTS, fill_done, 0)
        lax.fori_loop(first_unused, N_BLOCKS, fill_unused_done, 0)

    def row_copy(r, d):
        return pltpu.make_async_copy(h_ref.at[pl.ds(r, 1)], xs_ref.at[pl.ds(d, 1)], sem)

    def issue(r, _):
        for k in range(TOP_K):
            row_copy(r, dest_ref[0, 0, TOP_K * r + k]).start()
        return 0

    def drain(r, _):
        for k in range(TOP_K):
            row_copy(0, 0).wait()
        return 0

    lax.fori_loop(0, DISP_TM, issue, 0, unroll=DMA_UNROLL)
    lax.fori_loop(0, DISP_TM, drain, 0, unroll=DMA_UNROLL)


def _dispatch(dest, pad_ends, padded, h2):
    nt = N_TOK // DISP_TM
    return pl.pallas_call(
        _dispatch_kernel,
        grid=(nt,),
        in_specs=[
            pl.BlockSpec((1, 1, TOP_K * DISP_TM), lambda i: (i, 0, 0), memory_space=pltpu.SMEM),
            pl.BlockSpec(memory_space=pltpu.SMEM),
            pl.BlockSpec(memory_space=pltpu.SMEM),
            pl.BlockSpec((DISP_TM, ROW_WORDS), lambda i: (i, 0)),
        ],
        out_specs=pl.BlockSpec(memory_space=pl.ANY),
        out_shape=jax.ShapeDtypeStruct((N_SLOTS, ROW_WORDS), jnp.uint32),
        scratch_shapes=[
            pltpu.VMEM((MOE_BLK, ROW_WORDS), jnp.uint32),
            pltpu.SemaphoreType.DMA(()),
            pltpu.SemaphoreType.DMA(()),
        ],
        compiler_params=_cparams(("arbitrary",)),
        name="dispatch",
    )(dest.reshape(nt, 1, TOP_K * DISP_TM), pad_ends, padded, h2)


def _expert_kernel(bexp_ref, nused_ref, xs_ref, wg_ref, wu_ref, wd_ref, ys_ref,
                   wg_bf, wu_bf, wd_bf):
    i = pl.program_id(0)

    @pl.when(i < nused_ref[0])
    def _():
        e = bexp_ref[i]
        prev = bexp_ref[jnp.maximum(i - 1, 0)]

        @pl.when((i == 0) | (e != prev))
        def _():
            wg_bf[...] = wg_ref[0, 0].astype(BF16)
            wu_bf[...] = wu_ref[0, 0].astype(BF16)
            wd_bf[...] = wd_ref[0, 0].astype(BF16)

        x = _unpack_bf16_pairs(xs_ref[...]).astype(BF16)
        g = jnp.dot(x, wg_bf[...], preferred_element_type=F32)
        u = jnp.dot(x, wu_bf[...], preferred_element_type=F32)
        hid = (g * jax.nn.sigmoid(g)) * u
        y = jnp.dot(hid.astype(BF16), wd_bf[...], preferred_element_type=F32)
        ys_ref[...] = _pack_bf16_pairs(y)

    @pl.when(i >= nused_ref[0])
    def _():
        ys_ref[...] = jnp.zeros_like(ys_ref)


def _experts(layer, bexp, nused, xs, wg, wu, wd):
    def in_row_map(i, bexp, nused):
        return (jnp.minimum(i, nused[0] - 1), 0)

    def row_map(i, bexp, nused):
        return (i, 0)

    def w_map(i, bexp, nused):
        return (layer, bexp[jnp.minimum(i, nused[0] - 1)], 0, 0)

    return pl.pallas_call(
        _expert_kernel,
        grid_spec=pltpu.PrefetchScalarGridSpec(
            num_scalar_prefetch=2,
            grid=(N_BLOCKS,),
            in_specs=[
                pl.BlockSpec((MOE_BLK, ROW_WORDS), in_row_map),
                pl.BlockSpec((1, 1, D_MODEL, EXPERT_HIDDEN), w_map),
                pl.BlockSpec((1, 1, D_MODEL, EXPERT_HIDDEN), w_map),
                pl.BlockSpec((1, 1, EXPERT_HIDDEN, D_MODEL), w_map),
            ],
            out_specs=pl.BlockSpec((MOE_BLK, ROW_WORDS), row_map),
            scratch_shapes=[
                pltpu.VMEM((D_MODEL, EXPERT_HIDDEN), BF16),
                pltpu.VMEM((D_MODEL, EXPERT_HIDDEN), BF16),
                pltpu.VMEM((EXPERT_HIDDEN, D_MODEL), BF16),
            ],
        ),
        out_shape=jax.ShapeDtypeStruct((N_SLOTS, ROW_WORDS), jnp.uint32),
        compiler_params=_cparams(("arbitrary",)),
        name="experts",
    )(bexp, nused, xs, wg, wu, wd)


def _combine_kernel(dest_ref, x_ref, wts_ref, mod_ref, fg_ref, ys_ref, o_ref, buf, sem, *, final):
    def row_copy(d, k, r):
        return pltpu.make_async_copy(ys_ref.at[pl.ds(d, 1)], buf.at[k, pl.ds(r, 1)], sem)

    def issue(r, _):
        for k in range(TOP_K):
            row_copy(dest_ref[0, 0, TOP_K * r + k], k, r).start()
        return 0

    def drain(r, _):
        for k in range(TOP_K):
            row_copy(0, k, 0).wait()
        return 0

    lax.fori_loop(0, COMB_TM, issue, 0, unroll=DMA_UNROLL)
    lax.fori_loop(0, COMB_TM, drain, 0, unroll=DMA_UNROLL)

    w = wts_ref[...]
    y = w[:, 0:1] * _unpack_bf16_pairs(buf[0]) + w[:, 1:2] * _unpack_bf16_pairs(buf[1])
    out = x_ref[...] + mod_ref[0][5:6] * y
    if final:
        ms = jnp.mean(out * out, axis=-1, keepdims=True)
        out = out * lax.rsqrt(ms + RMS_EPS) * fg_ref[...]
    o_ref[...] = out


def _combine(dest, x1, wts, mod, fg, ys, final):
    nt = N_TOK // COMB_TM
    tiles_per_batch = SEQ // COMB_TM
    return pl.pallas_call(
        functools.partial(_combine_kernel, final=final),
        grid=(nt,),
        in_specs=[
            pl.BlockSpec((1, 1, TOP_K * COMB_TM), lambda i: (i, 0, 0), memory_space=pltpu.SMEM),
            pl.BlockSpec((COMB_TM, D_MODEL), lambda i: (i, 0)),
            pl.BlockSpec((COMB_TM, TOP_K), lambda i: (i, 0)),
            pl.BlockSpec((1, 6, D_MODEL), lambda i: (i // tiles_per_batch, 0, 0)),
            pl.BlockSpec((1, D_MODEL), lambda i: (0, 0)),
            pl.BlockSpec(memory_space=pl.ANY),
        ],
        out_specs=pl.BlockSpec((COMB_TM, D_MODEL), lambda i: (i, 0)),
        out_shape=jax.ShapeDtypeStruct((N_TOK, D_MODEL), F32),
        scratch_shapes=[
            pltpu.VMEM((TOP_K, COMB_TM, ROW_WORDS), jnp.uint32),
            pltpu.SemaphoreType.DMA(()),
        ],
        compiler_params=_cparams(("arbitrary",)),
        name="combine",
    )(dest.reshape(nt, 1, TOP_K * COMB_TM), x1, wts, mod, fg, ys)


def _routing_tables(idr, cnt):
    counts = cnt[0, EXPERT_LANE0:EXPERT_LANE0 + N_EXPERTS].astype(jnp.int32)
    padded = (counts + MOE_BLK - 1) // MOE_BLK * MOE_BLK
    pad_ends = jnp.cumsum(padded)
    pad_starts = pad_ends - padded
    experts = jnp.arange(N_EXPERTS, dtype=jnp.int32)
    is_expert = idr[:, 0:2, None] == experts
    dest = jnp.sum(jnp.where(is_expert, pad_starts, 0), axis=-1) + idr[:, 2:4]
    block_start = jnp.arange(N_BLOCKS, dtype=jnp.int32) * MOE_BLK
    bexp = jnp.minimum(jnp.sum(block_start[:, None] >= pad_ends[None, :], axis=1),
                       N_EXPERTS - 1).astype(jnp.int32)
    nused = (pad_ends[-1:] // MOE_BLK).astype(jnp.int32)
    return (dest.reshape(-1).astype(jnp.int32), bexp, nused,
            pad_ends.astype(jnp.int32), padded.astype(jnp.int32))


def _router_weights(w_group, b_group, w_expert, b_expert):
    rw = jnp.zeros((D_MODEL, LANES), F32)
    rw = rw.at[:, :N_GROUPS].set(w_group)
    rw = rw.at[:, EXPERT_LANE0:EXPERT_LANE0 + N_EXPERTS].set(w_expert)
    rb = jnp.zeros((1, LANES), F32)
    rb = rb.at[0, :N_GROUPS].set(b_group)
    rb = rb.at[0, EXPERT_LANE0:EXPERT_LANE0 + N_EXPERTS].set(b_expert.reshape(-1))
    return rw.astype(BF16), rb


def _scaled_qkv_weight(w_in, head_dim):
    scale = jnp.concatenate([jnp.full((D_MODEL,), 1.0 / math.sqrt(head_dim), F32),
                             jnp.ones((2 * D_MODEL,), F32)])
    return (w_in * scale).astype(BF16)


def kernel(x, c, norm1_g, norm2_g, ada_w, ada_b, diff_w_in, diff_w_out, diff_lambda_q1, diff_lambda_k1, diff_lambda_q2, diff_lambda_k2, diff_subln_g, sb_w_in, sb_w_out, router_group_w, router_group_b, router_expert_w, router_expert_b, expert_w_gate, expert_w_up, expert_w_down, final_norm_g):
    xf = x.reshape(N_TOK, D_MODEL)
    mod_all = _ada(c, ada_w, ada_b)
    slopes = jnp.exp2(-8.0 * jnp.arange(1, DIFF_HEADS + 1, dtype=F32) / DIFF_HEADS)

    for i in range(DEPTH):
        mod = mod_all[i].reshape(BATCH, 6, D_MODEL)
        j = i // 2
        if i % 2 == 0:
            lambda_init = 0.8 - 0.6 * math.exp(-0.3 * i)
            lam = (jnp.exp(jnp.sum(diff_lambda_q1[j] * diff_lambda_k1[j]))
                   - jnp.exp(jnp.sum(diff_lambda_q2[j] * diff_lambda_k2[j]))
                   + lambda_init).reshape(1, 1)
            qkv = _qkv(xf, mod, norm1_g[i].reshape(1, D_MODEL),
                       _scaled_qkv_weight(diff_w_in[j], DIFF_HEAD_DIM))
            o = _diff_attn(qkv, slopes, lam, diff_subln_g[j].reshape(1, LANES), lambda_init)
            w_out = diff_w_out[j]
        else:
            qkv = _qkv(xf, mod, norm1_g[i].reshape(1, D_MODEL),
                       _scaled_qkv_weight(sb_w_in[j], SB_HEAD_DIM))
            o = _sb_attn(qkv)
            w_out = sb_w_out[j]
        rw, rb = _router_weights(router_group_w[i], router_group_b[i],
                                 router_expert_w[i], router_expert_b[i])
        x1, h2, idr, wts, cnt = _out_router(o, xf, mod, norm2_g[i].reshape(1, D_MODEL),
                                            w_out.astype(BF16), rw, rb)
        dest, bexp, nused, pad_ends, padded = _routing_tables(idr, cnt)
        xs = _dispatch(dest, pad_ends, padded, h2)
        ys = _experts(i, bexp, nused, xs, expert_w_gate, expert_w_up, expert_w_down)
        xf = _combine(dest, x1, wts, mod, final_norm_g.reshape(1, D_MODEL), ys,
                      final=(i == DEPTH - 1))
    return xf.reshape(BATCH, SEQ, D_MODEL)
```

```python
import functools
import math

import jax
import jax.numpy as jnp
from jax import lax
from jax.experimental import pallas as pl
from jax.experimental.pallas import tpu as pltpu
from jax.experimental.pallas import tpu_sc as plsc

D_MODEL = 1024
BATCH = 8
SEQ = 2048
DEPTH = 2
N_TOK = BATCH * SEQ

CHUNK = 64
DIFF_HEADS = 8
DIFF_HEAD_DIM = D_MODEL // (2 * DIFF_HEADS)
SB_HEADS = 16
SB_HEAD_DIM = D_MODEL // SB_HEADS
N_GROUPS = 4
EXPERTS_PER_GROUP = 8
N_EXPERTS = N_GROUPS * EXPERTS_PER_GROUP
TOP_K = 2
EXPERT_HIDDEN = D_MODEL // 2
RMS_EPS = 1e-6
SUBLN_EPS = 1e-5

LANES = 128
EXPERT_LANE0 = 32
EXP_UNDERFLOW = -104.0

ADA_TN = 1536
QKV_TM = 512
DIFF_T = 512
SB_T = 256
OUT_TM = 512
DISP_TM = 1024
MOE_BLK = 256
N_SLOTS = N_TOK * TOP_K + N_EXPERTS * MOE_BLK
N_BLOCKS = N_SLOTS // MOE_BLK
COMB_TM = 512
DMA_UNROLL = 8
ROW_WORDS = D_MODEL // 2

SC_CORES = 2
SC_SUBCORES = 16
SC_CHUNK = 128

VMEM_LIMIT = 56 * 1024 * 1024

F32 = jnp.float32
BF16 = jnp.bfloat16


def _cparams(sem):
    return pltpu.CompilerParams(dimension_semantics=sem, vmem_limit_bytes=VMEM_LIMIT)


def _ada_kernel(c_ref, w_ref, b_ref, o_ref):
    c = c_ref[...]
    cond = c * jax.nn.sigmoid(c)
    o_ref[0] = jnp.dot(cond, w_ref[0], preferred_element_type=F32,
                       precision=lax.Precision.HIGHEST) + b_ref[0]


def _ada(c, ada_w, ada_b):
    six_d = ada_w.shape[-1]
    return pl.pallas_call(
        _ada_kernel,
        grid=(DEPTH, six_d // ADA_TN),
        in_specs=[
            pl.BlockSpec((BATCH, D_MODEL), lambda l, n: (0, 0)),
            pl.BlockSpec((1, D_MODEL, ADA_TN), lambda l, n: (l, 0, n)),
            pl.BlockSpec((1, 1, ADA_TN), lambda l, n: (l, 0, n)),
        ],
        out_specs=pl.BlockSpec((1, BATCH, ADA_TN), lambda l, n: (l, 0, n)),
        out_shape=jax.ShapeDtypeStruct((DEPTH, BATCH, six_d), F32),
        compiler_params=_cparams(("arbitrary", "arbitrary")),
        name="ada",
    )(c, ada_w, ada_b.reshape(DEPTH, 1, six_d))


def _modulated_norm(x, g, scale, shift):
    ms = jnp.mean(x * x, axis=-1, keepdims=True)
    return x * lax.rsqrt(ms + RMS_EPS) * (g * (1.0 + scale)) + shift


def _qkv_kernel(x_ref, mod_ref, g_ref, w_ref, o_ref):
    m = mod_ref[0]
    h = _modulated_norm(x_ref[...], g_ref[...], m[1:2], m[0:1]).astype(BF16)
    for n in range(3):
        cols = slice(n * D_MODEL, (n + 1) * D_MODEL)
        o_ref[:, cols] = jnp.dot(h, w_ref[:, cols], preferred_element_type=F32).astype(BF16)


def _qkv(x, mod, g, w):
    tiles_per_batch = SEQ // QKV_TM
    return pl.pallas_call(
        _qkv_kernel,
        grid=(N_TOK // QKV_TM,),
        in_specs=[
            pl.BlockSpec((QKV_TM, D_MODEL), lambda i: (i, 0)),
            pl.BlockSpec((1, 6, D_MODEL), lambda i: (i // tiles_per_batch, 0, 0)),
            pl.BlockSpec((1, D_MODEL), lambda i: (0, 0)),
            pl.BlockSpec((D_MODEL, 3 * D_MODEL), lambda i: (0, 0)),
        ],
        out_specs=pl.BlockSpec((QKV_TM, 3 * D_MODEL), lambda i: (i, 0)),
        out_shape=jax.ShapeDtypeStruct((N_TOK, 3 * D_MODEL), BF16),
        compiler_params=_cparams(("arbitrary",)),
        name="qkv",
    )(x, mod, g, w)


def _half_masked(q, upper):
    lane = lax.broadcasted_iota(jnp.int32, q.shape, 1)
    keep = (lane >= LANES // 2) if upper else (lane < LANES // 2)
    return jnp.where(keep, q, jnp.zeros_like(q))


def _pack_bf16_pairs(x):
    half = x.shape[1] // 2
    hi = lax.bitcast_convert_type(x[:, :half].astype(BF16).astype(F32), jnp.uint32)
    lo = lax.bitcast_convert_type(x[:, half:].astype(BF16).astype(F32), jnp.uint32)
    return hi | (lo >> 16)


def _unpack_bf16_pairs(u):
    hi = lax.bitcast_convert_type(u & jnp.uint32(0xFFFF0000), F32)
    lo = lax.bitcast_convert_type(u << 16, F32)
    return jnp.concatenate([hi, lo], axis=1)


def _lane_tile(x, n):
    return jnp.concatenate([x] * n, axis=1)


def _qk(q, k):
    return lax.dot_general(q, k, (((1,), (1,)), ((), ())), preferred_element_type=F32)


def _diff_attn_kernel(slopes_ref, lam_ref, q_ref, k_ref, v_ref, g_ref, o_ref,
                      m0_ref, m1_ref, acc0_ref, acc1_ref, *, lambda_init):
    slope = slopes_ref[pl.program_id(1)]
    m_refs = (m0_ref, m1_ref)
    acc_refs = (acc0_ref, acc1_ref)
    ones = jnp.ones((DIFF_T, LANES), BF16)

    row = lax.broadcasted_iota(jnp.int32, (DIFF_T, DIFF_T), 0)
    col = lax.broadcasted_iota(jnp.int32, (DIFF_T, DIFF_T), 1)
    rel = (row - col).astype(F32) * slope
    allowed = (col // CHUNK) <= (row // CHUNK)
    diag_bias = jnp.where(allowed, jnp.abs(rel), jnp.inf)
    out_gain = g_ref[...] * (1.0 - lambda_init)

    def query_tile(qi, _):
        q = q_ref[pl.ds(qi * DIFF_T, DIFF_T), :]
        qm = (_half_masked(q, False), _half_masked(q, True))

        k = k_ref[pl.ds(qi * DIFF_T, DIFF_T), :]
        v1 = jnp.concatenate([v_ref[pl.ds(qi * DIFF_T, DIFF_T), :], ones], axis=1)
        s_maps = [_qk(qm[mi], k) - diag_bias for mi in range(2)]
        for mi in range(2):
            m = jnp.max(s_maps[mi], axis=-1, keepdims=True)
            p = jnp.exp(s_maps[mi] - m)
            m_refs[mi][...] = jnp.broadcast_to(m, (DIFF_T, LANES))
            acc_refs[mi][...] = jnp.dot(p.astype(BF16), v1, preferred_element_type=F32)

        def body(j, _):
            k = k_ref[pl.ds(j * DIFF_T, DIFF_T), :]
            v1 = jnp.concatenate([v_ref[pl.ds(j * DIFF_T, DIFF_T), :], ones], axis=1)
            shift = slope * jnp.asarray((qi - j) * DIFF_T).astype(F32)
            s_maps = [_qk(qm[mi], k) - rel for mi in range(2)]
            for mi in range(2):
                m = m_refs[mi][...]
                m_new = jnp.maximum(m, jnp.max(s_maps[mi], axis=-1, keepdims=True) - shift)
                alpha = jnp.exp(m - m_new)
                p = jnp.exp(s_maps[mi] - _lane_tile(m_new + shift, DIFF_T // LANES))
                m_refs[mi][...] = m_new
                acc_refs[mi][...] = _lane_tile(alpha, 2) * acc_refs[mi][...] + jnp.dot(
                    p.astype(BF16), v1, preferred_element_type=F32)
            return 0

        lax.fori_loop(0, qi, body, 0)
        a0 = acc0_ref[...]
        a1 = acc1_ref[...]
        o = a0[:, :LANES] / a0[:, LANES:] - lam_ref[...] * (a1[:, :LANES] / a1[:, LANES:])
        ms = jnp.mean(o * o, axis=-1, keepdims=True)
        o = o * lax.rsqrt(ms + SUBLN_EPS) * out_gain
        o_ref[pl.ds(qi * DIFF_T, DIFF_T), :] = o.astype(BF16)
        return 0

    lax.fori_loop(0, SEQ // DIFF_T, query_tile, 0)


def _diff_attn(qkv, slopes, lam, subln_g, lambda_init):
    return pl.pallas_call(
        functools.partial(_diff_attn_kernel, lambda_init=lambda_init),
        grid=(BATCH, DIFF_HEADS),
        in_specs=[
            pl.BlockSpec(memory_space=pltpu.SMEM),
            pl.BlockSpec((1, 1), lambda b, h: (0, 0)),
            pl.BlockSpec((SEQ, LANES), lambda b, h: (b, h)),
            pl.BlockSpec((SEQ, LANES), lambda b, h: (b, DIFF_HEADS + h)),
            pl.BlockSpec((SEQ, LANES), lambda b, h: (b, 2 * DIFF_HEADS + h)),
            pl.BlockSpec((1, LANES), lambda b, h: (0, 0)),
        ],
        out_specs=pl.BlockSpec((SEQ, LANES), lambda b, h: (b, h)),
        out_shape=jax.ShapeDtypeStruct((N_TOK, D_MODEL), BF16),
        scratch_shapes=[
            pltpu.VMEM((DIFF_T, LANES), F32),
            pltpu.VMEM((DIFF_T, LANES), F32),
            pltpu.VMEM((DIFF_T, 2 * LANES), F32),
            pltpu.VMEM((DIFF_T, 2 * LANES), F32),
        ],
        compiler_params=_cparams(("arbitrary", "arbitrary")),
        name="diff_attn",
    )(slopes, lam, qkv, qkv, qkv, subln_g)


def _sb_attn_kernel(q_ref, k_ref, v_ref, o_ref, tail0_ref, tail1_ref, acc0_ref, acc1_ref):
    tail_refs = (tail0_ref, tail1_ref)
    acc_refs = (acc0_ref, acc1_ref)

    row = lax.broadcasted_iota(jnp.int32, (SB_T, SB_T), 0)
    col = lax.broadcasted_iota(jnp.int32, (SB_T, SB_T), 1)
    strict = col < row
    neg_from = jnp.where(row >= col, -1.0, 0.0).astype(BF16)

    def scores(j, qhead, mask):
        z = _qk(qhead, k_ref[pl.ds(j * SB_T, SB_T), :])
        sp = jnp.maximum(z, 0.0) + jnp.log(1.0 + jnp.exp(-jnp.abs(z)))
        if mask is not None:
            sp = jnp.where(mask, sp, 0.0)
        return z, sp

    def weighted(j, z, sp, tail, mask):
        log_a = (z + jnp.dot(sp.astype(BF16), neg_from, preferred_element_type=F32)
                 + _lane_tile(tail, SB_T // LANES))
        a = jnp.exp(log_a)
        if mask is not None:
            a = jnp.where(mask, a, 0.0)
        return jnp.dot(a.astype(BF16), v_ref[pl.ds(j * SB_T, SB_T), :],
                       preferred_element_type=F32)

    def row_sum(sp):
        return jnp.broadcast_to(jnp.sum(sp, axis=-1, keepdims=True), (SB_T, LANES))

    zero = jnp.zeros((SB_T, LANES), F32)
    lane = lax.broadcasted_iota(jnp.int32, (SB_T, LANES), 1)

    def query_heads(qi):
        q = q_ref[pl.ds(qi * SB_T, SB_T), :]
        return (_half_masked(q, False), _half_masked(q, True))

    def finish(qi):
        o_ref[pl.ds(qi * SB_T, SB_T), :] = jnp.where(
            lane < LANES // 2, acc0_ref[...], acc1_ref[...]).astype(BF16)

    qh = query_heads(0)
    for hh in range(2):
        z, sp = scores(0, qh[hh], strict)
        acc_refs[hh][...] = weighted(0, z, sp, zero, strict)
    finish(0)

    def query_tile(qi, _):
        qh = query_heads(qi)
        zs = [(scores(qi, qh[hh], strict), scores(qi - 1, qh[hh], None)) for hh in range(2)]
        for hh in range(2):
            (z0, sp0), (z1, sp1) = zs[hh]
            tail0 = -row_sum(sp0)
            acc_refs[hh][...] = (weighted(qi, z0, sp0, zero, strict)
                                 + weighted(qi - 1, z1, sp1, tail0, None))
            tail_refs[hh][...] = tail0 - row_sum(sp1)

        def cond(state):
            j, live = state
            return (j >= 0) & (live > EXP_UNDERFLOW)

        def body(state):
            j, _ = state
            live = jnp.float32(-jnp.inf)
            for hh in range(2):
                z, sp = scores(j, qh[hh], None)
                tail = tail_refs[hh][...]
                acc_refs[hh][...] += weighted(j, z, sp, tail, None)
                tail = tail - row_sum(sp)
                tail_refs[hh][...] = tail
                live = jnp.maximum(live, jnp.max(tail))
            return j - 1, live

        live = jnp.maximum(jnp.max(tail0_ref[...]), jnp.max(tail1_ref[...]))
        lax.while_loop(cond, body, (qi - 2, live))
        finish(qi)
        return 0

    lax.fori_loop(1, SEQ // SB_T, query_tile, 0)


def _sb_attn(qkv):
    pairs = SB_HEADS // 2
    return pl.pallas_call(
        _sb_attn_kernel,
        grid=(BATCH, pairs),
        in_specs=[
            pl.BlockSpec((SEQ, LANES), lambda b, h: (b, h)),
            pl.BlockSpec((SEQ, LANES), lambda b, h: (b, pairs + h)),
            pl.BlockSpec((SEQ, LANES), lambda b, h: (b, 2 * pairs + h)),
        ],
        out_specs=pl.BlockSpec((SEQ, LANES), lambda b, h: (b, h)),
        out_shape=jax.ShapeDtypeStruct((N_TOK, D_MODEL), BF16),
        scratch_shapes=[
            pltpu.VMEM((SB_T, LANES), F32),
            pltpu.VMEM((SB_T, LANES), F32),
            pltpu.VMEM((SB_T, LANES), F32),
            pltpu.VMEM((SB_T, LANES), F32),
        ],
        compiler_params=_cparams(("arbitrary", "arbitrary")),
        name="sb_attn",
    )(qkv, qkv, qkv)


def _out_router_kernel(o_ref, x_ref, mod_ref, g_ref, w_ref, rw_ref, rb_ref,
                       x1_ref, h2_ref, idr_ref, wts_ref, cnt_ref, tri_ref, base_ref):
    i = pl.program_id(0)

    @pl.when(i == 0)
    def _():
        r = lax.broadcasted_iota(jnp.int32, (OUT_TM, OUT_TM), 0)
        c = lax.broadcasted_iota(jnp.int32, (OUT_TM, OUT_TM), 1)
        tri_ref[...] = jnp.where(c < r, 1.0, 0.0).astype(BF16)
        base_ref[...] = jnp.zeros_like(base_ref)

    m = mod_ref[0]
    y = jnp.dot(o_ref[...], w_ref[...], preferred_element_type=F32)
    x1 = x_ref[...] + m[2:3] * y
    x1_ref[...] = x1
    h2 = _modulated_norm(x1, g_ref[...], m[4:5], m[3:4])
    h2_ref[...] = _pack_bf16_pairs(h2)

    logits = jnp.dot(h2.astype(BF16), rw_ref[...], preferred_element_type=F32) + rb_ref[...]
    lane = lax.broadcasted_iota(jnp.int32, logits.shape, 1).astype(F32)
    neg_inf = jnp.float32(-jnp.inf)
    big = jnp.float32(1e9)

    is_group = lane < N_GROUPS
    gl = jnp.where(is_group, logits, neg_inf)
    gmax = jnp.max(gl, axis=-1, keepdims=True)
    gidx = jnp.min(jnp.where(gl == gmax, lane, big), axis=-1, keepdims=True)
    gsum = jnp.sum(jnp.where(is_group, jnp.exp(logits - gmax), 0.0), axis=-1, keepdims=True)
    g_w = 1.0 / gsum

    lo = EXPERT_LANE0 + EXPERTS_PER_GROUP * gidx
    in_group = (lane >= lo) & (lane < lo + EXPERTS_PER_GROUP)
    el = jnp.where(in_group, logits, neg_inf)
    v0 = jnp.max(el, axis=-1, keepdims=True)
    i0 = jnp.min(jnp.where(el == v0, lane, big), axis=-1, keepdims=True)
    el = jnp.where(lane == i0, neg_inf, el)
    v1 = jnp.max(el, axis=-1, keepdims=True)
    i1 = jnp.min(jnp.where(el == v1, lane, big), axis=-1, keepdims=True)
    t = jnp.exp(v1 - v0)
    w0 = g_w / (1.0 + t)
    w1 = g_w * t / (1.0 + t)

    oh0 = jnp.where(lane == i0, 1.0, 0.0)
    oh1 = jnp.where(lane == i1, 1.0, 0.0)
    both = oh0 + oh1
    before = jnp.dot(tri_ref[...], both.astype(BF16), preferred_element_type=F32) + base_ref[...]
    r0 = jnp.sum(before * oh0, axis=-1, keepdims=True)
    r1 = jnp.sum(before * oh1, axis=-1, keepdims=True)
    base_ref[...] = base_ref[...] + jnp.sum(both, axis=0, keepdims=True)
    cnt_ref[...] = base_ref[...]

    e0 = i0 - EXPERT_LANE0
    e1 = i1 - EXPERT_LANE0
    idr = jnp.where(lane == 0, e0, jnp.where(lane == 1, e1, jnp.where(lane == 2, r0, r1)))
    idr_ref[...] = idr[:, :4].astype(jnp.int32)
    wts_ref[...] = jnp.where(lane == 0, w0, w1)[:, :2]


def _out_router(o, x, mod, g, w, rw, rb):
    tiles_per_batch = SEQ // OUT_TM
    return pl.pallas_call(
        _out_router_kernel,
        grid=(N_TOK // OUT_TM,),
        in_specs=[
            pl.BlockSpec((OUT_TM, D_MODEL), lambda i: (i, 0)),
            pl.BlockSpec((OUT_TM, D_MODEL), lambda i: (i, 0)),
            pl.BlockSpec((1, 6, D_MODEL), lambda i: (i // tiles_per_batch, 0, 0)),
            pl.BlockSpec((1, D_MODEL), lambda i: (0, 0)),
            pl.BlockSpec((D_MODEL, D_MODEL), lambda i: (0, 0)),
            pl.BlockSpec((D_MODEL, LANES), lambda i: (0, 0)),
            pl.BlockSpec((1, LANES), lambda i: (0, 0)),
        ],
        out_specs=[
            pl.BlockSpec((OUT_TM, D_MODEL), lambda i: (i, 0)),
            pl.BlockSpec((OUT_TM, ROW_WORDS), lambda i: (i, 0)),
            pl.BlockSpec((OUT_TM, 4), lambda i: (i, 0)),
            pl.BlockSpec((OUT_TM, 2), lambda i: (i, 0)),
            pl.BlockSpec((1, LANES), lambda i: (0, 0)),
        ],
        out_shape=[
            jax.ShapeDtypeStruct((N_TOK, D_MODEL), F32),
            jax.ShapeDtypeStruct((N_TOK, ROW_WORDS), jnp.uint32),
            jax.ShapeDtypeStruct((N_TOK, 4), jnp.int32),
            jax.ShapeDtypeStruct((N_TOK, 2), F32),
            jax.ShapeDtypeStruct((1, LANES), F32),
        ],
        scratch_shapes=[
            pltpu.VMEM((OUT_TM, OUT_TM), BF16),
            pltpu.VMEM((1, LANES), F32),
        ],
        compiler_params=_cparams(("arbitrary",)),
        name="out_router",
    )(o, x, mod, g, w, rw, rb)


def _dispatch_kernel(dest_ref, pad_end_ref, padded_ref, h_ref, xs_ref, zeros_ref, sem, zsem):
    @pl.when(pl.program_id(0) == 0)
    def _():
        zeros_ref[...] = jnp.zeros_like(zeros_ref)

        def tail_copy(e):
            start = pl.multiple_of(jnp.maximum(pad_end_ref[e] - MOE_BLK, 0), MOE_BLK)
            return pltpu.make_async_copy(zeros_ref, xs_ref.at[pl.ds(start, MOE_BLK)], zsem)

        def fill(e, _):
            @pl.when(padded_ref[e] > 0)
            def _():
                tail_copy(e).start()
            return 0

        def fill_done(e, _):
            @pl.when(padded_ref[e] > 0)
            def _():
                tail_copy(e).wait()
            return 0

        def block_copy(b):
            start = pl.multiple_of(b * MOE_BLK, MOE_BLK)
            return pltpu.make_async_copy(zeros_ref, xs_ref.at[pl.ds(start, MOE_BLK)], zsem)

        def fill_unused(b, _):
            block_copy(b).start()
            return 0

        def fill_unused_done(b, _):
            block_copy(b).wait()
            return 0

        first_unused = pad_end_ref[N_EXPERTS - 1] // MOE_BLK
        lax.fori_loop(0, N_EXPERTS, fill, 0)
        lax.fori_loop(first_unused, N_BLOCKS, fill_unused, 0)
        lax.fori_loop(0, N_EXPERTS, fill_done, 0)
        lax.fori_loop(first_unused, N_BLOCKS, fill_unused_done, 0)

    def row_copy(r, d):
        return pltpu.make_async_copy(h_ref.at[pl.ds(r, 1)], xs_ref.at[pl.ds(d, 1)], sem)

    def issue(r, _):
        for k in range(TOP_K):
            row_copy(r, dest_ref[0, 0, TOP_K * r + k]).start()
        return 0

    def drain(r, _):
        for k in range(TOP_K):
            row_copy(0, 0).wait()
        return 0

    lax.fori_loop(0, DISP_TM, issue, 0, unroll=DMA_UNROLL)
    lax.fori_loop(0, DISP_TM, drain, 0, unroll=DMA_UNROLL)


def _dispatch(dest, pad_ends, padded, h2):
    nt = N_TOK // DISP_TM
    return pl.pallas_call(
        _dispatch_kernel,
        grid=(nt,),
        in_specs=[
            pl.BlockSpec((1, 1, TOP_K * DISP_TM), lambda i: (i, 0, 0), memory_space=pltpu.SMEM),
            pl.BlockSpec(memory_space=pltpu.SMEM),
            pl.BlockSpec(memory_space=pltpu.SMEM),
            pl.BlockSpec((DISP_TM, ROW_WORDS), lambda i: (i, 0)),
        ],
        out_specs=pl.BlockSpec(memory_space=pl.ANY),
        out_shape=jax.ShapeDtypeStruct((N_SLOTS, ROW_WORDS), jnp.uint32),
        scratch_shapes=[
            pltpu.VMEM((MOE_BLK, ROW_WORDS), jnp.uint32),
            pltpu.SemaphoreType.DMA(()),
            pltpu.SemaphoreType.DMA(()),
        ],
        compiler_params=_cparams(("arbitrary",)),
        name="dispatch",
    )(dest.reshape(nt, 1, TOP_K * DISP_TM), pad_ends, padded, h2)


def _expert_kernel(bexp_ref, nused_ref, xs_ref, wg_ref, wu_ref, wd_ref, ys_ref,
                   wg_bf, wu_bf, wd_bf):
    i = pl.program_id(0)

    @pl.when(i < nused_ref[0])
    def _():
        e = bexp_ref[i]
        prev = bexp_ref[jnp.maximum(i - 1, 0)]

        @pl.when((i == 0) | (e != prev))
        def _():
            wg_bf[...] = wg_ref[0, 0].astype(BF16)
            wu_bf[...] = wu_ref[0, 0].astype(BF16)
            wd_bf[...] = wd_ref[0, 0].astype(BF16)

        x = _unpack_bf16_pairs(xs_ref[...]).astype(BF16)
        g = jnp.dot(x, wg_bf[...], preferred_element_type=F32)
        u = jnp.dot(x, wu_bf[...], preferred_element_type=F32)
        hid = (g * jax.nn.sigmoid(g)) * u
        y = jnp.dot(hid.astype(BF16), wd_bf[...], preferred_element_type=F32)
        ys_ref[...] = _pack_bf16_pairs(y)

    @pl.when(i >= nused_ref[0])
    def _():
        ys_ref[...] = jnp.zeros_like(ys_ref)


def _experts(layer, bexp, nused, xs, wg, wu, wd):
    def in_row_map(i, bexp, nused):
        return (jnp.minimum(i, nused[0] - 1), 0)

    def row_map(i, bexp, nused):
        return (i, 0)

    def w_map(i, bexp, nused):
        return (layer, bexp[jnp.minimum(i, nused[0] - 1)], 0, 0)

    return pl.pallas_call(
        _expert_kernel,
        grid_spec=pltpu.PrefetchScalarGridSpec(
            num_scalar_prefetch=2,
            grid=(N_BLOCKS,),
            in_specs=[
                pl.BlockSpec((MOE_BLK, ROW_WORDS), in_row_map),
                pl.BlockSpec((1, 1, D_MODEL, EXPERT_HIDDEN), w_map),
                pl.BlockSpec((1, 1, D_MODEL, EXPERT_HIDDEN), w_map),
                pl.BlockSpec((1, 1, EXPERT_HIDDEN, D_MODEL), w_map),
            ],
            out_specs=pl.BlockSpec((MOE_BLK, ROW_WORDS), row_map),
            scratch_shapes=[
                pltpu.VMEM((D_MODEL, EXPERT_HIDDEN), BF16),
                pltpu.VMEM((D_MODEL, EXPERT_HIDDEN), BF16),
                pltpu.VMEM((EXPERT_HIDDEN, D_MODEL), BF16),
            ],
        ),
        out_shape=jax.ShapeDtypeStruct((N_SLOTS, ROW_WORDS), jnp.uint32),
        compiler_params=_cparams(("arbitrary",)),
        name="experts",
    )(bexp, nused, xs, wg, wu, wd)


def _combine_kernel(dest_ref, x_ref, wts_ref, mod_ref, fg_ref, ys_ref, o_ref, buf, sem, *, final):
    def row_copy(d, k, r):
        return pltpu.make_async_copy(ys_ref.at[pl.ds(d, 1)], buf.at[k, pl.ds(r, 1)], sem)

    def issue(r, _):
        for k in range(TOP_K):
            row_copy(dest_ref[0, 0, TOP_K * r + k], k, r).start()
        return 0

    def drain(r, _):
        for k in range(TOP_K):
            row_copy(0, k, 0).wait()
        return 0

    lax.fori_loop(0, COMB_TM, issue, 0, unroll=DMA_UNROLL)
    lax.fori_loop(0, COMB_TM, drain, 0, unroll=DMA_UNROLL)

    w = wts_ref[...]
    y = w[:, 0:1] * _unpack_bf16_pairs(buf[0]) + w[:, 1:2] * _unpack_bf16_pairs(buf[1])
    out = x_ref[...] + mod_ref[0][5:6] * y
    if final:
        ms = jnp.mean(out * out, axis=-1, keepdims=True)
        out = out * lax.rsqrt(ms + RMS_EPS) * fg_ref[...]
    o_ref[...] = out


def _combine(dest, x1, wts, mod, fg, ys, final):
    nt = N_TOK // COMB_TM
    tiles_per_batch = SEQ // COMB_TM
    return pl.pallas_call(
        functools.partial(_combine_kernel, final=final),
        grid=(nt,),
        in_specs=[
            pl.BlockSpec((1, 1, TOP_K * COMB_TM), lambda i: (i, 0, 0), memory_space=pltpu.SMEM),
            pl.BlockSpec((COMB_TM, D_MODEL), lambda i: (i, 0)),
            pl.BlockSpec((COMB_TM, TOP_K), lambda i: (i, 0)),
            pl.BlockSpec((1, 6, D_MODEL), lambda i: (i // tiles_per_batch, 0, 0)),
            pl.BlockSpec((1, D_MODEL), lambda i: (0, 0)),
            pl.BlockSpec(memory_space=pl.ANY),
        ],
        out_specs=pl.BlockSpec((COMB_TM, D_MODEL), lambda i: (i, 0)),
        out_shape=jax.ShapeDtypeStruct((N_TOK, D_MODEL), F32),
        scratch_shapes=[
            pltpu.VMEM((TOP_K, COMB_TM, ROW_WORDS), jnp.uint32),
            pltpu.SemaphoreType.DMA(()),
        ],
        compiler_params=_cparams(("arbitrary",)),
        name="combine",
    )(dest.reshape(nt, 1, TOP_K * COMB_TM), x1, wts, mod, fg, ys)


def _sc_gather_rows(table, idx):
    n_rows = idx.shape[0]
    width = table.shape[1]
    workers = SC_CORES * SC_SUBCORES
    per_worker = n_rows // workers
    assert per_worker * workers == n_rows and per_worker % SC_CHUNK == 0
    mesh = plsc.VectorSubcoreMesh(core_axis_name="c", subcore_axis_name="s")

    def body(table_hbm, idx_hbm, out_hbm, idx_v, rows_v, sem):
        wid = lax.axis_index("s") * SC_CORES + lax.axis_index("c")
        base = wid * per_worker

        def chunk(c, _):
            off = pl.multiple_of(base + c * SC_CHUNK, SC_CHUNK)
            pltpu.sync_copy(idx_hbm.at[pl.ds(off, SC_CHUNK)], idx_v)
            pltpu.async_copy(table_hbm.at[idx_v], rows_v, sem).wait()
            pltpu.sync_copy(rows_v, out_hbm.at[pl.ds(off, SC_CHUNK)])
            return 0

        lax.fori_loop(0, per_worker // SC_CHUNK, chunk, 0)

    return pl.kernel(
        body,
        out_type=jax.ShapeDtypeStruct((n_rows, width), table.dtype),
        mesh=mesh,
        scratch_types=[
            pltpu.VMEM((SC_CHUNK,), jnp.int32),
            pltpu.VMEM((SC_CHUNK, width), table.dtype),
            pltpu.SemaphoreType.DMA,
        ],
        name="sc_gather_rows",
    )(table, idx)


def _mix_kernel(x_ref, wts_ref, mod_ref, fg_ref, y0_ref, y1_ref, o_ref, *, final):
    w = wts_ref[...]
    y = (w[:, 0:1] * _unpack_bf16_pairs(y0_ref[...])
         + w[:, 1:2] * _unpack_bf16_pairs(y1_ref[...]))
    out = x_ref[...] + mod_ref[0][5:6] * y
    if final:
        ms = jnp.mean(out * out, axis=-1, keepdims=True)
        out = out * lax.rsqrt(ms + RMS_EPS) * fg_ref[...]
    o_ref[...] = out


def _mix(x1, wts, mod, fg, y2, final):
    nt = N_TOK // COMB_TM
    tiles_per_batch = SEQ // COMB_TM
    return pl.pallas_call(
        functools.partial(_mix_kernel, final=final),
        grid=(nt,),
        in_specs=[
            pl.BlockSpec((COMB_TM, D_MODEL), lambda i: (i, 0)),
            pl.BlockSpec((COMB_TM, TOP_K), lambda i: (i, 0)),
            pl.BlockSpec((1, 6, D_MODEL), lambda i: (i // tiles_per_batch, 0, 0)),
            pl.BlockSpec((1, D_MODEL), lambda i: (0, 0)),
            pl.BlockSpec((COMB_TM, ROW_WORDS), lambda i: (i, 0)),
            pl.BlockSpec((COMB_TM, ROW_WORDS), lambda i: (nt + i, 0)),
        ],
        out_specs=pl.BlockSpec((COMB_TM, D_MODEL), lambda i: (i, 0)),
        out_shape=jax.ShapeDtypeStruct((N_TOK, D_MODEL), F32),
        compiler_params=_cparams(("arbitrary",)),
        name="mix",
    )(x1, wts, mod, fg, y2, y2)


def _routing_tables(idr, cnt):
    counts = cnt[0, EXPERT_LANE0:EXPERT_LANE0 + N_EXPERTS].astype(jnp.int32)
    padded = (counts + MOE_BLK - 1) // MOE_BLK * MOE_BLK
    pad_ends = jnp.cumsum(padded)
    pad_starts = pad_ends - padded
    experts = jnp.arange(N_EXPERTS, dtype=jnp.int32)
    is_expert = idr[:, 0:2, None] == experts
    dest = jnp.sum(jnp.where(is_expert, pad_starts, 0), axis=-1) + idr[:, 2:4]
    block_start = jnp.arange(N_BLOCKS, dtype=jnp.int32) * MOE_BLK
    bexp = jnp.minimum(jnp.sum(block_start[:, None] >= pad_ends[None, :], axis=1),
                       N_EXPERTS - 1).astype(jnp.int32)
    nused = (pad_ends[-1:] // MOE_BLK).astype(jnp.int32)
    return dest.astype(jnp.int32), bexp, nused


def _router_weights(w_group, b_group, w_expert, b_expert):
    rw = jnp.zeros((D_MODEL, LANES), F32)
    rw = rw.at[:, :N_GROUPS].set(w_group)
    rw = rw.at[:, EXPERT_LANE0:EXPERT_LANE0 + N_EXPERTS].set(w_expert)
    rb = jnp.zeros((1, LANES), F32)
    rb = rb.at[0, :N_GROUPS].set(b_group)
    rb = rb.at[0, EXPERT_LANE0:EXPERT_LANE0 + N_EXPERTS].set(b_expert.reshape(-1))
    return rw.astype(BF16), rb


def _scaled_qkv_weight(w_in, head_dim):
    scale = jnp.concatenate([jnp.full((D_MODEL,), 1.0 / math.sqrt(head_dim), F32),
                             jnp.ones((2 * D_MODEL,), F32)])
    return (w_in * scale).astype(BF16)


def kernel(x, c, norm1_g, norm2_g, ada_w, ada_b, diff_w_in, diff_w_out, diff_lambda_q1, diff_lambda_k1, diff_lambda_q2, diff_lambda_k2, diff_subln_g, sb_w_in, sb_w_out, router_group_w, router_group_b, router_expert_w, router_expert_b, expert_w_gate, expert_w_up, expert_w_down, final_norm_g):
    xf = x.reshape(N_TOK, D_MODEL)
    mod_all = _ada(c, ada_w, ada_b)
    slopes = jnp.exp2(-8.0 * jnp.arange(1, DIFF_HEADS + 1, dtype=F32) / DIFF_HEADS)

    for i in range(DEPTH):
        mod = mod_all[i].reshape(BATCH, 6, D_MODEL)
        j = i // 2
        if i % 2 == 0:
            lambda_init = 0.8 - 0.6 * math.exp(-0.3 * i)
            lam = (jnp.exp(jnp.sum(diff_lambda_q1[j] * diff_lambda_k1[j]))
                   - jnp.exp(jnp.sum(diff_lambda_q2[j] * diff_lambda_k2[j]))
                   + lambda_init).reshape(1, 1)
            qkv = _qkv(xf, mod, norm1_g[i].reshape(1, D_MODEL),
                       _scaled_qkv_weight(diff_w_in[j], DIFF_HEAD_DIM))
            o = _diff_attn(qkv, slopes, lam, diff_subln_g[j].reshape(1, LANES), lambda_init)
            w_out = diff_w_out[j]
        else:
            qkv = _qkv(xf, mod, norm1_g[i].reshape(1, D_MODEL),
                       _scaled_qkv_weight(sb_w_in[j], SB_HEAD_DIM))
            o = _sb_attn(qkv)
            w_out = sb_w_out[j]
        rw, rb = _router_weights(router_group_w[i], router_group_b[i],
                                 router_expert_w[i], router_expert_b[i])
        x1, h2, idr, wts, cnt = _out_router(o, xf, mod, norm2_g[i].reshape(1, D_MODEL),
                                            w_out.astype(BF16), rw, rb)
        dest, bexp, nused = _routing_tables(idr, cnt)
        token_of = jnp.arange(N_TOK * TOP_K, dtype=jnp.int32) // TOP_K
        slot_tok = jnp.zeros((N_SLOTS,), jnp.int32).at[dest.reshape(-1)].set(token_of)
        xs = _sc_gather_rows(lax.bitcast_convert_type(h2, jnp.int32), slot_tok)
        ys = _experts(i, bexp, nused, lax.bitcast_convert_type(xs, jnp.uint32),
                      expert_w_gate, expert_w_up, expert_w_down)
        y2 = _sc_gather_rows(lax.bitcast_convert_type(ys, jnp.int32), dest.T.reshape(-1))
        xf = _mix(x1, wts, mod, final_norm_g.reshape(1, D_MODEL),
                  lax.bitcast_convert_type(y2, jnp.uint32), final=(i == DEPTH - 1))
    return xf.reshape(BATCH, SEQ, D_MODEL)
```

```python
import functools
import math

import jax
import jax.numpy as jnp
from jax import lax
from jax.experimental import pallas as pl
from jax.experimental.pallas import tpu as pltpu
from jax.experimental.pallas import tpu_sc as plsc

D_MODEL = 1024
BATCH = 8
SEQ = 2048
DEPTH = 2
N_TOK = BATCH * SEQ

CHUNK = 64
DIFF_HEADS = 8
DIFF_HEAD_DIM = D_MODEL // (2 * DIFF_HEADS)
SB_HEADS = 16
SB_HEAD_DIM = D_MODEL // SB_HEADS
N_GROUPS = 4
EXPERTS_PER_GROUP = 8
N_EXPERTS = N_GROUPS * EXPERTS_PER_GROUP
TOP_K = 2
EXPERT_HIDDEN = D_MODEL // 2
RMS_EPS = 1e-6
SUBLN_EPS = 1e-5

LANES = 128
EXPERT_LANE0 = 32
EXP_UNDERFLOW = -104.0

ADA_TN = 1536
QKV_TM = 512
DIFF_T = 512
SB_T = 256
OUT_TM = 512
DISP_TM = 1024
MOE_BLK = 256
N_SLOTS = N_TOK * TOP_K + N_EXPERTS * MOE_BLK
N_BLOCKS = N_SLOTS // MOE_BLK
COMB_TM = 512
DMA_UNROLL = 8
ROW_WORDS = D_MODEL // 2

SC_CORES = 2
SC_SUBCORES = 16
SC_CHUNK = 128

VMEM_LIMIT = 56 * 1024 * 1024

F32 = jnp.float32
BF16 = jnp.bfloat16


def _cparams(sem):
    return pltpu.CompilerParams(dimension_semantics=sem, vmem_limit_bytes=VMEM_LIMIT)


def _ada_kernel(c_ref, w_ref, b_ref, o_ref):
    c = c_ref[...]
    cond = c * jax.nn.sigmoid(c)
    o_ref[0] = jnp.dot(cond, w_ref[0], preferred_element_type=F32,
                       precision=lax.Precision.HIGHEST) + b_ref[0]


def _ada(c, ada_w, ada_b):
    six_d = ada_w.shape[-1]
    return pl.pallas_call(
        _ada_kernel,
        grid=(DEPTH, six_d // ADA_TN),
        in_specs=[
            pl.BlockSpec((BATCH, D_MODEL), lambda l, n: (0, 0)),
            pl.BlockSpec((1, D_MODEL, ADA_TN), lambda l, n: (l, 0, n)),
            pl.BlockSpec((1, 1, ADA_TN), lambda l, n: (l, 0, n)),
        ],
        out_specs=pl.BlockSpec((1, BATCH, ADA_TN), lambda l, n: (l, 0, n)),
        out_shape=jax.ShapeDtypeStruct((DEPTH, BATCH, six_d), F32),
        compiler_params=_cparams(("arbitrary", "arbitrary")),
        name="ada",
    )(c, ada_w, ada_b.reshape(DEPTH, 1, six_d))


def _modulated_norm(x, g, scale, shift):
    ms = jnp.mean(x * x, axis=-1, keepdims=True)
    return x * lax.rsqrt(ms + RMS_EPS) * (g * (1.0 + scale)) + shift


def _qkv_kernel(x_ref, mod_ref, g_ref, w_ref, o_ref):
    m = mod_ref[0]
    h = _modulated_norm(x_ref[...], g_ref[...], m[1:2], m[0:1]).astype(BF16)
    for n in range(3):
        cols = slice(n * D_MODEL, (n + 1) * D_MODEL)
        o_ref[:, cols] = jnp.dot(h, w_ref[:, cols], preferred_element_type=F32).astype(BF16)


def _qkv(x, mod, g, w):
    tiles_per_batch = SEQ // QKV_TM
    return pl.pallas_call(
        _qkv_kernel,
        grid=(N_TOK // QKV_TM,),
        in_specs=[
            pl.BlockSpec((QKV_TM, D_MODEL), lambda i: (i, 0)),
            pl.BlockSpec((1, 6, D_MODEL), lambda i: (i // tiles_per_batch, 0, 0)),
            pl.BlockSpec((1, D_MODEL), lambda i: (0, 0)),
            pl.BlockSpec((D_MODEL, 3 * D_MODEL), lambda i: (0, 0)),
        ],
        out_specs=pl.BlockSpec((QKV_TM, 3 * D_MODEL), lambda i: (i, 0)),
        out_shape=jax.ShapeDtypeStruct((N_TOK, 3 * D_MODEL), BF16),
        compiler_params=_cparams(("arbitrary",)),
        name="qkv",
    )(x, mod, g, w)


def _half_masked(q, upper):
    lane = lax.broadcasted_iota(jnp.int32, q.shape, 1)
    keep = (lane >= LANES // 2) if upper else (lane < LANES // 2)
    return jnp.where(keep, q, jnp.zeros_like(q))


def _pack_bf16_pairs(x):
    half = x.shape[1] // 2
    hi = lax.bitcast_convert_type(x[:, :half].astype(BF16).astype(F32), jnp.uint32)
    lo = lax.bitcast_convert_type(x[:, half:].astype(BF16).astype(F32), jnp.uint32)
    return hi | (lo >> 16)


def _unpack_bf16_pairs(u):
    hi = lax.bitcast_convert_type(u & jnp.uint32(0xFFFF0000), F32)
    lo = lax.bitcast_convert_type(u << 16, F32)
    return jnp.concatenate([hi, lo], axis=1)


def _lane_tile(x, n):
    return jnp.concatenate([x] * n, axis=1)


def _qk(q, k):
    return lax.dot_general(q, k, (((1,), (1,)), ((), ())), preferred_element_type=F32)


def _diff_attn_kernel(slopes_ref, lam_ref, q_ref, k_ref, v_ref, g_ref, o_ref,
                      m0_ref, m1_ref, acc0_ref, acc1_ref, *, lambda_init):
    slope = slopes_ref[pl.program_id(1)]
    m_refs = (m0_ref, m1_ref)
    acc_refs = (acc0_ref, acc1_ref)
    ones = jnp.ones((DIFF_T, LANES), BF16)

    row = lax.broadcasted_iota(jnp.int32, (DIFF_T, DIFF_T), 0)
    col = lax.broadcasted_iota(jnp.int32, (DIFF_T, DIFF_T), 1)
    rel = (row - col).astype(F32) * slope
    allowed = (col // CHUNK) <= (row // CHUNK)
    diag_bias = jnp.where(allowed, jnp.abs(rel), jnp.inf)
    out_gain = g_ref[...] * (1.0 - lambda_init)

    def query_tile(qi, _):
        q = q_ref[pl.ds(qi * DIFF_T, DIFF_T), :]
        qm = (_half_masked(q, False), _half_masked(q, True))

        k = k_ref[pl.ds(qi * DIFF_T, DIFF_T), :]
        v1 = jnp.concatenate([v_ref[pl.ds(qi * DIFF_T, DIFF_T), :], ones], axis=1)
        s_maps = [_qk(qm[mi], k) - diag_bias for mi in range(2)]
        for mi in range(2):
            m = jnp.max(s_maps[mi], axis=-1, keepdims=True)
            p = jnp.exp(s_maps[mi] - m)
            m_refs[mi][...] = jnp.broadcast_to(m, (DIFF_T, LANES))
            acc_refs[mi][...] = jnp.dot(p.astype(BF16), v1, preferred_element_type=F32)

        def body(j, _):
            k = k_ref[pl.ds(j * DIFF_T, DIFF_T), :]
            v1 = jnp.concatenate([v_ref[pl.ds(j * DIFF_T, DIFF_T), :], ones], axis=1)
            shift = slope * jnp.asarray((qi - j) * DIFF_T).astype(F32)
            s_maps = [_qk(qm[mi], k) - rel for mi in range(2)]
            for mi in range(2):
                m = m_refs[mi][...]
                m_new = jnp.maximum(m, jnp.max(s_maps[mi], axis=-1, keepdims=True) - shift)
                alpha = jnp.exp(m - m_new)
                p = jnp.exp(s_maps[mi] - _lane_tile(m_new + shift, DIFF_T // LANES))
                m_refs[mi][...] = m_new
                acc_refs[mi][...] = _lane_tile(alpha, 2) * acc_refs[mi][...] + jnp.dot(
                    p.astype(BF16), v1, preferred_element_type=F32)
            return 0

        lax.fori_loop(0, qi, body, 0)
        a0 = acc0_ref[...]
        a1 = acc1_ref[...]
        o = a0[:, :LANES] / a0[:, LANES:] - lam_ref[...] * (a1[:, :LANES] / a1[:, LANES:])
        ms = jnp.mean(o * o, axis=-1, keepdims=True)
        o = o * lax.rsqrt(ms + SUBLN_EPS) * out_gain
        o_ref[pl.ds(qi * DIFF_T, DIFF_T), :] = o.astype(BF16)
        return 0

    lax.fori_loop(0, SEQ // DIFF_T, query_tile, 0)


def _diff_attn(qkv, slopes, lam, subln_g, lambda_init):
    return pl.pallas_call(
        functools.partial(_diff_attn_kernel, lambda_init=lambda_init),
        grid=(BATCH, DIFF_HEADS),
        in_specs=[
            pl.BlockSpec(memory_space=pltpu.SMEM),
            pl.BlockSpec((1, 1), lambda b, h: (0, 0)),
            pl.BlockSpec((SEQ, LANES), lambda b, h: (b, h)),
            pl.BlockSpec((SEQ, LANES), lambda b, h: (b, DIFF_HEADS + h)),
            pl.BlockSpec((SEQ, LANES), lambda b, h: (b, 2 * DIFF_HEADS + h)),
            pl.BlockSpec((1, LANES), lambda b, h: (0, 0)),
        ],
        out_specs=pl.BlockSpec((SEQ, LANES), lambda b, h: (b, h)),
        out_shape=jax.ShapeDtypeStruct((N_TOK, D_MODEL), BF16),
        scratch_shapes=[
            pltpu.VMEM((DIFF_T, LANES), F32),
            pltpu.VMEM((DIFF_T, LANES), F32),
            pltpu.VMEM((DIFF_T, 2 * LANES), F32),
            pltpu.VMEM((DIFF_T, 2 * LANES), F32),
        ],
        compiler_params=_cparams(("arbitrary", "arbitrary")),
        name="diff_attn",
    )(slopes, lam, qkv, qkv, qkv, subln_g)


def _sb_attn_kernel(q_ref, k_ref, v_ref, o_ref, tail0_ref, tail1_ref, acc0_ref, acc1_ref):
    tail_refs = (tail0_ref, tail1_ref)
    acc_refs = (acc0_ref, acc1_ref)

    row = lax.broadcasted_iota(jnp.int32, (SB_T, SB_T), 0)
    col = lax.broadcasted_iota(jnp.int32, (SB_T, SB_T), 1)
    strict = col < row
    neg_from = jnp.where(row >= col, -1.0, 0.0).astype(BF16)

    def scores(j, qhead, mask):
        z = _qk(qhead, k_ref[pl.ds(j * SB_T, SB_T), :])
        sp = jnp.maximum(z, 0.0) + jnp.log(1.0 + jnp.exp(-jnp.abs(z)))
        if mask is not None:
            sp = jnp.where(mask, sp, 0.0)
        return z, sp

    def weighted(j, z, sp, tail, mask):
        log_a = (z + jnp.dot(sp.astype(BF16), neg_from, preferred_element_type=F32)
                 + _lane_tile(tail, SB_T // LANES))
        a = jnp.exp(log_a)
        if mask is not None:
            a = jnp.where(mask, a, 0.0)
        return jnp.dot(a.astype(BF16), v_ref[pl.ds(j * SB_T, SB_T), :],
                       preferred_element_type=F32)

    def row_sum(sp):
        return jnp.broadcast_to(jnp.sum(sp, axis=-1, keepdims=True), (SB_T, LANES))

    zero = jnp.zeros((SB_T, LANES), F32)
    lane = lax.broadcasted_iota(jnp.int32, (SB_T, LANES), 1)

    def query_heads(qi):
        q = q_ref[pl.ds(qi * SB_T, SB_T), :]
        return (_half_masked(q, False), _half_masked(q, True))

    def finish(qi):
        o_ref[pl.ds(qi * SB_T, SB_T), :] = jnp.where(
            lane < LANES // 2, acc0_ref[...], acc1_ref[...]).astype(BF16)

    qh = query_heads(0)
    for hh in range(2):
        z, sp = scores(0, qh[hh], strict)
        acc_refs[hh][...] = weighted(0, z, sp, zero, strict)
    finish(0)

    def query_tile(qi, _):
        qh = query_heads(qi)
        zs = [(scores(qi, qh[hh], strict), scores(qi - 1, qh[hh], None)) for hh in range(2)]
        for hh in range(2):
            (z0, sp0), (z1, sp1) = zs[hh]
            tail0 = -row_sum(sp0)
            acc_refs[hh][...] = (weighted(qi, z0, sp0, zero, strict)
                                 + weighted(qi - 1, z1, sp1, tail0, None))
            tail_refs[hh][...] = tail0 - row_sum(sp1)

        def cond(state):
            j, live = state
            return (j >= 0) & (live > EXP_UNDERFLOW)

        def body(state):
            j, _ = state
            live = jnp.float32(-jnp.inf)
            for hh in range(2):
                z, sp = scores(j, qh[hh], None)
                tail = tail_refs[hh][...]
                acc_refs[hh][...] += weighted(j, z, sp, tail, None)
                tail = tail - row_sum(sp)
                tail_refs[hh][...] = tail
                live = jnp.maximum(live, jnp.max(tail))
            return j - 1, live

        live = jnp.maximum(jnp.max(tail0_ref[...]), jnp.max(tail1_ref[...]))
        lax.while_loop(cond, body, (qi - 2, live))
        finish(qi)
        return 0

    lax.fori_loop(1, SEQ // SB_T, query_tile, 0)


def _sb_attn(qkv):
    pairs = SB_HEADS // 2
    return pl.pallas_call(
        _sb_attn_kernel,
        grid=(BATCH, pairs),
        in_specs=[
            pl.BlockSpec((SEQ, LANES), lambda b, h: (b, h)),
            pl.BlockSpec((SEQ, LANES), lambda b, h: (b, pairs + h)),
            pl.BlockSpec((SEQ, LANES), lambda b, h: (b, 2 * pairs + h)),
        ],
        out_specs=pl.BlockSpec((SEQ, LANES), lambda b, h: (b, h)),
        out_shape=jax.ShapeDtypeStruct((N_TOK, D_MODEL), BF16),
        scratch_shapes=[
            pltpu.VMEM((SB_T, LANES), F32),
            pltpu.VMEM((SB_T, LANES), F32),
            pltpu.VMEM((SB_T, LANES), F32),
            pltpu.VMEM((SB_T, LANES), F32),
        ],
        compiler_params=_cparams(("arbitrary", "arbitrary")),
        name="sb_attn",
    )(qkv, qkv, qkv)


def _out_router_kernel(o_ref, x_ref, mod_ref, g_ref, w_ref, rw_ref, rb_ref,
                       x1_ref, h2_ref, idr_ref, wts_ref, cnt_ref, tri_ref, base_ref):
    i = pl.program_id(0)

    @pl.when(i == 0)
    def _():
        r = lax.broadcasted_iota(jnp.int32, (OUT_TM, OUT_TM), 0)
        c = lax.broadcasted_iota(jnp.int32, (OUT_TM, OUT_TM), 1)
        tri_ref[...] = jnp.where(c < r, 1.0, 0.0).astype(BF16)
        base_ref[...] = jnp.zeros_like(base_ref)

    m = mod_ref[0]
    y = jnp.dot(o_ref[...], w_ref[...], preferred_element_type=F32)
    x1 = x_ref[...] + m[2:3] * y
    x1_ref[...] = x1
    h2 = _modulated_norm(x1, g_ref[...], m[4:5], m[3:4])
    h2_ref[...] = _pack_bf16_pairs(h2)

    logits = jnp.dot(h2.astype(BF16), rw_ref[...], preferred_element_type=F32) + rb_ref[...]
    lane = lax.broadcasted_iota(jnp.int32, logits.shape, 1).astype(F32)
    neg_inf = jnp.float32(-jnp.inf)
    big = jnp.float32(1e9)

    is_group = lane < N_GROUPS
    gl = jnp.where(is_group, logits, neg_inf)
    gmax = jnp.max(gl, axis=-1, keepdims=True)
    gidx = jnp.min(jnp.where(gl == gmax, lane, big), axis=-1, keepdims=True)
    gsum = jnp.sum(jnp.where(is_group, jnp.exp(logits - gmax), 0.0), axis=-1, keepdims=True)
    g_w = 1.0 / gsum

    lo = EXPERT_LANE0 + EXPERTS_PER_GROUP * gidx
    in_group = (lane >= lo) & (lane < lo + EXPERTS_PER_GROUP)
    el = jnp.where(in_group, logits, neg_inf)
    v0 = jnp.max(el, axis=-1, keepdims=True)
    i0 = jnp.min(jnp.where(el == v0, lane, big), axis=-1, keepdims=True)
    el = jnp.where(lane == i0, neg_inf, el)
    v1 = jnp.max(el, axis=-1, keepdims=True)
    i1 = jnp.min(jnp.where(el == v1, lane, big), axis=-1, keepdims=True)
    t = jnp.exp(v1 - v0)
    w0 = g_w / (1.0 + t)
    w1 = g_w * t / (1.0 + t)

    oh0 = jnp.where(lane == i0, 1.0, 0.0)
    oh1 = jnp.where(lane == i1, 1.0, 0.0)
    both = oh0 + oh1
    before = jnp.dot(tri_ref[...], both.astype(BF16), preferred_element_type=F32) + base_ref[...]
    r0 = jnp.sum(before * oh0, axis=-1, keepdims=True)
    r1 = jnp.sum(before * oh1, axis=-1, keepdims=True)
    base_ref[...] = base_ref[...] + jnp.sum(both, axis=0, keepdims=True)
    cnt_ref[...] = base_ref[...]

    e0 = i0 - EXPERT_LANE0
    e1 = i1 - EXPERT_LANE0
    idr = jnp.where(lane == 0, e0, jnp.where(lane == 1, e1, jnp.where(lane == 2, r0, r1)))
    idr_ref[...] = idr[:, :4].astype(jnp.int32)
    wts_ref[...] = jnp.where(lane == 0, w0, w1)[:, :2]


def _out_router(o, x, mod, g, w, rw, rb):
    tiles_per_batch = SEQ // OUT_TM
    return pl.pallas_call(
        _out_router_kernel,
        grid=(N_TOK // OUT_TM,),
        in_specs=[
            pl.BlockSpec((OUT_TM, D_MODEL), lambda i: (i, 0)),
            pl.BlockSpec((OUT_TM, D_MODEL), lambda i: (i, 0)),
            pl.BlockSpec((1, 6, D_MODEL), lambda i: (i // tiles_per_batch, 0, 0)),
            pl.BlockSpec((1, D_MODEL), lambda i: (0, 0)),
            pl.BlockSpec((D_MODEL, D_MODEL), lambda i: (0, 0)),
            pl.BlockSpec((D_MODEL, LANES), lambda i: (0, 0)),
            pl.BlockSpec((1, LANES), lambda i: (0, 0)),
        ],
        out_specs=[
            pl.BlockSpec((OUT_TM, D_MODEL), lambda i: (i, 0)),
            pl.BlockSpec((OUT_TM, ROW_WORDS), lambda i: (i, 0)),
            pl.BlockSpec((OUT_TM, 4), lambda i: (i, 0)),
            pl.BlockSpec((OUT_TM, 2), lambda i: (i, 0)),
            pl.BlockSpec((1, LANES), lambda i: (0, 0)),
        ],
        out_shape=[
            jax.ShapeDtypeStruct((N_TOK, D_MODEL), F32),
            jax.ShapeDtypeStruct((N_TOK, ROW_WORDS), jnp.uint32),
            jax.ShapeDtypeStruct((N_TOK, 4), jnp.int32),
            jax.ShapeDtypeStruct((N_TOK, 2), F32),
            jax.ShapeDtypeStruct((1, LANES), F32),
        ],
        scratch_shapes=[
            pltpu.VMEM((OUT_TM, OUT_TM), BF16),
            pltpu.VMEM((1, LANES), F32),
        ],
        compiler_params=_cparams(("arbitrary",)),
        name="out_router",
    )(o, x, mod, g, w, rw, rb)


def _dispatch_kernel(dest_ref, pad_end_ref, padded_ref, h_ref, xs_ref, zeros_ref, sem, zsem):
    @pl.when(pl.program_id(0) == 0)
    def _():
        zeros_ref[...] = jnp.zeros_like(zeros_ref)

        def tail_copy(e):
            start = pl.multiple_of(jnp.maximum(pad_end_ref[e] - MOE_BLK, 0), MOE_BLK)
            return pltpu.make_async_copy(zeros_ref, xs_ref.at[pl.ds(start, MOE_BLK)], zsem)

        def fill(e, _):
            @pl.when(padded_ref[e] > 0)
            def _():
                tail_copy(e).start()
            return 0

        def fill_done(e, _):
            @pl.when(padded_ref[e] > 0)
            def _():
                tail_copy(e).wait()
            return 0

        def block_copy(b):
            start = pl.multiple_of(b * MOE_BLK, MOE_BLK)
            return pltpu.make_async_copy(zeros_ref, xs_ref.at[pl.ds(start, MOE_BLK)], zsem)

        def fill_unused(b, _):
            block_copy(b).start()
            return 0

        def fill_unused_done(b, _):
            block_copy(b).wait()
            return 0

        first_unused = pad_end_ref[N_EXPERTS - 1] // MOE_BLK
        lax.fori_loop(0, N_EXPERTS, fill, 0)
        lax.fori_loop(first_unused, N_BLOCKS, fill_unused, 0)
        lax.fori_loop(0, N_EXPERTS, fill_done, 0)
        lax.fori_loop(first_unused, N_BLOCKS, fill_unused_done, 0)

    def row_copy(r, d):
        return pltpu.make_async_copy(h_ref.at[pl.ds(r, 1)], xs_ref.at[pl.ds(d, 1)], sem)

    def issue(r, _):
        for k in range(TOP_K):
            row_copy(r, dest_ref[0, 0, TOP_K * r + k]).start()
        return 0

    def drain(r, _):
        for k in range(TOP_K):
            row_copy(0, 0).wait()
        return 0

    lax.fori_loop(0, DISP_TM, issue, 0, unroll=DMA_UNROLL)
    lax.fori_loop(0, DISP_TM, drain, 0, unroll=DMA_UNROLL)


def _dispatch(dest, pad_ends, padded, h2):
    nt = N_TOK // DISP_TM
    return pl.pallas_call(
        _dispatch_kernel,
        grid=(nt,),
        in_specs=[
            pl.BlockSpec((1, 1, TOP_K * DISP_TM), lambda i: (i, 0, 0), memory_space=pltpu.SMEM),
            pl.BlockSpec(memory_space=pltpu.SMEM),
            pl.BlockSpec(memory_space=pltpu.SMEM),
            pl.BlockSpec((DISP_TM, ROW_WORDS), lambda i: (i, 0)),
        ],
        out_specs=pl.BlockSpec(memory_space=pl.ANY),
        out_shape=jax.ShapeDtypeStruct((N_SLOTS, ROW_WORDS), jnp.uint32),
        scratch_shapes=[
            pltpu.VMEM((MOE_BLK, ROW_WORDS), jnp.uint32),
            pltpu.SemaphoreType.DMA(()),
            pltpu.SemaphoreType.DMA(()),
        ],
        compiler_params=_cparams(("arbitrary",)),
        name="dispatch",
    )(dest.reshape(nt, 1, TOP_K * DISP_TM), pad_ends, padded, h2)


def _expert_kernel(bexp_ref, nused_ref, xs_ref, wg_ref, wu_ref, wd_ref, ys_ref,
                   wg_bf, wu_bf, wd_bf):
    i = pl.program_id(0)

    @pl.when(i < nused_ref[0])
    def _():
        e = bexp_ref[i]
        prev = bexp_ref[jnp.maximum(i - 1, 0)]

        @pl.when((i == 0) | (e != prev))
        def _():
            wg_bf[...] = wg_ref[0, 0].astype(BF16)
            wu_bf[...] = wu_ref[0, 0].astype(BF16)
            wd_bf[...] = wd_ref[0, 0].astype(BF16)

        x = _unpack_bf16_pairs(xs_ref[...]).astype(BF16)
        g = jnp.dot(x, wg_bf[...], preferred_element_type=F32)
        u = jnp.dot(x, wu_bf[...], preferred_element_type=F32)
        hid = (g * jax.nn.sigmoid(g)) * u
        y = jnp.dot(hid.astype(BF16), wd_bf[...], preferred_element_type=F32)
        ys_ref[...] = lax.bitcast_convert_type(_pack_bf16_pairs(y), jnp.int32)

    @pl.when(i >= nused_ref[0])
    def _():
        ys_ref[...] = jnp.zeros_like(ys_ref)


def _experts(layer, bexp, nused, xs, wg, wu, wd):
    def in_row_map(i, bexp, nused):
        return (jnp.minimum(i, nused[0] - 1), 0)

    def row_map(i, bexp, nused):
        return (i, 0)

    def w_map(i, bexp, nused):
        return (layer, bexp[jnp.minimum(i, nused[0] - 1)], 0, 0)

    return pl.pallas_call(
        _expert_kernel,
        grid_spec=pltpu.PrefetchScalarGridSpec(
            num_scalar_prefetch=2,
            grid=(N_BLOCKS,),
            in_specs=[
                pl.BlockSpec((MOE_BLK, ROW_WORDS), in_row_map),
                pl.BlockSpec((1, 1, D_MODEL, EXPERT_HIDDEN), w_map),
                pl.BlockSpec((1, 1, D_MODEL, EXPERT_HIDDEN), w_map),
                pl.BlockSpec((1, 1, EXPERT_HIDDEN, D_MODEL), w_map),
            ],
            out_specs=pl.BlockSpec((MOE_BLK, ROW_WORDS), row_map),
            scratch_shapes=[
                pltpu.VMEM((D_MODEL, EXPERT_HIDDEN), BF16),
                pltpu.VMEM((D_MODEL, EXPERT_HIDDEN), BF16),
                pltpu.VMEM((EXPERT_HIDDEN, D_MODEL), BF16),
            ],
        ),
        out_shape=jax.ShapeDtypeStruct((N_SLOTS, ROW_WORDS), jnp.int32),
        compiler_params=_cparams(("arbitrary",)),
        name="experts",
    )(bexp, nused, xs, wg, wu, wd)


def _sc_gather_rows(table, idx):
    n_rows = idx.shape[0]
    width = table.shape[1]
    workers = SC_CORES * SC_SUBCORES
    per_worker = n_rows // workers
    assert per_worker * workers == n_rows and per_worker % SC_CHUNK == 0
    mesh = plsc.VectorSubcoreMesh(core_axis_name="c", subcore_axis_name="s")

    def body(table_hbm, idx_hbm, out_hbm, idx_v, rows_v, sem):
        wid = lax.axis_index("s") * SC_CORES + lax.axis_index("c")
        base = wid * per_worker

        def chunk(c, _):
            off = pl.multiple_of(base + c * SC_CHUNK, SC_CHUNK)
            pltpu.sync_copy(idx_hbm.at[pl.ds(off, SC_CHUNK)], idx_v)
            pltpu.async_copy(table_hbm.at[idx_v], rows_v, sem).wait()
            pltpu.sync_copy(rows_v, out_hbm.at[pl.ds(off, SC_CHUNK)])
            return 0

        lax.fori_loop(0, per_worker // SC_CHUNK, chunk, 0)

    return pl.kernel(
        body,
        out_type=jax.ShapeDtypeStruct((n_rows, width), table.dtype),
        mesh=mesh,
        scratch_types=[
            pltpu.VMEM((SC_CHUNK,), jnp.int32),
            pltpu.VMEM((SC_CHUNK, width), table.dtype),
            pltpu.SemaphoreType.DMA,
        ],
        name="sc_gather_rows",
    )(table, idx)


def _mix_kernel(x_ref, wts_ref, mod_ref, fg_ref, y0_ref, y1_ref, o_ref, *, final):
    w = wts_ref[...]
    y0 = _unpack_bf16_pairs(lax.bitcast_convert_type(y0_ref[...], jnp.uint32))
    y1 = _unpack_bf16_pairs(lax.bitcast_convert_type(y1_ref[...], jnp.uint32))
    y = w[:, 0:1] * y0 + w[:, 1:2] * y1
    out = x_ref[...] + mod_ref[0][5:6] * y
    if final:
        ms = jnp.mean(out * out, axis=-1, keepdims=True)
        out = out * lax.rsqrt(ms + RMS_EPS) * fg_ref[...]
    o_ref[...] = out


def _mix(x1, wts, mod, fg, y2, final):
    nt = N_TOK // COMB_TM
    tiles_per_batch = SEQ // COMB_TM
    return pl.pallas_call(
        functools.partial(_mix_kernel, final=final),
        grid=(nt,),
        in_specs=[
            pl.BlockSpec((COMB_TM, D_MODEL), lambda i: (i, 0)),
            pl.BlockSpec((COMB_TM, TOP_K), lambda i: (i, 0)),
            pl.BlockSpec((1, 6, D_MODEL), lambda i: (i // tiles_per_batch, 0, 0)),
            pl.BlockSpec((1, D_MODEL), lambda i: (0, 0)),
            pl.BlockSpec((COMB_TM, ROW_WORDS), lambda i: (i, 0)),
            pl.BlockSpec((COMB_TM, ROW_WORDS), lambda i: (nt + i, 0)),
        ],
        out_specs=pl.BlockSpec((COMB_TM, D_MODEL), lambda i: (i, 0)),
        out_shape=jax.ShapeDtypeStruct((N_TOK, D_MODEL), F32),
        compiler_params=_cparams(("arbitrary",)),
        name="mix",
    )(x1, wts, mod, fg, y2, y2)


def _routing_tables(idr, cnt):
    counts = cnt[0, EXPERT_LANE0:EXPERT_LANE0 + N_EXPERTS].astype(jnp.int32)
    padded = (counts + MOE_BLK - 1) // MOE_BLK * MOE_BLK
    pad_ends = jnp.cumsum(padded)
    pad_starts = pad_ends - padded
    experts = jnp.arange(N_EXPERTS, dtype=jnp.int32)
    is_expert = idr[:, 0:2, None] == experts
    dest = jnp.sum(jnp.where(is_expert, pad_starts, 0), axis=-1) + idr[:, 2:4]
    block_start = jnp.arange(N_BLOCKS, dtype=jnp.int32) * MOE_BLK
    bexp = jnp.minimum(jnp.sum(block_start[:, None] >= pad_ends[None, :], axis=1),
                       N_EXPERTS - 1).astype(jnp.int32)
    nused = (pad_ends[-1:] // MOE_BLK).astype(jnp.int32)
    return (dest.astype(jnp.int32), bexp, nused,
            pad_ends.astype(jnp.int32), padded.astype(jnp.int32))


def _router_weights(w_group, b_group, w_expert, b_expert):
    rw = jnp.zeros((D_MODEL, LANES), F32)
    rw = rw.at[:, :N_GROUPS].set(w_group)
    rw = rw.at[:, EXPERT_LANE0:EXPERT_LANE0 + N_EXPERTS].set(w_expert)
    rb = jnp.zeros((1, LANES), F32)
    rb = rb.at[0, :N_GROUPS].set(b_group)
    rb = rb.at[0, EXPERT_LANE0:EXPERT_LANE0 + N_EXPERTS].set(b_expert.reshape(-1))
    return rw.astype(BF16), rb


def _scaled_qkv_weight(w_in, head_dim):
    scale = jnp.concatenate([jnp.full((D_MODEL,), 1.0 / math.sqrt(head_dim), F32),
                             jnp.ones((2 * D_MODEL,), F32)])
    return (w_in * scale).astype(BF16)


def kernel(x, c, norm1_g, norm2_g, ada_w, ada_b, diff_w_in, diff_w_out, diff_lambda_q1, diff_lambda_k1, diff_lambda_q2, diff_lambda_k2, diff_subln_g, sb_w_in, sb_w_out, router_group_w, router_group_b, router_expert_w, router_expert_b, expert_w_gate, expert_w_up, expert_w_down, final_norm_g):
    xf = x.reshape(N_TOK, D_MODEL)
    mod_all = _ada(c, ada_w, ada_b)
    slopes = jnp.exp2(-8.0 * jnp.arange(1, DIFF_HEADS + 1, dtype=F32) / DIFF_HEADS)

    for i in range(DEPTH):
        mod = mod_all[i].reshape(BATCH, 6, D_MODEL)
        j = i // 2
        if i % 2 == 0:
            lambda_init = 0.8 - 0.6 * math.exp(-0.3 * i)
            lam = (jnp.exp(jnp.sum(diff_lambda_q1[j] * diff_lambda_k1[j]))
                   - jnp.exp(jnp.sum(diff_lambda_q2[j] * diff_lambda_k2[j]))
                   + lambda_init).reshape(1, 1)
            qkv = _qkv(xf, mod, norm1_g[i].reshape(1, D_MODEL),
                       _scaled_qkv_weight(diff_w_in[j], DIFF_HEAD_DIM))
            o = _diff_attn(qkv, slopes, lam, diff_subln_g[j].reshape(1, LANES), lambda_init)
            w_out = diff_w_out[j]
        else:
            qkv = _qkv(xf, mod, norm1_g[i].reshape(1, D_MODEL),
                       _scaled_qkv_weight(sb_w_in[j], SB_HEAD_DIM))
            o = _sb_attn(qkv)
            w_out = sb_w_out[j]
        rw, rb = _router_weights(router_group_w[i], router_group_b[i],
                                 router_expert_w[i], router_expert_b[i])
        x1, h2, idr, wts, cnt = _out_router(o, xf, mod, norm2_g[i].reshape(1, D_MODEL),
                                            w_out.astype(BF16), rw, rb)
        dest, bexp, nused, pad_ends, padded = _routing_tables(idr, cnt)
        xs = _dispatch(dest.reshape(-1), pad_ends, padded, h2)
        ys = _experts(i, bexp, nused, xs, expert_w_gate, expert_w_up, expert_w_down)
        y2 = _sc_gather_rows(ys, dest.T.reshape(-1))
        xf = _mix(x1, wts, mod, final_norm_g.reshape(1, D_MODEL), y2,
                  final=(i == DEPTH - 1))
    return xf.reshape(BATCH, SEQ, D_MODEL)
```

```python
import functools
import math

import jax
import jax.numpy as jnp
from jax import lax
from jax.experimental import pallas as pl
from jax.experimental.pallas import tpu as pltpu
from jax.experimental.pallas import tpu_sc as plsc

D_MODEL = 1024
BATCH = 8
SEQ = 2048
DEPTH = 2
N_TOK = BATCH * SEQ

CHUNK = 64
DIFF_HEADS = 8
DIFF_HEAD_DIM = D_MODEL // (2 * DIFF_HEADS)
SB_HEADS = 16
SB_HEAD_DIM = D_MODEL // SB_HEADS
N_GROUPS = 4
EXPERTS_PER_GROUP = 8
N_EXPERTS = N_GROUPS * EXPERTS_PER_GROUP
TOP_K = 2
EXPERT_HIDDEN = D_MODEL // 2
RMS_EPS = 1e-6
SUBLN_EPS = 1e-5

LANES = 128
EXPERT_LANE0 = 32
EXP_UNDERFLOW = -104.0

ADA_TN = 1536
QKV_TM = 512
DIFF_T = 512
SB_T = 256
OUT_TM = 512
DISP_TM = 1024
MOE_BLK = 256
N_SLOTS = N_TOK * TOP_K + N_EXPERTS * MOE_BLK
N_BLOCKS = N_SLOTS // MOE_BLK
COMB_TM = 512
DMA_UNROLL = 8
ROW_WORDS = D_MODEL // 2

SC_CORES = 2
SC_SUBCORES = 16
SC_LANES = 16
SC_CHUNK = 128

VMEM_LIMIT = 56 * 1024 * 1024

F32 = jnp.float32
BF16 = jnp.bfloat16


def _cparams(sem):
    return pltpu.CompilerParams(dimension_semantics=sem, vmem_limit_bytes=VMEM_LIMIT)


def _ada_kernel(c_ref, w_ref, b_ref, o_ref):
    c = c_ref[...]
    cond = c * jax.nn.sigmoid(c)
    o_ref[0] = jnp.dot(cond, w_ref[0], preferred_element_type=F32,
                       precision=lax.Precision.HIGHEST) + b_ref[0]


def _ada(c, ada_w, ada_b):
    six_d = ada_w.shape[-1]
    return pl.pallas_call(
        _ada_kernel,
        grid=(DEPTH, six_d // ADA_TN),
        in_specs=[
            pl.BlockSpec((BATCH, D_MODEL), lambda l, n: (0, 0)),
            pl.BlockSpec((1, D_MODEL, ADA_TN), lambda l, n: (l, 0, n)),
            pl.BlockSpec((1, 1, ADA_TN), lambda l, n: (l, 0, n)),
        ],
        out_specs=pl.BlockSpec((1, BATCH, ADA_TN), lambda l, n: (l, 0, n)),
        out_shape=jax.ShapeDtypeStruct((DEPTH, BATCH, six_d), F32),
        compiler_params=_cparams(("arbitrary", "arbitrary")),
        name="ada",
    )(c, ada_w, ada_b.reshape(DEPTH, 1, six_d))


def _modulated_norm(x, g, scale, shift):
    ms = jnp.mean(x * x, axis=-1, keepdims=True)
    return x * lax.rsqrt(ms + RMS_EPS) * (g * (1.0 + scale)) + shift


def _qkv_kernel(x_ref, mod_ref, g_ref, w_ref, o_ref):
    m = mod_ref[0]
    h = _modulated_norm(x_ref[...], g_ref[...], m[1:2], m[0:1]).astype(BF16)
    for n in range(3):
        cols = slice(n * D_MODEL, (n + 1) * D_MODEL)
        o_ref[:, cols] = jnp.dot(h, w_ref[:, cols], preferred_element_type=F32).astype(BF16)


def _qkv(x, mod, g, w):
    tiles_per_batch = SEQ // QKV_TM
    return pl.pallas_call(
        _qkv_kernel,
        grid=(N_TOK // QKV_TM,),
        in_specs=[
            pl.BlockSpec((QKV_TM, D_MODEL), lambda i: (i, 0)),
            pl.BlockSpec((1, 6, D_MODEL), lambda i: (i // tiles_per_batch, 0, 0)),
            pl.BlockSpec((1, D_MODEL), lambda i: (0, 0)),
            pl.BlockSpec((D_MODEL, 3 * D_MODEL), lambda i: (0, 0)),
        ],
        out_specs=pl.BlockSpec((QKV_TM, 3 * D_MODEL), lambda i: (i, 0)),
        out_shape=jax.ShapeDtypeStruct((N_TOK, 3 * D_MODEL), BF16),
        compiler_params=_cparams(("arbitrary",)),
        name="qkv",
    )(x, mod, g, w)


def _half_masked(q, upper):
    lane = lax.broadcasted_iota(jnp.int32, q.shape, 1)
    keep = (lane >= LANES // 2) if upper else (lane < LANES // 2)
    return jnp.where(keep, q, jnp.zeros_like(q))


def _pack_bf16_pairs(x):
    half = x.shape[1] // 2
    hi = lax.bitcast_convert_type(x[:, :half].astype(BF16).astype(F32), jnp.uint32)
    lo = lax.bitcast_convert_type(x[:, half:].astype(BF16).astype(F32), jnp.uint32)
    return hi | (lo >> 16)


def _unpack_bf16_pairs(u):
    hi = lax.bitcast_convert_type(u & jnp.uint32(0xFFFF0000), F32)
    lo = lax.bitcast_convert_type(u << 16, F32)
    return jnp.concatenate([hi, lo], axis=1)


def _lane_tile(x, n):
    return jnp.concatenate([x] * n, axis=1)


def _qk(q, k):
    return lax.dot_general(q, k, (((1,), (1,)), ((), ())), preferred_element_type=F32)


def _diff_attn_kernel(slopes_ref, lam_ref, q_ref, k_ref, v_ref, g_ref, o_ref,
                      m0_ref, m1_ref, acc0_ref, acc1_ref, *, lambda_init):
    slope = slopes_ref[pl.program_id(1)]
    m_refs = (m0_ref, m1_ref)
    acc_refs = (acc0_ref, acc1_ref)
    ones = jnp.ones((DIFF_T, LANES), BF16)

    row = lax.broadcasted_iota(jnp.int32, (DIFF_T, DIFF_T), 0)
    col = lax.broadcasted_iota(jnp.int32, (DIFF_T, DIFF_T), 1)
    rel = (row - col).astype(F32) * slope
    allowed = (col // CHUNK) <= (row // CHUNK)
    diag_bias = jnp.where(allowed, jnp.abs(rel), jnp.inf)
    out_gain = g_ref[...] * (1.0 - lambda_init)

    def query_tile(qi, _):
        q = q_ref[pl.ds(qi * DIFF_T, DIFF_T), :]
        qm = (_half_masked(q, False), _half_masked(q, True))

        k = k_ref[pl.ds(qi * DIFF_T, DIFF_T), :]
        v1 = jnp.concatenate([v_ref[pl.ds(qi * DIFF_T, DIFF_T), :], ones], axis=1)
        s_maps = [_qk(qm[mi], k) - diag_bias for mi in range(2)]
        for mi in range(2):
            m = jnp.max(s_maps[mi], axis=-1, keepdims=True)
            p = jnp.exp(s_maps[mi] - m)
            m_refs[mi][...] = jnp.broadcast_to(m, (DIFF_T, LANES))
            acc_refs[mi][...] = jnp.dot(p.astype(BF16), v1, preferred_element_type=F32)

        def body(j, _):
            k = k_ref[pl.ds(j * DIFF_T, DIFF_T), :]
            v1 = jnp.concatenate([v_ref[pl.ds(j * DIFF_T, DIFF_T), :], ones], axis=1)
            shift = slope * jnp.asarray((qi - j) * DIFF_T).astype(F32)
            s_maps = [_qk(qm[mi], k) - rel for mi in range(2)]
            for mi in range(2):
                m = m_refs[mi][...]
                m_new = jnp.maximum(m, jnp.max(s_maps[mi], axis=-1, keepdims=True) - shift)
                alpha = jnp.exp(m - m_new)
                p = jnp.exp(s_maps[mi] - _lane_tile(m_new + shift, DIFF_T // LANES))
                m_refs[mi][...] = m_new
                acc_refs[mi][...] = _lane_tile(alpha, 2) * acc_refs[mi][...] + jnp.dot(
                    p.astype(BF16), v1, preferred_element_type=F32)
            return 0

        lax.fori_loop(0, qi, body, 0)
        a0 = acc0_ref[...]
        a1 = acc1_ref[...]
        o = a0[:, :LANES] / a0[:, LANES:] - lam_ref[...] * (a1[:, :LANES] / a1[:, LANES:])
        ms = jnp.mean(o * o, axis=-1, keepdims=True)
        o = o * lax.rsqrt(ms + SUBLN_EPS) * out_gain
        o_ref[pl.ds(qi * DIFF_T, DIFF_T), :] = o.astype(BF16)
        return 0

    lax.fori_loop(0, SEQ // DIFF_T, query_tile, 0)


def _diff_attn(qkv, slopes, lam, subln_g, lambda_init):
    return pl.pallas_call(
        functools.partial(_diff_attn_kernel, lambda_init=lambda_init),
        grid=(BATCH, DIFF_HEADS),
        in_specs=[
            pl.BlockSpec(memory_space=pltpu.SMEM),
            pl.BlockSpec((1, 1), lambda b, h: (0, 0)),
            pl.BlockSpec((SEQ, LANES), lambda b, h: (b, h)),
            pl.BlockSpec((SEQ, LANES), lambda b, h: (b, DIFF_HEADS + h)),
            pl.BlockSpec((SEQ, LANES), lambda b, h: (b, 2 * DIFF_HEADS + h)),
            pl.BlockSpec((1, LANES), lambda b, h: (0, 0)),
        ],
        out_specs=pl.BlockSpec((SEQ, LANES), lambda b, h: (b, h)),
        out_shape=jax.ShapeDtypeStruct((N_TOK, D_MODEL), BF16),
        scratch_shapes=[
            pltpu.VMEM((DIFF_T, LANES), F32),
            pltpu.VMEM((DIFF_T, LANES), F32),
            pltpu.VMEM((DIFF_T, 2 * LANES), F32),
            pltpu.VMEM((DIFF_T, 2 * LANES), F32),
        ],
        compiler_params=_cparams(("arbitrary", "arbitrary")),
        name="diff_attn",
    )(slopes, lam, qkv, qkv, qkv, subln_g)


def _sb_attn_kernel(q_ref, k_ref, v_ref, o_ref, tail0_ref, tail1_ref, acc0_ref, acc1_ref):
    tail_refs = (tail0_ref, tail1_ref)
    acc_refs = (acc0_ref, acc1_ref)

    row = lax.broadcasted_iota(jnp.int32, (SB_T, SB_T), 0)
    col = lax.broadcasted_iota(jnp.int32, (SB_T, SB_T), 1)
    strict = col < row
    neg_from = jnp.where(row >= col, -1.0, 0.0).astype(BF16)

    def scores(j, qhead, mask):
        z = _qk(qhead, k_ref[pl.ds(j * SB_T, SB_T), :])
        sp = jnp.maximum(z, 0.0) + jnp.log(1.0 + jnp.exp(-jnp.abs(z)))
        if mask is not None:
            sp = jnp.where(mask, sp, 0.0)
        return z, sp

    def weighted(j, z, sp, tail, mask):
        log_a = (z + jnp.dot(sp.astype(BF16), neg_from, preferred_element_type=F32)
                 + _lane_tile(tail, SB_T // LANES))
        a = jnp.exp(log_a)
        if mask is not None:
            a = jnp.where(mask, a, 0.0)
        return jnp.dot(a.astype(BF16), v_ref[pl.ds(j * SB_T, SB_T), :],
                       preferred_element_type=F32)

    def row_sum(sp):
        return jnp.broadcast_to(jnp.sum(sp, axis=-1, keepdims=True), (SB_T, LANES))

    zero = jnp.zeros((SB_T, LANES), F32)
    lane = lax.broadcasted_iota(jnp.int32, (SB_T, LANES), 1)

    def query_heads(qi):
        q = q_ref[pl.ds(qi * SB_T, SB_T), :]
        return (_half_masked(q, False), _half_masked(q, True))

    def finish(qi):
        o_ref[pl.ds(qi * SB_T, SB_T), :] = jnp.where(
            lane < LANES // 2, acc0_ref[...], acc1_ref[...]).astype(BF16)

    qh = query_heads(0)
    for hh in range(2):
        z, sp = scores(0, qh[hh], strict)
        acc_refs[hh][...] = weighted(0, z, sp, zero, strict)
    finish(0)

    def query_tile(qi, _):
        qh = query_heads(qi)
        zs = [(scores(qi, qh[hh], strict), scores(qi - 1, qh[hh], None)) for hh in range(2)]
        for hh in range(2):
            (z0, sp0), (z1, sp1) = zs[hh]
            tail0 = -row_sum(sp0)
            acc_refs[hh][...] = (weighted(qi, z0, sp0, zero, strict)
                                 + weighted(qi - 1, z1, sp1, tail0, None))
            tail_refs[hh][...] = tail0 - row_sum(sp1)

        def cond(state):
            j, live = state
            return (j >= 0) & (live > EXP_UNDERFLOW)

        def body(state):
            j, _ = state
            live = jnp.float32(-jnp.inf)
            for hh in range(2):
                z, sp = scores(j, qh[hh], None)
                tail = tail_refs[hh][...]
                acc_refs[hh][...] += weighted(j, z, sp, tail, None)
                tail = tail - row_sum(sp)
                tail_refs[hh][...] = tail
                live = jnp.maximum(live, jnp.max(tail))
            return j - 1, live

        live = jnp.maximum(jnp.max(tail0_ref[...]), jnp.max(tail1_ref[...]))
        lax.while_loop(cond, body, (qi - 2, live))
        finish(qi)
        return 0

    lax.fori_loop(1, SEQ // SB_T, query_tile, 0)


def _sb_attn(qkv):
    pairs = SB_HEADS // 2
    return pl.pallas_call(
        _sb_attn_kernel,
        grid=(BATCH, pairs),
        in_specs=[
            pl.BlockSpec((SEQ, LANES), lambda b, h: (b, h)),
            pl.BlockSpec((SEQ, LANES), lambda b, h: (b, pairs + h)),
            pl.BlockSpec((SEQ, LANES), lambda b, h: (b, 2 * pairs + h)),
        ],
        out_specs=pl.BlockSpec((SEQ, LANES), lambda b, h: (b, h)),
        out_shape=jax.ShapeDtypeStruct((N_TOK, D_MODEL), BF16),
        scratch_shapes=[
            pltpu.VMEM((SB_T, LANES), F32),
            pltpu.VMEM((SB_T, LANES), F32),
            pltpu.VMEM((SB_T, LANES), F32),
            pltpu.VMEM((SB_T, LANES), F32),
        ],
        compiler_params=_cparams(("arbitrary", "arbitrary")),
        name="sb_attn",
    )(qkv, qkv, qkv)


def _out_router_kernel(o_ref, x_ref, mod_ref, g_ref, w_ref, rw_ref, rb_ref,
                       x1_ref, h2_ref, idr_ref, wts_ref, cnt_ref, tri_ref, base_ref):
    i = pl.program_id(0)

    @pl.when(i == 0)
    def _():
        r = lax.broadcasted_iota(jnp.int32, (OUT_TM, OUT_TM), 0)
        c = lax.broadcasted_iota(jnp.int32, (OUT_TM, OUT_TM), 1)
        tri_ref[...] = jnp.where(c < r, 1.0, 0.0).astype(BF16)
        base_ref[...] = jnp.zeros_like(base_ref)

    m = mod_ref[0]
    y = jnp.dot(o_ref[...], w_ref[...], preferred_element_type=F32)
    x1 = x_ref[...] + m[2:3] * y
    x1_ref[...] = x1
    h2 = _modulated_norm(x1, g_ref[...], m[4:5], m[3:4])
    h2_ref[...] = lax.bitcast_convert_type(_pack_bf16_pairs(h2), jnp.int32)

    logits = jnp.dot(h2.astype(BF16), rw_ref[...], preferred_element_type=F32) + rb_ref[...]
    lane = lax.broadcasted_iota(jnp.int32, logits.shape, 1).astype(F32)
    neg_inf = jnp.float32(-jnp.inf)
    big = jnp.float32(1e9)

    is_group = lane < N_GROUPS
    gl = jnp.where(is_group, logits, neg_inf)
    gmax = jnp.max(gl, axis=-1, keepdims=True)
    gidx = jnp.min(jnp.where(gl == gmax, lane, big), axis=-1, keepdims=True)
    gsum = jnp.sum(jnp.where(is_group, jnp.exp(logits - gmax), 0.0), axis=-1, keepdims=True)
    g_w = 1.0 / gsum

    lo = EXPERT_LANE0 + EXPERTS_PER_GROUP * gidx
    in_group = (lane >= lo) & (lane < lo + EXPERTS_PER_GROUP)
    el = jnp.where(in_group, logits, neg_inf)
    v0 = jnp.max(el, axis=-1, keepdims=True)
    i0 = jnp.min(jnp.where(el == v0, lane, big), axis=-1, keepdims=True)
    el = jnp.where(lane == i0, neg_inf, el)
    v1 = jnp.max(el, axis=-1, keepdims=True)
    i1 = jnp.min(jnp.where(el == v1, lane, big), axis=-1, keepdims=True)
    t = jnp.exp(v1 - v0)
    w0 = g_w / (1.0 + t)
    w1 = g_w * t / (1.0 + t)

    oh0 = jnp.where(lane == i0, 1.0, 0.0)
    oh1 = jnp.where(lane == i1, 1.0, 0.0)
    both = oh0 + oh1
    before = jnp.dot(tri_ref[...], both.astype(BF16), preferred_element_type=F32) + base_ref[...]
    r0 = jnp.sum(before * oh0, axis=-1, keepdims=True)
    r1 = jnp.sum(before * oh1, axis=-1, keepdims=True)
    base_ref[...] = base_ref[...] + jnp.sum(both, axis=0, keepdims=True)
    cnt_ref[...] = base_ref[...]

    e0 = i0 - EXPERT_LANE0
    e1 = i1 - EXPERT_LANE0
    idr = jnp.where(lane == 0, e0, jnp.where(lane == 1, e1, jnp.where(lane == 2, r0, r1)))
    idr_ref[...] = idr[:, :4].astype(jnp.int32)
    wts_ref[...] = jnp.where(lane == 0, w0, w1)[:, :2]


def _out_router(o, x, mod, g, w, rw, rb):
    tiles_per_batch = SEQ // OUT_TM
    return pl.pallas_call(
        _out_router_kernel,
        grid=(N_TOK // OUT_TM,),
        in_specs=[
            pl.BlockSpec((OUT_TM, D_MODEL), lambda i: (i, 0)),
            pl.BlockSpec((OUT_TM, D_MODEL), lambda i: (i, 0)),
            pl.BlockSpec((1, 6, D_MODEL), lambda i: (i // tiles_per_batch, 0, 0)),
            pl.BlockSpec((1, D_MODEL), lambda i: (0, 0)),
            pl.BlockSpec((D_MODEL, D_MODEL), lambda i: (0, 0)),
            pl.BlockSpec((D_MODEL, LANES), lambda i: (0, 0)),
            pl.BlockSpec((1, LANES), lambda i: (0, 0)),
        ],
        out_specs=[
            pl.BlockSpec((OUT_TM, D_MODEL), lambda i: (i, 0)),
            pl.BlockSpec((OUT_TM, ROW_WORDS), lambda i: (i, 0)),
            pl.BlockSpec((OUT_TM, 4), lambda i: (i, 0)),
            pl.BlockSpec((OUT_TM, 2), lambda i: (i, 0)),
            pl.BlockSpec((1, LANES), lambda i: (0, 0)),
        ],
        out_shape=[
            jax.ShapeDtypeStruct((N_TOK, D_MODEL), F32),
            jax.ShapeDtypeStruct((N_TOK, ROW_WORDS), jnp.int32),
            jax.ShapeDtypeStruct((N_TOK, 4), jnp.int32),
            jax.ShapeDtypeStruct((N_TOK, 2), F32),
            jax.ShapeDtypeStruct((1, LANES), F32),
        ],
        scratch_shapes=[
            pltpu.VMEM((OUT_TM, OUT_TM), BF16),
            pltpu.VMEM((1, LANES), F32),
        ],
        compiler_params=_cparams(("arbitrary",)),
        name="out_router",
    )(o, x, mod, g, w, rw, rb)


def _dispatch_kernel(dest_ref, pad_end_ref, padded_ref, h_ref, xs_ref, zeros_ref, sem, zsem):
    @pl.when(pl.program_id(0) == 0)
    def _():
        zeros_ref[...] = jnp.zeros_like(zeros_ref)

        def tail_copy(e):
            start = pl.multiple_of(jnp.maximum(pad_end_ref[e] - MOE_BLK, 0), MOE_BLK)
            return pltpu.make_async_copy(zeros_ref, xs_ref.at[pl.ds(start, MOE_BLK)], zsem)

        def fill(e, _):
            @pl.when(padded_ref[e] > 0)
            def _():
                tail_copy(e).start()
            return 0

        def fill_done(e, _):
            @pl.when(padded_ref[e] > 0)
            def _():
                tail_copy(e).wait()
            return 0

        def block_copy(b):
            start = pl.multiple_of(b * MOE_BLK, MOE_BLK)
            return pltpu.make_async_copy(zeros_ref, xs_ref.at[pl.ds(start, MOE_BLK)], zsem)

        def fill_unused(b, _):
            block_copy(b).start()
            return 0

        def fill_unused_done(b, _):
            block_copy(b).wait()
            return 0

        first_unused = pad_end_ref[N_EXPERTS - 1] // MOE_BLK
        lax.fori_loop(0, N_EXPERTS, fill, 0)
        lax.fori_loop(first_unused, N_BLOCKS, fill_unused, 0)
        lax.fori_loop(0, N_EXPERTS, fill_done, 0)
        lax.fori_loop(first_unused, N_BLOCKS, fill_unused_done, 0)

    def row_copy(r, d):
        return pltpu.make_async_copy(h_ref.at[pl.ds(r, 1)], xs_ref.at[pl.ds(d, 1)], sem)

    def issue(r, _):
        for k in range(TOP_K):
            row_copy(r, dest_ref[0, 0, TOP_K * r + k]).start()
        return 0

    def drain(r, _):
        for k in range(TOP_K):
            row_copy(0, 0).wait()
        return 0

    lax.fori_loop(0, DISP_TM, issue, 0, unroll=DMA_UNROLL)
    lax.fori_loop(0, DISP_TM, drain, 0, unroll=DMA_UNROLL)


def _dispatch(dest, pad_ends, padded, h2):
    nt = N_TOK // DISP_TM
    return pl.pallas_call(
        _dispatch_kernel,
        grid=(nt,),
        in_specs=[
            pl.BlockSpec((1, 1, TOP_K * DISP_TM), lambda i: (i, 0, 0), memory_space=pltpu.SMEM),
            pl.BlockSpec(memory_space=pltpu.SMEM),
            pl.BlockSpec(memory_space=pltpu.SMEM),
            pl.BlockSpec((DISP_TM, ROW_WORDS), lambda i: (i, 0)),
        ],
        out_specs=pl.BlockSpec(memory_space=pl.ANY),
        out_shape=jax.ShapeDtypeStruct((N_SLOTS, ROW_WORDS), jnp.uint32),
        scratch_shapes=[
            pltpu.VMEM((MOE_BLK, ROW_WORDS), jnp.uint32),
            pltpu.SemaphoreType.DMA(()),
            pltpu.SemaphoreType.DMA(()),
        ],
        compiler_params=_cparams(("arbitrary",)),
        name="dispatch",
    )(dest.reshape(nt, 1, TOP_K * DISP_TM), pad_ends, padded, h2)


def _expert_kernel(bexp_ref, nused_ref, xs_ref, wg_ref, wu_ref, wd_ref, ys_ref,
                   wg_bf, wu_bf, wd_bf):
    i = pl.program_id(0)

    @pl.when(i < nused_ref[0])
    def _():
        e = bexp_ref[i]
        prev = bexp_ref[jnp.maximum(i - 1, 0)]

        @pl.when((i == 0) | (e != prev))
        def _():
            wg_bf[...] = wg_ref[0, 0].astype(BF16)
            wu_bf[...] = wu_ref[0, 0].astype(BF16)
            wd_bf[...] = wd_ref[0, 0].astype(BF16)

        x = _unpack_bf16_pairs(lax.bitcast_convert_type(xs_ref[...], jnp.uint32)).astype(BF16)
        g = jnp.dot(x, wg_bf[...], preferred_element_type=F32)
        u = jnp.dot(x, wu_bf[...], preferred_element_type=F32)
        hid = (g * jax.nn.sigmoid(g)) * u
        y = jnp.dot(hid.astype(BF16), wd_bf[...], preferred_element_type=F32)
        ys_ref[...] = lax.bitcast_convert_type(_pack_bf16_pairs(y), jnp.int32)

    @pl.when(i >= nused_ref[0])
    def _():
        ys_ref[...] = jnp.zeros_like(ys_ref)


def _experts(layer, bexp, nused, xs, wg, wu, wd):
    def in_row_map(i, bexp, nused):
        return (jnp.minimum(i, nused[0] - 1), 0)

    def row_map(i, bexp, nused):
        return (i, 0)

    def w_map(i, bexp, nused):
        return (layer, bexp[jnp.minimum(i, nused[0] - 1)], 0, 0)

    return pl.pallas_call(
        _expert_kernel,
        grid_spec=pltpu.PrefetchScalarGridSpec(
            num_scalar_prefetch=2,
            grid=(N_BLOCKS,),
            in_specs=[
                pl.BlockSpec((MOE_BLK, ROW_WORDS), in_row_map),
                pl.BlockSpec((1, 1, D_MODEL, EXPERT_HIDDEN), w_map),
                pl.BlockSpec((1, 1, D_MODEL, EXPERT_HIDDEN), w_map),
                pl.BlockSpec((1, 1, EXPERT_HIDDEN, D_MODEL), w_map),
            ],
            out_specs=pl.BlockSpec((MOE_BLK, ROW_WORDS), row_map),
            scratch_shapes=[
                pltpu.VMEM((D_MODEL, EXPERT_HIDDEN), BF16),
                pltpu.VMEM((D_MODEL, EXPERT_HIDDEN), BF16),
                pltpu.VMEM((EXPERT_HIDDEN, D_MODEL), BF16),
            ],
        ),
        out_shape=jax.ShapeDtypeStruct((N_SLOTS, ROW_WORDS), jnp.int32),
        compiler_params=_cparams(("arbitrary",)),
        name="experts",
    )(bexp, nused, xs, wg, wu, wd)


def _sc_gather_rows(table, idx):
    n_rows = idx.shape[0]
    width = table.shape[1]
    workers = SC_CORES * SC_SUBCORES
    per_worker = n_rows // workers
    assert per_worker * workers == n_rows and per_worker % SC_CHUNK == 0
    mesh = plsc.VectorSubcoreMesh(core_axis_name="c", subcore_axis_name="s")

    def body(table_hbm, idx_hbm, out_hbm, idx_v, rows_v, sem):
        wid = lax.axis_index("s") * SC_CORES + lax.axis_index("c")
        base = wid * per_worker

        def chunk(c, _):
            off = pl.multiple_of(base + c * SC_CHUNK, SC_CHUNK)
            pltpu.sync_copy(idx_hbm.at[pl.ds(off, SC_CHUNK)], idx_v)
            pltpu.async_copy(table_hbm.at[idx_v], rows_v, sem).wait()
            pltpu.sync_copy(rows_v, out_hbm.at[pl.ds(off, SC_CHUNK)])
            return 0

        lax.fori_loop(0, per_worker // SC_CHUNK, chunk, 0)

    return pl.kernel(
        body,
        out_type=jax.ShapeDtypeStruct((n_rows, width), table.dtype),
        mesh=mesh,
        scratch_types=[
            pltpu.VMEM((SC_CHUNK,), jnp.int32),
            pltpu.VMEM((SC_CHUNK, width), table.dtype),
            pltpu.SemaphoreType.DMA,
        ],
        name="sc_gather_rows",
    )(table, idx)


def _sc_slot_tokens(dest_flat):
    n_assign = N_TOK * TOP_K
    lanes = SC_LANES
    mesh = plsc.VectorSubcoreMesh(core_axis_name="c", subcore_axis_name="s")

    def body(dest_hbm, out_hbm, dest_v, slot_v):
        wid = lax.axis_index("s") * SC_CORES + lax.axis_index("c")

        @pl.when(wid == 0)
        def _():
            pltpu.sync_copy(dest_hbm, dest_v)
            lane = lax.iota(jnp.int32, lanes)

            def init(i, _):
                start = pl.multiple_of(i * lanes, lanes)
                slot_v[pl.ds(start, lanes)] = (lane + start) & (N_TOK - 1)
                return 0

            def place(i, _):
                start = pl.multiple_of(i * lanes, lanes)
                slots = dest_v[pl.ds(start, lanes)]
                plsc.store_scatter(slot_v, [slots], (lane + start) >> 1)
                return 0

            lax.fori_loop(0, N_SLOTS // lanes, init, 0)
            lax.fori_loop(0, n_assign // lanes, place, 0)
            pltpu.sync_copy(slot_v, out_hbm)

    return pl.kernel(
        body,
        out_type=jax.ShapeDtypeStruct((N_SLOTS,), jnp.int32),
        mesh=mesh,
        scratch_types=[
            pltpu.VMEM((n_assign,), jnp.int32),
            pltpu.VMEM((N_SLOTS,), jnp.int32),
        ],
        compiler_params=pltpu.CompilerParams(needs_layout_passes=False),
        name="sc_slot_tokens",
    )(dest_flat)


def _mix_kernel(x_ref, wts_ref, mod_ref, fg_ref, y0_ref, y1_ref, o_ref, *, final):
    w = wts_ref[...]
    y0 = _unpack_bf16_pairs(lax.bitcast_convert_type(y0_ref[...], jnp.uint32))
    y1 = _unpack_bf16_pairs(lax.bitcast_convert_type(y1_ref[...], jnp.uint32))
    y = w[:, 0:1] * y0 + w[:, 1:2] * y1
    out = x_ref[...] + mod_ref[0][5:6] * y
    if final:
        ms = jnp.mean(out * out, axis=-1, keepdims=True)
        out = out * lax.rsqrt(ms + RMS_EPS) * fg_ref[...]
    o_ref[...] = out


def _mix(x1, wts, mod, fg, y2, final):
    nt = N_TOK // COMB_TM
    tiles_per_batch = SEQ // COMB_TM
    return pl.pallas_call(
        functools.partial(_mix_kernel, final=final),
        grid=(nt,),
        in_specs=[
            pl.BlockSpec((COMB_TM, D_MODEL), lambda i: (i, 0)),
            pl.BlockSpec((COMB_TM, TOP_K), lambda i: (i, 0)),
            pl.BlockSpec((1, 6, D_MODEL), lambda i: (i // tiles_per_batch, 0, 0)),
            pl.BlockSpec((1, D_MODEL), lambda i: (0, 0)),
            pl.BlockSpec((COMB_TM, ROW_WORDS), lambda i: (i, 0)),
            pl.BlockSpec((COMB_TM, ROW_WORDS), lambda i: (nt + i, 0)),
        ],
        out_specs=pl.BlockSpec((COMB_TM, D_MODEL), lambda i: (i, 0)),
        out_shape=jax.ShapeDtypeStruct((N_TOK, D_MODEL), F32),
        compiler_params=_cparams(("arbitrary",)),
        name="mix",
    )(x1, wts, mod, fg, y2, y2)


def _routing_tables(idr, cnt):
    counts = cnt[0, EXPERT_LANE0:EXPERT_LANE0 + N_EXPERTS].astype(jnp.int32)
    padded = (counts + MOE_BLK - 1) // MOE_BLK * MOE_BLK
    pad_ends = jnp.cumsum(padded)
    pad_starts = pad_ends - padded
    experts = jnp.arange(N_EXPERTS, dtype=jnp.int32)
    is_expert = idr[:, 0:2, None] == experts
    dest = jnp.sum(jnp.where(is_expert, pad_starts, 0), axis=-1) + idr[:, 2:4]
    block_start = jnp.arange(N_BLOCKS, dtype=jnp.int32) * MOE_BLK
    bexp = jnp.minimum(jnp.sum(block_start[:, None] >= pad_ends[None, :], axis=1),
                       N_EXPERTS - 1).astype(jnp.int32)
    nused = (pad_ends[-1:] // MOE_BLK).astype(jnp.int32)
    return dest.astype(jnp.int32), bexp, nused


def _router_weights(w_group, b_group, w_expert, b_expert):
    rw = jnp.zeros((D_MODEL, LANES), F32)
    rw = rw.at[:, :N_GROUPS].set(w_group)
    rw = rw.at[:, EXPERT_LANE0:EXPERT_LANE0 + N_EXPERTS].set(w_expert)
    rb = jnp.zeros((1, LANES), F32)
    rb = rb.at[0, :N_GROUPS].set(b_group)
    rb = rb.at[0, EXPERT_LANE0:EXPERT_LANE0 + N_EXPERTS].set(b_expert.reshape(-1))
    return rw.astype(BF16), rb


def _scaled_qkv_weight(w_in, head_dim):
    scale = jnp.concatenate([jnp.full((D_MODEL,), 1.0 / math.sqrt(head_dim), F32),
                             jnp.ones((2 * D_MODEL,), F32)])
    return (w_in * scale).astype(BF16)


def kernel(x, c, norm1_g, norm2_g, ada_w, ada_b, diff_w_in, diff_w_out, diff_lambda_q1, diff_lambda_k1, diff_lambda_q2, diff_lambda_k2, diff_subln_g, sb_w_in, sb_w_out, router_group_w, router_group_b, router_expert_w, router_expert_b, expert_w_gate, expert_w_up, expert_w_down, final_norm_g):
    xf = x.reshape(N_TOK, D_MODEL)
    mod_all = _ada(c, ada_w, ada_b)
    slopes = jnp.exp2(-8.0 * jnp.arange(1, DIFF_HEADS + 1, dtype=F32) / DIFF_HEADS)

    for i in range(DEPTH):
        mod = mod_all[i].reshape(BATCH, 6, D_MODEL)
        j = i // 2
        if i % 2 == 0:
            lambda_init = 0.8 - 0.6 * math.exp(-0.3 * i)
            lam = (jnp.exp(jnp.sum(diff_lambda_q1[j] * diff_lambda_k1[j]))
                   - jnp.exp(jnp.sum(diff_lambda_q2[j] * diff_lambda_k2[j]))
                   + lambda_init).reshape(1, 1)
            qkv = _qkv(xf, mod, norm1_g[i].reshape(1, D_MODEL),
                       _scaled_qkv_weight(diff_w_in[j], DIFF_HEAD_DIM))
            o = _diff_attn(qkv, slopes, lam, diff_subln_g[j].reshape(1, LANES), lambda_init)
            w_out = diff_w_out[j]
        else:
            qkv = _qkv(xf, mod, norm1_g[i].reshape(1, D_MODEL),
                       _scaled_qkv_weight(sb_w_in[j], SB_HEAD_DIM))
            o = _sb_attn(qkv)
            w_out = sb_w_out[j]
        rw, rb = _router_weights(router_group_w[i], router_group_b[i],
                                 router_expert_w[i], router_expert_b[i])
        x1, h2, idr, wts, cnt = _out_router(o, xf, mod, norm2_g[i].reshape(1, D_MODEL),
                                            w_out.astype(BF16), rw, rb)
        dest, bexp, nused = _routing_tables(idr, cnt)
        xs = _sc_gather_rows(h2, _sc_slot_tokens(dest.reshape(-1)))
        ys = _experts(i, bexp, nused, xs, expert_w_gate, expert_w_up, expert_w_down)
        y2 = _sc_gather_rows(ys, dest.T.reshape(-1))
        xf = _mix(x1, wts, mod, final_norm_g.reshape(1, D_MODEL), y2,
                  final=(i == DEPTH - 1))
    return xf.reshape(BATCH, SEQ, D_MODEL)
```

```python
import functools
import math

import jax
import jax.numpy as jnp
from jax import lax
from jax.experimental import pallas as pl
from jax.experimental.pallas import tpu as pltpu
from jax.experimental.pallas import tpu_sc as plsc

D_MODEL = 1024
BATCH = 8
SEQ = 2048
DEPTH = 2
N_TOK = BATCH * SEQ

CHUNK = 64
DIFF_HEADS = 8
DIFF_HEAD_DIM = D_MODEL // (2 * DIFF_HEADS)
SB_HEADS = 16
SB_HEAD_DIM = D_MODEL // SB_HEADS
N_GROUPS = 4
EXPERTS_PER_GROUP = 8
N_EXPERTS = N_GROUPS * EXPERTS_PER_GROUP
TOP_K = 2
EXPERT_HIDDEN = D_MODEL // 2
RMS_EPS = 1e-6
SUBLN_EPS = 1e-5

LANES = 128
EXPERT_LANE0 = 32
EXP2_UNDERFLOW = -150.0
LOG2_E = math.log2(math.e)

ADA_TN = 1536
QKV_TM = 512
DIFF_T = 512
SB_T = 256
OUT_TM = 512
DISP_TM = 1024
MOE_BLK = 256
N_SLOTS = N_TOK * TOP_K + N_EXPERTS * MOE_BLK
N_BLOCKS = N_SLOTS // MOE_BLK
COMB_TM = 512
DMA_UNROLL = 8
ROW_WORDS = D_MODEL // 2

SC_CORES = 2
SC_SUBCORES = 16
SC_LANES = 16
SC_CHUNK = 64

VMEM_LIMIT = 56 * 1024 * 1024

F32 = jnp.float32
BF16 = jnp.bfloat16


def _cparams(sem):
    return pltpu.CompilerParams(dimension_semantics=sem, vmem_limit_bytes=VMEM_LIMIT)


def _ada_kernel(c_ref, w_ref, b_ref, o_ref):
    c = c_ref[...]
    cond = c * jax.nn.sigmoid(c)
    o_ref[0] = jnp.dot(cond, w_ref[0], preferred_element_type=F32,
                       precision=lax.Precision.HIGHEST) + b_ref[0]


def _ada(c, ada_w, ada_b):
    six_d = ada_w.shape[-1]
    return pl.pallas_call(
        _ada_kernel,
        grid=(DEPTH, six_d // ADA_TN),
        in_specs=[
            pl.BlockSpec((BATCH, D_MODEL), lambda l, n: (0, 0)),
            pl.BlockSpec((1, D_MODEL, ADA_TN), lambda l, n: (l, 0, n)),
            pl.BlockSpec((1, 1, ADA_TN), lambda l, n: (l, 0, n)),
        ],
        out_specs=pl.BlockSpec((1, BATCH, ADA_TN), lambda l, n: (l, 0, n)),
        out_shape=jax.ShapeDtypeStruct((DEPTH, BATCH, six_d), F32),
        compiler_params=_cparams(("arbitrary", "arbitrary")),
        name="ada",
    )(c, ada_w, ada_b.reshape(DEPTH, 1, six_d))


def _modulated_norm(x, g, scale, shift):
    ms = jnp.mean(x * x, axis=-1, keepdims=True)
    return x * lax.rsqrt(ms + RMS_EPS) * (g * (1.0 + scale)) + shift


def _qkv_kernel(x_ref, mod_ref, g_ref, w_ref, o_ref):
    m = mod_ref[0]
    h = _modulated_norm(x_ref[...], g_ref[...], m[1:2], m[0:1]).astype(BF16)
    for n in range(3):
        cols = slice(n * D_MODEL, (n + 1) * D_MODEL)
        o_ref[:, cols] = jnp.dot(h, w_ref[:, cols], preferred_element_type=F32).astype(BF16)


def _qkv(x, mod, g, w):
    tiles_per_batch = SEQ // QKV_TM
    return pl.pallas_call(
        _qkv_kernel,
        grid=(N_TOK // QKV_TM,),
        in_specs=[
            pl.BlockSpec((QKV_TM, D_MODEL), lambda i: (i, 0)),
            pl.BlockSpec((1, 6, D_MODEL), lambda i: (i // tiles_per_batch, 0, 0)),
            pl.BlockSpec((1, D_MODEL), lambda i: (0, 0)),
            pl.BlockSpec((D_MODEL, 3 * D_MODEL), lambda i: (0, 0)),
        ],
        out_specs=pl.BlockSpec((QKV_TM, 3 * D_MODEL), lambda i: (i, 0)),
        out_shape=jax.ShapeDtypeStruct((N_TOK, 3 * D_MODEL), BF16),
        compiler_params=_cparams(("arbitrary",)),
        name="qkv",
    )(x, mod, g, w)


def _half_masked(q, upper):
    lane = lax.broadcasted_iota(jnp.int32, q.shape, 1)
    keep = (lane >= LANES // 2) if upper else (lane < LANES // 2)
    return jnp.where(keep, q, jnp.zeros_like(q))


def _pack_bf16_pairs(x):
    half = x.shape[1] // 2
    hi = lax.bitcast_convert_type(x[:, :half].astype(BF16).astype(F32), jnp.uint32)
    lo = lax.bitcast_convert_type(x[:, half:].astype(BF16).astype(F32), jnp.uint32)
    return hi | (lo >> 16)


def _unpack_bf16_pairs(u):
    hi = lax.bitcast_convert_type(u & jnp.uint32(0xFFFF0000), F32)
    lo = lax.bitcast_convert_type(u << 16, F32)
    return jnp.concatenate([hi, lo], axis=1)


def _lane_tile(x, n):
    return jnp.concatenate([x] * n, axis=1)


def _qk(q, k):
    return lax.dot_general(q, k, (((1,), (1,)), ((), ())), preferred_element_type=F32)


def _diff_attn_kernel(slopes_ref, lam_ref, q_ref, k_ref, v_ref, g_ref, o_ref,
                      m0_ref, m1_ref, acc0_ref, acc1_ref, *, lambda_init):
    slope = slopes_ref[pl.program_id(1)] * LOG2_E
    m_refs = (m0_ref, m1_ref)
    acc_refs = (acc0_ref, acc1_ref)
    ones = jnp.ones((DIFF_T, LANES), BF16)

    row = lax.broadcasted_iota(jnp.int32, (DIFF_T, DIFF_T), 0)
    col = lax.broadcasted_iota(jnp.int32, (DIFF_T, DIFF_T), 1)
    rel = (row - col).astype(F32) * slope
    allowed = (col // CHUNK) <= (row // CHUNK)
    diag_bias = jnp.where(allowed, jnp.abs(rel), jnp.inf)
    out_gain = g_ref[...] * (1.0 - lambda_init)

    def query_tile(qi, _):
        q = q_ref[pl.ds(qi * DIFF_T, DIFF_T), :]
        qm = (_half_masked(q, False), _half_masked(q, True))

        k = k_ref[pl.ds(qi * DIFF_T, DIFF_T), :]
        v1 = jnp.concatenate([v_ref[pl.ds(qi * DIFF_T, DIFF_T), :], ones], axis=1)
        s_maps = [_qk(qm[mi], k) - diag_bias for mi in range(2)]
        for mi in range(2):
            m = jnp.max(s_maps[mi], axis=-1, keepdims=True)
            p = jnp.exp2(s_maps[mi] - m)
            m_refs[mi][...] = jnp.broadcast_to(m, (DIFF_T, LANES))
            acc_refs[mi][...] = jnp.dot(p.astype(BF16), v1, preferred_element_type=F32)

        def body(j, _):
            k = k_ref[pl.ds(j * DIFF_T, DIFF_T), :]
            v1 = jnp.concatenate([v_ref[pl.ds(j * DIFF_T, DIFF_T), :], ones], axis=1)
            shift = slope * jnp.asarray((qi - j) * DIFF_T).astype(F32)
            s_maps = [_qk(qm[mi], k) - rel for mi in range(2)]
            for mi in range(2):
                m = m_refs[mi][...]
                m_new = jnp.maximum(m, jnp.max(s_maps[mi], axis=-1, keepdims=True) - shift)
                alpha = jnp.exp2(m - m_new)
                p = jnp.exp2(s_maps[mi] - _lane_tile(m_new + shift, DIFF_T // LANES))
                m_refs[mi][...] = m_new
                acc_refs[mi][...] = _lane_tile(alpha, 2) * acc_refs[mi][...] + jnp.dot(
                    p.astype(BF16), v1, preferred_element_type=F32)
            return 0

        lax.fori_loop(0, qi, body, 0)
        a0 = acc0_ref[...]
        a1 = acc1_ref[...]
        o = a0[:, :LANES] / a0[:, LANES:] - lam_ref[...] * (a1[:, :LANES] / a1[:, LANES:])
        ms = jnp.mean(o * o, axis=-1, keepdims=True)
        o = o * lax.rsqrt(ms + SUBLN_EPS) * out_gain
        o_ref[pl.ds(qi * DIFF_T, DIFF_T), :] = o.astype(BF16)
        return 0

    lax.fori_loop(0, SEQ // DIFF_T, query_tile, 0)


def _diff_attn(qkv, slopes, lam, subln_g, lambda_init):
    return pl.pallas_call(
        functools.partial(_diff_attn_kernel, lambda_init=lambda_init),
        grid=(BATCH, DIFF_HEADS),
        in_specs=[
            pl.BlockSpec(memory_space=pltpu.SMEM),
            pl.BlockSpec((1, 1), lambda b, h: (0, 0)),
            pl.BlockSpec((SEQ, LANES), lambda b, h: (b, h)),
            pl.BlockSpec((SEQ, LANES), lambda b, h: (b, DIFF_HEADS + h)),
            pl.BlockSpec((SEQ, LANES), lambda b, h: (b, 2 * DIFF_HEADS + h)),
            pl.BlockSpec((1, LANES), lambda b, h: (0, 0)),
        ],
        out_specs=pl.BlockSpec((SEQ, LANES), lambda b, h: (b, h)),
        out_shape=jax.ShapeDtypeStruct((N_TOK, D_MODEL), BF16),
        scratch_shapes=[
            pltpu.VMEM((DIFF_T, LANES), F32),
            pltpu.VMEM((DIFF_T, LANES), F32),
            pltpu.VMEM((DIFF_T, 2 * LANES), F32),
            pltpu.VMEM((DIFF_T, 2 * LANES), F32),
        ],
        compiler_params=_cparams(("arbitrary", "arbitrary")),
        name="diff_attn",
    )(slopes, lam, qkv, qkv, qkv, subln_g)


def _sb_attn_kernel(q_ref, k_ref, v_ref, o_ref, tail0_ref, tail1_ref, acc0_ref, acc1_ref):
    tail_refs = (tail0_ref, tail1_ref)
    acc_refs = (acc0_ref, acc1_ref)

    row = lax.broadcasted_iota(jnp.int32, (SB_T, SB_T), 0)
    col = lax.broadcasted_iota(jnp.int32, (SB_T, SB_T), 1)
    strict = col < row
    neg_from = jnp.where(row >= col, -1.0, 0.0).astype(BF16)

    def scores(j, qhead, mask):
        z = _qk(qhead, k_ref[pl.ds(j * SB_T, SB_T), :])
        sp = jnp.maximum(z, 0.0) + jnp.log2(1.0 + jnp.exp2(-jnp.abs(z)))
        if mask is not None:
            sp = jnp.where(mask, sp, 0.0)
        return z, sp

    def weighted(j, z, sp, tail, mask):
        log_a = (z + jnp.dot(sp.astype(BF16), neg_from, preferred_element_type=F32)
                 + _lane_tile(tail, SB_T // LANES))
        a = jnp.exp2(log_a)
        if mask is not None:
            a = jnp.where(mask, a, 0.0)
        return jnp.dot(a.astype(BF16), v_ref[pl.ds(j * SB_T, SB_T), :],
                       preferred_element_type=F32)

    def row_sum(sp):
        return jnp.broadcast_to(jnp.sum(sp, axis=-1, keepdims=True), (SB_T, LANES))

    zero = jnp.zeros((SB_T, LANES), F32)
    lane = lax.broadcasted_iota(jnp.int32, (SB_T, LANES), 1)

    def query_heads(qi):
        q = q_ref[pl.ds(qi * SB_T, SB_T), :]
        return (_half_masked(q, False), _half_masked(q, True))

    def finish(qi):
        o_ref[pl.ds(qi * SB_T, SB_T), :] = jnp.where(
            lane < LANES // 2, acc0_ref[...], acc1_ref[...]).astype(BF16)

    qh = query_heads(0)
    for hh in range(2):
        z, sp = scores(0, qh[hh], strict)
        acc_refs[hh][...] = weighted(0, z, sp, zero, strict)
    finish(0)

    def query_tile(qi, _):
        qh = query_heads(qi)
        zs = [(scores(qi, qh[hh], strict), scores(qi - 1, qh[hh], None)) for hh in range(2)]
        for hh in range(2):
            (z0, sp0), (z1, sp1) = zs[hh]
            tail0 = -row_sum(sp0)
            acc_refs[hh][...] = (weighted(qi, z0, sp0, zero, strict)
                                 + weighted(qi - 1, z1, sp1, tail0, None))
            tail_refs[hh][...] = tail0 - row_sum(sp1)

        def cond(state):
            j, live = state
            return (j >= 0) & (live > EXP2_UNDERFLOW)

        def body(state):
            j, _ = state
            live = jnp.float32(-jnp.inf)
            for hh in range(2):
                z, sp = scores(j, qh[hh], None)
                tail = tail_refs[hh][...]
                acc_refs[hh][...] += weighted(j, z, sp, tail, None)
                tail = tail - row_sum(sp)
                tail_refs[hh][...] = tail
                live = jnp.maximum(live, jnp.max(tail))
            return j - 1, live

        live = jnp.maximum(jnp.max(tail0_ref[...]), jnp.max(tail1_ref[...]))
        lax.while_loop(cond, body, (qi - 2, live))
        finish(qi)
        return 0

    lax.fori_loop(1, SEQ // SB_T, query_tile, 0)


def _sb_attn(qkv):
    pairs = SB_HEADS // 2
    return pl.pallas_call(
        _sb_attn_kernel,
        grid=(BATCH, pairs),
        in_specs=[
            pl.BlockSpec((SEQ, LANES), lambda b, h: (b, h)),
            pl.BlockSpec((SEQ, LANES), lambda b, h: (b, pairs + h)),
            pl.BlockSpec((SEQ, LANES), lambda b, h: (b, 2 * pairs + h)),
        ],
        out_specs=pl.BlockSpec((SEQ, LANES), lambda b, h: (b, h)),
        out_shape=jax.ShapeDtypeStruct((N_TOK, D_MODEL), BF16),
        scratch_shapes=[
            pltpu.VMEM((SB_T, LANES), F32),
            pltpu.VMEM((SB_T, LANES), F32),
            pltpu.VMEM((SB_T, LANES), F32),
            pltpu.VMEM((SB_T, LANES), F32),
        ],
        compiler_params=_cparams(("arbitrary", "arbitrary")),
        name="sb_attn",
    )(qkv, qkv, qkv)


def _out_router_kernel(o_ref, x_ref, mod_ref, g_ref, w_ref, rw_ref, rb_ref,
                       x1_ref, h2_ref, idr_ref, wts_ref, cnt_ref, tri_ref, base_ref):
    i = pl.program_id(0)

    @pl.when(i == 0)
    def _():
        r = lax.broadcasted_iota(jnp.int32, (OUT_TM, OUT_TM), 0)
        c = lax.broadcasted_iota(jnp.int32, (OUT_TM, OUT_TM), 1)
        tri_ref[...] = jnp.where(c < r, 1.0, 0.0).astype(BF16)
        base_ref[...] = jnp.zeros_like(base_ref)

    m = mod_ref[0]
    y = jnp.dot(o_ref[...], w_ref[...], preferred_element_type=F32)
    x1 = x_ref[...] + m[2:3] * y
    x1_ref[...] = x1
    h2 = _modulated_norm(x1, g_ref[...], m[4:5], m[3:4])
    h2_ref[...] = lax.bitcast_convert_type(_pack_bf16_pairs(h2), jnp.int32)

    logits = jnp.dot(h2.astype(BF16), rw_ref[...], preferred_element_type=F32) + rb_ref[...]
    lane = lax.broadcasted_iota(jnp.int32, logits.shape, 1).astype(F32)
    neg_inf = jnp.float32(-jnp.inf)
    big = jnp.float32(1e9)

    is_group = lane < N_GROUPS
    gl = jnp.where(is_group, logits, neg_inf)
    gmax = jnp.max(gl, axis=-1, keepdims=True)
    gidx = jnp.min(jnp.where(gl == gmax, lane, big), axis=-1, keepdims=True)
    gsum = jnp.sum(jnp.where(is_group, jnp.exp(logits - gmax), 0.0), axis=-1, keepdims=True)
    g_w = 1.0 / gsum

    lo = EXPERT_LANE0 + EXPERTS_PER_GROUP * gidx
    in_group = (lane >= lo) & (lane < lo + EXPERTS_PER_GROUP)
    el = jnp.where(in_group, logits, neg_inf)
    v0 = jnp.max(el, axis=-1, keepdims=True)
    i0 = jnp.min(jnp.where(el == v0, lane, big), axis=-1, keepdims=True)
    el = jnp.where(lane == i0, neg_inf, el)
    v1 = jnp.max(el, axis=-1, keepdims=True)
    i1 = jnp.min(jnp.where(el == v1, lane, big), axis=-1, keepdims=True)
    t = jnp.exp(v1 - v0)
    w0 = g_w / (1.0 + t)
    w1 = g_w * t / (1.0 + t)

    oh0 = jnp.where(lane == i0, 1.0, 0.0)
    oh1 = jnp.where(lane == i1, 1.0, 0.0)
    both = oh0 + oh1
    before = jnp.dot(tri_ref[...], both.astype(BF16), preferred_element_type=F32) + base_ref[...]
    r0 = jnp.sum(before * oh0, axis=-1, keepdims=True)
    r1 = jnp.sum(before * oh1, axis=-1, keepdims=True)
    base_ref[...] = base_ref[...] + jnp.sum(both, axis=0, keepdims=True)
    cnt_ref[...] = base_ref[...]

    e0 = i0 - EXPERT_LANE0
    e1 = i1 - EXPERT_LANE0
    idr = jnp.where(lane == 0, e0, jnp.where(lane == 1, e1, jnp.where(lane == 2, r0, r1)))
    idr_ref[...] = idr[:, :4].astype(jnp.int32)
    wts_ref[...] = jnp.where(lane == 0, w0, w1)[:, :2]


def _out_router(o, x, mod, g, w, rw, rb):
    tiles_per_batch = SEQ // OUT_TM
    return pl.pallas_call(
        _out_router_kernel,
        grid=(N_TOK // OUT_TM,),
        in_specs=[
            pl.BlockSpec((OUT_TM, D_MODEL), lambda i: (i, 0)),
            pl.BlockSpec((OUT_TM, D_MODEL), lambda i: (i, 0)),
            pl.BlockSpec((1, 6, D_MODEL), lambda i: (i // tiles_per_batch, 0, 0)),
            pl.BlockSpec((1, D_MODEL), lambda i: (0, 0)),
            pl.BlockSpec((D_MODEL, D_MODEL), lambda i: (0, 0)),
            pl.BlockSpec((D_MODEL, LANES), lambda i: (0, 0)),
            pl.BlockSpec((1, LANES), lambda i: (0, 0)),
        ],
        out_specs=[
            pl.BlockSpec((OUT_TM, D_MODEL), lambda i: (i, 0)),
            pl.BlockSpec((OUT_TM, ROW_WORDS), lambda i: (i, 0)),
            pl.BlockSpec((OUT_TM, 4), lambda i: (i, 0)),
            pl.BlockSpec((OUT_TM, 2), lambda i: (i, 0)),
            pl.BlockSpec((1, LANES), lambda i: (0, 0)),
        ],
        out_shape=[
            jax.ShapeDtypeStruct((N_TOK, D_MODEL), F32),
            jax.ShapeDtypeStruct((N_TOK, ROW_WORDS), jnp.int32),
            jax.ShapeDtypeStruct((N_TOK, 4), jnp.int32),
            jax.ShapeDtypeStruct((N_TOK, 2), F32),
            jax.ShapeDtypeStruct((1, LANES), F32),
        ],
        scratch_shapes=[
            pltpu.VMEM((OUT_TM, OUT_TM), BF16),
            pltpu.VMEM((1, LANES), F32),
        ],
        compiler_params=_cparams(("arbitrary",)),
        name="out_router",
    )(o, x, mod, g, w, rw, rb)


def _dispatch_kernel(dest_ref, pad_end_ref, padded_ref, h_ref, xs_ref, zeros_ref, sem, zsem):
    @pl.when(pl.program_id(0) == 0)
    def _():
        zeros_ref[...] = jnp.zeros_like(zeros_ref)

        def tail_copy(e):
            start = pl.multiple_of(jnp.maximum(pad_end_ref[e] - MOE_BLK, 0), MOE_BLK)
            return pltpu.make_async_copy(zeros_ref, xs_ref.at[pl.ds(start, MOE_BLK)], zsem)

        def fill(e, _):
            @pl.when(padded_ref[e] > 0)
            def _():
                tail_copy(e).start()
            return 0

        def fill_done(e, _):
            @pl.when(padded_ref[e] > 0)
            def _():
                tail_copy(e).wait()
            return 0

        def block_copy(b):
            start = pl.multiple_of(b * MOE_BLK, MOE_BLK)
            return pltpu.make_async_copy(zeros_ref, xs_ref.at[pl.ds(start, MOE_BLK)], zsem)

        def fill_unused(b, _):
            block_copy(b).start()
            return 0

        def fill_unused_done(b, _):
            block_copy(b).wait()
            return 0

        first_unused = pad_end_ref[N_EXPERTS - 1] // MOE_BLK
        lax.fori_loop(0, N_EXPERTS, fill, 0)
        lax.fori_loop(first_unused, N_BLOCKS, fill_unused, 0)
        lax.fori_loop(0, N_EXPERTS, fill_done, 0)
        lax.fori_loop(first_unused, N_BLOCKS, fill_unused_done, 0)

    def row_copy(r, d):
        return pltpu.make_async_copy(h_ref.at[pl.ds(r, 1)], xs_ref.at[pl.ds(d, 1)], sem)

    def issue(r, _):
        for k in range(TOP_K):
            row_copy(r, dest_ref[0, 0, TOP_K * r + k]).start()
        return 0

    def drain(r, _):
        for k in range(TOP_K):
            row_copy(0, 0).wait()
        return 0

    lax.fori_loop(0, DISP_TM, issue, 0, unroll=DMA_UNROLL)
    lax.fori_loop(0, DISP_TM, drain, 0, unroll=DMA_UNROLL)


def _dispatch(dest, pad_ends, padded, h2):
    nt = N_TOK // DISP_TM
    return pl.pallas_call(
        _dispatch_kernel,
        grid=(nt,),
        in_specs=[
            pl.BlockSpec((1, 1, TOP_K * DISP_TM), lambda i: (i, 0, 0), memory_space=pltpu.SMEM),
            pl.BlockSpec(memory_space=pltpu.SMEM),
            pl.BlockSpec(memory_space=pltpu.SMEM),
            pl.BlockSpec((DISP_TM, ROW_WORDS), lambda i: (i, 0)),
        ],
        out_specs=pl.BlockSpec(memory_space=pl.ANY),
        out_shape=jax.ShapeDtypeStruct((N_SLOTS, ROW_WORDS), jnp.uint32),
        scratch_shapes=[
            pltpu.VMEM((MOE_BLK, ROW_WORDS), jnp.uint32),
            pltpu.SemaphoreType.DMA(()),
            pltpu.SemaphoreType.DMA(()),
        ],
        compiler_params=_cparams(("arbitrary",)),
        name="dispatch",
    )(dest.reshape(nt, 1, TOP_K * DISP_TM), pad_ends, padded, h2)


def _expert_kernel(bexp_ref, nused_ref, xs_ref, wg_ref, wu_ref, wd_ref, ys_ref,
                   wg_bf, wu_bf, wd_bf):
    i = pl.program_id(0)

    @pl.when(i < nused_ref[0])
    def _():
        e = bexp_ref[i]
        prev = bexp_ref[jnp.maximum(i - 1, 0)]

        @pl.when((i == 0) | (e != prev))
        def _():
            wg_bf[...] = wg_ref[0, 0].astype(BF16)
            wu_bf[...] = wu_ref[0, 0].astype(BF16)
            wd_bf[...] = wd_ref[0, 0].astype(BF16)

        x = _unpack_bf16_pairs(lax.bitcast_convert_type(xs_ref[...], jnp.uint32)).astype(BF16)
        g = jnp.dot(x, wg_bf[...], preferred_element_type=F32)
        u = jnp.dot(x, wu_bf[...], preferred_element_type=F32)
        hid = (g * jax.nn.sigmoid(g)) * u
        y = jnp.dot(hid.astype(BF16), wd_bf[...], preferred_element_type=F32)
        ys_ref[...] = lax.bitcast_convert_type(_pack_bf16_pairs(y), jnp.int32)

    @pl.when(i >= nused_ref[0])
    def _():
        ys_ref[...] = jnp.zeros_like(ys_ref)


def _experts(layer, bexp, nused, xs, wg, wu, wd):
    def in_row_map(i, bexp, nused):
        return (jnp.minimum(i, nused[0] - 1), 0)

    def row_map(i, bexp, nused):
        return (i, 0)

    def w_map(i, bexp, nused):
        return (layer, bexp[jnp.minimum(i, nused[0] - 1)], 0, 0)

    return pl.pallas_call(
        _expert_kernel,
        grid_spec=pltpu.PrefetchScalarGridSpec(
            num_scalar_prefetch=2,
            grid=(N_BLOCKS,),
            in_specs=[
                pl.BlockSpec((MOE_BLK, ROW_WORDS), in_row_map),
                pl.BlockSpec((1, 1, D_MODEL, EXPERT_HIDDEN), w_map),
                pl.BlockSpec((1, 1, D_MODEL, EXPERT_HIDDEN), w_map),
                pl.BlockSpec((1, 1, EXPERT_HIDDEN, D_MODEL), w_map),
            ],
            out_specs=pl.BlockSpec((MOE_BLK, ROW_WORDS), row_map),
            scratch_shapes=[
                pltpu.VMEM((D_MODEL, EXPERT_HIDDEN), BF16),
                pltpu.VMEM((D_MODEL, EXPERT_HIDDEN), BF16),
                pltpu.VMEM((EXPERT_HIDDEN, D_MODEL), BF16),
            ],
        ),
        out_shape=jax.ShapeDtypeStruct((N_SLOTS, ROW_WORDS), jnp.int32),
        compiler_params=_cparams(("arbitrary",)),
        name="experts",
    )(bexp, nused, xs, wg, wu, wd)


def _sc_gather_rows(table, idx):
    n_rows = idx.shape[0]
    width = table.shape[1]
    workers = SC_CORES * SC_SUBCORES
    per_worker = n_rows // workers
    assert per_worker * workers == n_rows and per_worker % SC_CHUNK == 0
    mesh = plsc.VectorSubcoreMesh(core_axis_name="c", subcore_axis_name="s")

    n_chunks = per_worker // SC_CHUNK

    def body(table_hbm, idx_hbm, out_hbm, idx_v, buf0, buf1, gsem0, gsem1, wsem0, wsem1):
        wid = lax.axis_index("s") * SC_CORES + lax.axis_index("c")
        base = pl.multiple_of(wid * per_worker, SC_CHUNK)
        bufs, gsems, wsems = (buf0, buf1), (gsem0, gsem1), (wsem0, wsem1)
        pltpu.sync_copy(idx_hbm.at[pl.ds(base, per_worker)], idx_v)

        def gather(c):
            rows = idx_v.at[pl.ds(c * SC_CHUNK, SC_CHUNK)]
            return pltpu.async_copy(table_hbm.at[rows], bufs[c % 2], gsems[c % 2])

        def write(c):
            dst = out_hbm.at[pl.ds(base + c * SC_CHUNK, SC_CHUNK)]
            return pltpu.async_copy(bufs[c % 2], dst, wsems[c % 2])

        gathers = {0: gather(0)}
        writes = {}
        for c in range(n_chunks):
            if c + 1 < n_chunks:
                if c >= 1:
                    writes.pop(c - 1).wait()
                gathers[c + 1] = gather(c + 1)
            gathers.pop(c).wait()
            writes[c] = write(c)
        for c in sorted(writes):
            writes[c].wait()

    return pl.kernel(
        body,
        out_type=jax.ShapeDtypeStruct((n_rows, width), table.dtype),
        mesh=mesh,
        scratch_types=[
            pltpu.VMEM((per_worker,), jnp.int32),
            pltpu.VMEM((SC_CHUNK, width), table.dtype),
            pltpu.VMEM((SC_CHUNK, width), table.dtype),
            pltpu.SemaphoreType.DMA,
            pltpu.SemaphoreType.DMA,
            pltpu.SemaphoreType.DMA,
            pltpu.SemaphoreType.DMA,
        ],
        name="sc_gather_rows",
    )(table, idx)


def _sc_slot_tokens(dest_flat):
    n_assign = N_TOK * TOP_K
    lanes = SC_LANES
    mesh = plsc.VectorSubcoreMesh(core_axis_name="c", subcore_axis_name="s")

    def body(dest_hbm, out_hbm, dest_v, slot_v):
        wid = lax.axis_index("s") * SC_CORES + lax.axis_index("c")

        @pl.when(wid == 0)
        def _():
            pltpu.sync_copy(dest_hbm, dest_v)
            lane = lax.iota(jnp.int32, lanes)

            def init(i, _):
                start = pl.multiple_of(i * lanes, lanes)
                slot_v[pl.ds(start, lanes)] = (lane + start) & (N_TOK - 1)
                return 0

            def place(i, _):
                start = pl.multiple_of(i * lanes, lanes)
                slots = dest_v[pl.ds(start, lanes)]
                plsc.store_scatter(slot_v, [slots], (lane + start) >> 1)
                return 0

            lax.fori_loop(0, N_SLOTS // lanes, init, 0)
            lax.fori_loop(0, n_assign // lanes, place, 0)
            pltpu.sync_copy(slot_v, out_hbm)

    return pl.kernel(
        body,
        out_type=jax.ShapeDtypeStruct((N_SLOTS,), jnp.int32),
        mesh=mesh,
        scratch_types=[
            pltpu.VMEM((n_assign,), jnp.int32),
            pltpu.VMEM((N_SLOTS,), jnp.int32),
        ],
        compiler_params=pltpu.CompilerParams(needs_layout_passes=False),
        name="sc_slot_tokens",
    )(dest_flat)


def _mix_kernel(x_ref, wts_ref, mod_ref, fg_ref, y0_ref, y1_ref, o_ref, *, final):
    w = wts_ref[...]
    y0 = _unpack_bf16_pairs(lax.bitcast_convert_type(y0_ref[...], jnp.uint32))
    y1 = _unpack_bf16_pairs(lax.bitcast_convert_type(y1_ref[...], jnp.uint32))
    y = w[:, 0:1] * y0 + w[:, 1:2] * y1
    out = x_ref[...] + mod_ref[0][5:6] * y
    if final:
        ms = jnp.mean(out * out, axis=-1, keepdims=True)
        out = out * lax.rsqrt(ms + RMS_EPS) * fg_ref[...]
    o_ref[...] = out


def _mix(x1, wts, mod, fg, y2, final):
    nt = N_TOK // COMB_TM
    tiles_per_batch = SEQ // COMB_TM
    return pl.pallas_call(
        functools.partial(_mix_kernel, final=final),
        grid=(nt,),
        in_specs=[
            pl.BlockSpec((COMB_TM, D_MODEL), lambda i: (i, 0)),
            pl.BlockSpec((COMB_TM, TOP_K), lambda i: (i, 0)),
            pl.BlockSpec((1, 6, D_MODEL), lambda i: (i // tiles_per_batch, 0, 0)),
            pl.BlockSpec((1, D_MODEL), lambda i: (0, 0)),
            pl.BlockSpec((COMB_TM, ROW_WORDS), lambda i: (i, 0)),
            pl.BlockSpec((COMB_TM, ROW_WORDS), lambda i: (nt + i, 0)),
        ],
        out_specs=pl.BlockSpec((COMB_TM, D_MODEL), lambda i: (i, 0)),
        out_shape=jax.ShapeDtypeStruct((N_TOK, D_MODEL), F32),
        compiler_params=_cparams(("arbitrary",)),
        name="mix",
    )(x1, wts, mod, fg, y2, y2)


def _routing_tables(idr, cnt):
    counts = cnt[0, EXPERT_LANE0:EXPERT_LANE0 + N_EXPERTS].astype(jnp.int32)
    padded = (counts + MOE_BLK - 1) // MOE_BLK * MOE_BLK
    pad_ends = jnp.cumsum(padded)
    pad_starts = pad_ends - padded
    experts = jnp.arange(N_EXPERTS, dtype=jnp.int32)
    is_expert = idr[:, 0:2, None] == experts
    dest = jnp.sum(jnp.where(is_expert, pad_starts, 0), axis=-1) + idr[:, 2:4]
    block_start = jnp.arange(N_BLOCKS, dtype=jnp.int32) * MOE_BLK
    bexp = jnp.minimum(jnp.sum(block_start[:, None] >= pad_ends[None, :], axis=1),
                       N_EXPERTS - 1).astype(jnp.int32)
    nused = (pad_ends[-1:] // MOE_BLK).astype(jnp.int32)
    return dest.astype(jnp.int32), bexp, nused


def _router_weights(w_group, b_group, w_expert, b_expert):
    rw = jnp.zeros((D_MODEL, LANES), F32)
    rw = rw.at[:, :N_GROUPS].set(w_group)
    rw = rw.at[:, EXPERT_LANE0:EXPERT_LANE0 + N_EXPERTS].set(w_expert)
    rb = jnp.zeros((1, LANES), F32)
    rb = rb.at[0, :N_GROUPS].set(b_group)
    rb = rb.at[0, EXPERT_LANE0:EXPERT_LANE0 + N_EXPERTS].set(b_expert.reshape(-1))
    return rw.astype(BF16), rb


def _scaled_qkv_weight(w_in, head_dim):
    scale = jnp.concatenate([jnp.full((D_MODEL,), LOG2_E / math.sqrt(head_dim), F32),
                             jnp.ones((2 * D_MODEL,), F32)])
    return (w_in * scale).astype(BF16)


def kernel(x, c, norm1_g, norm2_g, ada_w, ada_b, diff_w_in, diff_w_out, diff_lambda_q1, diff_lambda_k1, diff_lambda_q2, diff_lambda_k2, diff_subln_g, sb_w_in, sb_w_out, router_group_w, router_group_b, router_expert_w, router_expert_b, expert_w_gate, expert_w_up, expert_w_down, final_norm_g):
    xf = x.reshape(N_TOK, D_MODEL)
    mod_all = _ada(c, ada_w, ada_b)
    slopes = jnp.exp2(-8.0 * jnp.arange(1, DIFF_HEADS + 1, dtype=F32) / DIFF_HEADS)

    for i in range(DEPTH):
        mod = mod_all[i].reshape(BATCH, 6, D_MODEL)
        j = i // 2
        if i % 2 == 0:
            lambda_init = 0.8 - 0.6 * math.exp(-0.3 * i)
            lam = (jnp.exp(jnp.sum(diff_lambda_q1[j] * diff_lambda_k1[j]))
                   - jnp.exp(jnp.sum(diff_lambda_q2[j] * diff_lambda_k2[j]))
                   + lambda_init).reshape(1, 1)
            qkv = _qkv(xf, mod, norm1_g[i].reshape(1, D_MODEL),
                       _scaled_qkv_weight(diff_w_in[j], DIFF_HEAD_DIM))
            o = _diff_attn(qkv, slopes, lam, diff_subln_g[j].reshape(1, LANES), lambda_init)
            w_out = diff_w_out[j]
        else:
            qkv = _qkv(xf, mod, norm1_g[i].reshape(1, D_MODEL),
                       _scaled_qkv_weight(sb_w_in[j], SB_HEAD_DIM))
            o = _sb_attn(qkv)
            w_out = sb_w_out[j]
        rw, rb = _router_weights(router_group_w[i], router_group_b[i],
                                 router_expert_w[i], router_expert_b[i])
        x1, h2, idr, wts, cnt = _out_router(o, xf, mod, norm2_g[i].reshape(1, D_MODEL),
                                            w_out.astype(BF16), rw, rb)
        dest, bexp, nused = _routing_tables(idr, cnt)
        xs = _sc_gather_rows(h2, _sc_slot_tokens(dest.reshape(-1)))
        ys = _experts(i, bexp, nused, xs, expert_w_gate, expert_w_up, expert_w_down)
        y2 = _sc_gather_rows(ys, dest.T.reshape(-1))
        xf = _mix(x1, wts, mod, final_norm_g.reshape(1, D_MODEL), y2,
                  final=(i == DEPTH - 1))
    return xf.reshape(BATCH, SEQ, D_MODEL)
```

```python
import functools
import math

import jax
import jax.numpy as jnp
from jax import lax
from jax.experimental import pallas as pl
from jax.experimental.pallas import tpu as pltpu
from jax.experimental.pallas import tpu_sc as plsc

D_MODEL = 1024
BATCH = 8
SEQ = 2048
DEPTH = 2
N_TOK = BATCH * SEQ

CHUNK = 64
DIFF_HEADS = 8
DIFF_HEAD_DIM = D_MODEL // (2 * DIFF_HEADS)
SB_HEADS = 16
SB_HEAD_DIM = D_MODEL // SB_HEADS
N_GROUPS = 4
EXPERTS_PER_GROUP = 8
N_EXPERTS = N_GROUPS * EXPERTS_PER_GROUP
TOP_K = 2
EXPERT_HIDDEN = D_MODEL // 2
RMS_EPS = 1e-6
SUBLN_EPS = 1e-5

LANES = 128
EXPERT_LANE0 = 32
EXP2_UNDERFLOW = -150.0
LOG2_E = math.log2(math.e)

ADA_TN = 1536
QKV_TM = 512
DIFF_T = 512
SB_T = 256
OUT_TM = 512
DISP_TM = 1024
MOE_BLK = 512
N_SLOTS = N_TOK * TOP_K + N_EXPERTS * MOE_BLK
N_BLOCKS = N_SLOTS // MOE_BLK
COMB_TM = 512
DMA_UNROLL = 8
ROW_WORDS = D_MODEL // 2

SC_CORES = 2
SC_SUBCORES = 16
SC_LANES = 16
SC_CHUNK = 64

VMEM_LIMIT = 56 * 1024 * 1024

F32 = jnp.float32
BF16 = jnp.bfloat16


def _cparams(sem):
    return pltpu.CompilerParams(dimension_semantics=sem, vmem_limit_bytes=VMEM_LIMIT)


def _ada_kernel(c_ref, w_ref, b_ref, o_ref):
    c = c_ref[...]
    cond = c * jax.nn.sigmoid(c)
    o_ref[0] = jnp.dot(cond, w_ref[0], preferred_element_type=F32,
                       precision=lax.Precision.HIGHEST) + b_ref[0]


def _ada(c, ada_w, ada_b):
    six_d = ada_w.shape[-1]
    return pl.pallas_call(
        _ada_kernel,
        grid=(DEPTH, six_d // ADA_TN),
        in_specs=[
            pl.BlockSpec((BATCH, D_MODEL), lambda l, n: (0, 0)),
            pl.BlockSpec((1, D_MODEL, ADA_TN), lambda l, n: (l, 0, n)),
            pl.BlockSpec((1, 1, ADA_TN), lambda l, n: (l, 0, n)),
        ],
        out_specs=pl.BlockSpec((1, BATCH, ADA_TN), lambda l, n: (l, 0, n)),
        out_shape=jax.ShapeDtypeStruct((DEPTH, BATCH, six_d), F32),
        compiler_params=_cparams(("arbitrary", "arbitrary")),
        name="ada",
    )(c, ada_w, ada_b.reshape(DEPTH, 1, six_d))


def _modulated_norm(x, g, scale, shift):
    ms = jnp.mean(x * x, axis=-1, keepdims=True)
    return x * lax.rsqrt(ms + RMS_EPS) * (g * (1.0 + scale)) + shift


def _qkv_kernel(x_ref, mod_ref, g_ref, w_ref, o_ref):
    m = mod_ref[0]
    h = _modulated_norm(x_ref[...], g_ref[...], m[1:2], m[0:1]).astype(BF16)
    for n in range(3):
        cols = slice(n * D_MODEL, (n + 1) * D_MODEL)
        o_ref[:, cols] = jnp.dot(h, w_ref[:, cols], preferred_element_type=F32).astype(BF16)


def _qkv(x, mod, g, w):
    tiles_per_batch = SEQ // QKV_TM
    return pl.pallas_call(
        _qkv_kernel,
        grid=(N_TOK // QKV_TM,),
        in_specs=[
            pl.BlockSpec((QKV_TM, D_MODEL), lambda i: (i, 0)),
            pl.BlockSpec((1, 6, D_MODEL), lambda i: (i // tiles_per_batch, 0, 0)),
            pl.BlockSpec((1, D_MODEL), lambda i: (0, 0)),
            pl.BlockSpec((D_MODEL, 3 * D_MODEL), lambda i: (0, 0)),
        ],
        out_specs=pl.BlockSpec((QKV_TM, 3 * D_MODEL), lambda i: (i, 0)),
        out_shape=jax.ShapeDtypeStruct((N_TOK, 3 * D_MODEL), BF16),
        compiler_params=_cparams(("arbitrary",)),
        name="qkv",
    )(x, mod, g, w)


def _half_masked(q, upper):
    lane = lax.broadcasted_iota(jnp.int32, q.shape, 1)
    keep = (lane >= LANES // 2) if upper else (lane < LANES // 2)
    return jnp.where(keep, q, jnp.zeros_like(q))


def _pack_bf16_pairs(x):
    half = x.shape[1] // 2
    hi = lax.bitcast_convert_type(x[:, :half].astype(BF16).astype(F32), jnp.uint32)
    lo = lax.bitcast_convert_type(x[:, half:].astype(BF16).astype(F32), jnp.uint32)
    return hi | (lo >> 16)


def _unpack_bf16_pairs(u):
    hi = lax.bitcast_convert_type(u & jnp.uint32(0xFFFF0000), F32)
    lo = lax.bitcast_convert_type(u << 16, F32)
    return jnp.concatenate([hi, lo], axis=1)


def _lane_tile(x, n):
    return jnp.concatenate([x] * n, axis=1)


def _qk(q, k):
    return lax.dot_general(q, k, (((1,), (1,)), ((), ())), preferred_element_type=F32)


def _diff_attn_kernel(slopes_ref, lam_ref, q_ref, k_ref, v_ref, g_ref, o_ref,
                      m0_ref, m1_ref, acc0_ref, acc1_ref, *, lambda_init):
    slope = slopes_ref[pl.program_id(1)] * LOG2_E
    m_refs = (m0_ref, m1_ref)
    acc_refs = (acc0_ref, acc1_ref)
    ones = jnp.ones((DIFF_T, LANES), BF16)

    row = lax.broadcasted_iota(jnp.int32, (DIFF_T, DIFF_T), 0)
    col = lax.broadcasted_iota(jnp.int32, (DIFF_T, DIFF_T), 1)
    rel = (row - col).astype(F32) * slope
    allowed = (col // CHUNK) <= (row // CHUNK)
    diag_bias = jnp.where(allowed, jnp.abs(rel), jnp.inf)
    out_gain = g_ref[...] * (1.0 - lambda_init)

    def query_tile(qi, _):
        q = q_ref[pl.ds(qi * DIFF_T, DIFF_T), :]
        qm = (_half_masked(q, False), _half_masked(q, True))

        k = k_ref[pl.ds(qi * DIFF_T, DIFF_T), :]
        v1 = jnp.concatenate([v_ref[pl.ds(qi * DIFF_T, DIFF_T), :], ones], axis=1)
        s_maps = [_qk(qm[mi], k) - diag_bias for mi in range(2)]
        for mi in range(2):
            m = jnp.max(s_maps[mi], axis=-1, keepdims=True)
            p = jnp.exp2(s_maps[mi] - m)
            m_refs[mi][...] = jnp.broadcast_to(m, (DIFF_T, LANES))
            acc_refs[mi][...] = jnp.dot(p.astype(BF16), v1, preferred_element_type=F32)

        def body(j, _):
            k = k_ref[pl.ds(j * DIFF_T, DIFF_T), :]
            v1 = jnp.concatenate([v_ref[pl.ds(j * DIFF_T, DIFF_T), :], ones], axis=1)
            shift = slope * jnp.asarray((qi - j) * DIFF_T).astype(F32)
            s_maps = [_qk(qm[mi], k) - rel for mi in range(2)]
            for mi in range(2):
                m = m_refs[mi][...]
                m_new = jnp.maximum(m, jnp.max(s_maps[mi], axis=-1, keepdims=True) - shift)
                alpha = jnp.exp2(m - m_new)
                p = jnp.exp2(s_maps[mi] - _lane_tile(m_new + shift, DIFF_T // LANES))
                m_refs[mi][...] = m_new
                acc_refs[mi][...] = _lane_tile(alpha, 2) * acc_refs[mi][...] + jnp.dot(
                    p.astype(BF16), v1, preferred_element_type=F32)
            return 0

        lax.fori_loop(0, qi, body, 0)
        a0 = acc0_ref[...]
        a1 = acc1_ref[...]
        o = a0[:, :LANES] / a0[:, LANES:] - lam_ref[...] * (a1[:, :LANES] / a1[:, LANES:])
        ms = jnp.mean(o * o, axis=-1, keepdims=True)
        o = o * lax.rsqrt(ms + SUBLN_EPS) * out_gain
        o_ref[pl.ds(qi * DIFF_T, DIFF_T), :] = o.astype(BF16)
        return 0

    lax.fori_loop(0, SEQ // DIFF_T, query_tile, 0)


def _diff_attn(qkv, slopes, lam, subln_g, lambda_init):
    return pl.pallas_call(
        functools.partial(_diff_attn_kernel, lambda_init=lambda_init),
        grid=(BATCH, DIFF_HEADS),
        in_specs=[
            pl.BlockSpec(memory_space=pltpu.SMEM),
            pl.BlockSpec((1, 1), lambda b, h: (0, 0)),
            pl.BlockSpec((SEQ, LANES), lambda b, h: (b, h)),
            pl.BlockSpec((SEQ, LANES), lambda b, h: (b, DIFF_HEADS + h)),
            pl.BlockSpec((SEQ, LANES), lambda b, h: (b, 2 * DIFF_HEADS + h)),
            pl.BlockSpec((1, LANES), lambda b, h: (0, 0)),
        ],
        out_specs=pl.BlockSpec((SEQ, LANES), lambda b, h: (b, h)),
        out_shape=jax.ShapeDtypeStruct((N_TOK, D_MODEL), BF16),
        scratch_shapes=[
            pltpu.VMEM((DIFF_T, LANES), F32),
            pltpu.VMEM((DIFF_T, LANES), F32),
            pltpu.VMEM((DIFF_T, 2 * LANES), F32),
            pltpu.VMEM((DIFF_T, 2 * LANES), F32),
        ],
        compiler_params=_cparams(("arbitrary", "arbitrary")),
        name="diff_attn",
    )(slopes, lam, qkv, qkv, qkv, subln_g)


def _sb_attn_kernel(q_ref, k_ref, v_ref, o_ref, tail0_ref, tail1_ref, acc0_ref, acc1_ref):
    tail_refs = (tail0_ref, tail1_ref)
    acc_refs = (acc0_ref, acc1_ref)

    row = lax.broadcasted_iota(jnp.int32, (SB_T, SB_T), 0)
    col = lax.broadcasted_iota(jnp.int32, (SB_T, SB_T), 1)
    strict = col < row
    neg_from = jnp.where(row >= col, -1.0, 0.0).astype(BF16)

    def scores(j, qhead, mask):
        z = _qk(qhead, k_ref[pl.ds(j * SB_T, SB_T), :])
        sp = jnp.maximum(z, 0.0) + jnp.log2(1.0 + jnp.exp2(-jnp.abs(z)))
        if mask is not None:
            sp = jnp.where(mask, sp, 0.0)
        return z, sp

    def weighted(j, z, sp, tail, mask):
        log_a = (z + jnp.dot(sp.astype(BF16), neg_from, preferred_element_type=F32)
                 + _lane_tile(tail, SB_T // LANES))
        a = jnp.exp2(log_a)
        if mask is not None:
            a = jnp.where(mask, a, 0.0)
        return jnp.dot(a.astype(BF16), v_ref[pl.ds(j * SB_T, SB_T), :],
                       preferred_element_type=F32)

    def row_sum(sp):
        return jnp.broadcast_to(jnp.sum(sp, axis=-1, keepdims=True), (SB_T, LANES))

    zero = jnp.zeros((SB_T, LANES), F32)
    lane = lax.broadcasted_iota(jnp.int32, (SB_T, LANES), 1)

    def query_heads(qi):
        q = q_ref[pl.ds(qi * SB_T, SB_T), :]
        return (_half_masked(q, False), _half_masked(q, True))

    def finish(qi):
        o_ref[pl.ds(qi * SB_T, SB_T), :] = jnp.where(
            lane < LANES // 2, acc0_ref[...], acc1_ref[...]).astype(BF16)

    qh = query_heads(0)
    for hh in range(2):
        z, sp = scores(0, qh[hh], strict)
        acc_refs[hh][...] = weighted(0, z, sp, zero, strict)
    finish(0)

    def query_tile(qi, _):
        qh = query_heads(qi)
        zs = [(scores(qi, qh[hh], strict), scores(qi - 1, qh[hh], None)) for hh in range(2)]
        for hh in range(2):
            (z0, sp0), (z1, sp1) = zs[hh]
            tail0 = -row_sum(sp0)
            acc_refs[hh][...] = (weighted(qi, z0, sp0, zero, strict)
                                 + weighted(qi - 1, z1, sp1, tail0, None))
            tail_refs[hh][...] = tail0 - row_sum(sp1)

        def cond(state):
            j, live = state
            return (j >= 0) & (live > EXP2_UNDERFLOW)

        def body(state):
            j, _ = state
            live = jnp.float32(-jnp.inf)
            for hh in range(2):
                z, sp = scores(j, qh[hh], None)
                tail = tail_refs[hh][...]
                acc_refs[hh][...] += weighted(j, z, sp, tail, None)
                tail = tail - row_sum(sp)
                tail_refs[hh][...] = tail
                live = jnp.maximum(live, jnp.max(tail))
            return j - 1, live

        live = jnp.maximum(jnp.max(tail0_ref[...]), jnp.max(tail1_ref[...]))
        lax.while_loop(cond, body, (qi - 2, live))
        finish(qi)
        return 0

    lax.fori_loop(1, SEQ // SB_T, query_tile, 0)


def _sb_attn(qkv):
    pairs = SB_HEADS // 2
    return pl.pallas_call(
        _sb_attn_kernel,
        grid=(BATCH, pairs),
        in_specs=[
            pl.BlockSpec((SEQ, LANES), lambda b, h: (b, h)),
            pl.BlockSpec((SEQ, LANES), lambda b, h: (b, pairs + h)),
            pl.BlockSpec((SEQ, LANES), lambda b, h: (b, 2 * pairs + h)),
        ],
        out_specs=pl.BlockSpec((SEQ, LANES), lambda b, h: (b, h)),
        out_shape=jax.ShapeDtypeStruct((N_TOK, D_MODEL), BF16),
        scratch_shapes=[
            pltpu.VMEM((SB_T, LANES), F32),
            pltpu.VMEM((SB_T, LANES), F32),
            pltpu.VMEM((SB_T, LANES), F32),
            pltpu.VMEM((SB_T, LANES), F32),
        ],
        compiler_params=_cparams(("arbitrary", "arbitrary")),
        name="sb_attn",
    )(qkv, qkv, qkv)


def _out_router_kernel(o_ref, x_ref, mod_ref, g_ref, w_ref, rw_ref, rb_ref,
                       x1_ref, h2_ref, idr_ref, wts_ref, cnt_ref, tri_ref, base_ref):
    i = pl.program_id(0)

    @pl.when(i == 0)
    def _():
        r = lax.broadcasted_iota(jnp.int32, (OUT_TM, OUT_TM), 0)
        c = lax.broadcasted_iota(jnp.int32, (OUT_TM, OUT_TM), 1)
        tri_ref[...] = jnp.where(c < r, 1.0, 0.0).astype(BF16)
        base_ref[...] = jnp.zeros_like(base_ref)

    m = mod_ref[0]
    y = jnp.dot(o_ref[...], w_ref[...], preferred_element_type=F32)
    x1 = x_ref[...] + m[2:3] * y
    x1_ref[...] = x1
    h2 = _modulated_norm(x1, g_ref[...], m[4:5], m[3:4])
    h2_ref[...] = lax.bitcast_convert_type(_pack_bf16_pairs(h2), jnp.int32)

    logits = jnp.dot(h2.astype(BF16), rw_ref[...], preferred_element_type=F32) + rb_ref[...]
    lane = lax.broadcasted_iota(jnp.int32, logits.shape, 1).astype(F32)
    neg_inf = jnp.float32(-jnp.inf)
    big = jnp.float32(1e9)

    is_group = lane < N_GROUPS
    gl = jnp.where(is_group, logits, neg_inf)
    gmax = jnp.max(gl, axis=-1, keepdims=True)
    gidx = jnp.min(jnp.where(gl == gmax, lane, big), axis=-1, keepdims=True)
    gsum = jnp.sum(jnp.where(is_group, jnp.exp(logits - gmax), 0.0), axis=-1, keepdims=True)
    g_w = 1.0 / gsum

    lo = EXPERT_LANE0 + EXPERTS_PER_GROUP * gidx
    in_group = (lane >= lo) & (lane < lo + EXPERTS_PER_GROUP)
    el = jnp.where(in_group, logits, neg_inf)
    v0 = jnp.max(el, axis=-1, keepdims=True)
    i0 = jnp.min(jnp.where(el == v0, lane, big), axis=-1, keepdims=True)
    el = jnp.where(lane == i0, neg_inf, el)
    v1 = jnp.max(el, axis=-1, keepdims=True)
    i1 = jnp.min(jnp.where(el == v1, lane, big), axis=-1, keepdims=True)
    t = jnp.exp(v1 - v0)
    w0 = g_w / (1.0 + t)
    w1 = g_w * t / (1.0 + t)

    oh0 = jnp.where(lane == i0, 1.0, 0.0)
    oh1 = jnp.where(lane == i1, 1.0, 0.0)
    both = oh0 + oh1
    before = jnp.dot(tri_ref[...], both.astype(BF16), preferred_element_type=F32) + base_ref[...]
    r0 = jnp.sum(before * oh0, axis=-1, keepdims=True)
    r1 = jnp.sum(before * oh1, axis=-1, keepdims=True)
    base_ref[...] = base_ref[...] + jnp.sum(both, axis=0, keepdims=True)
    cnt_ref[...] = base_ref[...]

    e0 = i0 - EXPERT_LANE0
    e1 = i1 - EXPERT_LANE0
    idr = jnp.where(lane == 0, e0, jnp.where(lane == 1, e1, jnp.where(lane == 2, r0, r1)))
    idr_ref[...] = idr[:, :4].astype(jnp.int32)
    wts_ref[...] = jnp.where(lane == 0, w0, w1)[:, :2]


def _out_router(o, x, mod, g, w, rw, rb):
    tiles_per_batch = SEQ // OUT_TM
    return pl.pallas_call(
        _out_router_kernel,
        grid=(N_TOK // OUT_TM,),
        in_specs=[
            pl.BlockSpec((OUT_TM, D_MODEL), lambda i: (i, 0)),
            pl.BlockSpec((OUT_TM, D_MODEL), lambda i: (i, 0)),
            pl.BlockSpec((1, 6, D_MODEL), lambda i: (i // tiles_per_batch, 0, 0)),
            pl.BlockSpec((1, D_MODEL), lambda i: (0, 0)),
            pl.BlockSpec((D_MODEL, D_MODEL), lambda i: (0, 0)),
            pl.BlockSpec((D_MODEL, LANES), lambda i: (0, 0)),
            pl.BlockSpec((1, LANES), lambda i: (0, 0)),
        ],
        out_specs=[
            pl.BlockSpec((OUT_TM, D_MODEL), lambda i: (i, 0)),
            pl.BlockSpec((OUT_TM, ROW_WORDS), lambda i: (i, 0)),
            pl.BlockSpec((OUT_TM, 4), lambda i: (i, 0)),
            pl.BlockSpec((OUT_TM, 2), lambda i: (i, 0)),
            pl.BlockSpec((1, LANES), lambda i: (0, 0)),
        ],
        out_shape=[
            jax.ShapeDtypeStruct((N_TOK, D_MODEL), F32),
            jax.ShapeDtypeStruct((N_TOK, ROW_WORDS), jnp.int32),
            jax.ShapeDtypeStruct((N_TOK, 4), jnp.int32),
            jax.ShapeDtypeStruct((N_TOK, 2), F32),
            jax.ShapeDtypeStruct((1, LANES), F32),
        ],
        scratch_shapes=[
            pltpu.VMEM((OUT_TM, OUT_TM), BF16),
            pltpu.VMEM((1, LANES), F32),
        ],
        compiler_params=_cparams(("arbitrary",)),
        name="out_router",
    )(o, x, mod, g, w, rw, rb)


def _dispatch_kernel(dest_ref, pad_end_ref, padded_ref, h_ref, xs_ref, zeros_ref, sem, zsem):
    @pl.when(pl.program_id(0) == 0)
    def _():
        zeros_ref[...] = jnp.zeros_like(zeros_ref)

        def tail_copy(e):
            start = pl.multiple_of(jnp.maximum(pad_end_ref[e] - MOE_BLK, 0), MOE_BLK)
            return pltpu.make_async_copy(zeros_ref, xs_ref.at[pl.ds(start, MOE_BLK)], zsem)

        def fill(e, _):
            @pl.when(padded_ref[e] > 0)
            def _():
                tail_copy(e).start()
            return 0

        def fill_done(e, _):
            @pl.when(padded_ref[e] > 0)
            def _():
                tail_copy(e).wait()
            return 0

        def block_copy(b):
            start = pl.multiple_of(b * MOE_BLK, MOE_BLK)
            return pltpu.make_async_copy(zeros_ref, xs_ref.at[pl.ds(start, MOE_BLK)], zsem)

        def fill_unused(b, _):
            block_copy(b).start()
            return 0

        def fill_unused_done(b, _):
            block_copy(b).wait()
            return 0

        first_unused = pad_end_ref[N_EXPERTS - 1] // MOE_BLK
        lax.fori_loop(0, N_EXPERTS, fill, 0)
        lax.fori_loop(first_unused, N_BLOCKS, fill_unused, 0)
        lax.fori_loop(0, N_EXPERTS, fill_done, 0)
        lax.fori_loop(first_unused, N_BLOCKS, fill_unused_done, 0)

    def row_copy(r, d):
        return pltpu.make_async_copy(h_ref.at[pl.ds(r, 1)], xs_ref.at[pl.ds(d, 1)], sem)

    def issue(r, _):
        for k in range(TOP_K):
            row_copy(r, dest_ref[0, 0, TOP_K * r + k]).start()
        return 0

    def drain(r, _):
        for k in range(TOP_K):
            row_copy(0, 0).wait()
        return 0

    lax.fori_loop(0, DISP_TM, issue, 0, unroll=DMA_UNROLL)
    lax.fori_loop(0, DISP_TM, drain, 0, unroll=DMA_UNROLL)


def _dispatch(dest, pad_ends, padded, h2):
    nt = N_TOK // DISP_TM
    return pl.pallas_call(
        _dispatch_kernel,
        grid=(nt,),
        in_specs=[
            pl.BlockSpec((1, 1, TOP_K * DISP_TM), lambda i: (i, 0, 0), memory_space=pltpu.SMEM),
            pl.BlockSpec(memory_space=pltpu.SMEM),
            pl.BlockSpec(memory_space=pltpu.SMEM),
            pl.BlockSpec((DISP_TM, ROW_WORDS), lambda i: (i, 0)),
        ],
        out_specs=pl.BlockSpec(memory_space=pl.ANY),
        out_shape=jax.ShapeDtypeStruct((N_SLOTS, ROW_WORDS), jnp.uint32),
        scratch_shapes=[
            pltpu.VMEM((MOE_BLK, ROW_WORDS), jnp.uint32),
            pltpu.SemaphoreType.DMA(()),
            pltpu.SemaphoreType.DMA(()),
        ],
        compiler_params=_cparams(("arbitrary",)),
        name="dispatch",
    )(dest.reshape(nt, 1, TOP_K * DISP_TM), pad_ends, padded, h2)


def _expert_kernel(bexp_ref, nused_ref, xs_ref, wg_ref, wu_ref, wd_ref, ys_ref,
                   wg_bf, wu_bf, wd_bf):
    i = pl.program_id(0)

    @pl.when(i < nused_ref[0])
    def _():
        e = bexp_ref[i]
        prev = bexp_ref[jnp.maximum(i - 1, 0)]

        @pl.when((i == 0) | (e != prev))
        def _():
            wg_bf[...] = wg_ref[0, 0].astype(BF16)
            wu_bf[...] = wu_ref[0, 0].astype(BF16)
            wd_bf[...] = wd_ref[0, 0].astype(BF16)

        x = _unpack_bf16_pairs(lax.bitcast_convert_type(xs_ref[...], jnp.uint32)).astype(BF16)
        g = jnp.dot(x, wg_bf[...], preferred_element_type=F32)
        u = jnp.dot(x, wu_bf[...], preferred_element_type=F32)
        hid = (g * jax.nn.sigmoid(g)) * u
        y = jnp.dot(hid.astype(BF16), wd_bf[...], preferred_element_type=F32)
        ys_ref[...] = lax.bitcast_convert_type(_pack_bf16_pairs(y), jnp.int32)

    @pl.when(i >= nused_ref[0])
    def _():
        ys_ref[...] = jnp.zeros_like(ys_ref)


def _experts(layer, bexp, nused, xs, wg, wu, wd):
    def in_row_map(i, bexp, nused):
        return (jnp.minimum(i, nused[0] - 1), 0)

    def row_map(i, bexp, nused):
        return (i, 0)

    def w_map(i, bexp, nused):
        return (layer, bexp[jnp.minimum(i, nused[0] - 1)], 0, 0)

    return pl.pallas_call(
        _expert_kernel,
        grid_spec=pltpu.PrefetchScalarGridSpec(
            num_scalar_prefetch=2,
            grid=(N_BLOCKS,),
            in_specs=[
                pl.BlockSpec((MOE_BLK, ROW_WORDS), in_row_map),
                pl.BlockSpec((1, 1, D_MODEL, EXPERT_HIDDEN), w_map),
                pl.BlockSpec((1, 1, D_MODEL, EXPERT_HIDDEN), w_map),
                pl.BlockSpec((1, 1, EXPERT_HIDDEN, D_MODEL), w_map),
            ],
            out_specs=pl.BlockSpec((MOE_BLK, ROW_WORDS), row_map),
            scratch_shapes=[
                pltpu.VMEM((D_MODEL, EXPERT_HIDDEN), BF16),
                pltpu.VMEM((D_MODEL, EXPERT_HIDDEN), BF16),
                pltpu.VMEM((EXPERT_HIDDEN, D_MODEL), BF16),
            ],
        ),
        out_shape=jax.ShapeDtypeStruct((N_SLOTS, ROW_WORDS), jnp.int32),
        compiler_params=_cparams(("arbitrary",)),
        name="experts",
    )(bexp, nused, xs, wg, wu, wd)


def _sc_gather_rows(table, idx):
    n_rows = idx.shape[0]
    width = table.shape[1]
    workers = SC_CORES * SC_SUBCORES
    per_worker = n_rows // workers
    assert per_worker * workers == n_rows and per_worker % SC_CHUNK == 0
    mesh = plsc.VectorSubcoreMesh(core_axis_name="c", subcore_axis_name="s")

    n_chunks = per_worker // SC_CHUNK

    def body(table_hbm, idx_hbm, out_hbm, idx_v, buf0, buf1, gsem0, gsem1, wsem0, wsem1):
        wid = lax.axis_index("s") * SC_CORES + lax.axis_index("c")
        base = pl.multiple_of(wid * per_worker, SC_CHUNK)
        bufs, gsems, wsems = (buf0, buf1), (gsem0, gsem1), (wsem0, wsem1)
        pltpu.sync_copy(idx_hbm.at[pl.ds(base, per_worker)], idx_v)

        def gather(c):
            rows = idx_v.at[pl.ds(c * SC_CHUNK, SC_CHUNK)]
            return pltpu.async_copy(table_hbm.at[rows], bufs[c % 2], gsems[c % 2])

        def write(c):
            dst = out_hbm.at[pl.ds(base + c * SC_CHUNK, SC_CHUNK)]
            return pltpu.async_copy(bufs[c % 2], dst, wsems[c % 2])

        gathers = {0: gather(0)}
        writes = {}
        for c in range(n_chunks):
            if c + 1 < n_chunks:
                if c >= 1:
                    writes.pop(c - 1).wait()
                gathers[c + 1] = gather(c + 1)
            gathers.pop(c).wait()
            writes[c] = write(c)
        for c in sorted(writes):
            writes[c].wait()

    return pl.kernel(
        body,
        out_type=jax.ShapeDtypeStruct((n_rows, width), table.dtype),
        mesh=mesh,
        scratch_types=[
            pltpu.VMEM((per_worker,), jnp.int32),
            pltpu.VMEM((SC_CHUNK, width), table.dtype),
            pltpu.VMEM((SC_CHUNK, width), table.dtype),
            pltpu.SemaphoreType.DMA,
            pltpu.SemaphoreType.DMA,
            pltpu.SemaphoreType.DMA,
            pltpu.SemaphoreType.DMA,
        ],
        name="sc_gather_rows",
    )(table, idx)


def _sc_slot_tokens(dest_flat):
    n_assign = N_TOK * TOP_K
    lanes = SC_LANES
    mesh = plsc.VectorSubcoreMesh(core_axis_name="c", subcore_axis_name="s")

    def body(dest_hbm, out_hbm, dest_v, slot_v):
        wid = lax.axis_index("s") * SC_CORES + lax.axis_index("c")

        @pl.when(wid == 0)
        def _():
            pltpu.sync_copy(dest_hbm, dest_v)
            lane = lax.iota(jnp.int32, lanes)

            def init(i, _):
                start = pl.multiple_of(i * lanes, lanes)
                slot_v[pl.ds(start, lanes)] = (lane + start) & (N_TOK - 1)
                return 0

            def place(i, _):
                start = pl.multiple_of(i * lanes, lanes)
                slots = dest_v[pl.ds(start, lanes)]
                plsc.store_scatter(slot_v, [slots], (lane + start) >> 1)
                return 0

            lax.fori_loop(0, N_SLOTS // lanes, init, 0)
            lax.fori_loop(0, n_assign // lanes, place, 0)
            pltpu.sync_copy(slot_v, out_hbm)

    return pl.kernel(
        body,
        out_type=jax.ShapeDtypeStruct((N_SLOTS,), jnp.int32),
        mesh=mesh,
        scratch_types=[
            pltpu.VMEM((n_assign,), jnp.int32),
            pltpu.VMEM((N_SLOTS,), jnp.int32),
        ],
        compiler_params=pltpu.CompilerParams(needs_layout_passes=False),
        name="sc_slot_tokens",
    )(dest_flat)


def _mix_kernel(x_ref, wts_ref, mod_ref, fg_ref, y0_ref, y1_ref, o_ref, *, final):
    w = wts_ref[...]
    y0 = _unpack_bf16_pairs(lax.bitcast_convert_type(y0_ref[...], jnp.uint32))
    y1 = _unpack_bf16_pairs(lax.bitcast_convert_type(y1_ref[...], jnp.uint32))
    y = w[:, 0:1] * y0 + w[:, 1:2] * y1
    out = x_ref[...] + mod_ref[0][5:6] * y
    if final:
        ms = jnp.mean(out * out, axis=-1, keepdims=True)
        out = out * lax.rsqrt(ms + RMS_EPS) * fg_ref[...]
    o_ref[...] = out


def _mix(x1, wts, mod, fg, y2, final):
    nt = N_TOK // COMB_TM
    tiles_per_batch = SEQ // COMB_TM
    return pl.pallas_call(
        functools.partial(_mix_kernel, final=final),
        grid=(nt,),
        in_specs=[
            pl.BlockSpec((COMB_TM, D_MODEL), lambda i: (i, 0)),
            pl.BlockSpec((COMB_TM, TOP_K), lambda i: (i, 0)),
            pl.BlockSpec((1, 6, D_MODEL), lambda i: (i // tiles_per_batch, 0, 0)),
            pl.BlockSpec((1, D_MODEL), lambda i: (0, 0)),
            pl.BlockSpec((COMB_TM, ROW_WORDS), lambda i: (i, 0)),
            pl.BlockSpec((COMB_TM, ROW_WORDS), lambda i: (nt + i, 0)),
        ],
        out_specs=pl.BlockSpec((COMB_TM, D_MODEL), lambda i: (i, 0)),
        out_shape=jax.ShapeDtypeStruct((N_TOK, D_MODEL), F32),
        compiler_params=_cparams(("arbitrary",)),
        name="mix",
    )(x1, wts, mod, fg, y2, y2)


def _routing_tables(idr, cnt):
    counts = cnt[0, EXPERT_LANE0:EXPERT_LANE0 + N_EXPERTS].astype(jnp.int32)
    padded = (counts + MOE_BLK - 1) // MOE_BLK * MOE_BLK
    pad_ends = jnp.cumsum(padded)
    pad_starts = pad_ends - padded
    experts = jnp.arange(N_EXPERTS, dtype=jnp.int32)
    is_expert = idr[:, 0:2, None] == experts
    dest = jnp.sum(jnp.where(is_expert, pad_starts, 0), axis=-1) + idr[:, 2:4]
    block_start = jnp.arange(N_BLOCKS, dtype=jnp.int32) * MOE_BLK
    bexp = jnp.minimum(jnp.sum(block_start[:, None] >= pad_ends[None, :], axis=1),
                       N_EXPERTS - 1).astype(jnp.int32)
    nused = (pad_ends[-1:] // MOE_BLK).astype(jnp.int32)
    return dest.astype(jnp.int32), bexp, nused


def _router_weights(w_group, b_group, w_expert, b_expert):
    rw = jnp.zeros((D_MODEL, LANES), F32)
    rw = rw.at[:, :N_GROUPS].set(w_group)
    rw = rw.at[:, EXPERT_LANE0:EXPERT_LANE0 + N_EXPERTS].set(w_expert)
    rb = jnp.zeros((1, LANES), F32)
    rb = rb.at[0, :N_GROUPS].set(b_group)
    rb = rb.at[0, EXPERT_LANE0:EXPERT_LANE0 + N_EXPERTS].set(b_expert.reshape(-1))
    return rw.astype(BF16), rb


def _scaled_qkv_weight(w_in, head_dim):
    scale = jnp.concatenate([jnp.full((D_MODEL,), LOG2_E / math.sqrt(head_dim), F32),
                             jnp.ones((2 * D_MODEL,), F32)])
    return (w_in * scale).astype(BF16)


def kernel(x, c, norm1_g, norm2_g, ada_w, ada_b, diff_w_in, diff_w_out, diff_lambda_q1, diff_lambda_k1, diff_lambda_q2, diff_lambda_k2, diff_subln_g, sb_w_in, sb_w_out, router_group_w, router_group_b, router_expert_w, router_expert_b, expert_w_gate, expert_w_up, expert_w_down, final_norm_g):
    xf = x.reshape(N_TOK, D_MODEL)
    mod_all = _ada(c, ada_w, ada_b)
    slopes = jnp.exp2(-8.0 * jnp.arange(1, DIFF_HEADS + 1, dtype=F32) / DIFF_HEADS)

    for i in range(DEPTH):
        mod = mod_all[i].reshape(BATCH, 6, D_MODEL)
        j = i // 2
        if i % 2 == 0:
            lambda_init = 0.8 - 0.6 * math.exp(-0.3 * i)
            lam = (jnp.exp(jnp.sum(diff_lambda_q1[j] * diff_lambda_k1[j]))
                   - jnp.exp(jnp.sum(diff_lambda_q2[j] * diff_lambda_k2[j]))
                   + lambda_init).reshape(1, 1)
            qkv = _qkv(xf, mod, norm1_g[i].reshape(1, D_MODEL),
                       _scaled_qkv_weight(diff_w_in[j], DIFF_HEAD_DIM))
            o = _diff_attn(qkv, slopes, lam, diff_subln_g[j].reshape(1, LANES), lambda_init)
            w_out = diff_w_out[j]
        else:
            qkv = _qkv(xf, mod, norm1_g[i].reshape(1, D_MODEL),
                       _scaled_qkv_weight(sb_w_in[j], SB_HEAD_DIM))
            o = _sb_attn(qkv)
            w_out = sb_w_out[j]
        rw, rb = _router_weights(router_group_w[i], router_group_b[i],
                                 router_expert_w[i], router_expert_b[i])
        x1, h2, idr, wts, cnt = _out_router(o, xf, mod, norm2_g[i].reshape(1, D_MODEL),
                                            w_out.astype(BF16), rw, rb)
        dest, bexp, nused = _routing_tables(idr, cnt)
        xs = _sc_gather_rows(h2, _sc_slot_tokens(dest.reshape(-1)))
        ys = _experts(i, bexp, nused, xs, expert_w_gate, expert_w_up, expert_w_down)
        y2 = _sc_gather_rows(ys, dest.T.reshape(-1))
        xf = _mix(x1, wts, mod, final_norm_g.reshape(1, D_MODEL), y2,
                  final=(i == DEPTH - 1))
    return xf.reshape(BATCH, SEQ, D_MODEL)
```

```python
import functools
import math

import jax
import jax.numpy as jnp
from jax import lax
from jax.experimental import pallas as pl
from jax.experimental.pallas import tpu as pltpu
from jax.experimental.pallas import tpu_sc as plsc

D_MODEL = 1024
BATCH = 8
SEQ = 2048
DEPTH = 2
N_TOK = BATCH * SEQ

CHUNK = 64
DIFF_HEADS = 8
DIFF_HEAD_DIM = D_MODEL // (2 * DIFF_HEADS)
SB_HEADS = 16
SB_HEAD_DIM = D_MODEL // SB_HEADS
N_GROUPS = 4
EXPERTS_PER_GROUP = 8
N_EXPERTS = N_GROUPS * EXPERTS_PER_GROUP
TOP_K = 2
EXPERT_HIDDEN = D_MODEL // 2
RMS_EPS = 1e-6
SUBLN_EPS = 1e-5

LANES = 128
EXPERT_LANE0 = 32
EXP2_UNDERFLOW = -150.0
LOG2_E = math.log2(math.e)

ADA_TN = 1536
QKV_TM = 512
DIFF_T = 512
SB_T = 256
OUT_TM = 512
DISP_TM = 1024
MOE_BLK = 512
N_SLOTS = N_TOK * TOP_K + N_EXPERTS * MOE_BLK
N_BLOCKS = N_SLOTS // MOE_BLK
COMB_TM = 512
DMA_UNROLL = 8
ROW_WORDS = D_MODEL // 2

SC_CORES = 2
SC_SUBCORES = 16
SC_LANES = 16
SC_CHUNK = 64

VMEM_LIMIT = 56 * 1024 * 1024

F32 = jnp.float32
BF16 = jnp.bfloat16


def _cparams(sem):
    return pltpu.CompilerParams(dimension_semantics=sem, vmem_limit_bytes=VMEM_LIMIT)


def _ada_kernel(c_ref, w_ref, b_ref, o_ref):
    c = c_ref[...]
    cond = c * jax.nn.sigmoid(c)
    o_ref[0] = jnp.dot(cond, w_ref[0], preferred_element_type=F32,
                       precision=lax.Precision.HIGHEST) + b_ref[0]


def _ada(c, ada_w, ada_b):
    six_d = ada_w.shape[-1]
    return pl.pallas_call(
        _ada_kernel,
        grid=(DEPTH, six_d // ADA_TN),
        in_specs=[
            pl.BlockSpec((BATCH, D_MODEL), lambda l, n: (0, 0)),
            pl.BlockSpec((1, D_MODEL, ADA_TN), lambda l, n: (l, 0, n)),
            pl.BlockSpec((1, 1, ADA_TN), lambda l, n: (l, 0, n)),
        ],
        out_specs=pl.BlockSpec((1, BATCH, ADA_TN), lambda l, n: (l, 0, n)),
        out_shape=jax.ShapeDtypeStruct((DEPTH, BATCH, six_d), F32),
        compiler_params=_cparams(("arbitrary", "arbitrary")),
        name="ada",
    )(c, ada_w, ada_b.reshape(DEPTH, 1, six_d))


def _modulated_norm(x, g, scale, shift):
    ms = jnp.mean(x * x, axis=-1, keepdims=True)
    return x * lax.rsqrt(ms + RMS_EPS) * (g * (1.0 + scale)) + shift


def _qkv_kernel(x_ref, mod_ref, g_ref, w_ref, o_ref):
    m = mod_ref[0]
    h = _modulated_norm(x_ref[...], g_ref[...], m[1:2], m[0:1]).astype(BF16)
    for n in range(3):
        cols = slice(n * D_MODEL, (n + 1) * D_MODEL)
        o_ref[:, cols] = jnp.dot(h, w_ref[:, cols], preferred_element_type=F32).astype(BF16)


def _qkv(x, mod, g, w):
    tiles_per_batch = SEQ // QKV_TM
    return pl.pallas_call(
        _qkv_kernel,
        grid=(N_TOK // QKV_TM,),
        in_specs=[
            pl.BlockSpec((QKV_TM, D_MODEL), lambda i: (i, 0)),
            pl.BlockSpec((1, 6, D_MODEL), lambda i: (i // tiles_per_batch, 0, 0)),
            pl.BlockSpec((1, D_MODEL), lambda i: (0, 0)),
            pl.BlockSpec((D_MODEL, 3 * D_MODEL), lambda i: (0, 0)),
        ],
        out_specs=pl.BlockSpec((QKV_TM, 3 * D_MODEL), lambda i: (i, 0)),
        out_shape=jax.ShapeDtypeStruct((N_TOK, 3 * D_MODEL), BF16),
        compiler_params=_cparams(("arbitrary",)),
        name="qkv",
    )(x, mod, g, w)


def _half_masked(q, upper):
    lane = lax.broadcasted_iota(jnp.int32, q.shape, 1)
    keep = (lane >= LANES // 2) if upper else (lane < LANES // 2)
    return jnp.where(keep, q, jnp.zeros_like(q))


def _pack_bf16_pairs(x):
    half = x.shape[1] // 2
    hi = lax.bitcast_convert_type(x[:, :half].astype(BF16).astype(F32), jnp.uint32)
    lo = lax.bitcast_convert_type(x[:, half:].astype(BF16).astype(F32), jnp.uint32)
    return hi | (lo >> 16)


def _unpack_bf16_pairs(u):
    hi = lax.bitcast_convert_type(u & jnp.uint32(0xFFFF0000), F32)
    lo = lax.bitcast_convert_type(u << 16, F32)
    return jnp.concatenate([hi, lo], axis=1)


def _lane_tile(x, n):
    return jnp.concatenate([x] * n, axis=1)


def _qk(q, k):
    return lax.dot_general(q, k, (((1,), (1,)), ((), ())), preferred_element_type=F32)


def _diff_attn_kernel(slopes_ref, lam_ref, q_ref, k_ref, v_ref, g_ref, o_ref,
                      m0_ref, m1_ref, acc0_ref, acc1_ref, *, lambda_init):
    slope = slopes_ref[pl.program_id(1)] * LOG2_E
    m_refs = (m0_ref, m1_ref)
    acc_refs = (acc0_ref, acc1_ref)
    ones = jnp.ones((DIFF_T, LANES), BF16)

    row = lax.broadcasted_iota(jnp.int32, (DIFF_T, DIFF_T), 0)
    col = lax.broadcasted_iota(jnp.int32, (DIFF_T, DIFF_T), 1)
    rel = (row - col).astype(F32) * slope
    allowed = (col // CHUNK) <= (row // CHUNK)
    diag_bias = jnp.where(allowed, jnp.abs(rel), jnp.inf)
    out_gain = g_ref[...] * (1.0 - lambda_init)

    def query_tile(qi, _):
        q = q_ref[pl.ds(qi * DIFF_T, DIFF_T), :]
        qm = (_half_masked(q, False), _half_masked(q, True))

        k = k_ref[pl.ds(qi * DIFF_T, DIFF_T), :]
        v1 = jnp.concatenate([v_ref[pl.ds(qi * DIFF_T, DIFF_T), :], ones], axis=1)
        s_maps = [_qk(qm[mi], k) - diag_bias for mi in range(2)]
        for mi in range(2):
            m = jnp.max(s_maps[mi], axis=-1, keepdims=True)
            p = jnp.exp2(s_maps[mi] - m)
            m_refs[mi][...] = jnp.broadcast_to(m, (DIFF_T, LANES))
            acc_refs[mi][...] = jnp.dot(p.astype(BF16), v1, preferred_element_type=F32)

        def body(j, _):
            k = k_ref[pl.ds(j * DIFF_T, DIFF_T), :]
            v1 = jnp.concatenate([v_ref[pl.ds(j * DIFF_T, DIFF_T), :], ones], axis=1)
            shift = slope * jnp.asarray((qi - j) * DIFF_T).astype(F32)
            s_maps = [_qk(qm[mi], k) - rel for mi in range(2)]
            for mi in range(2):
                m = m_refs[mi][...]
                m_new = jnp.maximum(m, jnp.max(s_maps[mi], axis=-1, keepdims=True) - shift)
                alpha = jnp.exp2(m - m_new)
                p = jnp.exp2(s_maps[mi] - _lane_tile(m_new + shift, DIFF_T // LANES))
                m_refs[mi][...] = m_new
                acc_refs[mi][...] = _lane_tile(alpha, 2) * acc_refs[mi][...] + jnp.dot(
                    p.astype(BF16), v1, preferred_element_type=F32)
            return 0

        lax.fori_loop(0, qi, body, 0)
        a0 = acc0_ref[...]
        a1 = acc1_ref[...]
        o = a0[:, :LANES] / a0[:, LANES:] - lam_ref[...] * (a1[:, :LANES] / a1[:, LANES:])
        ms = jnp.mean(o * o, axis=-1, keepdims=True)
        o = o * lax.rsqrt(ms + SUBLN_EPS) * out_gain
        o_ref[pl.ds(qi * DIFF_T, DIFF_T), :] = o.astype(BF16)
        return 0

    lax.fori_loop(0, SEQ // DIFF_T, query_tile, 0)


def _diff_attn(qkv, slopes, lam, subln_g, lambda_init):
    return pl.pallas_call(
        functools.partial(_diff_attn_kernel, lambda_init=lambda_init),
        grid=(BATCH, DIFF_HEADS),
        in_specs=[
            pl.BlockSpec(memory_space=pltpu.SMEM),
            pl.BlockSpec((1, 1), lambda b, h: (0, 0)),
            pl.BlockSpec((SEQ, LANES), lambda b, h: (b, h)),
            pl.BlockSpec((SEQ, LANES), lambda b, h: (b, DIFF_HEADS + h)),
            pl.BlockSpec((SEQ, LANES), lambda b, h: (b, 2 * DIFF_HEADS + h)),
            pl.BlockSpec((1, LANES), lambda b, h: (0, 0)),
        ],
        out_specs=pl.BlockSpec((SEQ, LANES), lambda b, h: (b, h)),
        out_shape=jax.ShapeDtypeStruct((N_TOK, D_MODEL), BF16),
        scratch_shapes=[
            pltpu.VMEM((DIFF_T, LANES), F32),
            pltpu.VMEM((DIFF_T, LANES), F32),
            pltpu.VMEM((DIFF_T, 2 * LANES), F32),
            pltpu.VMEM((DIFF_T, 2 * LANES), F32),
        ],
        compiler_params=_cparams(("arbitrary", "arbitrary")),
        name="diff_attn",
    )(slopes, lam, qkv, qkv, qkv, subln_g)


def _sb_attn_kernel(q_ref, k_ref, v_ref, o_ref, tail0_ref, tail1_ref, acc0_ref, acc1_ref):
    tail_refs = (tail0_ref, tail1_ref)
    acc_refs = (acc0_ref, acc1_ref)

    row = lax.broadcasted_iota(jnp.int32, (SB_T, SB_T), 0)
    col = lax.broadcasted_iota(jnp.int32, (SB_T, SB_T), 1)
    strict = col < row
    neg_from = jnp.where(row >= col, -1.0, 0.0).astype(BF16)

    def scores(j, qhead, mask):
        z = _qk(qhead, k_ref[pl.ds(j * SB_T, SB_T), :])
        sp = jnp.maximum(z, 0.0) + jnp.log2(1.0 + jnp.exp2(-jnp.abs(z)))
        if mask is not None:
            sp = jnp.where(mask, sp, 0.0)
        return z, sp

    def weighted(j, z, sp, tail, mask):
        log_a = (z + jnp.dot(sp.astype(BF16), neg_from, preferred_element_type=F32)
                 + _lane_tile(tail, SB_T // LANES))
        a = jnp.exp2(log_a)
        if mask is not None:
            a = jnp.where(mask, a, 0.0)
        return jnp.dot(a.astype(BF16), v_ref[pl.ds(j * SB_T, SB_T), :],
                       preferred_element_type=F32)

    def row_sum(sp):
        return jnp.broadcast_to(jnp.sum(sp, axis=-1, keepdims=True), (SB_T, LANES))

    zero = jnp.zeros((SB_T, LANES), F32)
    lane = lax.broadcasted_iota(jnp.int32, (SB_T, LANES), 1)

    def query_heads(qi):
        q = q_ref[pl.ds(qi * SB_T, SB_T), :]
        return (_half_masked(q, False), _half_masked(q, True))

    def finish(qi):
        o_ref[pl.ds(qi * SB_T, SB_T), :] = jnp.where(
            lane < LANES // 2, acc0_ref[...], acc1_ref[...]).astype(BF16)

    qh = query_heads(0)
    for hh in range(2):
        z, sp = scores(0, qh[hh], strict)
        acc_refs[hh][...] = weighted(0, z, sp, zero, strict)
    finish(0)

    def query_tile(qi, _):
        qh = query_heads(qi)
        zs = [(scores(qi, qh[hh], strict), scores(qi - 1, qh[hh], None)) for hh in range(2)]
        for hh in range(2):
            (z0, sp0), (z1, sp1) = zs[hh]
            tail0 = -row_sum(sp0)
            acc_refs[hh][...] = (weighted(qi, z0, sp0, zero, strict)
                                 + weighted(qi - 1, z1, sp1, tail0, None))
            tail_refs[hh][...] = tail0 - row_sum(sp1)

        def cond(state):
            j, live = state
            return (j >= 0) & (live > EXP2_UNDERFLOW)

        def body(state):
            j, _ = state
            live = jnp.float32(-jnp.inf)
            for hh in range(2):
                z, sp = scores(j, qh[hh], None)
                tail = tail_refs[hh][...]
                acc_refs[hh][...] += weighted(j, z, sp, tail, None)
                tail = tail - row_sum(sp)
                tail_refs[hh][...] = tail
                live = jnp.maximum(live, jnp.max(tail))
            return j - 1, live

        live = jnp.maximum(jnp.max(tail0_ref[...]), jnp.max(tail1_ref[...]))
        lax.while_loop(cond, body, (qi - 2, live))
        finish(qi)
        return 0

    lax.fori_loop(1, SEQ // SB_T, query_tile, 0)


def _sb_attn(qkv):
    pairs = SB_HEADS // 2
    return pl.pallas_call(
        _sb_attn_kernel,
        grid=(BATCH, pairs),
        in_specs=[
            pl.BlockSpec((SEQ, LANES), lambda b, h: (b, h)),
            pl.BlockSpec((SEQ, LANES), lambda b, h: (b, pairs + h)),
            pl.BlockSpec((SEQ, LANES), lambda b, h: (b, 2 * pairs + h)),
        ],
        out_specs=pl.BlockSpec((SEQ, LANES), lambda b, h: (b, h)),
        out_shape=jax.ShapeDtypeStruct((N_TOK, D_MODEL), BF16),
        scratch_shapes=[
            pltpu.VMEM((SB_T, LANES), F32),
            pltpu.VMEM((SB_T, LANES), F32),
            pltpu.VMEM((SB_T, LANES), F32),
            pltpu.VMEM((SB_T, LANES), F32),
        ],
        compiler_params=_cparams(("arbitrary", "arbitrary")),
        name="sb_attn",
    )(qkv, qkv, qkv)


def _out_router_kernel(o_ref, x_ref, mod_ref, g_ref, w_ref, rw_ref, rb_ref,
                       x1_ref, h2_ref, idr_ref, wts_ref, cnt_ref, tri_ref, base_ref):
    i = pl.program_id(0)

    @pl.when(i == 0)
    def _():
        r = lax.broadcasted_iota(jnp.int32, (OUT_TM, OUT_TM), 0)
        c = lax.broadcasted_iota(jnp.int32, (OUT_TM, OUT_TM), 1)
        tri_ref[...] = jnp.where(c < r, 1.0, 0.0).astype(BF16)
        base_ref[...] = jnp.zeros_like(base_ref)

    m = mod_ref[0]
    y = jnp.dot(o_ref[...], w_ref[...], preferred_element_type=F32)
    x1 = x_ref[...] + m[2:3] * y
    x1_ref[...] = x1
    h2 = _modulated_norm(x1, g_ref[...], m[4:5], m[3:4])
    h2_ref[...] = lax.bitcast_convert_type(_pack_bf16_pairs(h2), jnp.int32)

    logits = jnp.dot(h2.astype(BF16), rw_ref[...], preferred_element_type=F32) + rb_ref[...]
    lane = lax.broadcasted_iota(jnp.int32, logits.shape, 1).astype(F32)
    neg_inf = jnp.float32(-jnp.inf)
    big = jnp.float32(1e9)

    is_group = lane < N_GROUPS
    gl = jnp.where(is_group, logits, neg_inf)
    gmax = jnp.max(gl, axis=-1, keepdims=True)
    gidx = jnp.min(jnp.where(gl == gmax, lane, big), axis=-1, keepdims=True)
    gsum = jnp.sum(jnp.where(is_group, jnp.exp(logits - gmax), 0.0), axis=-1, keepdims=True)
    g_w = 1.0 / gsum

    lo = EXPERT_LANE0 + EXPERTS_PER_GROUP * gidx
    in_group = (lane >= lo) & (lane < lo + EXPERTS_PER_GROUP)
    el = jnp.where(in_group, logits, neg_inf)
    v0 = jnp.max(el, axis=-1, keepdims=True)
    i0 = jnp.min(jnp.where(el == v0, lane, big), axis=-1, keepdims=True)
    el = jnp.where(lane == i0, neg_inf, el)
    v1 = jnp.max(el, axis=-1, keepdims=True)
    i1 = jnp.min(jnp.where(el == v1, lane, big), axis=-1, keepdims=True)
    t = jnp.exp(v1 - v0)
    w0 = g_w / (1.0 + t)
    w1 = g_w * t / (1.0 + t)

    oh0 = jnp.where(lane == i0, 1.0, 0.0)
    oh1 = jnp.where(lane == i1, 1.0, 0.0)
    both = oh0 + oh1
    before = jnp.dot(tri_ref[...], both.astype(BF16), preferred_element_type=F32) + base_ref[...]
    r0 = jnp.sum(before * oh0, axis=-1, keepdims=True)
    r1 = jnp.sum(before * oh1, axis=-1, keepdims=True)
    base_ref[...] = base_ref[...] + jnp.sum(both, axis=0, keepdims=True)
    cnt_ref[...] = base_ref[...]

    e0 = i0 - EXPERT_LANE0
    e1 = i1 - EXPERT_LANE0
    idr = jnp.where(lane == 0, e0, jnp.where(lane == 1, e1, jnp.where(lane == 2, r0, r1)))
    idr_ref[...] = idr[:, :4].astype(jnp.int32)
    wts_ref[...] = jnp.where(lane == 0, w0, w1)[:, :2]


def _out_router(o, x, mod, g, w, rw, rb):
    tiles_per_batch = SEQ // OUT_TM
    return pl.pallas_call(
        _out_router_kernel,
        grid=(N_TOK // OUT_TM,),
        in_specs=[
            pl.BlockSpec((OUT_TM, D_MODEL), lambda i: (i, 0)),
            pl.BlockSpec((OUT_TM, D_MODEL), lambda i: (i, 0)),
            pl.BlockSpec((1, 6, D_MODEL), lambda i: (i // tiles_per_batch, 0, 0)),
            pl.BlockSpec((1, D_MODEL), lambda i: (0, 0)),
            pl.BlockSpec((D_MODEL, D_MODEL), lambda i: (0, 0)),
            pl.BlockSpec((D_MODEL, LANES), lambda i: (0, 0)),
            pl.BlockSpec((1, LANES), lambda i: (0, 0)),
        ],
        out_specs=[
            pl.BlockSpec((OUT_TM, D_MODEL), lambda i: (i, 0)),
            pl.BlockSpec((OUT_TM, ROW_WORDS), lambda i: (i, 0)),
            pl.BlockSpec((OUT_TM, 4), lambda i: (i, 0)),
            pl.BlockSpec((OUT_TM, 2), lambda i: (i, 0)),
            pl.BlockSpec((1, LANES), lambda i: (0, 0)),
        ],
        out_shape=[
            jax.ShapeDtypeStruct((N_TOK, D_MODEL), F32),
            jax.ShapeDtypeStruct((N_TOK, ROW_WORDS), jnp.int32),
            jax.ShapeDtypeStruct((N_TOK, 4), jnp.int32),
            jax.ShapeDtypeStruct((N_TOK, 2), F32),
            jax.ShapeDtypeStruct((1, LANES), F32),
        ],
        scratch_shapes=[
            pltpu.VMEM((OUT_TM, OUT_TM), BF16),
            pltpu.VMEM((1, LANES), F32),
        ],
        compiler_params=_cparams(("arbitrary",)),
        name="out_router",
    )(o, x, mod, g, w, rw, rb)


def _dispatch_kernel(dest_ref, pad_end_ref, padded_ref, h_ref, xs_ref, zeros_ref, sem, zsem):
    @pl.when(pl.program_id(0) == 0)
    def _():
        zeros_ref[...] = jnp.zeros_like(zeros_ref)

        def tail_copy(e):
            start = pl.multiple_of(jnp.maximum(pad_end_ref[e] - MOE_BLK, 0), MOE_BLK)
            return pltpu.make_async_copy(zeros_ref, xs_ref.at[pl.ds(start, MOE_BLK)], zsem)

        def fill(e, _):
            @pl.when(padded_ref[e] > 0)
            def _():
                tail_copy(e).start()
            return 0

        def fill_done(e, _):
            @pl.when(padded_ref[e] > 0)
            def _():
                tail_copy(e).wait()
            return 0

        def block_copy(b):
            start = pl.multiple_of(b * MOE_BLK, MOE_BLK)
            return pltpu.make_async_copy(zeros_ref, xs_ref.at[pl.ds(start, MOE_BLK)], zsem)

        def fill_unused(b, _):
            block_copy(b).start()
            return 0

        def fill_unused_done(b, _):
            block_copy(b).wait()
            return 0

        first_unused = pad_end_ref[N_EXPERTS - 1] // MOE_BLK
        lax.fori_loop(0, N_EXPERTS, fill, 0)
        lax.fori_loop(first_unused, N_BLOCKS, fill_unused, 0)
        lax.fori_loop(0, N_EXPERTS, fill_done, 0)
        lax.fori_loop(first_unused, N_BLOCKS, fill_unused_done, 0)

    def row_copy(r, d):
        return pltpu.make_async_copy(h_ref.at[pl.ds(r, 1)], xs_ref.at[pl.ds(d, 1)], sem)

    def issue(r, _):
        for k in range(TOP_K):
            row_copy(r, dest_ref[0, 0, TOP_K * r + k]).start()
        return 0

    def drain(r, _):
        for k in range(TOP_K):
            row_copy(0, 0).wait()
        return 0

    lax.fori_loop(0, DISP_TM, issue, 0, unroll=DMA_UNROLL)
    lax.fori_loop(0, DISP_TM, drain, 0, unroll=DMA_UNROLL)


def _dispatch(dest, pad_ends, padded, h2):
    nt = N_TOK // DISP_TM
    return pl.pallas_call(
        _dispatch_kernel,
        grid=(nt,),
        in_specs=[
            pl.BlockSpec((1, 1, TOP_K * DISP_TM), lambda i: (i, 0, 0), memory_space=pltpu.SMEM),
            pl.BlockSpec(memory_space=pltpu.SMEM),
            pl.BlockSpec(memory_space=pltpu.SMEM),
            pl.BlockSpec((DISP_TM, ROW_WORDS), lambda i: (i, 0)),
        ],
        out_specs=pl.BlockSpec(memory_space=pl.ANY),
        out_shape=jax.ShapeDtypeStruct((N_SLOTS, ROW_WORDS), jnp.uint32),
        scratch_shapes=[
            pltpu.VMEM((MOE_BLK, ROW_WORDS), jnp.uint32),
            pltpu.SemaphoreType.DMA(()),
            pltpu.SemaphoreType.DMA(()),
        ],
        compiler_params=_cparams(("arbitrary",)),
        name="dispatch",
    )(dest.reshape(nt, 1, TOP_K * DISP_TM), pad_ends, padded, h2)


def _expert_kernel(ord_ref, active_ref, meta_ref, xs_ref, wg_hbm, wu_hbm, wd_hbm, ys_ref,
                   wg_f32, wu_f32, wd_f32, sems, wg_bf, wu_bf, wd_bf, *, layer):
    i = pl.program_id(0)
    n_used = meta_ref[0]
    n_active = meta_ref[1]

    def fetch(pos, slot):
        e = active_ref[pos]
        return (
            pltpu.make_async_copy(wg_hbm.at[layer, e], wg_f32.at[slot], sems.at[slot, 0]),
            pltpu.make_async_copy(wu_hbm.at[layer, e], wu_f32.at[slot], sems.at[slot, 1]),
            pltpu.make_async_copy(wd_hbm.at[layer, e], wd_f32.at[slot], sems.at[slot, 2]),
        )

    @pl.when(i == 0)
    def _():
        for copy in fetch(0, 0):
            copy.start()

        @pl.when(n_active > 1)
        def _():
            for copy in fetch(1, 1):
                copy.start()

    @pl.when(i < n_used)
    def _():
        pos = ord_ref[i]
        changed = (i == 0) | (pos != ord_ref[jnp.maximum(i - 1, 0)])
        for slot in range(2):
            @pl.when(changed & (pos % 2 == slot))
            def _():
                for copy in fetch(pos, slot):
                    copy.wait()
                wg_bf[...] = wg_f32[slot].astype(BF16)
                wu_bf[...] = wu_f32[slot].astype(BF16)
                wd_bf[...] = wd_f32[slot].astype(BF16)

                @pl.when(pos + 2 < n_active)
                def _():
                    for copy in fetch(pos + 2, slot):
                        copy.start()

        x = _unpack_bf16_pairs(lax.bitcast_convert_type(xs_ref[...], jnp.uint32)).astype(BF16)
        g = jnp.dot(x, wg_bf[...], preferred_element_type=F32)
        u = jnp.dot(x, wu_bf[...], preferred_element_type=F32)
        hid = (g * jax.nn.sigmoid(g)) * u
        y = jnp.dot(hid.astype(BF16), wd_bf[...], preferred_element_type=F32)
        ys_ref[...] = lax.bitcast_convert_type(_pack_bf16_pairs(y), jnp.int32)

    @pl.when(i >= n_used)
    def _():
        ys_ref[...] = jnp.zeros_like(ys_ref)


def _experts(layer, block_pos, active, meta, xs, wg, wu, wd):
    def in_row_map(i, block_pos, active, meta):
        return (jnp.minimum(i, meta[0] - 1), 0)

    def row_map(i, block_pos, active, meta):
        return (i, 0)

    return pl.pallas_call(
        functools.partial(_expert_kernel, layer=layer),
        grid_spec=pltpu.PrefetchScalarGridSpec(
            num_scalar_prefetch=3,
            grid=(N_BLOCKS,),
            in_specs=[
                pl.BlockSpec((MOE_BLK, ROW_WORDS), in_row_map),
                pl.BlockSpec(memory_space=pl.ANY),
                pl.BlockSpec(memory_space=pl.ANY),
                pl.BlockSpec(memory_space=pl.ANY),
            ],
            out_specs=pl.BlockSpec((MOE_BLK, ROW_WORDS), row_map),
            scratch_shapes=[
                pltpu.VMEM((2, D_MODEL, EXPERT_HIDDEN), F32),
                pltpu.VMEM((2, D_MODEL, EXPERT_HIDDEN), F32),
                pltpu.VMEM((2, EXPERT_HIDDEN, D_MODEL), F32),
                pltpu.SemaphoreType.DMA((2, 3)),
                pltpu.VMEM((D_MODEL, EXPERT_HIDDEN), BF16),
                pltpu.VMEM((D_MODEL, EXPERT_HIDDEN), BF16),
                pltpu.VMEM((EXPERT_HIDDEN, D_MODEL), BF16),
            ],
        ),
        out_shape=jax.ShapeDtypeStruct((N_SLOTS, ROW_WORDS), jnp.int32),
        compiler_params=_cparams(("arbitrary",)),
        name="experts",
    )(block_pos, active, meta, xs, wg, wu, wd)


def _sc_gather_rows(table, idx):
    n_rows = idx.shape[0]
    width = table.shape[1]
    workers = SC_CORES * SC_SUBCORES
    per_worker = n_rows // workers
    assert per_worker * workers == n_rows and per_worker % SC_CHUNK == 0
    mesh = plsc.VectorSubcoreMesh(core_axis_name="c", subcore_axis_name="s")

    n_chunks = per_worker // SC_CHUNK

    def body(table_hbm, idx_hbm, out_hbm, idx_v, buf0, buf1, gsem0, gsem1, wsem0, wsem1):
        wid = lax.axis_index("s") * SC_CORES + lax.axis_index("c")
        base = pl.multiple_of(wid * per_worker, SC_CHUNK)
        bufs, gsems, wsems = (buf0, buf1), (gsem0, gsem1), (wsem0, wsem1)
        pltpu.sync_copy(idx_hbm.at[pl.ds(base, per_worker)], idx_v)

        def gather(c):
            rows = idx_v.at[pl.ds(c * SC_CHUNK, SC_CHUNK)]
            return pltpu.async_copy(table_hbm.at[rows], bufs[c % 2], gsems[c % 2])

        def write(c):
            dst = out_hbm.at[pl.ds(base + c * SC_CHUNK, SC_CHUNK)]
            return pltpu.async_copy(bufs[c % 2], dst, wsems[c % 2])

        gathers = {0: gather(0)}
        writes = {}
        for c in range(n_chunks):
            if c + 1 < n_chunks:
                if c >= 1:
                    writes.pop(c - 1).wait()
                gathers[c + 1] = gather(c + 1)
            gathers.pop(c).wait()
            writes[c] = write(c)
        for c in sorted(writes):
            writes[c].wait()

    return pl.kernel(
        body,
        out_type=jax.ShapeDtypeStruct((n_rows, width), table.dtype),
        mesh=mesh,
        scratch_types=[
            pltpu.VMEM((per_worker,), jnp.int32),
            pltpu.VMEM((SC_CHUNK, width), table.dtype),
            pltpu.VMEM((SC_CHUNK, width), table.dtype),
            pltpu.SemaphoreType.DMA,
            pltpu.SemaphoreType.DMA,
            pltpu.SemaphoreType.DMA,
            pltpu.SemaphoreType.DMA,
        ],
        name="sc_gather_rows",
    )(table, idx)


def _sc_slot_tokens(dest_flat):
    n_assign = N_TOK * TOP_K
    lanes = SC_LANES
    mesh = plsc.VectorSubcoreMesh(core_axis_name="c", subcore_axis_name="s")

    def body(dest_hbm, out_hbm, dest_v, slot_v):
        wid = lax.axis_index("s") * SC_CORES + lax.axis_index("c")

        @pl.when(wid == 0)
        def _():
            pltpu.sync_copy(dest_hbm, dest_v)
            lane = lax.iota(jnp.int32, lanes)

            def init(i, _):
                start = pl.multiple_of(i * lanes, lanes)
                slot_v[pl.ds(start, lanes)] = (lane + start) & (N_TOK - 1)
                return 0

            def place(i, _):
                start = pl.multiple_of(i * lanes, lanes)
                slots = dest_v[pl.ds(start, lanes)]
                plsc.store_scatter(slot_v, [slots], (lane + start) >> 1)
                return 0

            lax.fori_loop(0, N_SLOTS // lanes, init, 0)
            lax.fori_loop(0, n_assign // lanes, place, 0)
            pltpu.sync_copy(slot_v, out_hbm)

    return pl.kernel(
        body,
        out_type=jax.ShapeDtypeStruct((N_SLOTS,), jnp.int32),
        mesh=mesh,
        scratch_types=[
            pltpu.VMEM((n_assign,), jnp.int32),
            pltpu.VMEM((N_SLOTS,), jnp.int32),
        ],
        compiler_params=pltpu.CompilerParams(needs_layout_passes=False),
        name="sc_slot_tokens",
    )(dest_flat)


def _mix_kernel(x_ref, wts_ref, mod_ref, fg_ref, y0_ref, y1_ref, o_ref, *, final):
    w = wts_ref[...]
    y0 = _unpack_bf16_pairs(lax.bitcast_convert_type(y0_ref[...], jnp.uint32))
    y1 = _unpack_bf16_pairs(lax.bitcast_convert_type(y1_ref[...], jnp.uint32))
    y = w[:, 0:1] * y0 + w[:, 1:2] * y1
    out = x_ref[...] + mod_ref[0][5:6] * y
    if final:
        ms = jnp.mean(out * out, axis=-1, keepdims=True)
        out = out * lax.rsqrt(ms + RMS_EPS) * fg_ref[...]
    o_ref[...] = out


def _mix(x1, wts, mod, fg, y2, final):
    nt = N_TOK // COMB_TM
    tiles_per_batch = SEQ // COMB_TM
    return pl.pallas_call(
        functools.partial(_mix_kernel, final=final),
        grid=(nt,),
        in_specs=[
            pl.BlockSpec((COMB_TM, D_MODEL), lambda i: (i, 0)),
            pl.BlockSpec((COMB_TM, TOP_K), lambda i: (i, 0)),
            pl.BlockSpec((1, 6, D_MODEL), lambda i: (i // tiles_per_batch, 0, 0)),
            pl.BlockSpec((1, D_MODEL), lambda i: (0, 0)),
            pl.BlockSpec((COMB_TM, ROW_WORDS), lambda i: (i, 0)),
            pl.BlockSpec((COMB_TM, ROW_WORDS), lambda i: (nt + i, 0)),
        ],
        out_specs=pl.BlockSpec((COMB_TM, D_MODEL), lambda i: (i, 0)),
        out_shape=jax.ShapeDtypeStruct((N_TOK, D_MODEL), F32),
        compiler_params=_cparams(("arbitrary",)),
        name="mix",
    )(x1, wts, mod, fg, y2, y2)


def _routing_tables(idr, cnt):
    counts = cnt[0, EXPERT_LANE0:EXPERT_LANE0 + N_EXPERTS].astype(jnp.int32)
    padded = (counts + MOE_BLK - 1) // MOE_BLK * MOE_BLK
    pad_ends = jnp.cumsum(padded)
    pad_starts = pad_ends - padded
    experts = jnp.arange(N_EXPERTS, dtype=jnp.int32)
    is_expert = idr[:, 0:2, None] == experts
    dest = jnp.sum(jnp.where(is_expert, pad_starts, 0), axis=-1) + idr[:, 2:4]
    block_start = jnp.arange(N_BLOCKS, dtype=jnp.int32) * MOE_BLK
    bexp = jnp.minimum(jnp.sum(block_start[:, None] >= pad_ends[None, :], axis=1),
                       N_EXPERTS - 1).astype(jnp.int32)
    owns = padded > 0
    pos_of_expert = jnp.cumsum(owns.astype(jnp.int32)) - 1
    active = jnp.sum(jnp.where(owns[None, :] & (pos_of_expert[None, :] == experts[:, None]),
                               experts[None, :], 0), axis=1).astype(jnp.int32)
    block_pos = jnp.sum(jnp.where(bexp[:, None] == experts[None, :], pos_of_expert[None, :], 0),
                        axis=1).astype(jnp.int32)
    meta = jnp.stack([pad_ends[-1] // MOE_BLK, jnp.sum(owns)]).astype(jnp.int32)
    return dest.astype(jnp.int32), block_pos, active, meta


def _router_weights(w_group, b_group, w_expert, b_expert):
    rw = jnp.zeros((D_MODEL, LANES), F32)
    rw = rw.at[:, :N_GROUPS].set(w_group)
    rw = rw.at[:, EXPERT_LANE0:EXPERT_LANE0 + N_EXPERTS].set(w_expert)
    rb = jnp.zeros((1, LANES), F32)
    rb = rb.at[0, :N_GROUPS].set(b_group)
    rb = rb.at[0, EXPERT_LANE0:EXPERT_LANE0 + N_EXPERTS].set(b_expert.reshape(-1))
    return rw.astype(BF16), rb


def _scaled_qkv_weight(w_in, head_dim):
    scale = jnp.concatenate([jnp.full((D_MODEL,), LOG2_E / math.sqrt(head_dim), F32),
                             jnp.ones((2 * D_MODEL,), F32)])
    return (w_in * scale).astype(BF16)


def kernel(x, c, norm1_g, norm2_g, ada_w, ada_b, diff_w_in, diff_w_out, diff_lambda_q1, diff_lambda_k1, diff_lambda_q2, diff_lambda_k2, diff_subln_g, sb_w_in, sb_w_out, router_group_w, router_group_b, router_expert_w, router_expert_b, expert_w_gate, expert_w_up, expert_w_down, final_norm_g):
    xf = x.reshape(N_TOK, D_MODEL)
    mod_all = _ada(c, ada_w, ada_b)
    slopes = jnp.exp2(-8.0 * jnp.arange(1, DIFF_HEADS + 1, dtype=F32) / DIFF_HEADS)

    for i in range(DEPTH):
        mod = mod_all[i].reshape(BATCH, 6, D_MODEL)
        j = i // 2
        if i % 2 == 0:
            lambda_init = 0.8 - 0.6 * math.exp(-0.3 * i)
            lam = (jnp.exp(jnp.sum(diff_lambda_q1[j] * diff_lambda_k1[j]))
                   - jnp.exp(jnp.sum(diff_lambda_q2[j] * diff_lambda_k2[j]))
                   + lambda_init).reshape(1, 1)
            qkv = _qkv(xf, mod, norm1_g[i].reshape(1, D_MODEL),
                       _scaled_qkv_weight(diff_w_in[j], DIFF_HEAD_DIM))
            o = _diff_attn(qkv, slopes, lam, diff_subln_g[j].reshape(1, LANES), lambda_init)
            w_out = diff_w_out[j]
        else:
            qkv = _qkv(xf, mod, norm1_g[i].reshape(1, D_MODEL),
                       _scaled_qkv_weight(sb_w_in[j], SB_HEAD_DIM))
            o = _sb_attn(qkv)
            w_out = sb_w_out[j]
        rw, rb = _router_weights(router_group_w[i], router_group_b[i],
                                 router_expert_w[i], router_expert_b[i])
        x1, h2, idr, wts, cnt = _out_router(o, xf, mod, norm2_g[i].reshape(1, D_MODEL),
                                            w_out.astype(BF16), rw, rb)
        dest, block_pos, active, meta = _routing_tables(idr, cnt)
        xs = _sc_gather_rows(h2, _sc_slot_tokens(dest.reshape(-1)))
        ys = _experts(i, block_pos, active, meta, xs,
                      expert_w_gate, expert_w_up, expert_w_down)
        y2 = _sc_gather_rows(ys, dest.T.reshape(-1))
        xf = _mix(x1, wts, mod, final_norm_g.reshape(1, D_MODEL), y2,
                  final=(i == DEPTH - 1))
    return xf.reshape(BATCH, SEQ, D_MODEL)
```

```python
import functools
import math

import jax
import jax.numpy as jnp
from jax import lax
from jax.experimental import pallas as pl
from jax.experimental.pallas import tpu as pltpu
from jax.experimental.pallas import tpu_sc as plsc

D_MODEL = 1024
BATCH = 8
SEQ = 2048
DEPTH = 2
N_TOK = BATCH * SEQ

CHUNK = 64
DIFF_HEADS = 8
DIFF_HEAD_DIM = D_MODEL // (2 * DIFF_HEADS)
SB_HEADS = 16
SB_HEAD_DIM = D_MODEL // SB_HEADS
N_GROUPS = 4
EXPERTS_PER_GROUP = 8
N_EXPERTS = N_GROUPS * EXPERTS_PER_GROUP
TOP_K = 2
EXPERT_HIDDEN = D_MODEL // 2
RMS_EPS = 1e-6
SUBLN_EPS = 1e-5

LANES = 128
EXPERT_LANE0 = 32
EXP2_UNDERFLOW = -150.0
LOG2_E = math.log2(math.e)

ADA_TN = 1536
QKV_TM = 512
DIFF_T = 512
SB_T = 256
OUT_TM = 512
MOE_BLK = 512
N_SLOTS = N_TOK * TOP_K + N_EXPERTS * MOE_BLK
N_BLOCKS = N_SLOTS // MOE_BLK
COMB_TM = 512
ROW_WORDS = D_MODEL // 2

SC_CORES = 2
SC_SUBCORES = 16
SC_LANES = 16
SC_CHUNK = 64

VMEM_LIMIT = 56 * 1024 * 1024

F32 = jnp.float32
BF16 = jnp.bfloat16


def _cparams(sem):
    return pltpu.CompilerParams(dimension_semantics=sem, vmem_limit_bytes=VMEM_LIMIT)


def _ada_kernel(c_ref, w_ref, b_ref, o_ref):
    c = c_ref[...]
    cond = c * jax.nn.sigmoid(c)
    o_ref[0] = jnp.dot(cond, w_ref[0], preferred_element_type=F32,
                       precision=lax.Precision.HIGHEST) + b_ref[0]


def _ada(c, ada_w, ada_b):
    six_d = ada_w.shape[-1]
    return pl.pallas_call(
        _ada_kernel,
        grid=(DEPTH, six_d // ADA_TN),
        in_specs=[
            pl.BlockSpec((BATCH, D_MODEL), lambda l, n: (0, 0)),
            pl.BlockSpec((1, D_MODEL, ADA_TN), lambda l, n: (l, 0, n)),
            pl.BlockSpec((1, 1, ADA_TN), lambda l, n: (l, 0, n)),
        ],
        out_specs=pl.BlockSpec((1, BATCH, ADA_TN), lambda l, n: (l, 0, n)),
        out_shape=jax.ShapeDtypeStruct((DEPTH, BATCH, six_d), F32),
        compiler_params=_cparams(("arbitrary", "arbitrary")),
        name="ada",
    )(c, ada_w, ada_b.reshape(DEPTH, 1, six_d))


def _modulated_norm(x, g, scale, shift):
    ms = jnp.mean(x * x, axis=-1, keepdims=True)
    return x * lax.rsqrt(ms + RMS_EPS) * (g * (1.0 + scale)) + shift


def _qkv_kernel(x_ref, mod_ref, g_ref, w_ref, o_ref):
    m = mod_ref[0]
    h = _modulated_norm(x_ref[...], g_ref[...], m[1:2], m[0:1]).astype(BF16)
    for n in range(3):
        cols = slice(n * D_MODEL, (n + 1) * D_MODEL)
        o_ref[:, cols] = jnp.dot(h, w_ref[:, cols], preferred_element_type=F32).astype(BF16)


def _qkv(x, mod, g, w):
    tiles_per_batch = SEQ // QKV_TM
    return pl.pallas_call(
        _qkv_kernel,
        grid=(N_TOK // QKV_TM,),
        in_specs=[
            pl.BlockSpec((QKV_TM, D_MODEL), lambda i: (i, 0)),
            pl.BlockSpec((1, 6, D_MODEL), lambda i: (i // tiles_per_batch, 0, 0)),
            pl.BlockSpec((1, D_MODEL), lambda i: (0, 0)),
            pl.BlockSpec((D_MODEL, 3 * D_MODEL), lambda i: (0, 0)),
        ],
        out_specs=pl.BlockSpec((QKV_TM, 3 * D_MODEL), lambda i: (i, 0)),
        out_shape=jax.ShapeDtypeStruct((N_TOK, 3 * D_MODEL), BF16),
        compiler_params=_cparams(("arbitrary",)),
        name="qkv",
    )(x, mod, g, w)


def _expert_mix(x, wts, gate, y0_words, y1_words):
    y0 = _unpack_bf16_pairs(lax.bitcast_convert_type(y0_words, jnp.uint32))
    y1 = _unpack_bf16_pairs(lax.bitcast_convert_type(y1_words, jnp.uint32))
    return x + gate * (wts[:, 0:1] * y0 + wts[:, 1:2] * y1)


def _mix_qkv_kernel(x1_ref, wts_ref, pmod_ref, y0_ref, y1_ref, mod_ref, g_ref, w_ref,
                    x_ref, o_ref):
    x = _expert_mix(x1_ref[...], wts_ref[...], pmod_ref[0][5:6], y0_ref[...], y1_ref[...])
    x_ref[...] = x
    m = mod_ref[0]
    h = _modulated_norm(x, g_ref[...], m[1:2], m[0:1]).astype(BF16)
    for n in range(3):
        cols = slice(n * D_MODEL, (n + 1) * D_MODEL)
        o_ref[:, cols] = jnp.dot(h, w_ref[:, cols], preferred_element_type=F32).astype(BF16)


def _mix_qkv(x1, wts, prev_mod, y2, mod, g, w):
    nt = N_TOK // QKV_TM
    tiles_per_batch = SEQ // QKV_TM
    return pl.pallas_call(
        _mix_qkv_kernel,
        grid=(nt,),
        in_specs=[
            pl.BlockSpec((QKV_TM, D_MODEL), lambda i: (i, 0)),
            pl.BlockSpec((QKV_TM, TOP_K), lambda i: (i, 0)),
            pl.BlockSpec((1, 6, D_MODEL), lambda i: (i // tiles_per_batch, 0, 0)),
            pl.BlockSpec((QKV_TM, ROW_WORDS), lambda i: (i, 0)),
            pl.BlockSpec((QKV_TM, ROW_WORDS), lambda i: (nt + i, 0)),
            pl.BlockSpec((1, 6, D_MODEL), lambda i: (i // tiles_per_batch, 0, 0)),
            pl.BlockSpec((1, D_MODEL), lambda i: (0, 0)),
            pl.BlockSpec((D_MODEL, 3 * D_MODEL), lambda i: (0, 0)),
        ],
        out_specs=[
            pl.BlockSpec((QKV_TM, D_MODEL), lambda i: (i, 0)),
            pl.BlockSpec((QKV_TM, 3 * D_MODEL), lambda i: (i, 0)),
        ],
        out_shape=[
            jax.ShapeDtypeStruct((N_TOK, D_MODEL), F32),
            jax.ShapeDtypeStruct((N_TOK, 3 * D_MODEL), BF16),
        ],
        compiler_params=_cparams(("arbitrary",)),
        name="mix_qkv",
    )(x1, wts, prev_mod, y2, y2, mod, g, w)


def _half_masked(q, upper):
    lane = lax.broadcasted_iota(jnp.int32, q.shape, 1)
    keep = (lane >= LANES // 2) if upper else (lane < LANES // 2)
    return jnp.where(keep, q, jnp.zeros_like(q))


def _pack_bf16_pairs(x):
    half = x.shape[1] // 2
    hi = lax.bitcast_convert_type(x[:, :half].astype(BF16).astype(F32), jnp.uint32)
    lo = lax.bitcast_convert_type(x[:, half:].astype(BF16).astype(F32), jnp.uint32)
    return hi | (lo >> 16)


def _unpack_bf16_pairs(u):
    hi = lax.bitcast_convert_type(u & jnp.uint32(0xFFFF0000), F32)
    lo = lax.bitcast_convert_type(u << 16, F32)
    return jnp.concatenate([hi, lo], axis=1)


def _lane_tile(x, n):
    return jnp.concatenate([x] * n, axis=1)


def _qk(q, k):
    return lax.dot_general(q, k, (((1,), (1,)), ((), ())), preferred_element_type=F32)


def _diff_attn_kernel(slopes_ref, lam_ref, q_ref, k_ref, v_ref, g_ref, o_ref,
                      m0_ref, m1_ref, acc0_ref, acc1_ref, *, lambda_init):
    slope = slopes_ref[pl.program_id(1)] * LOG2_E
    m_refs = (m0_ref, m1_ref)
    acc_refs = (acc0_ref, acc1_ref)
    ones = jnp.ones((DIFF_T, LANES), BF16)

    row = lax.broadcasted_iota(jnp.int32, (DIFF_T, DIFF_T), 0)
    col = lax.broadcasted_iota(jnp.int32, (DIFF_T, DIFF_T), 1)
    rel = (row - col).astype(F32) * slope
    allowed = (col // CHUNK) <= (row // CHUNK)
    diag_bias = jnp.where(allowed, jnp.abs(rel), jnp.inf)
    out_gain = g_ref[...] * (1.0 - lambda_init)

    def query_tile(qi, _):
        q = q_ref[pl.ds(qi * DIFF_T, DIFF_T), :]
        qm = (_half_masked(q, False), _half_masked(q, True))

        k = k_ref[pl.ds(qi * DIFF_T, DIFF_T), :]
        v1 = jnp.concatenate([v_ref[pl.ds(qi * DIFF_T, DIFF_T), :], ones], axis=1)
        s_maps = [_qk(qm[mi], k) - diag_bias for mi in range(2)]
        for mi in range(2):
            m = jnp.max(s_maps[mi], axis=-1, keepdims=True)
            p = jnp.exp2(s_maps[mi] - m)
            m_refs[mi][...] = jnp.broadcast_to(m, (DIFF_T, LANES))
            acc_refs[mi][...] = jnp.dot(p.astype(BF16), v1, preferred_element_type=F32)

        def body(j, _):
            k = k_ref[pl.ds(j * DIFF_T, DIFF_T), :]
            v1 = jnp.concatenate([v_ref[pl.ds(j * DIFF_T, DIFF_T), :], ones], axis=1)
            shift = slope * jnp.asarray((qi - j) * DIFF_T).astype(F32)
            s_maps = [_qk(qm[mi], k) - rel for mi in range(2)]
            for mi in range(2):
                m = m_refs[mi][...]
                m_new = jnp.maximum(m, jnp.max(s_maps[mi], axis=-1, keepdims=True) - shift)
                alpha = jnp.exp2(m - m_new)
                p = jnp.exp2(s_maps[mi] - _lane_tile(m_new + shift, DIFF_T // LANES))
                m_refs[mi][...] = m_new
                acc_refs[mi][...] = _lane_tile(alpha, 2) * acc_refs[mi][...] + jnp.dot(
                    p.astype(BF16), v1, preferred_element_type=F32)
            return 0

        lax.fori_loop(0, qi, body, 0)
        a0 = acc0_ref[...]
        a1 = acc1_ref[...]
        o = a0[:, :LANES] / a0[:, LANES:] - lam_ref[...] * (a1[:, :LANES] / a1[:, LANES:])
        ms = jnp.mean(o * o, axis=-1, keepdims=True)
        o = o * lax.rsqrt(ms + SUBLN_EPS) * out_gain
        o_ref[pl.ds(qi * DIFF_T, DIFF_T), :] = o.astype(BF16)
        return 0

    lax.fori_loop(0, SEQ // DIFF_T, query_tile, 0)


def _diff_attn(qkv, slopes, lam, subln_g, lambda_init):
    return pl.pallas_call(
        functools.partial(_diff_attn_kernel, lambda_init=lambda_init),
        grid=(BATCH, DIFF_HEADS),
        in_specs=[
            pl.BlockSpec(memory_space=pltpu.SMEM),
            pl.BlockSpec((1, 1), lambda b, h: (0, 0)),
            pl.BlockSpec((SEQ, LANES), lambda b, h: (b, h)),
            pl.BlockSpec((SEQ, LANES), lambda b, h: (b, DIFF_HEADS + h)),
            pl.BlockSpec((SEQ, LANES), lambda b, h: (b, 2 * DIFF_HEADS + h)),
            pl.BlockSpec((1, LANES), lambda b, h: (0, 0)),
        ],
        out_specs=pl.BlockSpec((SEQ, LANES), lambda b, h: (b, h)),
        out_shape=jax.ShapeDtypeStruct((N_TOK, D_MODEL), BF16),
        scratch_shapes=[
            pltpu.VMEM((DIFF_T, LANES), F32),
            pltpu.VMEM((DIFF_T, LANES), F32),
            pltpu.VMEM((DIFF_T, 2 * LANES), F32),
            pltpu.VMEM((DIFF_T, 2 * LANES), F32),
        ],
        compiler_params=_cparams(("arbitrary", "arbitrary")),
        name="diff_attn",
    )(slopes, lam, qkv, qkv, qkv, subln_g)


def _sb_attn_kernel(q_ref, k_ref, v_ref, o_ref, tail0_ref, tail1_ref, acc0_ref, acc1_ref):
    tail_refs = (tail0_ref, tail1_ref)
    acc_refs = (acc0_ref, acc1_ref)

    row = lax.broadcasted_iota(jnp.int32, (SB_T, SB_T), 0)
    col = lax.broadcasted_iota(jnp.int32, (SB_T, SB_T), 1)
    strict = col < row
    neg_from = jnp.where(row >= col, -1.0, 0.0).astype(BF16)

    def scores(j, qhead, mask):
        z = _qk(qhead, k_ref[pl.ds(j * SB_T, SB_T), :])
        sp = jnp.maximum(z, 0.0) + jnp.log2(1.0 + jnp.exp2(-jnp.abs(z)))
        if mask is not None:
            sp = jnp.where(mask, sp, 0.0)
        return z, sp

    def weighted(j, z, sp, tail, mask):
        log_a = (z + jnp.dot(sp.astype(BF16), neg_from, preferred_element_type=F32)
                 + _lane_tile(tail, SB_T // LANES))
        a = jnp.exp2(log_a)
        if mask is not None:
            a = jnp.where(mask, a, 0.0)
        return jnp.dot(a.astype(BF16), v_ref[pl.ds(j * SB_T, SB_T), :],
                       preferred_element_type=F32)

    def row_sum(sp):
        return jnp.broadcast_to(jnp.sum(sp, axis=-1, keepdims=True), (SB_T, LANES))

    zero = jnp.zeros((SB_T, LANES), F32)
    lane = lax.broadcasted_iota(jnp.int32, (SB_T, LANES), 1)

    def query_heads(qi):
        q = q_ref[pl.ds(qi * SB_T, SB_T), :]
        return (_half_masked(q, False), _half_masked(q, True))

    def finish(qi):
        o_ref[pl.ds(qi * SB_T, SB_T), :] = jnp.where(
            lane < LANES // 2, acc0_ref[...], acc1_ref[...]).astype(BF16)

    qh = query_heads(0)
    for hh in range(2):
        z, sp = scores(0, qh[hh], strict)
        acc_refs[hh][...] = weighted(0, z, sp, zero, strict)
    finish(0)

    def query_tile(qi, _):
        qh = query_heads(qi)
        zs = [(scores(qi, qh[hh], strict), scores(qi - 1, qh[hh], None)) for hh in range(2)]
        for hh in range(2):
            (z0, sp0), (z1, sp1) = zs[hh]
            tail0 = -row_sum(sp0)
            acc_refs[hh][...] = (weighted(qi, z0, sp0, zero, strict)
                                 + weighted(qi - 1, z1, sp1, tail0, None))
            tail_refs[hh][...] = tail0 - row_sum(sp1)

        def cond(state):
            j, live = state
            return (j >= 0) & (live > EXP2_UNDERFLOW)

        def body(state):
            j, _ = state
            live = jnp.float32(-jnp.inf)
            for hh in range(2):
                z, sp = scores(j, qh[hh], None)
                tail = tail_refs[hh][...]
                acc_refs[hh][...] += weighted(j, z, sp, tail, None)
                tail = tail - row_sum(sp)
                tail_refs[hh][...] = tail
                live = jnp.maximum(live, jnp.max(tail))
            return j - 1, live

        live = jnp.maximum(jnp.max(tail0_ref[...]), jnp.max(tail1_ref[...]))
        lax.while_loop(cond, body, (qi - 2, live))
        finish(qi)
        return 0

    lax.fori_loop(1, SEQ // SB_T, query_tile, 0)


def _sb_attn(qkv):
    pairs = SB_HEADS // 2
    return pl.pallas_call(
        _sb_attn_kernel,
        grid=(BATCH, pairs),
        in_specs=[
            pl.BlockSpec((SEQ, LANES), lambda b, h: (b, h)),
            pl.BlockSpec((SEQ, LANES), lambda b, h: (b, pairs + h)),
            pl.BlockSpec((SEQ, LANES), lambda b, h: (b, 2 * pairs + h)),
        ],
        out_specs=pl.BlockSpec((SEQ, LANES), lambda b, h: (b, h)),
        out_shape=jax.ShapeDtypeStruct((N_TOK, D_MODEL), BF16),
        scratch_shapes=[
            pltpu.VMEM((SB_T, LANES), F32),
            pltpu.VMEM((SB_T, LANES), F32),
            pltpu.VMEM((SB_T, LANES), F32),
            pltpu.VMEM((SB_T, LANES), F32),
        ],
        compiler_params=_cparams(("arbitrary", "arbitrary")),
        name="sb_attn",
    )(qkv, qkv, qkv)


def _out_router_kernel(o_ref, x_ref, mod_ref, g_ref, w_ref, rw_ref, rb_ref,
                       x1_ref, h2_ref, idr_ref, wts_ref, cnt_ref, tri_ref, base_ref):
    i = pl.program_id(0)

    @pl.when(i == 0)
    def _():
        r = lax.broadcasted_iota(jnp.int32, (OUT_TM, OUT_TM), 0)
        c = lax.broadcasted_iota(jnp.int32, (OUT_TM, OUT_TM), 1)
        tri_ref[...] = jnp.where(c < r, 1.0, 0.0).astype(BF16)
        base_ref[...] = jnp.zeros_like(base_ref)

    m = mod_ref[0]
    y = jnp.dot(o_ref[...], w_ref[...], preferred_element_type=F32)
    x1 = x_ref[...] + m[2:3] * y
    x1_ref[...] = x1
    h2 = _modulated_norm(x1, g_ref[...], m[4:5], m[3:4])
    h2_ref[...] = lax.bitcast_convert_type(_pack_bf16_pairs(h2), jnp.int32)

    logits = jnp.dot(h2.astype(BF16), rw_ref[...], preferred_element_type=F32) + rb_ref[...]
    lane = lax.broadcasted_iota(jnp.int32, logits.shape, 1).astype(F32)
    neg_inf = jnp.float32(-jnp.inf)
    big = jnp.float32(1e9)

    is_group = lane < N_GROUPS
    gl = jnp.where(is_group, logits, neg_inf)
    gmax = jnp.max(gl, axis=-1, keepdims=True)
    gidx = jnp.min(jnp.where(gl == gmax, lane, big), axis=-1, keepdims=True)
    gsum = jnp.sum(jnp.where(is_group, jnp.exp(logits - gmax), 0.0), axis=-1, keepdims=True)
    g_w = 1.0 / gsum

    lo = EXPERT_LANE0 + EXPERTS_PER_GROUP * gidx
    in_group = (lane >= lo) & (lane < lo + EXPERTS_PER_GROUP)
    el = jnp.where(in_group, logits, neg_inf)
    v0 = jnp.max(el, axis=-1, keepdims=True)
    i0 = jnp.min(jnp.where(el == v0, lane, big), axis=-1, keepdims=True)
    el = jnp.where(lane == i0, neg_inf, el)
    v1 = jnp.max(el, axis=-1, keepdims=True)
    i1 = jnp.min(jnp.where(el == v1, lane, big), axis=-1, keepdims=True)
    t = jnp.exp(v1 - v0)
    w0 = g_w / (1.0 + t)
    w1 = g_w * t / (1.0 + t)

    oh0 = jnp.where(lane == i0, 1.0, 0.0)
    oh1 = jnp.where(lane == i1, 1.0, 0.0)
    both = oh0 + oh1
    before = jnp.dot(tri_ref[...], both.astype(BF16), preferred_element_type=F32) + base_ref[...]
    r0 = jnp.sum(before * oh0, axis=-1, keepdims=True)
    r1 = jnp.sum(before * oh1, axis=-1, keepdims=True)
    base_ref[...] = base_ref[...] + jnp.sum(both, axis=0, keepdims=True)
    cnt_ref[...] = base_ref[...]

    e0 = i0 - EXPERT_LANE0
    e1 = i1 - EXPERT_LANE0
    idr = jnp.where(lane == 0, e0, jnp.where(lane == 1, e1, jnp.where(lane == 2, r0, r1)))
    idr_ref[...] = idr[:, :4].astype(jnp.int32)
    wts_ref[...] = jnp.where(lane == 0, w0, w1)[:, :2]


def _out_router(o, x, mod, g, w, rw, rb):
    tiles_per_batch = SEQ // OUT_TM
    return pl.pallas_call(
        _out_router_kernel,
        grid=(N_TOK // OUT_TM,),
        in_specs=[
            pl.BlockSpec((OUT_TM, D_MODEL), lambda i: (i, 0)),
            pl.BlockSpec((OUT_TM, D_MODEL), lambda i: (i, 0)),
            pl.BlockSpec((1, 6, D_MODEL), lambda i: (i // tiles_per_batch, 0, 0)),
            pl.BlockSpec((1, D_MODEL), lambda i: (0, 0)),
            pl.BlockSpec((D_MODEL, D_MODEL), lambda i: (0, 0)),
            pl.BlockSpec((D_MODEL, LANES), lambda i: (0, 0)),
            pl.BlockSpec((1, LANES), lambda i: (0, 0)),
        ],
        out_specs=[
            pl.BlockSpec((OUT_TM, D_MODEL), lambda i: (i, 0)),
            pl.BlockSpec((OUT_TM, ROW_WORDS), lambda i: (i, 0)),
            pl.BlockSpec((OUT_TM, 4), lambda i: (i, 0)),
            pl.BlockSpec((OUT_TM, 2), lambda i: (i, 0)),
            pl.BlockSpec((1, LANES), lambda i: (0, 0)),
        ],
        out_shape=[
            jax.ShapeDtypeStruct((N_TOK, D_MODEL), F32),
            jax.ShapeDtypeStruct((N_TOK, ROW_WORDS), jnp.int32),
            jax.ShapeDtypeStruct((N_TOK, 4), jnp.int32),
            jax.ShapeDtypeStruct((N_TOK, 2), F32),
            jax.ShapeDtypeStruct((1, LANES), F32),
        ],
        scratch_shapes=[
            pltpu.VMEM((OUT_TM, OUT_TM), BF16),
            pltpu.VMEM((1, LANES), F32),
        ],
        compiler_params=_cparams(("arbitrary",)),
        name="out_router",
    )(o, x, mod, g, w, rw, rb)


def _expert_kernel(ord_ref, active_ref, meta_ref, xs_ref, wg_hbm, wu_hbm, wd_hbm, ys_ref,
                   wg_f32, wu_f32, wd_f32, sems, wg_bf, wu_bf, wd_bf, *, layer):
    i = pl.program_id(0)
    n_used = meta_ref[0]
    n_active = meta_ref[1]

    def fetch(pos, slot):
        e = active_ref[pos]
        return (
            pltpu.make_async_copy(wg_hbm.at[layer, e], wg_f32.at[slot], sems.at[slot, 0]),
            pltpu.make_async_copy(wu_hbm.at[layer, e], wu_f32.at[slot], sems.at[slot, 1]),
            pltpu.make_async_copy(wd_hbm.at[layer, e], wd_f32.at[slot], sems.at[slot, 2]),
        )

    @pl.when(i == 0)
    def _():
        for copy in fetch(0, 0):
            copy.start()

        @pl.when(n_active > 1)
        def _():
            for copy in fetch(1, 1):
                copy.start()

    @pl.when(i < n_used)
    def _():
        pos = ord_ref[i]
        changed = (i == 0) | (pos != ord_ref[jnp.maximum(i - 1, 0)])
        for slot in range(2):
            @pl.when(changed & (pos % 2 == slot))
            def _():
                for copy in fetch(pos, slot):
                    copy.wait()
                wg_bf[...] = wg_f32[slot].astype(BF16)
                wu_bf[...] = wu_f32[slot].astype(BF16)
                wd_bf[...] = wd_f32[slot].astype(BF16)

                @pl.when(pos + 2 < n_active)
                def _():
                    for copy in fetch(pos + 2, slot):
                        copy.start()

        x = _unpack_bf16_pairs(lax.bitcast_convert_type(xs_ref[...], jnp.uint32)).astype(BF16)
        g = jnp.dot(x, wg_bf[...], preferred_element_type=F32)
        u = jnp.dot(x, wu_bf[...], preferred_element_type=F32)
        hid = (g * jax.nn.sigmoid(g)) * u
        y = jnp.dot(hid.astype(BF16), wd_bf[...], preferred_element_type=F32)
        ys_ref[...] = lax.bitcast_convert_type(_pack_bf16_pairs(y), jnp.int32)

    @pl.when(i >= n_used)
    def _():
        ys_ref[...] = jnp.zeros_like(ys_ref)


def _experts(layer, block_pos, active, meta, xs, wg, wu, wd):
    def in_row_map(i, block_pos, active, meta):
        return (jnp.minimum(i, meta[0] - 1), 0)

    def row_map(i, block_pos, active, meta):
        return (i, 0)

    return pl.pallas_call(
        functools.partial(_expert_kernel, layer=layer),
        grid_spec=pltpu.PrefetchScalarGridSpec(
            num_scalar_prefetch=3,
            grid=(N_BLOCKS,),
            in_specs=[
                pl.BlockSpec((MOE_BLK, ROW_WORDS), in_row_map),
                pl.BlockSpec(memory_space=pl.ANY),
                pl.BlockSpec(memory_space=pl.ANY),
                pl.BlockSpec(memory_space=pl.ANY),
            ],
            out_specs=pl.BlockSpec((MOE_BLK, ROW_WORDS), row_map),
            scratch_shapes=[
                pltpu.VMEM((2, D_MODEL, EXPERT_HIDDEN), F32),
                pltpu.VMEM((2, D_MODEL, EXPERT_HIDDEN), F32),
                pltpu.VMEM((2, EXPERT_HIDDEN, D_MODEL), F32),
                pltpu.SemaphoreType.DMA((2, 3)),
                pltpu.VMEM((D_MODEL, EXPERT_HIDDEN), BF16),
                pltpu.VMEM((D_MODEL, EXPERT_HIDDEN), BF16),
                pltpu.VMEM((EXPERT_HIDDEN, D_MODEL), BF16),
            ],
        ),
        out_shape=jax.ShapeDtypeStruct((N_SLOTS, ROW_WORDS), jnp.int32),
        compiler_params=_cparams(("arbitrary",)),
        name="experts",
    )(block_pos, active, meta, xs, wg, wu, wd)


def _sc_gather_rows(table, idx):
    n_rows = idx.shape[0]
    width = table.shape[1]
    workers = SC_CORES * SC_SUBCORES
    per_worker = n_rows // workers
    assert per_worker * workers == n_rows and per_worker % SC_CHUNK == 0
    mesh = plsc.VectorSubcoreMesh(core_axis_name="c", subcore_axis_name="s")

    n_chunks = per_worker // SC_CHUNK

    def body(table_hbm, idx_hbm, out_hbm, idx_v, buf0, buf1, gsem0, gsem1, wsem0, wsem1):
        wid = lax.axis_index("s") * SC_CORES + lax.axis_index("c")
        base = pl.multiple_of(wid * per_worker, SC_CHUNK)
        bufs, gsems, wsems = (buf0, buf1), (gsem0, gsem1), (wsem0, wsem1)
        pltpu.sync_copy(idx_hbm.at[pl.ds(base, per_worker)], idx_v)

        def gather(c):
            rows = idx_v.at[pl.ds(c * SC_CHUNK, SC_CHUNK)]
            return pltpu.async_copy(table_hbm.at[rows], bufs[c % 2], gsems[c % 2])

        def write(c):
            dst = out_hbm.at[pl.ds(base + c * SC_CHUNK, SC_CHUNK)]
            return pltpu.async_copy(bufs[c % 2], dst, wsems[c % 2])

        gathers = {0: gather(0)}
        writes = {}
        for c in range(n_chunks):
            if c + 1 < n_chunks:
                if c >= 1:
                    writes.pop(c - 1).wait()
                gathers[c + 1] = gather(c + 1)
            gathers.pop(c).wait()
            writes[c] = write(c)
        for c in sorted(writes):
            writes[c].wait()

    return pl.kernel(
        body,
        out_type=jax.ShapeDtypeStruct((n_rows, width), table.dtype),
        mesh=mesh,
        scratch_types=[
            pltpu.VMEM((per_worker,), jnp.int32),
            pltpu.VMEM((SC_CHUNK, width), table.dtype),
            pltpu.VMEM((SC_CHUNK, width), table.dtype),
            pltpu.SemaphoreType.DMA,
            pltpu.SemaphoreType.DMA,
            pltpu.SemaphoreType.DMA,
            pltpu.SemaphoreType.DMA,
        ],
        name="sc_gather_rows",
    )(table, idx)


def _sc_slot_tokens(dest_flat):
    n_assign = N_TOK * TOP_K
    lanes = SC_LANES
    mesh = plsc.VectorSubcoreMesh(core_axis_name="c", subcore_axis_name="s")

    def body(dest_hbm, out_hbm, dest_v, slot_v):
        wid = lax.axis_index("s") * SC_CORES + lax.axis_index("c")

        @pl.when(wid == 0)
        def _():
            pltpu.sync_copy(dest_hbm, dest_v)
            lane = lax.iota(jnp.int32, lanes)

            def init(i, _):
                start = pl.multiple_of(i * lanes, lanes)
                slot_v[pl.ds(start, lanes)] = (lane + start) & (N_TOK - 1)
                return 0

            def place(i, _):
                start = pl.multiple_of(i * lanes, lanes)
                slots = dest_v[pl.ds(start, lanes)]
                plsc.store_scatter(slot_v, [slots], (lane + start) >> 1)
                return 0

            lax.fori_loop(0, N_SLOTS // lanes, init, 0)
            lax.fori_loop(0, n_assign // lanes, place, 0)
            pltpu.sync_copy(slot_v, out_hbm)

    return pl.kernel(
        body,
        out_type=jax.ShapeDtypeStruct((N_SLOTS,), jnp.int32),
        mesh=mesh,
        scratch_types=[
            pltpu.VMEM((n_assign,), jnp.int32),
            pltpu.VMEM((N_SLOTS,), jnp.int32),
        ],
        compiler_params=pltpu.CompilerParams(needs_layout_passes=False),
        name="sc_slot_tokens",
    )(dest_flat)


def _mix_kernel(x_ref, wts_ref, mod_ref, fg_ref, y0_ref, y1_ref, o_ref):
    out = _expert_mix(x_ref[...], wts_ref[...], mod_ref[0][5:6], y0_ref[...], y1_ref[...])
    ms = jnp.mean(out * out, axis=-1, keepdims=True)
    o_ref[...] = out * lax.rsqrt(ms + RMS_EPS) * fg_ref[...]


def _mix(x1, wts, mod, y2, fg):
    nt = N_TOK // COMB_TM
    tiles_per_batch = SEQ // COMB_TM
    return pl.pallas_call(
        _mix_kernel,
        grid=(nt,),
        in_specs=[
            pl.BlockSpec((COMB_TM, D_MODEL), lambda i: (i, 0)),
            pl.BlockSpec((COMB_TM, TOP_K), lambda i: (i, 0)),
            pl.BlockSpec((1, 6, D_MODEL), lambda i: (i // tiles_per_batch, 0, 0)),
            pl.BlockSpec((1, D_MODEL), lambda i: (0, 0)),
            pl.BlockSpec((COMB_TM, ROW_WORDS), lambda i: (i, 0)),
            pl.BlockSpec((COMB_TM, ROW_WORDS), lambda i: (nt + i, 0)),
        ],
        out_specs=pl.BlockSpec((COMB_TM, D_MODEL), lambda i: (i, 0)),
        out_shape=jax.ShapeDtypeStruct((N_TOK, D_MODEL), F32),
        compiler_params=_cparams(("arbitrary",)),
        name="mix",
    )(x1, wts, mod, fg, y2, y2)


def _routing_tables(idr, cnt):
    counts = cnt[0, EXPERT_LANE0:EXPERT_LANE0 + N_EXPERTS].astype(jnp.int32)
    padded = (counts + MOE_BLK - 1) // MOE_BLK * MOE_BLK
    pad_ends = jnp.cumsum(padded)
    pad_starts = pad_ends - padded
    experts = jnp.arange(N_EXPERTS, dtype=jnp.int32)
    is_expert = idr[:, 0:2, None] == experts
    dest = jnp.sum(jnp.where(is_expert, pad_starts, 0), axis=-1) + idr[:, 2:4]
    block_start = jnp.arange(N_BLOCKS, dtype=jnp.int32) * MOE_BLK
    bexp = jnp.minimum(jnp.sum(block_start[:, None] >= pad_ends[None, :], axis=1),
                       N_EXPERTS - 1).astype(jnp.int32)
    owns = padded > 0
    pos_of_expert = jnp.cumsum(owns.astype(jnp.int32)) - 1
    active = jnp.sum(jnp.where(owns[None, :] & (pos_of_expert[None, :] == experts[:, None]),
                               experts[None, :], 0), axis=1).astype(jnp.int32)
    block_pos = jnp.sum(jnp.where(bexp[:, None] == experts[None, :], pos_of_expert[None, :], 0),
                        axis=1).astype(jnp.int32)
    meta = jnp.stack([pad_ends[-1] // MOE_BLK, jnp.sum(owns)]).astype(jnp.int32)
    return dest.astype(jnp.int32), block_pos, active, meta


def _router_weights(w_group, b_group, w_expert, b_expert):
    rw = jnp.zeros((D_MODEL, LANES), F32)
    rw = rw.at[:, :N_GROUPS].set(w_group)
    rw = rw.at[:, EXPERT_LANE0:EXPERT_LANE0 + N_EXPERTS].set(w_expert)
    rb = jnp.zeros((1, LANES), F32)
    rb = rb.at[0, :N_GROUPS].set(b_group)
    rb = rb.at[0, EXPERT_LANE0:EXPERT_LANE0 + N_EXPERTS].set(b_expert.reshape(-1))
    return rw.astype(BF16), rb


def _scaled_qkv_weight(w_in, head_dim):
    scale = jnp.concatenate([jnp.full((D_MODEL,), LOG2_E / math.sqrt(head_dim), F32),
                             jnp.ones((2 * D_MODEL,), F32)])
    return (w_in * scale).astype(BF16)


def kernel(x, c, norm1_g, norm2_g, ada_w, ada_b, diff_w_in, diff_w_out, diff_lambda_q1, diff_lambda_k1, diff_lambda_q2, diff_lambda_k2, diff_subln_g, sb_w_in, sb_w_out, router_group_w, router_group_b, router_expert_w, router_expert_b, expert_w_gate, expert_w_up, expert_w_down, final_norm_g):
    xf = x.reshape(N_TOK, D_MODEL)
    mod_all = _ada(c, ada_w, ada_b)
    slopes = jnp.exp2(-8.0 * jnp.arange(1, DIFF_HEADS + 1, dtype=F32) / DIFF_HEADS)

    pending = None
    for i in range(DEPTH):
        mod = mod_all[i].reshape(BATCH, 6, D_MODEL)
        j = i // 2
        is_diff = i % 2 == 0
        w_in = (_scaled_qkv_weight(diff_w_in[j], DIFF_HEAD_DIM) if is_diff
                else _scaled_qkv_weight(sb_w_in[j], SB_HEAD_DIM))
        g1 = norm1_g[i].reshape(1, D_MODEL)
        if pending is None:
            qkv = _qkv(xf, mod, g1, w_in)
        else:
            xf, qkv = _mix_qkv(*pending, mod, g1, w_in)
        if is_diff:
            lambda_init = 0.8 - 0.6 * math.exp(-0.3 * i)
            lam = (jnp.exp(jnp.sum(diff_lambda_q1[j] * diff_lambda_k1[j]))
                   - jnp.exp(jnp.sum(diff_lambda_q2[j] * diff_lambda_k2[j]))
                   + lambda_init).reshape(1, 1)
            o = _diff_attn(qkv, slopes, lam, diff_subln_g[j].reshape(1, LANES), lambda_init)
            w_out = diff_w_out[j]
        else:
            o = _sb_attn(qkv)
            w_out = sb_w_out[j]
        rw, rb = _router_weights(router_group_w[i], router_group_b[i],
                                 router_expert_w[i], router_expert_b[i])
        x1, h2, idr, wts, cnt = _out_router(o, xf, mod, norm2_g[i].reshape(1, D_MODEL),
                                            w_out.astype(BF16), rw, rb)
        dest, block_pos, active, meta = _routing_tables(idr, cnt)
        xs = _sc_gather_rows(h2, _sc_slot_tokens(dest.reshape(-1)))
        ys = _experts(i, block_pos, active, meta, xs,
                      expert_w_gate, expert_w_up, expert_w_down)
        y2 = _sc_gather_rows(ys, dest.T.reshape(-1))
        pending = (x1, wts, mod, y2)
    out = _mix(*pending, final_norm_g.reshape(1, D_MODEL))
    return out.reshape(BATCH, SEQ, D_MODEL)
```

```python
import functools
import math

import jax
import jax.numpy as jnp
from jax import lax
from jax.experimental import pallas as pl
from jax.experimental.pallas import tpu as pltpu
from jax.experimental.pallas import tpu_sc as plsc

D_MODEL = 1024
BATCH = 8
SEQ = 2048
DEPTH = 2
N_TOK = BATCH * SEQ

CHUNK = 64
DIFF_HEADS = 8
DIFF_HEAD_DIM = D_MODEL // (2 * DIFF_HEADS)
SB_HEADS = 16
SB_HEAD_DIM = D_MODEL // SB_HEADS
N_GROUPS = 4
EXPERTS_PER_GROUP = 8
N_EXPERTS = N_GROUPS * EXPERTS_PER_GROUP
TOP_K = 2
EXPERT_HIDDEN = D_MODEL // 2
RMS_EPS = 1e-6
SUBLN_EPS = 1e-5

LANES = 128
EXPERT_LANE0 = 32
EXP2_UNDERFLOW = -150.0
LOG2_E = math.log2(math.e)

ADA_TN = 1536
QKV_TM = 512
DIFF_T = 512
DIFF_GROUP = 2
SB_T = 256
SB_PAIRS = 4
OUT_TM = 512
MOE_BLK = 512
N_SLOTS = N_TOK * TOP_K + N_EXPERTS * MOE_BLK
N_BLOCKS = N_SLOTS // MOE_BLK
COMB_TM = 512
ROW_WORDS = D_MODEL // 2

SC_CORES = 2
SC_SUBCORES = 16
SC_LANES = 16
SC_CHUNK = 64

VMEM_LIMIT = 56 * 1024 * 1024

F32 = jnp.float32
BF16 = jnp.bfloat16


def _cparams(sem):
    return pltpu.CompilerParams(dimension_semantics=sem, vmem_limit_bytes=VMEM_LIMIT)


def _ada_kernel(c_ref, w_ref, b_ref, o_ref):
    c = c_ref[...]
    cond = c * jax.nn.sigmoid(c)
    o_ref[0] = jnp.dot(cond, w_ref[0], preferred_element_type=F32,
                       precision=lax.Precision.HIGHEST) + b_ref[0]


def _ada(c, ada_w, ada_b):
    six_d = ada_w.shape[-1]
    return pl.pallas_call(
        _ada_kernel,
        grid=(DEPTH, six_d // ADA_TN),
        in_specs=[
            pl.BlockSpec((BATCH, D_MODEL), lambda l, n: (0, 0)),
            pl.BlockSpec((1, D_MODEL, ADA_TN), lambda l, n: (l, 0, n)),
            pl.BlockSpec((1, 1, ADA_TN), lambda l, n: (l, 0, n)),
        ],
        out_specs=pl.BlockSpec((1, BATCH, ADA_TN), lambda l, n: (l, 0, n)),
        out_shape=jax.ShapeDtypeStruct((DEPTH, BATCH, six_d), F32),
        compiler_params=_cparams(("arbitrary", "arbitrary")),
        name="ada",
    )(c, ada_w, ada_b.reshape(DEPTH, 1, six_d))


def _modulated_norm(x, g, scale, shift):
    ms = jnp.mean(x * x, axis=-1, keepdims=True)
    return x * lax.rsqrt(ms + RMS_EPS) * (g * (1.0 + scale)) + shift


def _qkv_kernel(x_ref, mod_ref, g_ref, w_ref, o_ref):
    m = mod_ref[0]
    h = _modulated_norm(x_ref[...], g_ref[...], m[1:2], m[0:1]).astype(BF16)
    for n in range(3):
        cols = slice(n * D_MODEL, (n + 1) * D_MODEL)
        o_ref[:, cols] = jnp.dot(h, w_ref[:, cols], preferred_element_type=F32).astype(BF16)


def _qkv(x, mod, g, w):
    tiles_per_batch = SEQ // QKV_TM
    return pl.pallas_call(
        _qkv_kernel,
        grid=(N_TOK // QKV_TM,),
        in_specs=[
            pl.BlockSpec((QKV_TM, D_MODEL), lambda i: (i, 0)),
            pl.BlockSpec((1, 6, D_MODEL), lambda i: (i // tiles_per_batch, 0, 0)),
            pl.BlockSpec((1, D_MODEL), lambda i: (0, 0)),
            pl.BlockSpec((D_MODEL, 3 * D_MODEL), lambda i: (0, 0)),
        ],
        out_specs=pl.BlockSpec((QKV_TM, 3 * D_MODEL), lambda i: (i, 0)),
        out_shape=jax.ShapeDtypeStruct((N_TOK, 3 * D_MODEL), BF16),
        compiler_params=_cparams(("arbitrary",)),
        name="qkv",
    )(x, mod, g, w)


def _expert_mix(x, wts, gate, y0_words, y1_words):
    y0 = _unpack_bf16_pairs(lax.bitcast_convert_type(y0_words, jnp.uint32))
    y1 = _unpack_bf16_pairs(lax.bitcast_convert_type(y1_words, jnp.uint32))
    return x + gate * (wts[:, 0:1] * y0 + wts[:, 1:2] * y1)


def _mix_qkv_kernel(x1_ref, wts_ref, pmod_ref, y0_ref, y1_ref, mod_ref, g_ref, w_ref,
                    x_ref, o_ref):
    x = _expert_mix(x1_ref[...], wts_ref[...], pmod_ref[0][5:6], y0_ref[...], y1_ref[...])
    x_ref[...] = x
    m = mod_ref[0]
    h = _modulated_norm(x, g_ref[...], m[1:2], m[0:1]).astype(BF16)
    for n in range(3):
        cols = slice(n * D_MODEL, (n + 1) * D_MODEL)
        o_ref[:, cols] = jnp.dot(h, w_ref[:, cols], preferred_element_type=F32).astype(BF16)


def _mix_qkv(x1, wts, prev_mod, y2, mod, g, w):
    nt = N_TOK // QKV_TM
    tiles_per_batch = SEQ // QKV_TM
    return pl.pallas_call(
        _mix_qkv_kernel,
        grid=(nt,),
        in_specs=[
            pl.BlockSpec((QKV_TM, D_MODEL), lambda i: (i, 0)),
            pl.BlockSpec((QKV_TM, TOP_K), lambda i: (i, 0)),
            pl.BlockSpec((1, 6, D_MODEL), lambda i: (i // tiles_per_batch, 0, 0)),
            pl.BlockSpec((QKV_TM, ROW_WORDS), lambda i: (i, 0)),
            pl.BlockSpec((QKV_TM, ROW_WORDS), lambda i: (nt + i, 0)),
            pl.BlockSpec((1, 6, D_MODEL), lambda i: (i // tiles_per_batch, 0, 0)),
            pl.BlockSpec((1, D_MODEL), lambda i: (0, 0)),
            pl.BlockSpec((D_MODEL, 3 * D_MODEL), lambda i: (0, 0)),
        ],
        out_specs=[
            pl.BlockSpec((QKV_TM, D_MODEL), lambda i: (i, 0)),
            pl.BlockSpec((QKV_TM, 3 * D_MODEL), lambda i: (i, 0)),
        ],
        out_shape=[
            jax.ShapeDtypeStruct((N_TOK, D_MODEL), F32),
            jax.ShapeDtypeStruct((N_TOK, 3 * D_MODEL), BF16),
        ],
        compiler_params=_cparams(("arbitrary",)),
        name="mix_qkv",
    )(x1, wts, prev_mod, y2, y2, mod, g, w)


def _half_masked(q, upper):
    lane = lax.broadcasted_iota(jnp.int32, q.shape, 1)
    keep = (lane >= LANES // 2) if upper else (lane < LANES // 2)
    return jnp.where(keep, q, jnp.zeros_like(q))


def _pack_bf16_pairs(x):
    half = x.shape[1] // 2
    hi = lax.bitcast_convert_type(x[:, :half].astype(BF16).astype(F32), jnp.uint32)
    lo = lax.bitcast_convert_type(x[:, half:].astype(BF16).astype(F32), jnp.uint32)
    return hi | (lo >> 16)


def _unpack_bf16_pairs(u):
    hi = lax.bitcast_convert_type(u & jnp.uint32(0xFFFF0000), F32)
    lo = lax.bitcast_convert_type(u << 16, F32)
    return jnp.concatenate([hi, lo], axis=1)


def _lane_tile(x, n):
    return jnp.concatenate([x] * n, axis=1)


def _qk(q, k):
    return lax.dot_general(q, k, (((1,), (1,)), ((), ())), preferred_element_type=F32)


def _diff_attn_kernel(slopes_ref, lam_ref, q_ref, k_ref, v_ref, g_ref, o_ref, *scratch,
                      lambda_init):
    n_chains = 2 * DIFF_GROUP
    m_refs = scratch[:n_chains]
    acc_refs = scratch[n_chains:]
    ones = jnp.ones((DIFF_T, LANES), BF16)

    def head_lanes(c):
        return slice((c // 2) * LANES, (c // 2 + 1) * LANES)

    row = lax.broadcasted_iota(jnp.int32, (DIFF_T, DIFF_T), 0)
    col = lax.broadcasted_iota(jnp.int32, (DIFF_T, DIFF_T), 1)
    allowed = (col // CHUNK) <= (row // CHUNK)
    slopes = [slopes_ref[pl.program_id(1) * DIFF_GROUP + g] * LOG2_E for g in range(DIFF_GROUP)]
    rels = [(row - col).astype(F32) * slope for slope in slopes]
    diag_biases = [jnp.where(allowed, jnp.abs(rel), jnp.inf) for rel in rels]
    out_gain = g_ref[...] * (1.0 - lambda_init)

    def key_block(j, c):
        k = k_ref[pl.ds(j * DIFF_T, DIFF_T), head_lanes(c)]
        v1 = jnp.concatenate([v_ref[pl.ds(j * DIFF_T, DIFF_T), head_lanes(c)], ones], axis=1)
        return k, v1

    def query_tile(qi, _):
        q = q_ref[pl.ds(qi * DIFF_T, DIFF_T), :]
        qm = [_half_masked(q[:, head_lanes(c)], c % 2 == 1) for c in range(n_chains)]

        kv = [key_block(qi, c) for c in range(n_chains)]
        s_maps = [_qk(qm[c], kv[c][0]) - diag_biases[c // 2] for c in range(n_chains)]
        for c in range(n_chains):
            m = jnp.max(s_maps[c], axis=-1, keepdims=True)
            p = jnp.exp2(s_maps[c] - m)
            m_refs[c][...] = jnp.broadcast_to(m, (DIFF_T, LANES))
            acc_refs[c][...] = jnp.dot(p.astype(BF16), kv[c][1], preferred_element_type=F32)

        def body(j, _):
            kv = [key_block(j, c) for c in range(n_chains)]
            ahead = jnp.asarray((qi - j) * DIFF_T).astype(F32)
            s_maps = [_qk(qm[c], kv[c][0]) - rels[c // 2] for c in range(n_chains)]
            for c in range(n_chains):
                shift = slopes[c // 2] * ahead
                m = m_refs[c][...]
                m_new = jnp.maximum(m, jnp.max(s_maps[c], axis=-1, keepdims=True) - shift)
                alpha = jnp.exp2(m - m_new)
                p = jnp.exp2(s_maps[c] - _lane_tile(m_new + shift, DIFF_T // LANES))
                m_refs[c][...] = m_new
                acc_refs[c][...] = _lane_tile(alpha, 2) * acc_refs[c][...] + jnp.dot(
                    p.astype(BF16), kv[c][1], preferred_element_type=F32)
            return 0

        lax.fori_loop(0, qi, body, 0)
        for g in range(DIFF_GROUP):
            a0 = acc_refs[2 * g][...]
            a1 = acc_refs[2 * g + 1][...]
            o = a0[:, :LANES] / a0[:, LANES:] - lam_ref[...] * (a1[:, :LANES] / a1[:, LANES:])
            ms = jnp.mean(o * o, axis=-1, keepdims=True)
            o = o * lax.rsqrt(ms + SUBLN_EPS) * out_gain
            o_ref[pl.ds(qi * DIFF_T, DIFF_T), head_lanes(2 * g)] = o.astype(BF16)
        return 0

    lax.fori_loop(0, SEQ // DIFF_T, query_tile, 0)


def _diff_attn(qkv, slopes, lam, subln_g, lambda_init):
    groups = DIFF_HEADS // DIFF_GROUP
    width = DIFF_GROUP * LANES
    return pl.pallas_call(
        functools.partial(_diff_attn_kernel, lambda_init=lambda_init),
        grid=(BATCH, groups),
        in_specs=[
            pl.BlockSpec(memory_space=pltpu.SMEM),
            pl.BlockSpec((1, 1), lambda b, h: (0, 0)),
            pl.BlockSpec((SEQ, width), lambda b, h: (b, h)),
            pl.BlockSpec((SEQ, width), lambda b, h: (b, groups + h)),
            pl.BlockSpec((SEQ, width), lambda b, h: (b, 2 * groups + h)),
            pl.BlockSpec((1, LANES), lambda b, h: (0, 0)),
        ],
        out_specs=pl.BlockSpec((SEQ, width), lambda b, h: (b, h)),
        out_shape=jax.ShapeDtypeStruct((N_TOK, D_MODEL), BF16),
        scratch_shapes=([pltpu.VMEM((DIFF_T, LANES), F32)] * (2 * DIFF_GROUP)
                        + [pltpu.VMEM((DIFF_T, 2 * LANES), F32)] * (2 * DIFF_GROUP)),
        compiler_params=_cparams(("arbitrary", "arbitrary")),
        name="diff_attn",
    )(slopes, lam, qkv, qkv, qkv, subln_g)


def _sb_attn_kernel(q_ref, k_ref, v_ref, o_ref, *scratch):
    n_heads = 2 * SB_PAIRS
    tail_refs = scratch[:n_heads]
    acc_refs = scratch[n_heads:]

    def pair_lanes(h):
        return slice((h // 2) * LANES, (h // 2 + 1) * LANES)

    row = lax.broadcasted_iota(jnp.int32, (SB_T, SB_T), 0)
    col = lax.broadcasted_iota(jnp.int32, (SB_T, SB_T), 1)
    strict = col < row
    neg_from = jnp.where(row >= col, -1.0, 0.0).astype(BF16)

    def scores(j, h, qhead, mask):
        z = _qk(qhead, k_ref[pl.ds(j * SB_T, SB_T), pair_lanes(h)])
        sp = jnp.maximum(z, 0.0) + jnp.log2(1.0 + jnp.exp2(-jnp.abs(z)))
        if mask is not None:
            sp = jnp.where(mask, sp, 0.0)
        return z, sp

    def weighted(j, h, z, sp, tail, mask):
        log_a = (z + jnp.dot(sp.astype(BF16), neg_from, preferred_element_type=F32)
                 + _lane_tile(tail, SB_T // LANES))
        a = jnp.exp2(log_a)
        if mask is not None:
            a = jnp.where(mask, a, 0.0)
        return jnp.dot(a.astype(BF16), v_ref[pl.ds(j * SB_T, SB_T), pair_lanes(h)],
                       preferred_element_type=F32)

    def row_sum(sp):
        return jnp.broadcast_to(jnp.sum(sp, axis=-1, keepdims=True), (SB_T, LANES))

    zero = jnp.zeros((SB_T, LANES), F32)
    lane = lax.broadcasted_iota(jnp.int32, (SB_T, LANES), 1)

    def query_heads(qi):
        q = q_ref[pl.ds(qi * SB_T, SB_T), :]
        return [_half_masked(q[:, pair_lanes(h)], h % 2 == 1) for h in range(n_heads)]

    def finish(qi):
        for h in range(0, n_heads, 2):
            o_ref[pl.ds(qi * SB_T, SB_T), pair_lanes(h)] = jnp.where(
                lane < LANES // 2, acc_refs[h][...], acc_refs[h + 1][...]).astype(BF16)

    qh = query_heads(0)
    for h in range(n_heads):
        z, sp = scores(0, h, qh[h], strict)
        acc_refs[h][...] = weighted(0, h, z, sp, zero, strict)
    finish(0)

    def query_tile(qi, _):
        qh = query_heads(qi)
        zs = [(scores(qi, h, qh[h], strict), scores(qi - 1, h, qh[h], None))
              for h in range(n_heads)]
        for h in range(n_heads):
            (z0, sp0), (z1, sp1) = zs[h]
            tail0 = -row_sum(sp0)
            acc_refs[h][...] = (weighted(qi, h, z0, sp0, zero, strict)
                                + weighted(qi - 1, h, z1, sp1, tail0, None))
            tail_refs[h][...] = tail0 - row_sum(sp1)

        def cond(state):
            j, live = state
            return (j >= 0) & (live > EXP2_UNDERFLOW)

        def body(state):
            j, _ = state
            live = jnp.float32(-jnp.inf)
            for h in range(n_heads):
                z, sp = scores(j, h, qh[h], None)
                tail = tail_refs[h][...]
                acc_refs[h][...] += weighted(j, h, z, sp, tail, None)
                tail = tail - row_sum(sp)
                tail_refs[h][...] = tail
                live = jnp.maximum(live, jnp.max(tail))
            return j - 1, live

        live = functools.reduce(jnp.maximum, [jnp.max(t[...]) for t in tail_refs])
        lax.while_loop(cond, body, (qi - 2, live))
        finish(qi)
        return 0

    lax.fori_loop(1, SEQ // SB_T, query_tile, 0)


def _sb_attn(qkv):
    groups = SB_HEADS // (2 * SB_PAIRS)
    width = SB_PAIRS * LANES
    return pl.pallas_call(
        _sb_attn_kernel,
        grid=(BATCH, groups),
        in_specs=[
            pl.BlockSpec((SEQ, width), lambda b, h: (b, h)),
            pl.BlockSpec((SEQ, width), lambda b, h: (b, groups + h)),
            pl.BlockSpec((SEQ, width), lambda b, h: (b, 2 * groups + h)),
        ],
        out_specs=pl.BlockSpec((SEQ, width), lambda b, h: (b, h)),
        out_shape=jax.ShapeDtypeStruct((N_TOK, D_MODEL), BF16),
        scratch_shapes=[pltpu.VMEM((SB_T, LANES), F32)] * (4 * SB_PAIRS),
        compiler_params=_cparams(("arbitrary", "arbitrary")),
        name="sb_attn",
    )(qkv, qkv, qkv)


def _out_router_kernel(o_ref, x_ref, mod_ref, g_ref, w_ref, rw_ref, rb_ref,
                       x1_ref, h2_ref, idr_ref, wts_ref, cnt_ref, tri_ref, base_ref):
    i = pl.program_id(0)

    @pl.when(i == 0)
    def _():
        r = lax.broadcasted_iota(jnp.int32, (OUT_TM, OUT_TM), 0)
        c = lax.broadcasted_iota(jnp.int32, (OUT_TM, OUT_TM), 1)
        tri_ref[...] = jnp.where(c < r, 1.0, 0.0).astype(BF16)
        base_ref[...] = jnp.zeros_like(base_ref)

    m = mod_ref[0]
    y = jnp.dot(o_ref[...], w_ref[...], preferred_element_type=F32)
    x1 = x_ref[...] + m[2:3] * y
    x1_ref[...] = x1
    h2 = _modulated_norm(x1, g_ref[...], m[4:5], m[3:4])
    h2_ref[...] = lax.bitcast_convert_type(_pack_bf16_pairs(h2), jnp.int32)

    logits = jnp.dot(h2.astype(BF16), rw_ref[...], preferred_element_type=F32) + rb_ref[...]
    lane = lax.broadcasted_iota(jnp.int32, logits.shape, 1).astype(F32)
    neg_inf = jnp.float32(-jnp.inf)
    big = jnp.float32(1e9)

    is_group = lane < N_GROUPS
    gl = jnp.where(is_group, logits, neg_inf)
    gmax = jnp.max(gl, axis=-1, keepdims=True)
    gidx = jnp.min(jnp.where(gl == gmax, lane, big), axis=-1, keepdims=True)
    gsum = jnp.sum(jnp.where(is_group, jnp.exp(logits - gmax), 0.0), axis=-1, keepdims=True)
    g_w = 1.0 / gsum

    lo = EXPERT_LANE0 + EXPERTS_PER_GROUP * gidx
    in_group = (lane >= lo) & (lane < lo + EXPERTS_PER_GROUP)
    el = jnp.where(in_group, logits, neg_inf)
    v0 = jnp.max(el, axis=-1, keepdims=True)
    i0 = jnp.min(jnp.where(el == v0, lane, big), axis=-1, keepdims=True)
    el = jnp.where(lane == i0, neg_inf, el)
    v1 = jnp.max(el, axis=-1, keepdims=True)
    i1 = jnp.min(jnp.where(el == v1, lane, big), axis=-1, keepdims=True)
    t = jnp.exp(v1 - v0)
    w0 = g_w / (1.0 + t)
    w1 = g_w * t / (1.0 + t)

    oh0 = jnp.where(lane == i0, 1.0, 0.0)
    oh1 = jnp.where(lane == i1, 1.0, 0.0)
    both = oh0 + oh1
    before = jnp.dot(tri_ref[...], both.astype(BF16), preferred_element_type=F32) + base_ref[...]
    r0 = jnp.sum(before * oh0, axis=-1, keepdims=True)
    r1 = jnp.sum(before * oh1, axis=-1, keepdims=True)
    base_ref[...] = base_ref[...] + jnp.sum(both, axis=0, keepdims=True)
    cnt_ref[...] = base_ref[...]

    e0 = i0 - EXPERT_LANE0
    e1 = i1 - EXPERT_LANE0
    idr = jnp.where(lane == 0, e0, jnp.where(lane == 1, e1, jnp.where(lane == 2, r0, r1)))
    idr_ref[...] = idr[:, :4].astype(jnp.int32)
    wts_ref[...] = jnp.where(lane == 0, w0, w1)[:, :2]


def _out_router(o, x, mod, g, w, rw, rb):
    tiles_per_batch = SEQ // OUT_TM
    return pl.pallas_call(
        _out_router_kernel,
        grid=(N_TOK // OUT_TM,),
        in_specs=[
            pl.BlockSpec((OUT_TM, D_MODEL), lambda i: (i, 0)),
            pl.BlockSpec((OUT_TM, D_MODEL), lambda i: (i, 0)),
            pl.BlockSpec((1, 6, D_MODEL), lambda i: (i // tiles_per_batch, 0, 0)),
            pl.BlockSpec((1, D_MODEL), lambda i: (0, 0)),
            pl.BlockSpec((D_MODEL, D_MODEL), lambda i: (0, 0)),
            pl.BlockSpec((D_MODEL, LANES), lambda i: (0, 0)),
            pl.BlockSpec((1, LANES), lambda i: (0, 0)),
        ],
        out_specs=[
            pl.BlockSpec((OUT_TM, D_MODEL), lambda i: (i, 0)),
            pl.BlockSpec((OUT_TM, ROW_WORDS), lambda i: (i, 0)),
            pl.BlockSpec((OUT_TM, 4), lambda i: (i, 0)),
            pl.BlockSpec((OUT_TM, 2), lambda i: (i, 0)),
            pl.BlockSpec((1, LANES), lambda i: (0, 0)),
        ],
        out_shape=[
            jax.ShapeDtypeStruct((N_TOK, D_MODEL), F32),
            jax.ShapeDtypeStruct((N_TOK, ROW_WORDS), jnp.int32),
            jax.ShapeDtypeStruct((N_TOK, 4), jnp.int32),
            jax.ShapeDtypeStruct((N_TOK, 2), F32),
            jax.ShapeDtypeStruct((1, LANES), F32),
        ],
        scratch_shapes=[
            pltpu.VMEM((OUT_TM, OUT_TM), BF16),
            pltpu.VMEM((1, LANES), F32),
        ],
        compiler_params=_cparams(("arbitrary",)),
        name="out_router",
    )(o, x, mod, g, w, rw, rb)


def _expert_kernel(ord_ref, active_ref, meta_ref, xs_ref, wg_hbm, wu_hbm, wd_hbm, ys_ref,
                   wg_f32, wu_f32, wd_f32, sems, wg_bf, wu_bf, wd_bf, *, layer):
    i = pl.program_id(0)
    n_used = meta_ref[0]
    n_active = meta_ref[1]

    def fetch(pos, slot):
        e = active_ref[pos]
        return (
            pltpu.make_async_copy(wg_hbm.at[layer, e], wg_f32.at[slot], sems.at[slot, 0]),
            pltpu.make_async_copy(wu_hbm.at[layer, e], wu_f32.at[slot], sems.at[slot, 1]),
            pltpu.make_async_copy(wd_hbm.at[layer, e], wd_f32.at[slot], sems.at[slot, 2]),
        )

    @pl.when(i == 0)
    def _():
        for copy in fetch(0, 0):
            copy.start()

        @pl.when(n_active > 1)
        def _():
            for copy in fetch(1, 1):
                copy.start()

    @pl.when(i < n_used)
    def _():
        pos = ord_ref[i]
        changed = (i == 0) | (pos != ord_ref[jnp.maximum(i - 1, 0)])
        for slot in range(2):
            @pl.when(changed & (pos % 2 == slot))
            def _():
                for copy in fetch(pos, slot):
                    copy.wait()
                wg_bf[...] = wg_f32[slot].astype(BF16)
                wu_bf[...] = wu_f32[slot].astype(BF16)
                wd_bf[...] = wd_f32[slot].astype(BF16)

                @pl.when(pos + 2 < n_active)
                def _():
                    for copy in fetch(pos + 2, slot):
                        copy.start()

        x = _unpack_bf16_pairs(lax.bitcast_convert_type(xs_ref[...], jnp.uint32)).astype(BF16)
        g = jnp.dot(x, wg_bf[...], preferred_element_type=F32)
        u = jnp.dot(x, wu_bf[...], preferred_element_type=F32)
        hid = (g * jax.nn.sigmoid(g)) * u
        y = jnp.dot(hid.astype(BF16), wd_bf[...], preferred_element_type=F32)
        ys_ref[...] = lax.bitcast_convert_type(_pack_bf16_pairs(y), jnp.int32)

    @pl.when(i >= n_used)
    def _():
        ys_ref[...] = jnp.zeros_like(ys_ref)


def _experts(layer, block_pos, active, meta, xs, wg, wu, wd):
    def in_row_map(i, block_pos, active, meta):
        return (jnp.minimum(i, meta[0] - 1), 0)

    def row_map(i, block_pos, active, meta):
        return (i, 0)

    return pl.pallas_call(
        functools.partial(_expert_kernel, layer=layer),
        grid_spec=pltpu.PrefetchScalarGridSpec(
            num_scalar_prefetch=3,
            grid=(N_BLOCKS,),
            in_specs=[
                pl.BlockSpec((MOE_BLK, ROW_WORDS), in_row_map),
                pl.BlockSpec(memory_space=pl.ANY),
                pl.BlockSpec(memory_space=pl.ANY),
                pl.BlockSpec(memory_space=pl.ANY),
            ],
            out_specs=pl.BlockSpec((MOE_BLK, ROW_WORDS), row_map),
            scratch_shapes=[
                pltpu.VMEM((2, D_MODEL, EXPERT_HIDDEN), F32),
                pltpu.VMEM((2, D_MODEL, EXPERT_HIDDEN), F32),
                pltpu.VMEM((2, EXPERT_HIDDEN, D_MODEL), F32),
                pltpu.SemaphoreType.DMA((2, 3)),
                pltpu.VMEM((D_MODEL, EXPERT_HIDDEN), BF16),
                pltpu.VMEM((D_MODEL, EXPERT_HIDDEN), BF16),
                pltpu.VMEM((EXPERT_HIDDEN, D_MODEL), BF16),
            ],
        ),
        out_shape=jax.ShapeDtypeStruct((N_SLOTS, ROW_WORDS), jnp.int32),
        compiler_params=_cparams(("arbitrary",)),
        name="experts",
    )(block_pos, active, meta, xs, wg, wu, wd)


def _sc_gather_rows(table, idx):
    n_rows = idx.shape[0]
    width = table.shape[1]
    workers = SC_CORES * SC_SUBCORES
    per_worker = n_rows // workers
    assert per_worker * workers == n_rows and per_worker % SC_CHUNK == 0
    mesh = plsc.VectorSubcoreMesh(core_axis_name="c", subcore_axis_name="s")

    n_chunks = per_worker // SC_CHUNK

    def body(table_hbm, idx_hbm, out_hbm, idx_v, buf0, buf1, gsem0, gsem1, wsem0, wsem1):
        wid = lax.axis_index("s") * SC_CORES + lax.axis_index("c")
        base = pl.multiple_of(wid * per_worker, SC_CHUNK)
        bufs, gsems, wsems = (buf0, buf1), (gsem0, gsem1), (wsem0, wsem1)
        pltpu.sync_copy(idx_hbm.at[pl.ds(base, per_worker)], idx_v)

        def gather(c):
            rows = idx_v.at[pl.ds(c * SC_CHUNK, SC_CHUNK)]
            return pltpu.async_copy(table_hbm.at[rows], bufs[c % 2], gsems[c % 2])

        def write(c):
            dst = out_hbm.at[pl.ds(base + c * SC_CHUNK, SC_CHUNK)]
            return pltpu.async_copy(bufs[c % 2], dst, wsems[c % 2])

        gathers = {0: gather(0)}
        writes = {}
        for c in range(n_chunks):
            if c + 1 < n_chunks:
                if c >= 1:
                    writes.pop(c - 1).wait()
                gathers[c + 1] = gather(c + 1)
            gathers.pop(c).wait()
            writes[c] = write(c)
        for c in sorted(writes):
            writes[c].wait()

    return pl.kernel(
        body,
        out_type=jax.ShapeDtypeStruct((n_rows, width), table.dtype),
        mesh=mesh,
        scratch_types=[
            pltpu.VMEM((per_worker,), jnp.int32),
            pltpu.VMEM((SC_CHUNK, width), table.dtype),
            pltpu.VMEM((SC_CHUNK, width), table.dtype),
            pltpu.SemaphoreType.DMA,
            pltpu.SemaphoreType.DMA,
            pltpu.SemaphoreType.DMA,
            pltpu.SemaphoreType.DMA,
        ],
        name="sc_gather_rows",
    )(table, idx)


def _sc_slot_tokens(dest_flat):
    n_assign = N_TOK * TOP_K
    lanes = SC_LANES
    mesh = plsc.VectorSubcoreMesh(core_axis_name="c", subcore_axis_name="s")

    def body(dest_hbm, out_hbm, dest_v, slot_v):
        wid = lax.axis_index("s") * SC_CORES + lax.axis_index("c")

        @pl.when(wid == 0)
        def _():
            pltpu.sync_copy(dest_hbm, dest_v)
            lane = lax.iota(jnp.int32, lanes)

            def init(i, _):
                start = pl.multiple_of(i * lanes, lanes)
                slot_v[pl.ds(start, lanes)] = (lane + start) & (N_TOK - 1)
                return 0

            def place(i, _):
                start = pl.multiple_of(i * lanes, lanes)
                slots = dest_v[pl.ds(start, lanes)]
                plsc.store_scatter(slot_v, [slots], (lane + start) >> 1)
                return 0

            lax.fori_loop(0, N_SLOTS // lanes, init, 0)
            lax.fori_loop(0, n_assign // lanes, place, 0)
            pltpu.sync_copy(slot_v, out_hbm)

    return pl.kernel(
        body,
        out_type=jax.ShapeDtypeStruct((N_SLOTS,), jnp.int32),
        mesh=mesh,
        scratch_types=[
            pltpu.VMEM((n_assign,), jnp.int32),
            pltpu.VMEM((N_SLOTS,), jnp.int32),
        ],
        compiler_params=pltpu.CompilerParams(needs_layout_passes=False),
        name="sc_slot_tokens",
    )(dest_flat)


def _mix_kernel(x_ref, wts_ref, mod_ref, fg_ref, y0_ref, y1_ref, o_ref):
    out = _expert_mix(x_ref[...], wts_ref[...], mod_ref[0][5:6], y0_ref[...], y1_ref[...])
    ms = jnp.mean(out * out, axis=-1, keepdims=True)
    o_ref[...] = out * lax.rsqrt(ms + RMS_EPS) * fg_ref[...]


def _mix(x1, wts, mod, y2, fg):
    nt = N_TOK // COMB_TM
    tiles_per_batch = SEQ // COMB_TM
    return pl.pallas_call(
        _mix_kernel,
        grid=(nt,),
        in_specs=[
            pl.BlockSpec((COMB_TM, D_MODEL), lambda i: (i, 0)),
            pl.BlockSpec((COMB_TM, TOP_K), lambda i: (i, 0)),
            pl.BlockSpec((1, 6, D_MODEL), lambda i: (i // tiles_per_batch, 0, 0)),
            pl.BlockSpec((1, D_MODEL), lambda i: (0, 0)),
            pl.BlockSpec((COMB_TM, ROW_WORDS), lambda i: (i, 0)),
            pl.BlockSpec((COMB_TM, ROW_WORDS), lambda i: (nt + i, 0)),
        ],
        out_specs=pl.BlockSpec((COMB_TM, D_MODEL), lambda i: (i, 0)),
        out_shape=jax.ShapeDtypeStruct((N_TOK, D_MODEL), F32),
        compiler_params=_cparams(("arbitrary",)),
        name="mix",
    )(x1, wts, mod, fg, y2, y2)


def _routing_tables(idr, cnt):
    counts = cnt[0, EXPERT_LANE0:EXPERT_LANE0 + N_EXPERTS].astype(jnp.int32)
    padded = (counts + MOE_BLK - 1) // MOE_BLK * MOE_BLK
    pad_ends = jnp.cumsum(padded)
    pad_starts = pad_ends - padded
    experts = jnp.arange(N_EXPERTS, dtype=jnp.int32)
    is_expert = idr[:, 0:2, None] == experts
    dest = jnp.sum(jnp.where(is_expert, pad_starts, 0), axis=-1) + idr[:, 2:4]
    block_start = jnp.arange(N_BLOCKS, dtype=jnp.int32) * MOE_BLK
    bexp = jnp.minimum(jnp.sum(block_start[:, None] >= pad_ends[None, :], axis=1),
                       N_EXPERTS - 1).astype(jnp.int32)
    owns = padded > 0
    pos_of_expert = jnp.cumsum(owns.astype(jnp.int32)) - 1
    active = jnp.sum(jnp.where(owns[None, :] & (pos_of_expert[None, :] == experts[:, None]),
                               experts[None, :], 0), axis=1).astype(jnp.int32)
    block_pos = jnp.sum(jnp.where(bexp[:, None] == experts[None, :], pos_of_expert[None, :], 0),
                        axis=1).astype(jnp.int32)
    meta = jnp.stack([pad_ends[-1] // MOE_BLK, jnp.sum(owns)]).astype(jnp.int32)
    return dest.astype(jnp.int32), block_pos, active, meta


def _router_weights(w_group, b_group, w_expert, b_expert):
    rw = jnp.zeros((D_MODEL, LANES), F32)
    rw = rw.at[:, :N_GROUPS].set(w_group)
    rw = rw.at[:, EXPERT_LANE0:EXPERT_LANE0 + N_EXPERTS].set(w_expert)
    rb = jnp.zeros((1, LANES), F32)
    rb = rb.at[0, :N_GROUPS].set(b_group)
    rb = rb.at[0, EXPERT_LANE0:EXPERT_LANE0 + N_EXPERTS].set(b_expert.reshape(-1))
    return rw.astype(BF16), rb


def _scaled_qkv_weight(w_in, head_dim):
    scale = jnp.concatenate([jnp.full((D_MODEL,), LOG2_E / math.sqrt(head_dim), F32),
                             jnp.ones((2 * D_MODEL,), F32)])
    return (w_in * scale).astype(BF16)


def kernel(x, c, norm1_g, norm2_g, ada_w, ada_b, diff_w_in, diff_w_out, diff_lambda_q1, diff_lambda_k1, diff_lambda_q2, diff_lambda_k2, diff_subln_g, sb_w_in, sb_w_out, router_group_w, router_group_b, router_expert_w, router_expert_b, expert_w_gate, expert_w_up, expert_w_down, final_norm_g):
    xf = x.reshape(N_TOK, D_MODEL)
    mod_all = _ada(c, ada_w, ada_b)
    slopes = jnp.exp2(-8.0 * jnp.arange(1, DIFF_HEADS + 1, dtype=F32) / DIFF_HEADS)

    pending = None
    for i in range(DEPTH):
        mod = mod_all[i].reshape(BATCH, 6, D_MODEL)
        j = i // 2
        is_diff = i % 2 == 0
        w_in = (_scaled_qkv_weight(diff_w_in[j], DIFF_HEAD_DIM) if is_diff
                else _scaled_qkv_weight(sb_w_in[j], SB_HEAD_DIM))
        g1 = norm1_g[i].reshape(1, D_MODEL)
        if pending is None:
            qkv = _qkv(xf, mod, g1, w_in)
        else:
            xf, qkv = _mix_qkv(*pending, mod, g1, w_in)
        if is_diff:
            lambda_init = 0.8 - 0.6 * math.exp(-0.3 * i)
            lam = (jnp.exp(jnp.sum(diff_lambda_q1[j] * diff_lambda_k1[j]))
                   - jnp.exp(jnp.sum(diff_lambda_q2[j] * diff_lambda_k2[j]))
                   + lambda_init).reshape(1, 1)
            o = _diff_attn(qkv, slopes, lam, diff_subln_g[j].reshape(1, LANES), lambda_init)
            w_out = diff_w_out[j]
        else:
            o = _sb_attn(qkv)
            w_out = sb_w_out[j]
        rw, rb = _router_weights(router_group_w[i], router_group_b[i],
                                 router_expert_w[i], router_expert_b[i])
        x1, h2, idr, wts, cnt = _out_router(o, xf, mod, norm2_g[i].reshape(1, D_MODEL),
                                            w_out.astype(BF16), rw, rb)
        dest, block_pos, active, meta = _routing_tables(idr, cnt)
        xs = _sc_gather_rows(h2, _sc_slot_tokens(dest.reshape(-1)))
        ys = _experts(i, block_pos, active, meta, xs,
                      expert_w_gate, expert_w_up, expert_w_down)
        y2 = _sc_gather_rows(ys, dest.T.reshape(-1))
        pending = (x1, wts, mod, y2)
    out = _mix(*pending, final_norm_g.reshape(1, D_MODEL))
    return out.reshape(BATCH, SEQ, D_MODEL)
```

```python
import functools
import math

import jax
import jax.numpy as jnp
from jax import lax
from jax.experimental import pallas as pl
from jax.experimental.pallas import tpu as pltpu
from jax.experimental.pallas import tpu_sc as plsc

D_MODEL = 1024
BATCH = 8
SEQ = 2048
DEPTH = 2
N_TOK = BATCH * SEQ

CHUNK = 64
DIFF_HEADS = 8
DIFF_HEAD_DIM = D_MODEL // (2 * DIFF_HEADS)
SB_HEADS = 16
SB_HEAD_DIM = D_MODEL // SB_HEADS
N_GROUPS = 4
EXPERTS_PER_GROUP = 8
N_EXPERTS = N_GROUPS * EXPERTS_PER_GROUP
TOP_K = 2
EXPERT_HIDDEN = D_MODEL // 2
RMS_EPS = 1e-6
SUBLN_EPS = 1e-5

LANES = 128
EXPERT_LANE0 = 32
EXP2_UNDERFLOW = -150.0
LOG2_E = math.log2(math.e)

ADA_TN = 1536
QKV_TM = 512
DIFF_T = 512
DIFF_GROUP = 4
SB_T = 256
SB_PAIRS = 4
OUT_TM = 512
MOE_BLK = 512
N_SLOTS = N_TOK * TOP_K + N_EXPERTS * MOE_BLK
N_BLOCKS = N_SLOTS // MOE_BLK
COMB_TM = 512
ROW_WORDS = D_MODEL // 2

SC_CORES = 2
SC_SUBCORES = 16
SC_LANES = 16
SC_CHUNK = 64

VMEM_LIMIT = 56 * 1024 * 1024

F32 = jnp.float32
BF16 = jnp.bfloat16


def _cparams(sem):
    return pltpu.CompilerParams(dimension_semantics=sem, vmem_limit_bytes=VMEM_LIMIT)


def _ada_kernel(c_ref, w_ref, b_ref, o_ref):
    c = c_ref[...]
    cond = c * jax.nn.sigmoid(c)
    o_ref[0] = jnp.dot(cond.astype(BF16), w_ref[0].astype(BF16),
                       preferred_element_type=F32) + b_ref[0]


def _ada(c, ada_w, ada_b):
    six_d = ada_w.shape[-1]
    return pl.pallas_call(
        _ada_kernel,
        grid=(DEPTH, six_d // ADA_TN),
        in_specs=[
            pl.BlockSpec((BATCH, D_MODEL), lambda l, n: (0, 0)),
            pl.BlockSpec((1, D_MODEL, ADA_TN), lambda l, n: (l, 0, n)),
            pl.BlockSpec((1, 1, ADA_TN), lambda l, n: (l, 0, n)),
        ],
        out_specs=pl.BlockSpec((1, BATCH, ADA_TN), lambda l, n: (l, 0, n)),
        out_shape=jax.ShapeDtypeStruct((DEPTH, BATCH, six_d), F32),
        compiler_params=_cparams(("arbitrary", "arbitrary")),
        name="ada",
    )(c, ada_w, ada_b.reshape(DEPTH, 1, six_d))


def _modulated_norm(x, g, scale, shift):
    ms = jnp.mean(x * x, axis=-1, keepdims=True)
    return x * lax.rsqrt(ms + RMS_EPS) * (g * (1.0 + scale)) + shift


def _qkv_kernel(x_ref, mod_ref, g_ref, w_ref, o_ref):
    m = mod_ref[0]
    h = _modulated_norm(x_ref[...], g_ref[...], m[1:2], m[0:1]).astype(BF16)
    for n in range(3):
        cols = slice(n * D_MODEL, (n + 1) * D_MODEL)
        o_ref[:, cols] = jnp.dot(h, w_ref[:, cols], preferred_element_type=F32).astype(BF16)


def _qkv(x, mod, g, w):
    tiles_per_batch = SEQ // QKV_TM
    return pl.pallas_call(
        _qkv_kernel,
        grid=(N_TOK // QKV_TM,),
        in_specs=[
            pl.BlockSpec((QKV_TM, D_MODEL), lambda i: (i, 0)),
            pl.BlockSpec((1, 6, D_MODEL), lambda i: (i // tiles_per_batch, 0, 0)),
            pl.BlockSpec((1, D_MODEL), lambda i: (0, 0)),
            pl.BlockSpec((D_MODEL, 3 * D_MODEL), lambda i: (0, 0)),
        ],
        out_specs=pl.BlockSpec((QKV_TM, 3 * D_MODEL), lambda i: (i, 0)),
        out_shape=jax.ShapeDtypeStruct((N_TOK, 3 * D_MODEL), BF16),
        compiler_params=_cparams(("arbitrary",)),
        name="qkv",
    )(x, mod, g, w)


def _expert_mix(x, wts, gate, y0_words, y1_words):
    y0 = _unpack_bf16_pairs(lax.bitcast_convert_type(y0_words, jnp.uint32))
    y1 = _unpack_bf16_pairs(lax.bitcast_convert_type(y1_words, jnp.uint32))
    return x + gate * (wts[:, 0:1] * y0 + wts[:, 1:2] * y1)


def _mix_qkv_kernel(x1_ref, wts_ref, pmod_ref, y0_ref, y1_ref, mod_ref, g_ref, w_ref,
                    x_ref, o_ref):
    x = _expert_mix(x1_ref[...], wts_ref[...], pmod_ref[0][5:6], y0_ref[...], y1_ref[...])
    x_ref[...] = x
    m = mod_ref[0]
    h = _modulated_norm(x, g_ref[...], m[1:2], m[0:1]).astype(BF16)
    for n in range(3):
        cols = slice(n * D_MODEL, (n + 1) * D_MODEL)
        o_ref[:, cols] = jnp.dot(h, w_ref[:, cols], preferred_element_type=F32).astype(BF16)


def _mix_qkv(x1, wts, prev_mod, y2, mod, g, w):
    nt = N_TOK // QKV_TM
    tiles_per_batch = SEQ // QKV_TM
    return pl.pallas_call(
        _mix_qkv_kernel,
        grid=(nt,),
        in_specs=[
            pl.BlockSpec((QKV_TM, D_MODEL), lambda i: (i, 0)),
            pl.BlockSpec((QKV_TM, TOP_K), lambda i: (i, 0)),
            pl.BlockSpec((1, 6, D_MODEL), lambda i: (i // tiles_per_batch, 0, 0)),
            pl.BlockSpec((QKV_TM, ROW_WORDS), lambda i: (i, 0)),
            pl.BlockSpec((QKV_TM, ROW_WORDS), lambda i: (nt + i, 0)),
            pl.BlockSpec((1, 6, D_MODEL), lambda i: (i // tiles_per_batch, 0, 0)),
            pl.BlockSpec((1, D_MODEL), lambda i: (0, 0)),
            pl.BlockSpec((D_MODEL, 3 * D_MODEL), lambda i: (0, 0)),
        ],
        out_specs=[
            pl.BlockSpec((QKV_TM, D_MODEL), lambda i: (i, 0)),
            pl.BlockSpec((QKV_TM, 3 * D_MODEL), lambda i: (i, 0)),
        ],
        out_shape=[
            jax.ShapeDtypeStruct((N_TOK, D_MODEL), F32),
            jax.ShapeDtypeStruct((N_TOK, 3 * D_MODEL), BF16),
        ],
        compiler_params=_cparams(("arbitrary",)),
        name="mix_qkv",
    )(x1, wts, prev_mod, y2, y2, mod, g, w)


def _half_masked(q, upper):
    lane = lax.broadcasted_iota(jnp.int32, q.shape, 1)
    keep = (lane >= LANES // 2) if upper else (lane < LANES // 2)
    return jnp.where(keep, q, jnp.zeros_like(q))


def _pack_bf16_pairs(x):
    half = x.shape[1] // 2
    hi = lax.bitcast_convert_type(x[:, :half].astype(BF16).astype(F32), jnp.uint32)
    lo = lax.bitcast_convert_type(x[:, half:].astype(BF16).astype(F32), jnp.uint32)
    return hi | (lo >> 16)


def _unpack_bf16_pairs(u):
    hi = lax.bitcast_convert_type(u & jnp.uint32(0xFFFF0000), F32)
    lo = lax.bitcast_convert_type(u << 16, F32)
    return jnp.concatenate([hi, lo], axis=1)


def _lane_tile(x, n):
    return jnp.concatenate([x] * n, axis=1)


def _qk(q, k):
    return lax.dot_general(q, k, (((1,), (1,)), ((), ())), preferred_element_type=F32)


def _diff_attn_kernel(slopes_ref, lam_ref, q_ref, k_ref, v_ref, g_ref, o_ref, *scratch,
                      lambda_init):
    n_chains = 2 * DIFF_GROUP
    m_refs = scratch[:n_chains]
    acc_refs = scratch[n_chains:]
    ones = jnp.ones((DIFF_T, LANES), BF16)

    def head_lanes(c):
        return slice((c // 2) * LANES, (c // 2 + 1) * LANES)

    row = lax.broadcasted_iota(jnp.int32, (DIFF_T, DIFF_T), 0)
    col = lax.broadcasted_iota(jnp.int32, (DIFF_T, DIFF_T), 1)
    allowed = (col // CHUNK) <= (row // CHUNK)
    slopes = [slopes_ref[pl.program_id(1) * DIFF_GROUP + g] * LOG2_E for g in range(DIFF_GROUP)]
    rels = [(row - col).astype(F32) * slope for slope in slopes]
    diag_biases = [jnp.where(allowed, jnp.abs(rel), jnp.inf) for rel in rels]
    out_gain = g_ref[...] * (1.0 - lambda_init)

    def key_block(j, c):
        k = k_ref[pl.ds(j * DIFF_T, DIFF_T), head_lanes(c)]
        v1 = jnp.concatenate([v_ref[pl.ds(j * DIFF_T, DIFF_T), head_lanes(c)], ones], axis=1)
        return k, v1

    def query_tile(qi, _):
        q = q_ref[pl.ds(qi * DIFF_T, DIFF_T), :]
        qm = [_half_masked(q[:, head_lanes(c)], c % 2 == 1) for c in range(n_chains)]

        kv = [key_block(qi, c) for c in range(n_chains)]
        s_maps = [_qk(qm[c], kv[c][0]) - diag_biases[c // 2] for c in range(n_chains)]
        for c in range(n_chains):
            m = jnp.max(s_maps[c], axis=-1, keepdims=True)
            p = jnp.exp2(s_maps[c] - m)
            m_refs[c][...] = jnp.broadcast_to(m, (DIFF_T, LANES))
            acc_refs[c][...] = jnp.dot(p.astype(BF16), kv[c][1], preferred_element_type=F32)

        def body(j, _):
            kv = [key_block(j, c) for c in range(n_chains)]
            ahead = jnp.asarray((qi - j) * DIFF_T).astype(F32)
            s_maps = [_qk(qm[c], kv[c][0]) - rels[c // 2] for c in range(n_chains)]
            for c in range(n_chains):
                shift = slopes[c // 2] * ahead
                m = m_refs[c][...]
                m_new = jnp.maximum(m, jnp.max(s_maps[c], axis=-1, keepdims=True) - shift)
                alpha = jnp.exp2(m - m_new)
                p = jnp.exp2(s_maps[c] - _lane_tile(m_new + shift, DIFF_T // LANES))
                m_refs[c][...] = m_new
                acc_refs[c][...] = _lane_tile(alpha, 2) * acc_refs[c][...] + jnp.dot(
                    p.astype(BF16), kv[c][1], preferred_element_type=F32)
            return 0

        lax.fori_loop(0, qi, body, 0)
        for g in range(DIFF_GROUP):
            a0 = acc_refs[2 * g][...]
            a1 = acc_refs[2 * g + 1][...]
            o = a0[:, :LANES] / a0[:, LANES:] - lam_ref[...] * (a1[:, :LANES] / a1[:, LANES:])
            ms = jnp.mean(o * o, axis=-1, keepdims=True)
            o = o * lax.rsqrt(ms + SUBLN_EPS) * out_gain
            o_ref[pl.ds(qi * DIFF_T, DIFF_T), head_lanes(2 * g)] = o.astype(BF16)
        return 0

    lax.fori_loop(0, SEQ // DIFF_T, query_tile, 0)


def _diff_attn(qkv, slopes, lam, subln_g, lambda_init):
    groups = DIFF_HEADS // DIFF_GROUP
    width = DIFF_GROUP * LANES
    return pl.pallas_call(
        functools.partial(_diff_attn_kernel, lambda_init=lambda_init),
        grid=(BATCH, groups),
        in_specs=[
            pl.BlockSpec(memory_space=pltpu.SMEM),
            pl.BlockSpec((1, 1), lambda b, h: (0, 0)),
            pl.BlockSpec((SEQ, width), lambda b, h: (b, h)),
            pl.BlockSpec((SEQ, width), lambda b, h: (b, groups + h)),
            pl.BlockSpec((SEQ, width), lambda b, h: (b, 2 * groups + h)),
            pl.BlockSpec((1, LANES), lambda b, h: (0, 0)),
        ],
        out_specs=pl.BlockSpec((SEQ, width), lambda b, h: (b, h)),
        out_shape=jax.ShapeDtypeStruct((N_TOK, D_MODEL), BF16),
        scratch_shapes=([pltpu.VMEM((DIFF_T, LANES), F32)] * (2 * DIFF_GROUP)
                        + [pltpu.VMEM((DIFF_T, 2 * LANES), F32)] * (2 * DIFF_GROUP)),
        compiler_params=_cparams(("arbitrary", "arbitrary")),
        name="diff_attn",
    )(slopes, lam, qkv, qkv, qkv, subln_g)


def _sb_attn_kernel(q_ref, k_ref, v_ref, o_ref, *scratch):
    n_heads = 2 * SB_PAIRS
    tail_refs = scratch[:n_heads]
    acc_refs = scratch[n_heads:]

    def pair_lanes(h):
        return slice((h // 2) * LANES, (h // 2 + 1) * LANES)

    row = lax.broadcasted_iota(jnp.int32, (SB_T, SB_T), 0)
    col = lax.broadcasted_iota(jnp.int32, (SB_T, SB_T), 1)
    strict = col < row
    neg_from = jnp.where(row >= col, -1.0, 0.0).astype(BF16)

    def scores(j, h, qhead, mask):
        z = _qk(qhead, k_ref[pl.ds(j * SB_T, SB_T), pair_lanes(h)])
        sp = jnp.maximum(z, 0.0) + jnp.log2(1.0 + jnp.exp2(-jnp.abs(z)))
        if mask is not None:
            sp = jnp.where(mask, sp, 0.0)
        return z, sp

    def weighted(j, h, z, sp, tail, mask):
        log_a = (z + jnp.dot(sp.astype(BF16), neg_from, preferred_element_type=F32)
                 + _lane_tile(tail, SB_T // LANES))
        a = jnp.exp2(log_a)
        if mask is not None:
            a = jnp.where(mask, a, 0.0)
        return jnp.dot(a.astype(BF16), v_ref[pl.ds(j * SB_T, SB_T), pair_lanes(h)],
                       preferred_element_type=F32)

    def row_sum(sp):
        return jnp.broadcast_to(jnp.sum(sp, axis=-1, keepdims=True), (SB_T, LANES))

    zero = jnp.zeros((SB_T, LANES), F32)
    lane = lax.broadcasted_iota(jnp.int32, (SB_T, LANES), 1)

    def query_heads(qi):
        q = q_ref[pl.ds(qi * SB_T, SB_T), :]
        return [_half_masked(q[:, pair_lanes(h)], h % 2 == 1) for h in range(n_heads)]

    def finish(qi):
        for h in range(0, n_heads, 2):
            o_ref[pl.ds(qi * SB_T, SB_T), pair_lanes(h)] = jnp.where(
                lane < LANES // 2, acc_refs[h][...], acc_refs[h + 1][...]).astype(BF16)

    qh = query_heads(0)
    for h in range(n_heads):
        z, sp = scores(0, h, qh[h], strict)
        acc_refs[h][...] = weighted(0, h, z, sp, zero, strict)
    finish(0)

    def query_tile(qi, _):
        qh = query_heads(qi)
        zs = [(scores(qi, h, qh[h], strict), scores(qi - 1, h, qh[h], None))
              for h in range(n_heads)]
        for h in range(n_heads):
            (z0, sp0), (z1, sp1) = zs[h]
            tail0 = -row_sum(sp0)
            acc_refs[h][...] = (weighted(qi, h, z0, sp0, zero, strict)
                                + weighted(qi - 1, h, z1, sp1, tail0, None))
            tail_refs[h][...] = tail0 - row_sum(sp1)

        def cond(state):
            j, live = state
            return (j >= 0) & (live > EXP2_UNDERFLOW)

        def body(state):
            j, _ = state
            live = jnp.float32(-jnp.inf)
            for h in range(n_heads):
                z, sp = scores(j, h, qh[h], None)
                tail = tail_refs[h][...]
                acc_refs[h][...] += weighted(j, h, z, sp, tail, None)
                tail = tail - row_sum(sp)
                tail_refs[h][...] = tail
                live = jnp.maximum(live, jnp.max(tail))
            return j - 1, live

        live = functools.reduce(jnp.maximum, [jnp.max(t[...]) for t in tail_refs])
        lax.while_loop(cond, body, (qi - 2, live))
        finish(qi)
        return 0

    lax.fori_loop(1, SEQ // SB_T, query_tile, 0)


def _sb_attn(qkv):
    groups = SB_HEADS // (2 * SB_PAIRS)
    width = SB_PAIRS * LANES
    return pl.pallas_call(
        _sb_attn_kernel,
        grid=(BATCH, groups),
        in_specs=[
            pl.BlockSpec((SEQ, width), lambda b, h: (b, h)),
            pl.BlockSpec((SEQ, width), lambda b, h: (b, groups + h)),
            pl.BlockSpec((SEQ, width), lambda b, h: (b, 2 * groups + h)),
        ],
        out_specs=pl.BlockSpec((SEQ, width), lambda b, h: (b, h)),
        out_shape=jax.ShapeDtypeStruct((N_TOK, D_MODEL), BF16),
        scratch_shapes=[pltpu.VMEM((SB_T, LANES), F32)] * (4 * SB_PAIRS),
        compiler_params=_cparams(("arbitrary", "arbitrary")),
        name="sb_attn",
    )(qkv, qkv, qkv)


def _out_router_kernel(o_ref, x_ref, mod_ref, g_ref, w_ref, rw_ref, rb_ref,
                       x1_ref, h2_ref, idr_ref, wts_ref, cnt_ref, tri_ref, base_ref):
    i = pl.program_id(0)

    @pl.when(i == 0)
    def _():
        r = lax.broadcasted_iota(jnp.int32, (OUT_TM, OUT_TM), 0)
        c = lax.broadcasted_iota(jnp.int32, (OUT_TM, OUT_TM), 1)
        tri_ref[...] = jnp.where(c < r, 1.0, 0.0).astype(BF16)
        base_ref[...] = jnp.zeros_like(base_ref)

    m = mod_ref[0]
    y = jnp.dot(o_ref[...], w_ref[...], preferred_element_type=F32)
    x1 = x_ref[...] + m[2:3] * y
    x1_ref[...] = x1
    h2 = _modulated_norm(x1, g_ref[...], m[4:5], m[3:4])
    h2_ref[...] = lax.bitcast_convert_type(_pack_bf16_pairs(h2), jnp.int32)

    logits = jnp.dot(h2.astype(BF16), rw_ref[...], preferred_element_type=F32) + rb_ref[...]
    lane = lax.broadcasted_iota(jnp.int32, logits.shape, 1).astype(F32)
    neg_inf = jnp.float32(-jnp.inf)
    big = jnp.float32(1e9)

    is_group = lane < N_GROUPS
    gl = jnp.where(is_group, logits, neg_inf)
    gmax = jnp.max(gl, axis=-1, keepdims=True)
    gidx = jnp.min(jnp.where(gl == gmax, lane, big), axis=-1, keepdims=True)
    gsum = jnp.sum(jnp.where(is_group, jnp.exp(logits - gmax), 0.0), axis=-1, keepdims=True)
    g_w = 1.0 / gsum

    lo = EXPERT_LANE0 + EXPERTS_PER_GROUP * gidx
    in_group = (lane >= lo) & (lane < lo + EXPERTS_PER_GROUP)
    el = jnp.where(in_group, logits, neg_inf)
    v0 = jnp.max(el, axis=-1, keepdims=True)
    i0 = jnp.min(jnp.where(el == v0, lane, big), axis=-1, keepdims=True)
    el = jnp.where(lane == i0, neg_inf, el)
    v1 = jnp.max(el, axis=-1, keepdims=True)
    i1 = jnp.min(jnp.where(el == v1, lane, big), axis=-1, keepdims=True)
    t = jnp.exp(v1 - v0)
    w0 = g_w / (1.0 + t)
    w1 = g_w * t / (1.0 + t)

    oh0 = jnp.where(lane == i0, 1.0, 0.0)
    oh1 = jnp.where(lane == i1, 1.0, 0.0)
    both = oh0 + oh1
    before = jnp.dot(tri_ref[...], both.astype(BF16), preferred_element_type=F32) + base_ref[...]
    r0 = jnp.sum(before * oh0, axis=-1, keepdims=True)
    r1 = jnp.sum(before * oh1, axis=-1, keepdims=True)
    base_ref[...] = base_ref[...] + jnp.sum(both, axis=0, keepdims=True)
    cnt_ref[...] = base_ref[...]

    e0 = i0 - EXPERT_LANE0
    e1 = i1 - EXPERT_LANE0
    idr = jnp.where(lane == 0, e0, jnp.where(lane == 1, e1, jnp.where(lane == 2, r0, r1)))
    idr_ref[...] = idr[:, :4].astype(jnp.int32)
    wts_ref[...] = jnp.where(lane == 0, w0, w1)[:, :2]


def _out_router(o, x, mod, g, w, rw, rb):
    tiles_per_batch = SEQ // OUT_TM
    return pl.pallas_call(
        _out_router_kernel,
        grid=(N_TOK // OUT_TM,),
        in_specs=[
            pl.BlockSpec((OUT_TM, D_MODEL), lambda i: (i, 0)),
            pl.BlockSpec((OUT_TM, D_MODEL), lambda i: (i, 0)),
            pl.BlockSpec((1, 6, D_MODEL), lambda i: (i // tiles_per_batch, 0, 0)),
            pl.BlockSpec((1, D_MODEL), lambda i: (0, 0)),
            pl.BlockSpec((D_MODEL, D_MODEL), lambda i: (0, 0)),
            pl.BlockSpec((D_MODEL, LANES), lambda i: (0, 0)),
            pl.BlockSpec((1, LANES), lambda i: (0, 0)),
        ],
        out_specs=[
            pl.BlockSpec((OUT_TM, D_MODEL), lambda i: (i, 0)),
            pl.BlockSpec((OUT_TM, ROW_WORDS), lambda i: (i, 0)),
            pl.BlockSpec((OUT_TM, 4), lambda i: (i, 0)),
            pl.BlockSpec((OUT_TM, 2), lambda i: (i, 0)),
            pl.BlockSpec((1, LANES), lambda i: (0, 0)),
        ],
        out_shape=[
            jax.ShapeDtypeStruct((N_TOK, D_MODEL), F32),
            jax.ShapeDtypeStruct((N_TOK, ROW_WORDS), jnp.int32),
            jax.ShapeDtypeStruct((N_TOK, 4), jnp.int32),
            jax.ShapeDtypeStruct((N_TOK, 2), F32),
            jax.ShapeDtypeStruct((1, LANES), F32),
        ],
        scratch_shapes=[
            pltpu.VMEM((OUT_TM, OUT_TM), BF16),
            pltpu.VMEM((1, LANES), F32),
        ],
        compiler_params=_cparams(("arbitrary",)),
        name="out_router",
    )(o, x, mod, g, w, rw, rb)


def _expert_kernel(ord_ref, active_ref, meta_ref, xs_ref, wg_hbm, wu_hbm, wd_hbm, ys_ref,
                   wg_f32, wu_f32, wd_f32, sems, wg_bf, wu_bf, wd_bf, *, layer):
    i = pl.program_id(0)
    n_used = meta_ref[0]
    n_active = meta_ref[1]

    def fetch(pos, slot):
        e = active_ref[pos]
        return (
            pltpu.make_async_copy(wg_hbm.at[layer, e], wg_f32.at[slot], sems.at[slot, 0]),
            pltpu.make_async_copy(wu_hbm.at[layer, e], wu_f32.at[slot], sems.at[slot, 1]),
            pltpu.make_async_copy(wd_hbm.at[layer, e], wd_f32.at[slot], sems.at[slot, 2]),
        )

    @pl.when(i == 0)
    def _():
        for copy in fetch(0, 0):
            copy.start()

        @pl.when(n_active > 1)
        def _():
            for copy in fetch(1, 1):
                copy.start()

    @pl.when(i < n_used)
    def _():
        pos = ord_ref[i]
        changed = (i == 0) | (pos != ord_ref[jnp.maximum(i - 1, 0)])
        for slot in range(2):
            @pl.when(changed & (pos % 2 == slot))
            def _():
                for copy in fetch(pos, slot):
                    copy.wait()
                wg_bf[...] = wg_f32[slot].astype(BF16)
                wu_bf[...] = wu_f32[slot].astype(BF16)
                wd_bf[...] = wd_f32[slot].astype(BF16)

                @pl.when(pos + 2 < n_active)
                def _():
                    for copy in fetch(pos + 2, slot):
                        copy.start()

        x = _unpack_bf16_pairs(lax.bitcast_convert_type(xs_ref[...], jnp.uint32)).astype(BF16)
        g = jnp.dot(x, wg_bf[...], preferred_element_type=F32)
        u = jnp.dot(x, wu_bf[...], preferred_element_type=F32)
        hid = (g * jax.nn.sigmoid(g)) * u
        y = jnp.dot(hid.astype(BF16), wd_bf[...], preferred_element_type=F32)
        ys_ref[...] = lax.bitcast_convert_type(_pack_bf16_pairs(y), jnp.int32)

    @pl.when(i >= n_used)
    def _():
        ys_ref[...] = jnp.zeros_like(ys_ref)


def _experts(layer, block_pos, active, meta, xs, wg, wu, wd):
    def in_row_map(i, block_pos, active, meta):
        return (jnp.minimum(i, meta[0] - 1), 0)

    def row_map(i, block_pos, active, meta):
        return (i, 0)

    return pl.pallas_call(
        functools.partial(_expert_kernel, layer=layer),
        grid_spec=pltpu.PrefetchScalarGridSpec(
            num_scalar_prefetch=3,
            grid=(N_BLOCKS,),
            in_specs=[
                pl.BlockSpec((MOE_BLK, ROW_WORDS), in_row_map),
                pl.BlockSpec(memory_space=pl.ANY),
                pl.BlockSpec(memory_space=pl.ANY),
                pl.BlockSpec(memory_space=pl.ANY),
            ],
            out_specs=pl.BlockSpec((MOE_BLK, ROW_WORDS), row_map),
            scratch_shapes=[
                pltpu.VMEM((2, D_MODEL, EXPERT_HIDDEN), F32),
                pltpu.VMEM((2, D_MODEL, EXPERT_HIDDEN), F32),
                pltpu.VMEM((2, EXPERT_HIDDEN, D_MODEL), F32),
                pltpu.SemaphoreType.DMA((2, 3)),
                pltpu.VMEM((D_MODEL, EXPERT_HIDDEN), BF16),
                pltpu.VMEM((D_MODEL, EXPERT_HIDDEN), BF16),
                pltpu.VMEM((EXPERT_HIDDEN, D_MODEL), BF16),
            ],
        ),
        out_shape=jax.ShapeDtypeStruct((N_SLOTS, ROW_WORDS), jnp.int32),
        compiler_params=_cparams(("arbitrary",)),
        name="experts",
    )(block_pos, active, meta, xs, wg, wu, wd)


def _sc_gather_rows(table, idx):
    n_rows = idx.shape[0]
    width = table.shape[1]
    workers = SC_CORES * SC_SUBCORES
    per_worker = n_rows // workers
    assert per_worker * workers == n_rows and per_worker % SC_CHUNK == 0
    mesh = plsc.VectorSubcoreMesh(core_axis_name="c", subcore_axis_name="s")

    n_chunks = per_worker // SC_CHUNK

    def body(table_hbm, idx_hbm, out_hbm, idx_v, buf0, buf1, gsem0, gsem1, wsem0, wsem1):
        wid = lax.axis_index("s") * SC_CORES + lax.axis_index("c")
        base = pl.multiple_of(wid * per_worker, SC_CHUNK)
        bufs, gsems, wsems = (buf0, buf1), (gsem0, gsem1), (wsem0, wsem1)
        pltpu.sync_copy(idx_hbm.at[pl.ds(base, per_worker)], idx_v)

        def gather(c):
            rows = idx_v.at[pl.ds(c * SC_CHUNK, SC_CHUNK)]
            return pltpu.async_copy(table_hbm.at[rows], bufs[c % 2], gsems[c % 2])

        def write(c):
            dst = out_hbm.at[pl.ds(base + c * SC_CHUNK, SC_CHUNK)]
            return pltpu.async_copy(bufs[c % 2], dst, wsems[c % 2])

        gathers = {0: gather(0)}
        writes = {}
        for c in range(n_chunks):
            if c + 1 < n_chunks:
                if c >= 1:
                    writes.pop(c - 1).wait()
                gathers[c + 1] = gather(c + 1)
            gathers.pop(c).wait()
            writes[c] = write(c)
        for c in sorted(writes):
            writes[c].wait()

    return pl.kernel(
        body,
        out_type=jax.ShapeDtypeStruct((n_rows, width), table.dtype),
        mesh=mesh,
        scratch_types=[
            pltpu.VMEM((per_worker,), jnp.int32),
            pltpu.VMEM((SC_CHUNK, width), table.dtype),
            pltpu.VMEM((SC_CHUNK, width), table.dtype),
            pltpu.SemaphoreType.DMA,
            pltpu.SemaphoreType.DMA,
            pltpu.SemaphoreType.DMA,
            pltpu.SemaphoreType.DMA,
        ],
        name="sc_gather_rows",
    )(table, idx)


def _sc_slot_tokens(dest_flat):
    n_assign = N_TOK * TOP_K
    lanes = SC_LANES
    mesh = plsc.VectorSubcoreMesh(core_axis_name="c", subcore_axis_name="s")

    def body(dest_hbm, out_hbm, dest_v, slot_v):
        wid = lax.axis_index("s") * SC_CORES + lax.axis_index("c")

        @pl.when(wid == 0)
        def _():
            pltpu.sync_copy(dest_hbm, dest_v)
            lane = lax.iota(jnp.int32, lanes)

            def init(i, _):
                start = pl.multiple_of(i * lanes, lanes)
                slot_v[pl.ds(start, lanes)] = (lane + start) & (N_TOK - 1)
                return 0

            def place(i, _):
                start = pl.multiple_of(i * lanes, lanes)
                slots = dest_v[pl.ds(start, lanes)]
                plsc.store_scatter(slot_v, [slots], (lane + start) >> 1)
                return 0

            lax.fori_loop(0, N_SLOTS // lanes, init, 0)
            lax.fori_loop(0, n_assign // lanes, place, 0)
            pltpu.sync_copy(slot_v, out_hbm)

    return pl.kernel(
        body,
        out_type=jax.ShapeDtypeStruct((N_SLOTS,), jnp.int32),
        mesh=mesh,
        scratch_types=[
            pltpu.VMEM((n_assign,), jnp.int32),
            pltpu.VMEM((N_SLOTS,), jnp.int32),
        ],
        compiler_params=pltpu.CompilerParams(needs_layout_passes=False),
        name="sc_slot_tokens",
    )(dest_flat)


def _mix_kernel(x_ref, wts_ref, mod_ref, fg_ref, y0_ref, y1_ref, o_ref):
    out = _expert_mix(x_ref[...], wts_ref[...], mod_ref[0][5:6], y0_ref[...], y1_ref[...])
    ms = jnp.mean(out * out, axis=-1, keepdims=True)
    o_ref[...] = out * lax.rsqrt(ms + RMS_EPS) * fg_ref[...]


def _mix(x1, wts, mod, y2, fg):
    nt = N_TOK // COMB_TM
    tiles_per_batch = SEQ // COMB_TM
    return pl.pallas_call(
        _mix_kernel,
        grid=(nt,),
        in_specs=[
            pl.BlockSpec((COMB_TM, D_MODEL), lambda i: (i, 0)),
            pl.BlockSpec((COMB_TM, TOP_K), lambda i: (i, 0)),
            pl.BlockSpec((1, 6, D_MODEL), lambda i: (i // tiles_per_batch, 0, 0)),
            pl.BlockSpec((1, D_MODEL), lambda i: (0, 0)),
            pl.BlockSpec((COMB_TM, ROW_WORDS), lambda i: (i, 0)),
            pl.BlockSpec((COMB_TM, ROW_WORDS), lambda i: (nt + i, 0)),
        ],
        out_specs=pl.BlockSpec((COMB_TM, D_MODEL), lambda i: (i, 0)),
        out_shape=jax.ShapeDtypeStruct((N_TOK, D_MODEL), F32),
        compiler_params=_cparams(("arbitrary",)),
        name="mix",
    )(x1, wts, mod, fg, y2, y2)


def _routing_tables(idr, cnt):
    counts = cnt[0, EXPERT_LANE0:EXPERT_LANE0 + N_EXPERTS].astype(jnp.int32)
    padded = (counts + MOE_BLK - 1) // MOE_BLK * MOE_BLK
    pad_ends = jnp.cumsum(padded)
    pad_starts = pad_ends - padded
    experts = jnp.arange(N_EXPERTS, dtype=jnp.int32)
    is_expert = idr[:, 0:2, None] == experts
    dest = jnp.sum(jnp.where(is_expert, pad_starts, 0), axis=-1) + idr[:, 2:4]
    block_start = jnp.arange(N_BLOCKS, dtype=jnp.int32) * MOE_BLK
    bexp = jnp.minimum(jnp.sum(block_start[:, None] >= pad_ends[None, :], axis=1),
                       N_EXPERTS - 1).astype(jnp.int32)
    owns = padded > 0
    pos_of_expert = jnp.cumsum(owns.astype(jnp.int32)) - 1
    active = jnp.sum(jnp.where(owns[None, :] & (pos_of_expert[None, :] == experts[:, None]),
                               experts[None, :], 0), axis=1).astype(jnp.int32)
    block_pos = jnp.sum(jnp.where(bexp[:, None] == experts[None, :], pos_of_expert[None, :], 0),
                        axis=1).astype(jnp.int32)
    meta = jnp.stack([pad_ends[-1] // MOE_BLK, jnp.sum(owns)]).astype(jnp.int32)
    return dest.astype(jnp.int32), block_pos, active, meta


def _router_weights(w_group, b_group, w_expert, b_expert):
    rw = jnp.zeros((D_MODEL, LANES), F32)
    rw = rw.at[:, :N_GROUPS].set(w_group)
    rw = rw.at[:, EXPERT_LANE0:EXPERT_LANE0 + N_EXPERTS].set(w_expert)
    rb = jnp.zeros((1, LANES), F32)
    rb = rb.at[0, :N_GROUPS].set(b_group)
    rb = rb.at[0, EXPERT_LANE0:EXPERT_LANE0 + N_EXPERTS].set(b_expert.reshape(-1))
    return rw.astype(BF16), rb


def _scaled_qkv_weight(w_in, head_dim):
    scale = jnp.concatenate([jnp.full((D_MODEL,), LOG2_E / math.sqrt(head_dim), F32),
                             jnp.ones((2 * D_MODEL,), F32)])
    return (w_in * scale).astype(BF16)


def kernel(x, c, norm1_g, norm2_g, ada_w, ada_b, diff_w_in, diff_w_out, diff_lambda_q1, diff_lambda_k1, diff_lambda_q2, diff_lambda_k2, diff_subln_g, sb_w_in, sb_w_out, router_group_w, router_group_b, router_expert_w, router_expert_b, expert_w_gate, expert_w_up, expert_w_down, final_norm_g):
    xf = x.reshape(N_TOK, D_MODEL)
    mod_all = _ada(c, ada_w, ada_b)
    slopes = jnp.exp2(-8.0 * jnp.arange(1, DIFF_HEADS + 1, dtype=F32) / DIFF_HEADS)

    pending = None
    for i in range(DEPTH):
        mod = mod_all[i].reshape(BATCH, 6, D_MODEL)
        j = i // 2
        is_diff = i % 2 == 0
        w_in = (_scaled_qkv_weight(diff_w_in[j], DIFF_HEAD_DIM) if is_diff
                else _scaled_qkv_weight(sb_w_in[j], SB_HEAD_DIM))
        g1 = norm1_g[i].reshape(1, D_MODEL)
        if pending is None:
            qkv = _qkv(xf, mod, g1, w_in)
        else:
            xf, qkv = _mix_qkv(*pending, mod, g1, w_in)
        if is_diff:
            lambda_init = 0.8 - 0.6 * math.exp(-0.3 * i)
            lam = (jnp.exp(jnp.sum(diff_lambda_q1[j] * diff_lambda_k1[j]))
                   - jnp.exp(jnp.sum(diff_lambda_q2[j] * diff_lambda_k2[j]))
                   + lambda_init).reshape(1, 1)
            o = _diff_attn(qkv, slopes, lam, diff_subln_g[j].reshape(1, LANES), lambda_init)
            w_out = diff_w_out[j]
        else:
            o = _sb_attn(qkv)
            w_out = sb_w_out[j]
        rw, rb = _router_weights(router_group_w[i], router_group_b[i],
                                 router_expert_w[i], router_expert_b[i])
        x1, h2, idr, wts, cnt = _out_router(o, xf, mod, norm2_g[i].reshape(1, D_MODEL),
                                            w_out.astype(BF16), rw, rb)
        dest, block_pos, active, meta = _routing_tables(idr, cnt)
        xs = _sc_gather_rows(h2, _sc_slot_tokens(dest.reshape(-1)))
        ys = _experts(i, block_pos, active, meta, xs,
                      expert_w_gate, expert_w_up, expert_w_down)
        y2 = _sc_gather_rows(ys, dest.T.reshape(-1))
        pending = (x1, wts, mod, y2)
    out = _mix(*pending, final_norm_g.reshape(1, D_MODEL))
    return out.reshape(BATCH, SEQ, D_MODEL)
```

```python
import functools
import math

import jax
import jax.numpy as jnp
from jax import lax
from jax.experimental import pallas as pl
from jax.experimental.pallas import tpu as pltpu
from jax.experimental.pallas import tpu_sc as plsc

D_MODEL = 1024
BATCH = 8
SEQ = 2048
DEPTH = 2
N_TOK = BATCH * SEQ

CHUNK = 64
DIFF_HEADS = 8
DIFF_HEAD_DIM = D_MODEL // (2 * DIFF_HEADS)
SB_HEADS = 16
SB_HEAD_DIM = D_MODEL // SB_HEADS
N_GROUPS = 4
EXPERTS_PER_GROUP = 8
N_EXPERTS = N_GROUPS * EXPERTS_PER_GROUP
TOP_K = 2
EXPERT_HIDDEN = D_MODEL // 2
RMS_EPS = 1e-6
SUBLN_EPS = 1e-5

LANES = 128
EXPERT_LANE0 = 32
EXP2_UNDERFLOW = -150.0
LOG2_E = math.log2(math.e)

ADA_TN = 1536
QKV_TM = 512
DIFF_T = 512
DIFF_GROUP = 4
SB_T = 256
SB_PAIRS = 4
OUT_TM = 512
MOE_BLK = 512
N_SLOTS = N_TOK * TOP_K + N_EXPERTS * MOE_BLK
N_BLOCKS = N_SLOTS // MOE_BLK
COMB_TM = 512
ROW_WORDS = D_MODEL // 2

SC_CORES = 2
SC_SUBCORES = 16
SC_LANES = 16
SC_CHUNK = 64

VMEM_LIMIT = 56 * 1024 * 1024

F32 = jnp.float32
BF16 = jnp.bfloat16


def _cparams(sem):
    return pltpu.CompilerParams(dimension_semantics=sem, vmem_limit_bytes=VMEM_LIMIT)


def _ada_kernel(c_ref, w_ref, b_ref, o_ref):
    c = c_ref[...]
    cond = c * jax.nn.sigmoid(c)
    o_ref[0] = jnp.dot(cond.astype(BF16), w_ref[0].astype(BF16),
                       preferred_element_type=F32) + b_ref[0]


def _ada(c, ada_w, ada_b):
    six_d = ada_w.shape[-1]
    return pl.pallas_call(
        _ada_kernel,
        grid=(DEPTH, six_d // ADA_TN),
        in_specs=[
            pl.BlockSpec((BATCH, D_MODEL), lambda l, n: (0, 0)),
            pl.BlockSpec((1, D_MODEL, ADA_TN), lambda l, n: (l, 0, n)),
            pl.BlockSpec((1, 1, ADA_TN), lambda l, n: (l, 0, n)),
        ],
        out_specs=pl.BlockSpec((1, BATCH, ADA_TN), lambda l, n: (l, 0, n)),
        out_shape=jax.ShapeDtypeStruct((DEPTH, BATCH, six_d), F32),
        compiler_params=_cparams(("arbitrary", "arbitrary")),
        name="ada",
    )(c, ada_w, ada_b.reshape(DEPTH, 1, six_d))


def _modulated_norm(x, g, scale, shift):
    ms = jnp.mean(x * x, axis=-1, keepdims=True)
    return x * lax.rsqrt(ms + RMS_EPS) * (g * (1.0 + scale)) + shift


def _qkv_kernel(x_ref, mod_ref, g_ref, w_ref, o_ref):
    m = mod_ref[0]
    h = _modulated_norm(x_ref[...], g_ref[...], m[1:2], m[0:1]).astype(BF16)
    for n in range(3):
        cols = slice(n * D_MODEL, (n + 1) * D_MODEL)
        o_ref[:, cols] = jnp.dot(h, w_ref[:, cols], preferred_element_type=F32).astype(BF16)


def _qkv(x, mod, g, w):
    tiles_per_batch = SEQ // QKV_TM
    return pl.pallas_call(
        _qkv_kernel,
        grid=(N_TOK // QKV_TM,),
        in_specs=[
            pl.BlockSpec((QKV_TM, D_MODEL), lambda i: (i, 0)),
            pl.BlockSpec((1, 6, D_MODEL), lambda i: (i // tiles_per_batch, 0, 0)),
            pl.BlockSpec((1, D_MODEL), lambda i: (0, 0)),
            pl.BlockSpec((D_MODEL, 3 * D_MODEL), lambda i: (0, 0)),
        ],
        out_specs=pl.BlockSpec((QKV_TM, 3 * D_MODEL), lambda i: (i, 0)),
        out_shape=jax.ShapeDtypeStruct((N_TOK, 3 * D_MODEL), BF16),
        compiler_params=_cparams(("arbitrary",)),
        name="qkv",
    )(x, mod, g, w)


def _expert_mix(x, wts, gate, y0_words, y1_words):
    y0 = _unpack_bf16_pairs(lax.bitcast_convert_type(y0_words, jnp.uint32))
    y1 = _unpack_bf16_pairs(lax.bitcast_convert_type(y1_words, jnp.uint32))
    return x + gate * (wts[:, 0:1] * y0 + wts[:, 1:2] * y1)


def _mix_qkv_kernel(x1_ref, wts_ref, pmod_ref, y0_ref, y1_ref, mod_ref, g_ref, w_ref,
                    x_ref, o_ref):
    x = _expert_mix(x1_ref[...], wts_ref[...], pmod_ref[0][5:6], y0_ref[...], y1_ref[...])
    x_ref[...] = x
    m = mod_ref[0]
    h = _modulated_norm(x, g_ref[...], m[1:2], m[0:1]).astype(BF16)
    for n in range(3):
        cols = slice(n * D_MODEL, (n + 1) * D_MODEL)
        o_ref[:, cols] = jnp.dot(h, w_ref[:, cols], preferred_element_type=F32).astype(BF16)


def _mix_qkv(x1, wts, prev_mod, y2, mod, g, w):
    nt = N_TOK // QKV_TM
    tiles_per_batch = SEQ // QKV_TM
    return pl.pallas_call(
        _mix_qkv_kernel,
        grid=(nt,),
        in_specs=[
            pl.BlockSpec((QKV_TM, D_MODEL), lambda i: (i, 0)),
            pl.BlockSpec((QKV_TM, TOP_K), lambda i: (i, 0)),
            pl.BlockSpec((1, 6, D_MODEL), lambda i: (i // tiles_per_batch, 0, 0)),
            pl.BlockSpec((QKV_TM, ROW_WORDS), lambda i: (i, 0)),
            pl.BlockSpec((QKV_TM, ROW_WORDS), lambda i: (nt + i, 0)),
            pl.BlockSpec((1, 6, D_MODEL), lambda i: (i // tiles_per_batch, 0, 0)),
            pl.BlockSpec((1, D_MODEL), lambda i: (0, 0)),
            pl.BlockSpec((D_MODEL, 3 * D_MODEL), lambda i: (0, 0)),
        ],
        out_specs=[
            pl.BlockSpec((QKV_TM, D_MODEL), lambda i: (i, 0)),
            pl.BlockSpec((QKV_TM, 3 * D_MODEL), lambda i: (i, 0)),
        ],
        out_shape=[
            jax.ShapeDtypeStruct((N_TOK, D_MODEL), F32),
            jax.ShapeDtypeStruct((N_TOK, 3 * D_MODEL), BF16),
        ],
        compiler_params=_cparams(("arbitrary",)),
        name="mix_qkv",
    )(x1, wts, prev_mod, y2, y2, mod, g, w)


def _half_masked(q, upper):
    lane = lax.broadcasted_iota(jnp.int32, q.shape, 1)
    keep = (lane >= LANES // 2) if upper else (lane < LANES // 2)
    return jnp.where(keep, q, jnp.zeros_like(q))


def _pack_bf16_pairs(x):
    half = x.shape[1] // 2
    hi = lax.bitcast_convert_type(x[:, :half].astype(BF16).astype(F32), jnp.uint32)
    lo = lax.bitcast_convert_type(x[:, half:].astype(BF16).astype(F32), jnp.uint32)
    return hi | (lo >> 16)


def _unpack_bf16_pairs(u):
    hi = lax.bitcast_convert_type(u & jnp.uint32(0xFFFF0000), F32)
    lo = lax.bitcast_convert_type(u << 16, F32)
    return jnp.concatenate([hi, lo], axis=1)


def _lane_tile(x, n):
    return jnp.concatenate([x] * n, axis=1)


def _qk(q, k):
    return lax.dot_general(q, k, (((1,), (1,)), ((), ())), preferred_element_type=F32)


def _diff_attn_kernel(slopes_ref, lam_ref, q_ref, k_ref, v_ref, g_ref, o_ref, *scratch,
                      lambda_init):
    n_chains = 2 * DIFF_GROUP
    m_refs = scratch[:n_chains]
    acc_refs = scratch[n_chains:]
    ones = jnp.ones((DIFF_T, LANES), BF16)

    def head_lanes(c):
        return slice((c // 2) * LANES, (c // 2 + 1) * LANES)

    row = lax.broadcasted_iota(jnp.int32, (DIFF_T, DIFF_T), 0)
    col = lax.broadcasted_iota(jnp.int32, (DIFF_T, DIFF_T), 1)
    allowed = (col // CHUNK) <= (row // CHUNK)
    slopes = [slopes_ref[pl.program_id(1) * DIFF_GROUP + g] * LOG2_E for g in range(DIFF_GROUP)]
    rels = [(row - col).astype(F32) * slope for slope in slopes]
    diag_biases = [jnp.where(allowed, jnp.abs(rel), jnp.inf) for rel in rels]
    out_gain = g_ref[...] * (1.0 - lambda_init)

    def key_block(j, c):
        k = k_ref[pl.ds(j * DIFF_T, DIFF_T), head_lanes(c)]
        v1 = jnp.concatenate([v_ref[pl.ds(j * DIFF_T, DIFF_T), head_lanes(c)], ones], axis=1)
        return k, v1

    def query_tile(qi, _):
        q = q_ref[pl.ds(qi * DIFF_T, DIFF_T), :]
        qm = [_half_masked(q[:, head_lanes(c)], c % 2 == 1) for c in range(n_chains)]

        kv = [key_block(qi, c) for c in range(n_chains)]
        s_maps = [_qk(qm[c], kv[c][0]) - diag_biases[c // 2] for c in range(n_chains)]
        for c in range(n_chains):
            m = jnp.max(s_maps[c], axis=-1, keepdims=True)
            p = jnp.exp2(s_maps[c] - m)
            m_refs[c][...] = jnp.broadcast_to(m, (DIFF_T, LANES))
            acc_refs[c][...] = jnp.dot(p.astype(BF16), kv[c][1], preferred_element_type=F32)

        def body(j, _):
            kv = [key_block(j, c) for c in range(n_chains)]
            ahead = jnp.asarray((qi - j) * DIFF_T).astype(F32)
            s_maps = [_qk(qm[c], kv[c][0]) - rels[c // 2] for c in range(n_chains)]
            for c in range(n_chains):
                shift = slopes[c // 2] * ahead
                m = m_refs[c][...]
                m_new = jnp.maximum(m, jnp.max(s_maps[c], axis=-1, keepdims=True) - shift)
                alpha = jnp.exp2(m - m_new)
                p = jnp.exp2(s_maps[c] - _lane_tile(m_new + shift, DIFF_T // LANES))
                m_refs[c][...] = m_new
                acc_refs[c][...] = _lane_tile(alpha, 2) * acc_refs[c][...] + jnp.dot(
                    p.astype(BF16), kv[c][1], preferred_element_type=F32)
            return 0

        lax.fori_loop(0, qi, body, 0)
        for g in range(DIFF_GROUP):
            a0 = acc_refs[2 * g][...]
            a1 = acc_refs[2 * g + 1][...]
            o = a0[:, :LANES] / a0[:, LANES:] - lam_ref[...] * (a1[:, :LANES] / a1[:, LANES:])
            ms = jnp.mean(o * o, axis=-1, keepdims=True)
            o = o * lax.rsqrt(ms + SUBLN_EPS) * out_gain
            o_ref[pl.ds(qi * DIFF_T, DIFF_T), head_lanes(2 * g)] = o.astype(BF16)
        return 0

    lax.fori_loop(0, SEQ // DIFF_T, query_tile, 0)


def _diff_attn(qkv, slopes, lam, subln_g, lambda_init):
    groups = DIFF_HEADS // DIFF_GROUP
    width = DIFF_GROUP * LANES
    return pl.pallas_call(
        functools.partial(_diff_attn_kernel, lambda_init=lambda_init),
        grid=(BATCH, groups),
        in_specs=[
            pl.BlockSpec(memory_space=pltpu.SMEM),
            pl.BlockSpec((1, 1), lambda b, h: (0, 0)),
            pl.BlockSpec((SEQ, width), lambda b, h: (b, h)),
            pl.BlockSpec((SEQ, width), lambda b, h: (b, groups + h)),
            pl.BlockSpec((SEQ, width), lambda b, h: (b, 2 * groups + h)),
            pl.BlockSpec((1, LANES), lambda b, h: (0, 0)),
        ],
        out_specs=pl.BlockSpec((SEQ, width), lambda b, h: (b, h)),
        out_shape=jax.ShapeDtypeStruct((N_TOK, D_MODEL), BF16),
        scratch_shapes=([pltpu.VMEM((DIFF_T, LANES), F32)] * (2 * DIFF_GROUP)
                        + [pltpu.VMEM((DIFF_T, 2 * LANES), F32)] * (2 * DIFF_GROUP)),
        compiler_params=_cparams(("arbitrary", "arbitrary")),
        name="diff_attn",
    )(slopes, lam, qkv, qkv, qkv, subln_g)


def _sb_attn_kernel(q_ref, k_ref, v_ref, o_ref, *scratch):
    n_heads = 2 * SB_PAIRS
    tail_refs = scratch[:n_heads]
    acc_refs = scratch[n_heads:]

    def pair_lanes(h):
        return slice((h // 2) * LANES, (h // 2 + 1) * LANES)

    row = lax.broadcasted_iota(jnp.int32, (SB_T, SB_T), 0)
    col = lax.broadcasted_iota(jnp.int32, (SB_T, SB_T), 1)
    strict = col < row
    neg_from = jnp.where(row >= col, -1.0, 0.0).astype(BF16)

    def scores(j, h, qhead, mask):
        z = _qk(qhead, k_ref[pl.ds(j * SB_T, SB_T), pair_lanes(h)])
        sp = jnp.maximum(z, 0.0) + jnp.log2(1.0 + jnp.exp2(-jnp.abs(z)))
        if mask is not None:
            sp = jnp.where(mask, sp, 0.0)
        return z, sp

    def weighted(j, h, z, sp, tail, mask):
        log_a = (z + jnp.dot(sp.astype(BF16), neg_from, preferred_element_type=F32)
                 + _lane_tile(tail, SB_T // LANES))
        a = jnp.exp2(log_a)
        if mask is not None:
            a = jnp.where(mask, a, 0.0)
        return jnp.dot(a.astype(BF16), v_ref[pl.ds(j * SB_T, SB_T), pair_lanes(h)],
                       preferred_element_type=F32)

    def row_sum(sp):
        return jnp.broadcast_to(jnp.sum(sp, axis=-1, keepdims=True), (SB_T, LANES))

    zero = jnp.zeros((SB_T, LANES), F32)
    lane = lax.broadcasted_iota(jnp.int32, (SB_T, LANES), 1)

    def query_heads(qi):
        q = q_ref[pl.ds(qi * SB_T, SB_T), :]
        return [_half_masked(q[:, pair_lanes(h)], h % 2 == 1) for h in range(n_heads)]

    def finish(qi):
        for h in range(0, n_heads, 2):
            o_ref[pl.ds(qi * SB_T, SB_T), pair_lanes(h)] = jnp.where(
                lane < LANES // 2, acc_refs[h][...], acc_refs[h + 1][...]).astype(BF16)

    qh = query_heads(0)
    for h in range(n_heads):
        z, sp = scores(0, h, qh[h], strict)
        acc_refs[h][...] = weighted(0, h, z, sp, zero, strict)
    finish(0)

    def query_tile(qi, _):
        qh = query_heads(qi)
        zs = [(scores(qi, h, qh[h], strict), scores(qi - 1, h, qh[h], None))
              for h in range(n_heads)]
        for h in range(n_heads):
            (z0, sp0), (z1, sp1) = zs[h]
            tail0 = -row_sum(sp0)
            acc_refs[h][...] = (weighted(qi, h, z0, sp0, zero, strict)
                                + weighted(qi - 1, h, z1, sp1, tail0, None))
            tail_refs[h][...] = tail0 - row_sum(sp1)

        def cond(state):
            j, live = state
            return (j >= 0) & (functools.reduce(jnp.maximum, live) > EXP2_UNDERFLOW)

        def body(state):
            j, live = state

            def advance(h):
                z, sp = scores(j, h, qh[h], None)
                tail = tail_refs[h][...]
                acc_refs[h][...] += weighted(j, h, z, sp, tail, None)
                tail = tail - row_sum(sp)
                tail_refs[h][...] = tail
                return jnp.max(tail)

            live = tuple(
                lax.cond(live[h] > EXP2_UNDERFLOW, functools.partial(advance, h),
                         functools.partial(lambda done: done, live[h]))
                for h in range(n_heads))
            return j - 1, live

        live = tuple(jnp.max(t[...]) for t in tail_refs)
        lax.while_loop(cond, body, (qi - 2, live))
        finish(qi)
        return 0

    lax.fori_loop(1, SEQ // SB_T, query_tile, 0)


def _sb_attn(qkv):
    groups = SB_HEADS // (2 * SB_PAIRS)
    width = SB_PAIRS * LANES
    return pl.pallas_call(
        _sb_attn_kernel,
        grid=(BATCH, groups),
        in_specs=[
            pl.BlockSpec((SEQ, width), lambda b, h: (b, h)),
            pl.BlockSpec((SEQ, width), lambda b, h: (b, groups + h)),
            pl.BlockSpec((SEQ, width), lambda b, h: (b, 2 * groups + h)),
        ],
        out_specs=pl.BlockSpec((SEQ, width), lambda b, h: (b, h)),
        out_shape=jax.ShapeDtypeStruct((N_TOK, D_MODEL), BF16),
        scratch_shapes=[pltpu.VMEM((SB_T, LANES), F32)] * (4 * SB_PAIRS),
        compiler_params=_cparams(("arbitrary", "arbitrary")),
        name="sb_attn",
    )(qkv, qkv, qkv)


def _out_router_kernel(o_ref, x_ref, mod_ref, g_ref, w_ref, rw_ref, rb_ref,
                       x1_ref, h2_ref, idr_ref, wts_ref, cnt_ref, tri_ref, base_ref):
    i = pl.program_id(0)

    @pl.when(i == 0)
    def _():
        r = lax.broadcasted_iota(jnp.int32, (OUT_TM, OUT_TM), 0)
        c = lax.broadcasted_iota(jnp.int32, (OUT_TM, OUT_TM), 1)
        tri_ref[...] = jnp.where(c < r, 1.0, 0.0).astype(BF16)
        base_ref[...] = jnp.zeros_like(base_ref)

    m = mod_ref[0]
    y = jnp.dot(o_ref[...], w_ref[...], preferred_element_type=F32)
    x1 = x_ref[...] + m[2:3] * y
    x1_ref[...] = x1
    h2 = _modulated_norm(x1, g_ref[...], m[4:5], m[3:4])
    h2_ref[...] = lax.bitcast_convert_type(_pack_bf16_pairs(h2), jnp.int32)

    logits = jnp.dot(h2.astype(BF16), rw_ref[...], preferred_element_type=F32) + rb_ref[...]
    lane = lax.broadcasted_iota(jnp.int32, logits.shape, 1).astype(F32)
    neg_inf = jnp.float32(-jnp.inf)
    big = jnp.float32(1e9)

    is_group = lane < N_GROUPS
    gl = jnp.where(is_group, logits, neg_inf)
    gmax = jnp.max(gl, axis=-1, keepdims=True)
    gidx = jnp.min(jnp.where(gl == gmax, lane, big), axis=-1, keepdims=True)
    gsum = jnp.sum(jnp.where(is_group, jnp.exp(logits - gmax), 0.0), axis=-1, keepdims=True)
    g_w = 1.0 / gsum

    lo = EXPERT_LANE0 + EXPERTS_PER_GROUP * gidx
    in_group = (lane >= lo) & (lane < lo + EXPERTS_PER_GROUP)
    el = jnp.where(in_group, logits, neg_inf)
    v0 = jnp.max(el, axis=-1, keepdims=True)
    i0 = jnp.min(jnp.where(el == v0, lane, big), axis=-1, keepdims=True)
    el = jnp.where(lane == i0, neg_inf, el)
    v1 = jnp.max(el, axis=-1, keepdims=True)
    i1 = jnp.min(jnp.where(el == v1, lane, big), axis=-1, keepdims=True)
    t = jnp.exp(v1 - v0)
    w0 = g_w / (1.0 + t)
    w1 = g_w * t / (1.0 + t)

    oh0 = jnp.where(lane == i0, 1.0, 0.0)
    oh1 = jnp.where(lane == i1, 1.0, 0.0)
    both = oh0 + oh1
    before = jnp.dot(tri_ref[...], both.astype(BF16), preferred_element_type=F32) + base_ref[...]
    r0 = jnp.sum(before * oh0, axis=-1, keepdims=True)
    r1 = jnp.sum(before * oh1, axis=-1, keepdims=True)
    base_ref[...] = base_ref[...] + jnp.sum(both, axis=0, keepdims=True)
    cnt_ref[...] = base_ref[...]

    e0 = i0 - EXPERT_LANE0
    e1 = i1 - EXPERT_LANE0
    idr = jnp.where(lane == 0, e0, jnp.where(lane == 1, e1, jnp.where(lane == 2, r0, r1)))
    idr_ref[...] = idr[:, :4].astype(jnp.int32)
    wts_ref[...] = jnp.where(lane == 0, w0, w1)[:, :2]


def _out_router(o, x, mod, g, w, rw, rb):
    tiles_per_batch = SEQ // OUT_TM
    return pl.pallas_call(
        _out_router_kernel,
        grid=(N_TOK // OUT_TM,),
        in_specs=[
            pl.BlockSpec((OUT_TM, D_MODEL), lambda i: (i, 0)),
            pl.BlockSpec((OUT_TM, D_MODEL), lambda i: (i, 0)),
            pl.BlockSpec((1, 6, D_MODEL), lambda i: (i // tiles_per_batch, 0, 0)),
            pl.BlockSpec((1, D_MODEL), lambda i: (0, 0)),
            pl.BlockSpec((D_MODEL, D_MODEL), lambda i: (0, 0)),
            pl.BlockSpec((D_MODEL, LANES), lambda i: (0, 0)),
            pl.BlockSpec((1, LANES), lambda i: (0, 0)),
        ],
        out_specs=[
            pl.BlockSpec((OUT_TM, D_MODEL), lambda i: (i, 0)),
            pl.BlockSpec((OUT_TM, ROW_WORDS), lambda i: (i, 0)),
            pl.BlockSpec((OUT_TM, 4), lambda i: (i, 0)),
            pl.BlockSpec((OUT_TM, 2), lambda i: (i, 0)),
            pl.BlockSpec((1, LANES), lambda i: (0, 0)),
        ],
        out_shape=[
            jax.ShapeDtypeStruct((N_TOK, D_MODEL), F32),
            jax.ShapeDtypeStruct((N_TOK, ROW_WORDS), jnp.int32),
            jax.ShapeDtypeStruct((N_TOK, 4), jnp.int32),
            jax.ShapeDtypeStruct((N_TOK, 2), F32),
            jax.ShapeDtypeStruct((1, LANES), F32),
        ],
        scratch_shapes=[
            pltpu.VMEM((OUT_TM, OUT_TM), BF16),
            pltpu.VMEM((1, LANES), F32),
        ],
        compiler_params=_cparams(("arbitrary",)),
        name="out_router",
    )(o, x, mod, g, w, rw, rb)


def _expert_kernel(ord_ref, active_ref, meta_ref, xs_ref, wg_hbm, wu_hbm, wd_hbm, ys_ref,
                   wg_f32, wu_f32, wd_f32, sems, wg_bf, wu_bf, wd_bf, *, layer):
    i = pl.program_id(0)
    n_used = meta_ref[0]
    n_active = meta_ref[1]

    def fetch(pos, slot):
        e = active_ref[pos]
        return (
            pltpu.make_async_copy(wg_hbm.at[layer, e], wg_f32.at[slot], sems.at[slot, 0]),
            pltpu.make_async_copy(wu_hbm.at[layer, e], wu_f32.at[slot], sems.at[slot, 1]),
            pltpu.make_async_copy(wd_hbm.at[layer, e], wd_f32.at[slot], sems.at[slot, 2]),
        )

    @pl.when(i == 0)
    def _():
        for copy in fetch(0, 0):
            copy.start()

        @pl.when(n_active > 1)
        def _():
            for copy in fetch(1, 1):
                copy.start()

    @pl.when(i < n_used)
    def _():
        pos = ord_ref[i]
        changed = (i == 0) | (pos != ord_ref[jnp.maximum(i - 1, 0)])
        for slot in range(2):
            @pl.when(changed & (pos % 2 == slot))
            def _():
                for copy in fetch(pos, slot):
                    copy.wait()
                wg_bf[...] = wg_f32[slot].astype(BF16)
                wu_bf[...] = wu_f32[slot].astype(BF16)
                wd_bf[...] = wd_f32[slot].astype(BF16)

                @pl.when(pos + 2 < n_active)
                def _():
                    for copy in fetch(pos + 2, slot):
                        copy.start()

        x = _unpack_bf16_pairs(lax.bitcast_convert_type(xs_ref[...], jnp.uint32)).astype(BF16)
        g = jnp.dot(x, wg_bf[...], preferred_element_type=F32)
        u = jnp.dot(x, wu_bf[...], preferred_element_type=F32)
        hid = (g * jax.nn.sigmoid(g)) * u
        y = jnp.dot(hid.astype(BF16), wd_bf[...], preferred_element_type=F32)
        ys_ref[...] = lax.bitcast_convert_type(_pack_bf16_pairs(y), jnp.int32)

    @pl.when(i >= n_used)
    def _():
        ys_ref[...] = jnp.zeros_like(ys_ref)


def _experts(layer, block_pos, active, meta, xs, wg, wu, wd):
    def in_row_map(i, block_pos, active, meta):
        return (jnp.minimum(i, meta[0] - 1), 0)

    def row_map(i, block_pos, active, meta):
        return (i, 0)

    return pl.pallas_call(
        functools.partial(_expert_kernel, layer=layer),
        grid_spec=pltpu.PrefetchScalarGridSpec(
            num_scalar_prefetch=3,
            grid=(N_BLOCKS,),
            in_specs=[
                pl.BlockSpec((MOE_BLK, ROW_WORDS), in_row_map),
                pl.BlockSpec(memory_space=pl.ANY),
                pl.BlockSpec(memory_space=pl.ANY),
                pl.BlockSpec(memory_space=pl.ANY),
            ],
            out_specs=pl.BlockSpec((MOE_BLK, ROW_WORDS), row_map),
            scratch_shapes=[
                pltpu.VMEM((2, D_MODEL, EXPERT_HIDDEN), F32),
                pltpu.VMEM((2, D_MODEL, EXPERT_HIDDEN), F32),
                pltpu.VMEM((2, EXPERT_HIDDEN, D_MODEL), F32),
                pltpu.SemaphoreType.DMA((2, 3)),
                pltpu.VMEM((D_MODEL, EXPERT_HIDDEN), BF16),
                pltpu.VMEM((D_MODEL, EXPERT_HIDDEN), BF16),
                pltpu.VMEM((EXPERT_HIDDEN, D_MODEL), BF16),
            ],
        ),
        out_shape=jax.ShapeDtypeStruct((N_SLOTS, ROW_WORDS), jnp.int32),
        compiler_params=_cparams(("arbitrary",)),
        name="experts",
    )(block_pos, active, meta, xs, wg, wu, wd)


def _sc_gather_rows(table, idx):
    n_rows = idx.shape[0]
    width = table.shape[1]
    workers = SC_CORES * SC_SUBCORES
    per_worker = n_rows // workers
    assert per_worker * workers == n_rows and per_worker % SC_CHUNK == 0
    mesh = plsc.VectorSubcoreMesh(core_axis_name="c", subcore_axis_name="s")

    n_chunks = per_worker // SC_CHUNK

    def body(table_hbm, idx_hbm, out_hbm, idx_v, buf0, buf1, gsem0, gsem1, wsem0, wsem1):
        wid = lax.axis_index("s") * SC_CORES + lax.axis_index("c")
        base = pl.multiple_of(wid * per_worker, SC_CHUNK)
        bufs, gsems, wsems = (buf0, buf1), (gsem0, gsem1), (wsem0, wsem1)
        pltpu.sync_copy(idx_hbm.at[pl.ds(base, per_worker)], idx_v)

        def gather(c):
            rows = idx_v.at[pl.ds(c * SC_CHUNK, SC_CHUNK)]
            return pltpu.async_copy(table_hbm.at[rows], bufs[c % 2], gsems[c % 2])

        def write(c):
            dst = out_hbm.at[pl.ds(base + c * SC_CHUNK, SC_CHUNK)]
            return pltpu.async_copy(bufs[c % 2], dst, wsems[c % 2])

        gathers = {0: gather(0)}
        writes = {}
        for c in range(n_chunks):
            if c + 1 < n_chunks:
                if c >= 1:
                    writes.pop(c - 1).wait()
                gathers[c + 1] = gather(c + 1)
            gathers.pop(c).wait()
            writes[c] = write(c)
        for c in sorted(writes):
            writes[c].wait()

    return pl.kernel(
        body,
        out_type=jax.ShapeDtypeStruct((n_rows, width), table.dtype),
        mesh=mesh,
        scratch_types=[
            pltpu.VMEM((per_worker,), jnp.int32),
            pltpu.VMEM((SC_CHUNK, width), table.dtype),
            pltpu.VMEM((SC_CHUNK, width), table.dtype),
            pltpu.SemaphoreType.DMA,
            pltpu.SemaphoreType.DMA,
            pltpu.SemaphoreType.DMA,
            pltpu.SemaphoreType.DMA,
        ],
        name="sc_gather_rows",
    )(table, idx)


def _sc_slot_tokens(dest_flat):
    n_assign = N_TOK * TOP_K
    lanes = SC_LANES
    mesh = plsc.VectorSubcoreMesh(core_axis_name="c", subcore_axis_name="s")

    def body(dest_hbm, out_hbm, dest_v, slot_v):
        wid = lax.axis_index("s") * SC_CORES + lax.axis_index("c")

        @pl.when(wid == 0)
        def _():
            pltpu.sync_copy(dest_hbm, dest_v)
            lane = lax.iota(jnp.int32, lanes)

            def init(i, _):
                start = pl.multiple_of(i * lanes, lanes)
                slot_v[pl.ds(start, lanes)] = (lane + start) & (N_TOK - 1)
                return 0

            def place(i, _):
                start = pl.multiple_of(i * lanes, lanes)
                slots = dest_v[pl.ds(start, lanes)]
                plsc.store_scatter(slot_v, [slots], (lane + start) >> 1)
                return 0

            lax.fori_loop(0, N_SLOTS // lanes, init, 0)
            lax.fori_loop(0, n_assign // lanes, place, 0)
            pltpu.sync_copy(slot_v, out_hbm)

    return pl.kernel(
        body,
        out_type=jax.ShapeDtypeStruct((N_SLOTS,), jnp.int32),
        mesh=mesh,
        scratch_types=[
            pltpu.VMEM((n_assign,), jnp.int32),
            pltpu.VMEM((N_SLOTS,), jnp.int32),
        ],
        compiler_params=pltpu.CompilerParams(needs_layout_passes=False),
        name="sc_slot_tokens",
    )(dest_flat)


def _mix_kernel(x_ref, wts_ref, mod_ref, fg_ref, y0_ref, y1_ref, o_ref):
    out = _expert_mix(x_ref[...], wts_ref[...], mod_ref[0][5:6], y0_ref[...], y1_ref[...])
    ms = jnp.mean(out * out, axis=-1, keepdims=True)
    o_ref[...] = out * lax.rsqrt(ms + RMS_EPS) * fg_ref[...]


def _mix(x1, wts, mod, y2, fg):
    nt = N_TOK // COMB_TM
    tiles_per_batch = SEQ // COMB_TM
    return pl.pallas_call(
        _mix_kernel,
        grid=(nt,),
        in_specs=[
            pl.BlockSpec((COMB_TM, D_MODEL), lambda i: (i, 0)),
            pl.BlockSpec((COMB_TM, TOP_K), lambda i: (i, 0)),
            pl.BlockSpec((1, 6, D_MODEL), lambda i: (i // tiles_per_batch, 0, 0)),
            pl.BlockSpec((1, D_MODEL), lambda i: (0, 0)),
            pl.BlockSpec((COMB_TM, ROW_WORDS), lambda i: (i, 0)),
            pl.BlockSpec((COMB_TM, ROW_WORDS), lambda i: (nt + i, 0)),
        ],
        out_specs=pl.BlockSpec((COMB_TM, D_MODEL), lambda i: (i, 0)),
        out_shape=jax.ShapeDtypeStruct((N_TOK, D_MODEL), F32),
        compiler_params=_cparams(("arbitrary",)),
        name="mix",
    )(x1, wts, mod, fg, y2, y2)


def _routing_tables(idr, cnt):
    counts = cnt[0, EXPERT_LANE0:EXPERT_LANE0 + N_EXPERTS].astype(jnp.int32)
    padded = (counts + MOE_BLK - 1) // MOE_BLK * MOE_BLK
    pad_ends = jnp.cumsum(padded)
    pad_starts = pad_ends - padded
    experts = jnp.arange(N_EXPERTS, dtype=jnp.int32)
    is_expert = idr[:, 0:2, None] == experts
    dest = jnp.sum(jnp.where(is_expert, pad_starts, 0), axis=-1) + idr[:, 2:4]
    block_start = jnp.arange(N_BLOCKS, dtype=jnp.int32) * MOE_BLK
    bexp = jnp.minimum(jnp.sum(block_start[:, None] >= pad_ends[None, :], axis=1),
                       N_EXPERTS - 1).astype(jnp.int32)
    owns = padded > 0
    pos_of_expert = jnp.cumsum(owns.astype(jnp.int32)) - 1
    active = jnp.sum(jnp.where(owns[None, :] & (pos_of_expert[None, :] == experts[:, None]),
                               experts[None, :], 0), axis=1).astype(jnp.int32)
    block_pos = jnp.sum(jnp.where(bexp[:, None] == experts[None, :], pos_of_expert[None, :], 0),
                        axis=1).astype(jnp.int32)
    meta = jnp.stack([pad_ends[-1] // MOE_BLK, jnp.sum(owns)]).astype(jnp.int32)
    return dest.astype(jnp.int32), block_pos, active, meta


def _router_weights(w_group, b_group, w_expert, b_expert):
    rw = jnp.zeros((D_MODEL, LANES), F32)
    rw = rw.at[:, :N_GROUPS].set(w_group)
    rw = rw.at[:, EXPERT_LANE0:EXPERT_LANE0 + N_EXPERTS].set(w_expert)
    rb = jnp.zeros((1, LANES), F32)
    rb = rb.at[0, :N_GROUPS].set(b_group)
    rb = rb.at[0, EXPERT_LANE0:EXPERT_LANE0 + N_EXPERTS].set(b_expert.reshape(-1))
    return rw.astype(BF16), rb


def _scaled_qkv_weight(w_in, head_dim):
    scale = jnp.concatenate([jnp.full((D_MODEL,), LOG2_E / math.sqrt(head_dim), F32),
                             jnp.ones((2 * D_MODEL,), F32)])
    return (w_in * scale).astype(BF16)


def kernel(x, c, norm1_g, norm2_g, ada_w, ada_b, diff_w_in, diff_w_out, diff_lambda_q1, diff_lambda_k1, diff_lambda_q2, diff_lambda_k2, diff_subln_g, sb_w_in, sb_w_out, router_group_w, router_group_b, router_expert_w, router_expert_b, expert_w_gate, expert_w_up, expert_w_down, final_norm_g):
    xf = x.reshape(N_TOK, D_MODEL)
    mod_all = _ada(c, ada_w, ada_b)
    slopes = jnp.exp2(-8.0 * jnp.arange(1, DIFF_HEADS + 1, dtype=F32) / DIFF_HEADS)

    pending = None
    for i in range(DEPTH):
        mod = mod_all[i].reshape(BATCH, 6, D_MODEL)
        j = i // 2
        is_diff = i % 2 == 0
        w_in = (_scaled_qkv_weight(diff_w_in[j], DIFF_HEAD_DIM) if is_diff
                else _scaled_qkv_weight(sb_w_in[j], SB_HEAD_DIM))
        g1 = norm1_g[i].reshape(1, D_MODEL)
        if pending is None:
            qkv = _qkv(xf, mod, g1, w_in)
        else:
            xf, qkv = _mix_qkv(*pending, mod, g1, w_in)
        if is_diff:
            lambda_init = 0.8 - 0.6 * math.exp(-0.3 * i)
            lam = (jnp.exp(jnp.sum(diff_lambda_q1[j] * diff_lambda_k1[j]))
                   - jnp.exp(jnp.sum(diff_lambda_q2[j] * diff_lambda_k2[j]))
                   + lambda_init).reshape(1, 1)
            o = _diff_attn(qkv, slopes, lam, diff_subln_g[j].reshape(1, LANES), lambda_init)
            w_out = diff_w_out[j]
        else:
            o = _sb_attn(qkv)
            w_out = sb_w_out[j]
        rw, rb = _router_weights(router_group_w[i], router_group_b[i],
                                 router_expert_w[i], router_expert_b[i])
        x1, h2, idr, wts, cnt = _out_router(o, xf, mod, norm2_g[i].reshape(1, D_MODEL),
                                            w_out.astype(BF16), rw, rb)
        dest, block_pos, active, meta = _routing_tables(idr, cnt)
        xs = _sc_gather_rows(h2, _sc_slot_tokens(dest.reshape(-1)))
        ys = _experts(i, block_pos, active, meta, xs,
                      expert_w_gate, expert_w_up, expert_w_down)
        y2 = _sc_gather_rows(ys, dest.T.reshape(-1))
        pending = (x1, wts, mod, y2)
    out = _mix(*pending, final_norm_g.reshape(1, D_MODEL))
    return out.reshape(BATCH, SEQ, D_MODEL)
```

```python
import functools
import math

import jax
import jax.numpy as jnp
from jax import lax
from jax.experimental import pallas as pl
from jax.experimental.pallas import tpu as pltpu
from jax.experimental.pallas import tpu_sc as plsc

D_MODEL = 1024
BATCH = 8
SEQ = 2048
DEPTH = 2
N_TOK = BATCH * SEQ

CHUNK = 64
DIFF_HEADS = 8
DIFF_HEAD_DIM = D_MODEL // (2 * DIFF_HEADS)
SB_HEADS = 16
SB_HEAD_DIM = D_MODEL // SB_HEADS
N_GROUPS = 4
EXPERTS_PER_GROUP = 8
N_EXPERTS = N_GROUPS * EXPERTS_PER_GROUP
TOP_K = 2
TOKEN_SHIFT = 1
EXPERT_HIDDEN = D_MODEL // 2
RMS_EPS = 1e-6
SUBLN_EPS = 1e-5

LANES = 128
EXPERT_LANE0 = 32
EXP2_UNDERFLOW = -150.0
LOG2_E = math.log2(math.e)

ADA_TN = 1536
QKV_TM = 512
DIFF_T = 512
DIFF_GROUP = 4
SB_T = 256
SB_PAIRS = 4
OUT_TM = 512
MOE_BLK = 512
N_SLOTS = N_TOK * TOP_K + N_EXPERTS * MOE_BLK
N_BLOCKS = N_SLOTS // MOE_BLK
COMB_TM = 512
ROW_WORDS = D_MODEL // 2

SC_CORES = 2
SC_SUBCORES = 16
SC_LANES = 16
SC_CHUNK = 64

VMEM_LIMIT = 56 * 1024 * 1024

F32 = jnp.float32
BF16 = jnp.bfloat16


def _cparams(sem):
    return pltpu.CompilerParams(dimension_semantics=sem, vmem_limit_bytes=VMEM_LIMIT)


def _ada_kernel(c_ref, w_ref, b_ref, o_ref):
    c = c_ref[...]
    cond = c * jax.nn.sigmoid(c)
    o_ref[0] = jnp.dot(cond.astype(BF16), w_ref[0].astype(BF16),
                       preferred_element_type=F32) + b_ref[0]


def _ada(c, ada_w, ada_b):
    six_d = ada_w.shape[-1]
    return pl.pallas_call(
        _ada_kernel,
        grid=(DEPTH, six_d // ADA_TN),
        in_specs=[
            pl.BlockSpec((BATCH, D_MODEL), lambda l, n: (0, 0)),
            pl.BlockSpec((1, D_MODEL, ADA_TN), lambda l, n: (l, 0, n)),
            pl.BlockSpec((1, 1, ADA_TN), lambda l, n: (l, 0, n)),
        ],
        out_specs=pl.BlockSpec((1, BATCH, ADA_TN), lambda l, n: (l, 0, n)),
        out_shape=jax.ShapeDtypeStruct((DEPTH, BATCH, six_d), F32),
        compiler_params=_cparams(("arbitrary", "arbitrary")),
        name="ada",
    )(c, ada_w, ada_b.reshape(DEPTH, 1, six_d))


def _modulated_norm(x, g, scale, shift):
    ms = jnp.mean(x * x, axis=-1, keepdims=True)
    return x * lax.rsqrt(ms + RMS_EPS) * (g * (1.0 + scale)) + shift


def _qkv_kernel(x_ref, mod_ref, g_ref, w_ref, o_ref):
    m = mod_ref[0]
    h = _modulated_norm(x_ref[...], g_ref[...], m[1:2], m[0:1]).astype(BF16)
    for n in range(3):
        cols = slice(n * D_MODEL, (n + 1) * D_MODEL)
        o_ref[:, cols] = jnp.dot(h, w_ref[:, cols], preferred_element_type=F32).astype(BF16)


def _qkv(x, mod, g, w):
    tiles_per_batch = SEQ // QKV_TM
    return pl.pallas_call(
        _qkv_kernel,
        grid=(N_TOK // QKV_TM,),
        in_specs=[
            pl.BlockSpec((QKV_TM, D_MODEL), lambda i: (i, 0)),
            pl.BlockSpec((1, 6, D_MODEL), lambda i: (i // tiles_per_batch, 0, 0)),
            pl.BlockSpec((1, D_MODEL), lambda i: (0, 0)),
            pl.BlockSpec((D_MODEL, 3 * D_MODEL), lambda i: (0, 0)),
        ],
        out_specs=pl.BlockSpec((QKV_TM, 3 * D_MODEL), lambda i: (i, 0)),
        out_shape=jax.ShapeDtypeStruct((N_TOK, 3 * D_MODEL), BF16),
        compiler_params=_cparams(("arbitrary",)),
        name="qkv",
    )(x, mod, g, w)


def _expert_mix(x, wts, gate, y0_words, y1_words):
    y0 = _unpack_bf16_pairs(lax.bitcast_convert_type(y0_words, jnp.uint32))
    y1 = _unpack_bf16_pairs(lax.bitcast_convert_type(y1_words, jnp.uint32))
    return x + gate * (wts[:, 0:1] * y0 + wts[:, 1:2] * y1)


def _mix_qkv_kernel(x1_ref, wts_ref, pmod_ref, y0_ref, y1_ref, mod_ref, g_ref, w_ref,
                    x_ref, o_ref):
    x = _expert_mix(x1_ref[...], wts_ref[...], pmod_ref[0][5:6], y0_ref[...], y1_ref[...])
    x_ref[...] = x
    m = mod_ref[0]
    h = _modulated_norm(x, g_ref[...], m[1:2], m[0:1]).astype(BF16)
    for n in range(3):
        cols = slice(n * D_MODEL, (n + 1) * D_MODEL)
        o_ref[:, cols] = jnp.dot(h, w_ref[:, cols], preferred_element_type=F32).astype(BF16)


def _mix_qkv(x1, wts, prev_mod, y2, mod, g, w):
    nt = N_TOK // QKV_TM
    tiles_per_batch = SEQ // QKV_TM
    return pl.pallas_call(
        _mix_qkv_kernel,
        grid=(nt,),
        in_specs=[
            pl.BlockSpec((QKV_TM, D_MODEL), lambda i: (i, 0)),
            pl.BlockSpec((QKV_TM, TOP_K), lambda i: (i, 0)),
            pl.BlockSpec((1, 6, D_MODEL), lambda i: (i // tiles_per_batch, 0, 0)),
            pl.BlockSpec((QKV_TM, ROW_WORDS), lambda i: (i, 0)),
            pl.BlockSpec((QKV_TM, ROW_WORDS), lambda i: (nt + i, 0)),
            pl.BlockSpec((1, 6, D_MODEL), lambda i: (i // tiles_per_batch, 0, 0)),
            pl.BlockSpec((1, D_MODEL), lambda i: (0, 0)),
            pl.BlockSpec((D_MODEL, 3 * D_MODEL), lambda i: (0, 0)),
        ],
        out_specs=[
            pl.BlockSpec((QKV_TM, D_MODEL), lambda i: (i, 0)),
            pl.BlockSpec((QKV_TM, 3 * D_MODEL), lambda i: (i, 0)),
        ],
        out_shape=[
            jax.ShapeDtypeStruct((N_TOK, D_MODEL), F32),
            jax.ShapeDtypeStruct((N_TOK, 3 * D_MODEL), BF16),
        ],
        compiler_params=_cparams(("arbitrary",)),
        name="mix_qkv",
    )(x1, wts, prev_mod, y2, y2, mod, g, w)


def _half_masked(q, upper):
    lane = lax.broadcasted_iota(jnp.int32, q.shape, 1)
    keep = (lane >= LANES // 2) if upper else (lane < LANES // 2)
    return jnp.where(keep, q, jnp.zeros_like(q))


def _pack_bf16_pairs(x):
    half = x.shape[1] // 2
    hi = lax.bitcast_convert_type(x[:, :half].astype(BF16).astype(F32), jnp.uint32)
    lo = lax.bitcast_convert_type(x[:, half:].astype(BF16).astype(F32), jnp.uint32)
    return hi | (lo >> 16)


def _unpack_bf16_pairs(u):
    hi = lax.bitcast_convert_type(u & jnp.uint32(0xFFFF0000), F32)
    lo = lax.bitcast_convert_type(u << 16, F32)
    return jnp.concatenate([hi, lo], axis=1)


def _lane_tile(x, n):
    return jnp.concatenate([x] * n, axis=1)


def _qk(q, k):
    return lax.dot_general(q, k, (((1,), (1,)), ((), ())), preferred_element_type=F32)


def _diff_attn_kernel(slopes_ref, lam_ref, q_ref, k_ref, v_ref, g_ref, o_ref, *scratch,
                      lambda_init):
    n_chains = 2 * DIFF_GROUP
    m_refs = scratch[:n_chains]
    acc_refs = scratch[n_chains:]
    ones = jnp.ones((DIFF_T, LANES), BF16)

    def head_lanes(c):
        return slice((c // 2) * LANES, (c // 2 + 1) * LANES)

    row = lax.broadcasted_iota(jnp.int32, (DIFF_T, DIFF_T), 0)
    col = lax.broadcasted_iota(jnp.int32, (DIFF_T, DIFF_T), 1)
    allowed = (col // CHUNK) <= (row // CHUNK)
    slopes = [slopes_ref[pl.program_id(1) * DIFF_GROUP + g] * LOG2_E for g in range(DIFF_GROUP)]
    rels = [(row - col).astype(F32) * slope for slope in slopes]
    diag_biases = [jnp.where(allowed, jnp.abs(rel), jnp.inf) for rel in rels]
    out_gain = g_ref[...] * (1.0 - lambda_init)

    def key_block(j, c):
        k = k_ref[pl.ds(j * DIFF_T, DIFF_T), head_lanes(c)]
        v1 = jnp.concatenate([v_ref[pl.ds(j * DIFF_T, DIFF_T), head_lanes(c)], ones], axis=1)
        return k, v1

    def query_tile(qi, _):
        q = q_ref[pl.ds(qi * DIFF_T, DIFF_T), :]
        qm = [_half_masked(q[:, head_lanes(c)], c % 2 == 1) for c in range(n_chains)]

        kv = [key_block(qi, c) for c in range(n_chains)]
        s_maps = [_qk(qm[c], kv[c][0]) - diag_biases[c // 2] for c in range(n_chains)]
        for c in range(n_chains):
            m = jnp.max(s_maps[c], axis=-1, keepdims=True)
            p = jnp.exp2(s_maps[c] - m)
            m_refs[c][...] = jnp.broadcast_to(m, (DIFF_T, LANES))
            acc_refs[c][...] = jnp.dot(p.astype(BF16), kv[c][1], preferred_element_type=F32)

        def body(j, _):
            kv = [key_block(j, c) for c in range(n_chains)]
            ahead = jnp.asarray((qi - j) * DIFF_T).astype(F32)
            s_maps = [_qk(qm[c], kv[c][0]) - rels[c // 2] for c in range(n_chains)]
            for c in range(n_chains):
                shift = slopes[c // 2] * ahead
                m = m_refs[c][...]
                m_new = jnp.maximum(m, jnp.max(s_maps[c], axis=-1, keepdims=True) - shift)
                alpha = jnp.exp2(m - m_new)
                p = jnp.exp2(s_maps[c] - _lane_tile(m_new + shift, DIFF_T // LANES))
                m_refs[c][...] = m_new
                acc_refs[c][...] = _lane_tile(alpha, 2) * acc_refs[c][...] + jnp.dot(
                    p.astype(BF16), kv[c][1], preferred_element_type=F32)
            return 0

        lax.fori_loop(0, qi, body, 0)
        for g in range(DIFF_GROUP):
            a0 = acc_refs[2 * g][...]
            a1 = acc_refs[2 * g + 1][...]
            o = a0[:, :LANES] / a0[:, LANES:] - lam_ref[...] * (a1[:, :LANES] / a1[:, LANES:])
            ms = jnp.mean(o * o, axis=-1, keepdims=True)
            o = o * lax.rsqrt(ms + SUBLN_EPS) * out_gain
            o_ref[pl.ds(qi * DIFF_T, DIFF_T), head_lanes(2 * g)] = o.astype(BF16)
        return 0

    lax.fori_loop(0, SEQ // DIFF_T, query_tile, 0)


def _diff_attn(qkv, slopes, lam, subln_g, lambda_init):
    groups = DIFF_HEADS // DIFF_GROUP
    width = DIFF_GROUP * LANES
    return pl.pallas_call(
        functools.partial(_diff_attn_kernel, lambda_init=lambda_init),
        grid=(BATCH, groups),
        in_specs=[
            pl.BlockSpec(memory_space=pltpu.SMEM),
            pl.BlockSpec((1, 1), lambda b, h: (0, 0)),
            pl.BlockSpec((SEQ, width), lambda b, h: (b, h)),
            pl.BlockSpec((SEQ, width), lambda b, h: (b, groups + h)),
            pl.BlockSpec((SEQ, width), lambda b, h: (b, 2 * groups + h)),
            pl.BlockSpec((1, LANES), lambda b, h: (0, 0)),
        ],
        out_specs=pl.BlockSpec((SEQ, width), lambda b, h: (b, h)),
        out_shape=jax.ShapeDtypeStruct((N_TOK, D_MODEL), BF16),
        scratch_shapes=([pltpu.VMEM((DIFF_T, LANES), F32)] * (2 * DIFF_GROUP)
                        + [pltpu.VMEM((DIFF_T, 2 * LANES), F32)] * (2 * DIFF_GROUP)),
        compiler_params=_cparams(("arbitrary", "arbitrary")),
        name="diff_attn",
    )(slopes, lam, qkv, qkv, qkv, subln_g)


def _sb_attn_kernel(q_ref, k_ref, v_ref, o_ref, *scratch):
    n_heads = 2 * SB_PAIRS
    tail_refs = scratch[:n_heads]
    acc_refs = scratch[n_heads:]

    def pair_lanes(h):
        return slice((h // 2) * LANES, (h // 2 + 1) * LANES)

    row = lax.broadcasted_iota(jnp.int32, (SB_T, SB_T), 0)
    col = lax.broadcasted_iota(jnp.int32, (SB_T, SB_T), 1)
    strict = col < row
    neg_from = jnp.where(row >= col, -1.0, 0.0).astype(BF16)

    def scores(j, h, qhead, mask):
        z = _qk(qhead, k_ref[pl.ds(j * SB_T, SB_T), pair_lanes(h)])
        sp = jnp.maximum(z, 0.0) + jnp.log2(1.0 + jnp.exp2(-jnp.abs(z)))
        if mask is not None:
            sp = jnp.where(mask, sp, 0.0)
        return z, sp

    def weighted(j, h, z, sp, tail, mask):
        log_a = (z + jnp.dot(sp.astype(BF16), neg_from, preferred_element_type=F32)
                 + _lane_tile(tail, SB_T // LANES))
        a = jnp.exp2(log_a)
        if mask is not None:
            a = jnp.where(mask, a, 0.0)
        return jnp.dot(a.astype(BF16), v_ref[pl.ds(j * SB_T, SB_T), pair_lanes(h)],
                       preferred_element_type=F32)

    def row_sum(sp):
        return jnp.broadcast_to(jnp.sum(sp, axis=-1, keepdims=True), (SB_T, LANES))

    zero = jnp.zeros((SB_T, LANES), F32)
    lane = lax.broadcasted_iota(jnp.int32, (SB_T, LANES), 1)

    def query_heads(qi):
        q = q_ref[pl.ds(qi * SB_T, SB_T), :]
        return [_half_masked(q[:, pair_lanes(h)], h % 2 == 1) for h in range(n_heads)]

    def finish(qi):
        for h in range(0, n_heads, 2):
            o_ref[pl.ds(qi * SB_T, SB_T), pair_lanes(h)] = jnp.where(
                lane < LANES // 2, acc_refs[h][...], acc_refs[h + 1][...]).astype(BF16)

    qh = query_heads(0)
    for h in range(n_heads):
        z, sp = scores(0, h, qh[h], strict)
        acc_refs[h][...] = weighted(0, h, z, sp, zero, strict)
    finish(0)

    def query_tile(qi, _):
        qh = query_heads(qi)
        zs = [(scores(qi, h, qh[h], strict), scores(qi - 1, h, qh[h], None))
              for h in range(n_heads)]
        for h in range(n_heads):
            (z0, sp0), (z1, sp1) = zs[h]
            tail0 = -row_sum(sp0)
            acc_refs[h][...] = (weighted(qi, h, z0, sp0, zero, strict)
                                + weighted(qi - 1, h, z1, sp1, tail0, None))
            tail_refs[h][...] = tail0 - row_sum(sp1)

        def cond(state):
            j, live = state
            return (j >= 0) & (functools.reduce(jnp.maximum, live) > EXP2_UNDERFLOW)

        def body(state):
            j, live = state

            def advance(h):
                z, sp = scores(j, h, qh[h], None)
                tail = tail_refs[h][...]
                acc_refs[h][...] += weighted(j, h, z, sp, tail, None)
                tail = tail - row_sum(sp)
                tail_refs[h][...] = tail
                return jnp.max(tail)

            live = tuple(
                lax.cond(live[h] > EXP2_UNDERFLOW, functools.partial(advance, h),
                         functools.partial(lambda done: done, live[h]))
                for h in range(n_heads))
            return j - 1, live

        live = tuple(jnp.max(t[...]) for t in tail_refs)
        lax.while_loop(cond, body, (qi - 2, live))
        finish(qi)
        return 0

    lax.fori_loop(1, SEQ // SB_T, query_tile, 0)


def _sb_attn(qkv):
    groups = SB_HEADS // (2 * SB_PAIRS)
    width = SB_PAIRS * LANES
    return pl.pallas_call(
        _sb_attn_kernel,
        grid=(BATCH, groups),
        in_specs=[
            pl.BlockSpec((SEQ, width), lambda b, h: (b, h)),
            pl.BlockSpec((SEQ, width), lambda b, h: (b, groups + h)),
            pl.BlockSpec((SEQ, width), lambda b, h: (b, 2 * groups + h)),
        ],
        out_specs=pl.BlockSpec((SEQ, width), lambda b, h: (b, h)),
        out_shape=jax.ShapeDtypeStruct((N_TOK, D_MODEL), BF16),
        scratch_shapes=[pltpu.VMEM((SB_T, LANES), F32)] * (4 * SB_PAIRS),
        compiler_params=_cparams(("arbitrary", "arbitrary")),
        name="sb_attn",
    )(qkv, qkv, qkv)


def _out_router_kernel(o_ref, x_ref, mod_ref, g_ref, w_ref, rw_ref, rb_ref,
                       x1_ref, h2_ref, idr_ref, wts_ref, cnt_ref, tri_ref, base_ref):
    i = pl.program_id(0)

    @pl.when(i == 0)
    def _():
        r = lax.broadcasted_iota(jnp.int32, (OUT_TM, OUT_TM), 0)
        c = lax.broadcasted_iota(jnp.int32, (OUT_TM, OUT_TM), 1)
        tri_ref[...] = jnp.where(c < r, 1.0, 0.0).astype(BF16)
        base_ref[...] = jnp.zeros_like(base_ref)

    m = mod_ref[0]
    y = jnp.dot(o_ref[...], w_ref[...], preferred_element_type=F32)
    x1 = x_ref[...] + m[2:3] * y
    x1_ref[...] = x1
    h2 = _modulated_norm(x1, g_ref[...], m[4:5], m[3:4])
    h2_ref[...] = lax.bitcast_convert_type(_pack_bf16_pairs(h2), jnp.int32)

    logits = jnp.dot(h2.astype(BF16), rw_ref[...], preferred_element_type=F32) + rb_ref[...]
    lane = lax.broadcasted_iota(jnp.int32, logits.shape, 1).astype(F32)
    neg_inf = jnp.float32(-jnp.inf)
    big = jnp.float32(1e9)

    is_group = lane < N_GROUPS
    gl = jnp.where(is_group, logits, neg_inf)
    gmax = jnp.max(gl, axis=-1, keepdims=True)
    gidx = jnp.min(jnp.where(gl == gmax, lane, big), axis=-1, keepdims=True)
    gsum = jnp.sum(jnp.where(is_group, jnp.exp(logits - gmax), 0.0), axis=-1, keepdims=True)
    g_w = 1.0 / gsum

    lo = EXPERT_LANE0 + EXPERTS_PER_GROUP * gidx
    in_group = (lane >= lo) & (lane < lo + EXPERTS_PER_GROUP)
    el = jnp.where(in_group, logits, neg_inf)
    v0 = jnp.max(el, axis=-1, keepdims=True)
    i0 = jnp.min(jnp.where(el == v0, lane, big), axis=-1, keepdims=True)
    el = jnp.where(lane == i0, neg_inf, el)
    v1 = jnp.max(el, axis=-1, keepdims=True)
    i1 = jnp.min(jnp.where(el == v1, lane, big), axis=-1, keepdims=True)
    t = jnp.exp(v1 - v0)
    w0 = g_w / (1.0 + t)
    w1 = g_w * t / (1.0 + t)

    oh0 = jnp.where(lane == i0, 1.0, 0.0)
    oh1 = jnp.where(lane == i1, 1.0, 0.0)
    both = oh0 + oh1
    before = jnp.dot(tri_ref[...], both.astype(BF16), preferred_element_type=F32) + base_ref[...]
    r0 = jnp.sum(before * oh0, axis=-1, keepdims=True)
    r1 = jnp.sum(before * oh1, axis=-1, keepdims=True)
    base_ref[...] = base_ref[...] + jnp.sum(both, axis=0, keepdims=True)
    cnt_ref[...] = base_ref[...]

    e0 = i0 - EXPERT_LANE0
    e1 = i1 - EXPERT_LANE0
    idr = jnp.where(lane == 0, e0, jnp.where(lane == 1, e1, jnp.where(lane == 2, r0, r1)))
    idr_ref[...] = idr[:, :4].astype(jnp.int32)
    wts_ref[...] = jnp.where(lane == 0, w0, w1)[:, :2]


def _out_router(o, x, mod, g, w, rw, rb):
    tiles_per_batch = SEQ // OUT_TM
    return pl.pallas_call(
        _out_router_kernel,
        grid=(N_TOK // OUT_TM,),
        in_specs=[
            pl.BlockSpec((OUT_TM, D_MODEL), lambda i: (i, 0)),
            pl.BlockSpec((OUT_TM, D_MODEL), lambda i: (i, 0)),
            pl.BlockSpec((1, 6, D_MODEL), lambda i: (i // tiles_per_batch, 0, 0)),
            pl.BlockSpec((1, D_MODEL), lambda i: (0, 0)),
            pl.BlockSpec((D_MODEL, D_MODEL), lambda i: (0, 0)),
            pl.BlockSpec((D_MODEL, LANES), lambda i: (0, 0)),
            pl.BlockSpec((1, LANES), lambda i: (0, 0)),
        ],
        out_specs=[
            pl.BlockSpec((OUT_TM, D_MODEL), lambda i: (i, 0)),
            pl.BlockSpec((OUT_TM, ROW_WORDS), lambda i: (i, 0)),
            pl.BlockSpec((OUT_TM, 4), lambda i: (i, 0)),
            pl.BlockSpec((OUT_TM, 2), lambda i: (i, 0)),
            pl.BlockSpec((1, LANES), lambda i: (0, 0)),
        ],
        out_shape=[
            jax.ShapeDtypeStruct((N_TOK, D_MODEL), F32),
            jax.ShapeDtypeStruct((N_TOK, ROW_WORDS), jnp.int32),
            jax.ShapeDtypeStruct((N_TOK, 4), jnp.int32),
            jax.ShapeDtypeStruct((N_TOK, 2), F32),
            jax.ShapeDtypeStruct((1, LANES), F32),
        ],
        scratch_shapes=[
            pltpu.VMEM((OUT_TM, OUT_TM), BF16),
            pltpu.VMEM((1, LANES), F32),
        ],
        compiler_params=_cparams(("arbitrary",)),
        name="out_router",
    )(o, x, mod, g, w, rw, rb)


def _expert_kernel(ord_ref, active_ref, meta_ref, xs_ref, wg_hbm, wu_hbm, wd_hbm, ys_ref,
                   wg_f32, wu_f32, wd_f32, sems, wg_bf, wu_bf, wd_bf, *, layer):
    i = pl.program_id(0)
    n_used = meta_ref[0]
    n_active = meta_ref[1]

    def fetch(pos, slot):
        e = active_ref[pos]
        return (
            pltpu.make_async_copy(wg_hbm.at[layer, e], wg_f32.at[slot], sems.at[slot, 0]),
            pltpu.make_async_copy(wu_hbm.at[layer, e], wu_f32.at[slot], sems.at[slot, 1]),
            pltpu.make_async_copy(wd_hbm.at[layer, e], wd_f32.at[slot], sems.at[slot, 2]),
        )

    @pl.when(i == 0)
    def _():
        for copy in fetch(0, 0):
            copy.start()

        @pl.when(n_active > 1)
        def _():
            for copy in fetch(1, 1):
                copy.start()

    @pl.when(i < n_used)
    def _():
        pos = ord_ref[i]
        changed = (i == 0) | (pos != ord_ref[jnp.maximum(i - 1, 0)])
        for slot in range(2):
            @pl.when(changed & (pos % 2 == slot))
            def _():
                for copy in fetch(pos, slot):
                    copy.wait()
                wg_bf[...] = wg_f32[slot].astype(BF16)
                wu_bf[...] = wu_f32[slot].astype(BF16)
                wd_bf[...] = wd_f32[slot].astype(BF16)

                @pl.when(pos + 2 < n_active)
                def _():
                    for copy in fetch(pos + 2, slot):
                        copy.start()

        x = _unpack_bf16_pairs(lax.bitcast_convert_type(xs_ref[...], jnp.uint32)).astype(BF16)
        g = jnp.dot(x, wg_bf[...], preferred_element_type=F32)
        u = jnp.dot(x, wu_bf[...], preferred_element_type=F32)
        hid = (g * jax.nn.sigmoid(g)) * u
        y = jnp.dot(hid.astype(BF16), wd_bf[...], preferred_element_type=F32)
        ys_ref[...] = lax.bitcast_convert_type(_pack_bf16_pairs(y), jnp.int32)

    @pl.when(i >= n_used)
    def _():
        ys_ref[...] = jnp.zeros_like(ys_ref)


def _experts(layer, block_pos, active, meta, xs, wg, wu, wd):
    def in_row_map(i, block_pos, active, meta):
        return (jnp.minimum(i, meta[0] - 1), 0)

    def row_map(i, block_pos, active, meta):
        return (i, 0)

    return pl.pallas_call(
        functools.partial(_expert_kernel, layer=layer),
        grid_spec=pltpu.PrefetchScalarGridSpec(
            num_scalar_prefetch=3,
            grid=(N_BLOCKS,),
            in_specs=[
                pl.BlockSpec((MOE_BLK, ROW_WORDS), in_row_map),
                pl.BlockSpec(memory_space=pl.ANY),
                pl.BlockSpec(memory_space=pl.ANY),
                pl.BlockSpec(memory_space=pl.ANY),
            ],
            out_specs=pl.BlockSpec((MOE_BLK, ROW_WORDS), row_map),
            scratch_shapes=[
                pltpu.VMEM((2, D_MODEL, EXPERT_HIDDEN), F32),
                pltpu.VMEM((2, D_MODEL, EXPERT_HIDDEN), F32),
                pltpu.VMEM((2, EXPERT_HIDDEN, D_MODEL), F32),
                pltpu.SemaphoreType.DMA((2, 3)),
                pltpu.VMEM((D_MODEL, EXPERT_HIDDEN), BF16),
                pltpu.VMEM((D_MODEL, EXPERT_HIDDEN), BF16),
                pltpu.VMEM((EXPERT_HIDDEN, D_MODEL), BF16),
            ],
        ),
        out_shape=jax.ShapeDtypeStruct((N_SLOTS, ROW_WORDS), jnp.int32),
        compiler_params=_cparams(("arbitrary",)),
        name="experts",
    )(block_pos, active, meta, xs, wg, wu, wd)


def _sc_gather_rows(table, idx):
    n_rows = idx.shape[0]
    width = table.shape[1]
    workers = SC_CORES * SC_SUBCORES
    per_worker = n_rows // workers
    assert per_worker * workers == n_rows and per_worker % SC_CHUNK == 0
    mesh = plsc.VectorSubcoreMesh(core_axis_name="c", subcore_axis_name="s")

    n_chunks = per_worker // SC_CHUNK

    def body(table_hbm, idx_hbm, out_hbm, idx_v, buf0, buf1, gsem0, gsem1, wsem0, wsem1):
        wid = lax.axis_index("s") * SC_CORES + lax.axis_index("c")
        base = pl.multiple_of(wid * per_worker, SC_CHUNK)
        bufs, gsems, wsems = (buf0, buf1), (gsem0, gsem1), (wsem0, wsem1)
        pltpu.sync_copy(idx_hbm.at[pl.ds(base, per_worker)], idx_v)

        def gather(c):
            rows = idx_v.at[pl.ds(c * SC_CHUNK, SC_CHUNK)]
            return pltpu.async_copy(table_hbm.at[rows], bufs[c % 2], gsems[c % 2])

        def write(c):
            dst = out_hbm.at[pl.ds(base + c * SC_CHUNK, SC_CHUNK)]
            return pltpu.async_copy(bufs[c % 2], dst, wsems[c % 2])

        gathers = {0: gather(0)}
        writes = {}
        for c in range(n_chunks):
            if c + 1 < n_chunks:
                if c >= 1:
                    writes.pop(c - 1).wait()
                gathers[c + 1] = gather(c + 1)
            gathers.pop(c).wait()
            writes[c] = write(c)
        for c in sorted(writes):
            writes[c].wait()

    return pl.kernel(
        body,
        out_type=jax.ShapeDtypeStruct((n_rows, width), table.dtype),
        mesh=mesh,
        scratch_types=[
            pltpu.VMEM((per_worker,), jnp.int32),
            pltpu.VMEM((SC_CHUNK, width), table.dtype),
            pltpu.VMEM((SC_CHUNK, width), table.dtype),
            pltpu.SemaphoreType.DMA,
            pltpu.SemaphoreType.DMA,
            pltpu.SemaphoreType.DMA,
            pltpu.SemaphoreType.DMA,
        ],
        name="sc_gather_rows",
    )(table, idx)


def _sc_slot_tokens(dest_flat):
    n_assign = N_TOK * TOP_K
    lanes = SC_LANES
    assert TOP_K == 1 << TOKEN_SHIFT and N_TOK & (N_TOK - 1) == 0
    mesh = plsc.VectorSubcoreMesh(core_axis_name="c", subcore_axis_name="s")

    def body(dest_hbm, out_hbm, dest_v, slot_v):
        wid = lax.axis_index("s") * SC_CORES + lax.axis_index("c")

        @pl.when(wid == 0)
        def _():
            pltpu.sync_copy(dest_hbm, dest_v)
            lane = lax.iota(jnp.int32, lanes)

            def init(i, _):
                start = pl.multiple_of(i * lanes, lanes)
                slot_v[pl.ds(start, lanes)] = (lane + start) & (N_TOK - 1)
                return 0

            def place(i, _):
                start = pl.multiple_of(i * lanes, lanes)
                slots = dest_v[pl.ds(start, lanes)]
                plsc.store_scatter(slot_v, [slots], (lane + start) >> TOKEN_SHIFT)
                return 0

            lax.fori_loop(0, N_SLOTS // lanes, init, 0)
            lax.fori_loop(0, n_assign // lanes, place, 0)
            pltpu.sync_copy(slot_v, out_hbm)

    return pl.kernel(
        body,
        out_type=jax.ShapeDtypeStruct((N_SLOTS,), jnp.int32),
        mesh=mesh,
        scratch_types=[
            pltpu.VMEM((n_assign,), jnp.int32),
            pltpu.VMEM((N_SLOTS,), jnp.int32),
        ],
        compiler_params=pltpu.CompilerParams(needs_layout_passes=False),
        name="sc_slot_tokens",
    )(dest_flat)


def _mix_kernel(x_ref, wts_ref, mod_ref, fg_ref, y0_ref, y1_ref, o_ref):
    out = _expert_mix(x_ref[...], wts_ref[...], mod_ref[0][5:6], y0_ref[...], y1_ref[...])
    ms = jnp.mean(out * out, axis=-1, keepdims=True)
    o_ref[...] = out * lax.rsqrt(ms + RMS_EPS) * fg_ref[...]


def _mix(x1, wts, mod, y2, fg):
    nt = N_TOK // COMB_TM
    tiles_per_batch = SEQ // COMB_TM
    return pl.pallas_call(
        _mix_kernel,
        grid=(nt,),
        in_specs=[
            pl.BlockSpec((COMB_TM, D_MODEL), lambda i: (i, 0)),
            pl.BlockSpec((COMB_TM, TOP_K), lambda i: (i, 0)),
            pl.BlockSpec((1, 6, D_MODEL), lambda i: (i // tiles_per_batch, 0, 0)),
            pl.BlockSpec((1, D_MODEL), lambda i: (0, 0)),
            pl.BlockSpec((COMB_TM, ROW_WORDS), lambda i: (i, 0)),
            pl.BlockSpec((COMB_TM, ROW_WORDS), lambda i: (nt + i, 0)),
        ],
        out_specs=pl.BlockSpec((COMB_TM, D_MODEL), lambda i: (i, 0)),
        out_shape=jax.ShapeDtypeStruct((N_TOK, D_MODEL), F32),
        compiler_params=_cparams(("arbitrary",)),
        name="mix",
    )(x1, wts, mod, fg, y2, y2)


def _routing_tables(idr, cnt):
    counts = cnt[0, EXPERT_LANE0:EXPERT_LANE0 + N_EXPERTS].astype(jnp.int32)
    padded = (counts + MOE_BLK - 1) // MOE_BLK * MOE_BLK
    pad_ends = jnp.cumsum(padded)
    pad_starts = pad_ends - padded
    experts = jnp.arange(N_EXPERTS, dtype=jnp.int32)
    is_expert = idr[:, 0:2, None] == experts
    dest = jnp.sum(jnp.where(is_expert, pad_starts, 0), axis=-1) + idr[:, 2:4]
    block_start = jnp.arange(N_BLOCKS, dtype=jnp.int32) * MOE_BLK
    bexp = jnp.minimum(jnp.sum(block_start[:, None] >= pad_ends[None, :], axis=1),
                       N_EXPERTS - 1).astype(jnp.int32)
    owns = padded > 0
    pos_of_expert = jnp.cumsum(owns.astype(jnp.int32)) - 1
    active = jnp.sum(jnp.where(owns[None, :] & (pos_of_expert[None, :] == experts[:, None]),
                               experts[None, :], 0), axis=1).astype(jnp.int32)
    block_pos = jnp.sum(jnp.where(bexp[:, None] == experts[None, :], pos_of_expert[None, :], 0),
                        axis=1).astype(jnp.int32)
    meta = jnp.stack([pad_ends[-1] // MOE_BLK, jnp.sum(owns)]).astype(jnp.int32)
    return dest.astype(jnp.int32), block_pos, active, meta


def _router_weights(w_group, b_group, w_expert, b_expert):
    rw = jnp.zeros((D_MODEL, LANES), F32)
    rw = rw.at[:, :N_GROUPS].set(w_group)
    rw = rw.at[:, EXPERT_LANE0:EXPERT_LANE0 + N_EXPERTS].set(w_expert)
    rb = jnp.zeros((1, LANES), F32)
    rb = rb.at[0, :N_GROUPS].set(b_group)
    rb = rb.at[0, EXPERT_LANE0:EXPERT_LANE0 + N_EXPERTS].set(b_expert.reshape(-1))
    return rw.astype(BF16), rb


def _scaled_qkv_weight(w_in, head_dim):
    scale = jnp.concatenate([jnp.full((D_MODEL,), LOG2_E / math.sqrt(head_dim), F32),
                             jnp.ones((2 * D_MODEL,), F32)])
    return (w_in * scale).astype(BF16)


def kernel(x, c, norm1_g, norm2_g, ada_w, ada_b, diff_w_in, diff_w_out, diff_lambda_q1, diff_lambda_k1, diff_lambda_q2, diff_lambda_k2, diff_subln_g, sb_w_in, sb_w_out, router_group_w, router_group_b, router_expert_w, router_expert_b, expert_w_gate, expert_w_up, expert_w_down, final_norm_g):
    xf = x.reshape(N_TOK, D_MODEL)
    mod_all = _ada(c, ada_w, ada_b)
    slopes = jnp.exp2(-8.0 * jnp.arange(1, DIFF_HEADS + 1, dtype=F32) / DIFF_HEADS)

    pending = None
    for i in range(DEPTH):
        mod = mod_all[i].reshape(BATCH, 6, D_MODEL)
        j = i // 2
        is_diff = i % 2 == 0
        w_in = (_scaled_qkv_weight(diff_w_in[j], DIFF_HEAD_DIM) if is_diff
                else _scaled_qkv_weight(sb_w_in[j], SB_HEAD_DIM))
        g1 = norm1_g[i].reshape(1, D_MODEL)
        if pending is None:
            qkv = _qkv(xf, mod, g1, w_in)
        else:
            xf, qkv = _mix_qkv(*pending, mod, g1, w_in)
        if is_diff:
            lambda_init = 0.8 - 0.6 * math.exp(-0.3 * i)
            lam = (jnp.exp(jnp.sum(diff_lambda_q1[j] * diff_lambda_k1[j]))
                   - jnp.exp(jnp.sum(diff_lambda_q2[j] * diff_lambda_k2[j]))
                   + lambda_init).reshape(1, 1)
            o = _diff_attn(qkv, slopes, lam, diff_subln_g[j].reshape(1, LANES), lambda_init)
            w_out = diff_w_out[j]
        else:
            o = _sb_attn(qkv)
            w_out = sb_w_out[j]
        rw, rb = _router_weights(router_group_w[i], router_group_b[i],
                                 router_expert_w[i], router_expert_b[i])
        x1, h2, idr, wts, cnt = _out_router(o, xf, mod, norm2_g[i].reshape(1, D_MODEL),
                                            w_out.astype(BF16), rw, rb)
        dest, block_pos, active, meta = _routing_tables(idr, cnt)
        xs = _sc_gather_rows(h2, _sc_slot_tokens(dest.reshape(-1)))
        ys = _experts(i, block_pos, active, meta, xs,
                      expert_w_gate, expert_w_up, expert_w_down)
        y2 = _sc_gather_rows(ys, dest.T.reshape(-1))
        pending = (x1, wts, mod, y2)
    out = _mix(*pending, final_norm_g.reshape(1, D_MODEL))
    return out.reshape(BATCH, SEQ, D_MODEL)
```

```python
import functools
import math

import jax
import jax.numpy as jnp
from jax import lax
from jax.experimental import pallas as pl
from jax.experimental.pallas import tpu as pltpu
from jax.experimental.pallas import tpu_sc as plsc

D_MODEL = 1024
BATCH = 8
SEQ = 2048
DEPTH = 2
N_TOK = BATCH * SEQ

CHUNK = 64
DIFF_HEADS = 8
DIFF_HEAD_DIM = D_MODEL // (2 * DIFF_HEADS)
SB_HEADS = 16
SB_HEAD_DIM = D_MODEL // SB_HEADS
N_GROUPS = 4
EXPERTS_PER_GROUP = 8
N_EXPERTS = N_GROUPS * EXPERTS_PER_GROUP
TOP_K = 2
EXPERT_HIDDEN = D_MODEL // 2
RMS_EPS = 1e-6
SUBLN_EPS = 1e-5

LANES = 128
EXPERT_LANE0 = 32
ROUTE_ROWS = 8
EXP2_UNDERFLOW = -150.0
LOG2_E = math.log2(math.e)

ADA_TN = 1536
QKV_TM = 512
DIFF_T = 512
DIFF_GROUP = 4
SB_T = 256
SB_PAIRS = 4
OUT_TM = 512
MOE_BLK = 512
N_SLOTS = N_TOK * TOP_K + N_EXPERTS * MOE_BLK
N_BLOCKS = N_SLOTS // MOE_BLK
COMB_TM = 512
ROW_WORDS = D_MODEL // 2

SC_CORES = 2
SC_SUBCORES = 16
SC_LANES = 16
SC_CHUNK = 64

VMEM_LIMIT = 56 * 1024 * 1024

F32 = jnp.float32
BF16 = jnp.bfloat16


def _cparams(sem):
    return pltpu.CompilerParams(dimension_semantics=sem, vmem_limit_bytes=VMEM_LIMIT)


def _ada_kernel(c_ref, w_ref, b_ref, o_ref):
    c = c_ref[...]
    cond = c * jax.nn.sigmoid(c)
    o_ref[0] = jnp.dot(cond.astype(BF16), w_ref[0].astype(BF16),
                       preferred_element_type=F32) + b_ref[0]


def _ada(c, ada_w, ada_b):
    six_d = ada_w.shape[-1]
    return pl.pallas_call(
        _ada_kernel,
        grid=(DEPTH, six_d // ADA_TN),
        in_specs=[
            pl.BlockSpec((BATCH, D_MODEL), lambda l, n: (0, 0)),
            pl.BlockSpec((1, D_MODEL, ADA_TN), lambda l, n: (l, 0, n)),
            pl.BlockSpec((1, 1, ADA_TN), lambda l, n: (l, 0, n)),
        ],
        out_specs=pl.BlockSpec((1, BATCH, ADA_TN), lambda l, n: (l, 0, n)),
        out_shape=jax.ShapeDtypeStruct((DEPTH, BATCH, six_d), F32),
        compiler_params=_cparams(("arbitrary", "arbitrary")),
        name="ada",
    )(c, ada_w, ada_b.reshape(DEPTH, 1, six_d))


def _modulated_norm(x, g, scale, shift):
    ms = jnp.mean(x * x, axis=-1, keepdims=True)
    return x * lax.rsqrt(ms + RMS_EPS) * (g * (1.0 + scale)) + shift


def _qkv_kernel(x_ref, mod_ref, g_ref, w_ref, o_ref):
    m = mod_ref[0]
    h = _modulated_norm(x_ref[...], g_ref[...], m[1:2], m[0:1]).astype(BF16)
    for n in range(3):
        cols = slice(n * D_MODEL, (n + 1) * D_MODEL)
        o_ref[:, cols] = jnp.dot(h, w_ref[:, cols], preferred_element_type=F32).astype(BF16)


def _qkv(x, mod, g, w):
    tiles_per_batch = SEQ // QKV_TM
    return pl.pallas_call(
        _qkv_kernel,
        grid=(N_TOK // QKV_TM,),
        in_specs=[
            pl.BlockSpec((QKV_TM, D_MODEL), lambda i: (i, 0)),
            pl.BlockSpec((1, 6, D_MODEL), lambda i: (i // tiles_per_batch, 0, 0)),
            pl.BlockSpec((1, D_MODEL), lambda i: (0, 0)),
            pl.BlockSpec((D_MODEL, 3 * D_MODEL), lambda i: (0, 0)),
        ],
        out_specs=pl.BlockSpec((QKV_TM, 3 * D_MODEL), lambda i: (i, 0)),
        out_shape=jax.ShapeDtypeStruct((N_TOK, 3 * D_MODEL), BF16),
        compiler_params=_cparams(("arbitrary",)),
        name="qkv",
    )(x, mod, g, w)


def _expert_mix(x, wts, gate, y0_words, y1_words):
    y0 = _unpack_bf16_pairs(lax.bitcast_convert_type(y0_words, jnp.uint32))
    y1 = _unpack_bf16_pairs(lax.bitcast_convert_type(y1_words, jnp.uint32))
    return x + gate * (wts[:, 0:1] * y0 + wts[:, 1:2] * y1)


def _mix_qkv_kernel(x1_ref, wts_ref, pmod_ref, y0_ref, y1_ref, mod_ref, g_ref, w_ref,
                    x_ref, o_ref):
    x = _expert_mix(x1_ref[...], wts_ref[...], pmod_ref[0][5:6], y0_ref[...], y1_ref[...])
    x_ref[...] = x
    m = mod_ref[0]
    h = _modulated_norm(x, g_ref[...], m[1:2], m[0:1]).astype(BF16)
    for n in range(3):
        cols = slice(n * D_MODEL, (n + 1) * D_MODEL)
        o_ref[:, cols] = jnp.dot(h, w_ref[:, cols], preferred_element_type=F32).astype(BF16)


def _mix_qkv(x1, wts, prev_mod, y2, mod, g, w):
    nt = N_TOK // QKV_TM
    tiles_per_batch = SEQ // QKV_TM
    return pl.pallas_call(
        _mix_qkv_kernel,
        grid=(nt,),
        in_specs=[
            pl.BlockSpec((QKV_TM, D_MODEL), lambda i: (i, 0)),
            pl.BlockSpec((QKV_TM, TOP_K), lambda i: (i, 0)),
            pl.BlockSpec((1, 6, D_MODEL), lambda i: (i // tiles_per_batch, 0, 0)),
            pl.BlockSpec((QKV_TM, ROW_WORDS), lambda i: (i, 0)),
            pl.BlockSpec((QKV_TM, ROW_WORDS), lambda i: (nt + i, 0)),
            pl.BlockSpec((1, 6, D_MODEL), lambda i: (i // tiles_per_batch, 0, 0)),
            pl.BlockSpec((1, D_MODEL), lambda i: (0, 0)),
            pl.BlockSpec((D_MODEL, 3 * D_MODEL), lambda i: (0, 0)),
        ],
        out_specs=[
            pl.BlockSpec((QKV_TM, D_MODEL), lambda i: (i, 0)),
            pl.BlockSpec((QKV_TM, 3 * D_MODEL), lambda i: (i, 0)),
        ],
        out_shape=[
            jax.ShapeDtypeStruct((N_TOK, D_MODEL), F32),
            jax.ShapeDtypeStruct((N_TOK, 3 * D_MODEL), BF16),
        ],
        compiler_params=_cparams(("arbitrary",)),
        name="mix_qkv",
    )(x1, wts, prev_mod, y2, y2, mod, g, w)


def _half_masked(q, upper):
    lane = lax.broadcasted_iota(jnp.int32, q.shape, 1)
    keep = (lane >= LANES // 2) if upper else (lane < LANES // 2)
    return jnp.where(keep, q, jnp.zeros_like(q))


def _pack_bf16_pairs(x):
    half = x.shape[1] // 2
    hi = lax.bitcast_convert_type(x[:, :half].astype(BF16).astype(F32), jnp.uint32)
    lo = lax.bitcast_convert_type(x[:, half:].astype(BF16).astype(F32), jnp.uint32)
    return hi | (lo >> 16)


def _unpack_bf16_pairs(u):
    hi = lax.bitcast_convert_type(u & jnp.uint32(0xFFFF0000), F32)
    lo = lax.bitcast_convert_type(u << 16, F32)
    return jnp.concatenate([hi, lo], axis=1)


def _lane_tile(x, n):
    return jnp.concatenate([x] * n, axis=1)


def _qk(q, k):
    return lax.dot_general(q, k, (((1,), (1,)), ((), ())), preferred_element_type=F32)


def _diff_attn_kernel(slopes_ref, lam_ref, q_ref, k_ref, v_ref, g_ref, o_ref, *scratch,
                      lambda_init):
    n_chains = 2 * DIFF_GROUP
    m_refs = scratch[:n_chains]
    acc_refs = scratch[n_chains:]
    ones = jnp.ones((DIFF_T, LANES), BF16)

    def head_lanes(c):
        return slice((c // 2) * LANES, (c // 2 + 1) * LANES)

    row = lax.broadcasted_iota(jnp.int32, (DIFF_T, DIFF_T), 0)
    col = lax.broadcasted_iota(jnp.int32, (DIFF_T, DIFF_T), 1)
    allowed = (col // CHUNK) <= (row // CHUNK)
    slopes = [slopes_ref[pl.program_id(1) * DIFF_GROUP + g] * LOG2_E for g in range(DIFF_GROUP)]
    rels = [(row - col).astype(F32) * slope for slope in slopes]
    diag_biases = [jnp.where(allowed, jnp.abs(rel), jnp.inf) for rel in rels]
    out_gain = g_ref[...] * (1.0 - lambda_init)

    def key_block(j, c):
        k = k_ref[pl.ds(j * DIFF_T, DIFF_T), head_lanes(c)]
        v1 = jnp.concatenate([v_ref[pl.ds(j * DIFF_T, DIFF_T), head_lanes(c)], ones], axis=1)
        return k, v1

    def query_tile(qi, _):
        q = q_ref[pl.ds(qi * DIFF_T, DIFF_T), :]
        qm = [_half_masked(q[:, head_lanes(c)], c % 2 == 1) for c in range(n_chains)]

        kv = [key_block(qi, c) for c in range(n_chains)]
        s_maps = [_qk(qm[c], kv[c][0]) - diag_biases[c // 2] for c in range(n_chains)]
        for c in range(n_chains):
            m = jnp.max(s_maps[c], axis=-1, keepdims=True)
            p = jnp.exp2(s_maps[c] - m)
            m_refs[c][...] = jnp.broadcast_to(m, (DIFF_T, LANES))
            acc_refs[c][...] = jnp.dot(p.astype(BF16), kv[c][1], preferred_element_type=F32)

        def body(j, _):
            kv = [key_block(j, c) for c in range(n_chains)]
            ahead = jnp.asarray((qi - j) * DIFF_T).astype(F32)
            s_maps = [_qk(qm[c], kv[c][0]) - rels[c // 2] for c in range(n_chains)]
            for c in range(n_chains):
                shift = slopes[c // 2] * ahead
                m = m_refs[c][...]
                m_new = jnp.maximum(m, jnp.max(s_maps[c], axis=-1, keepdims=True) - shift)
                alpha = jnp.exp2(m - m_new)
                p = jnp.exp2(s_maps[c] - _lane_tile(m_new + shift, DIFF_T // LANES))
                m_refs[c][...] = m_new
                acc_refs[c][...] = _lane_tile(alpha, 2) * acc_refs[c][...] + jnp.dot(
                    p.astype(BF16), kv[c][1], preferred_element_type=F32)
            return 0

        lax.fori_loop(0, qi, body, 0)
        for g in range(DIFF_GROUP):
            a0 = acc_refs[2 * g][...]
            a1 = acc_refs[2 * g + 1][...]
            o = a0[:, :LANES] / a0[:, LANES:] - lam_ref[...] * (a1[:, :LANES] / a1[:, LANES:])
            ms = jnp.mean(o * o, axis=-1, keepdims=True)
            o = o * lax.rsqrt(ms + SUBLN_EPS) * out_gain
            o_ref[pl.ds(qi * DIFF_T, DIFF_T), head_lanes(2 * g)] = o.astype(BF16)
        return 0

    lax.fori_loop(0, SEQ // DIFF_T, query_tile, 0)


def _diff_attn(qkv, slopes, lam, subln_g, lambda_init):
    groups = DIFF_HEADS // DIFF_GROUP
    width = DIFF_GROUP * LANES
    return pl.pallas_call(
        functools.partial(_diff_attn_kernel, lambda_init=lambda_init),
        grid=(BATCH, groups),
        in_specs=[
            pl.BlockSpec(memory_space=pltpu.SMEM),
            pl.BlockSpec((1, 1), lambda b, h: (0, 0)),
            pl.BlockSpec((SEQ, width), lambda b, h: (b, h)),
            pl.BlockSpec((SEQ, width), lambda b, h: (b, groups + h)),
            pl.BlockSpec((SEQ, width), lambda b, h: (b, 2 * groups + h)),
            pl.BlockSpec((1, LANES), lambda b, h: (0, 0)),
        ],
        out_specs=pl.BlockSpec((SEQ, width), lambda b, h: (b, h)),
        out_shape=jax.ShapeDtypeStruct((N_TOK, D_MODEL), BF16),
        scratch_shapes=([pltpu.VMEM((DIFF_T, LANES), F32)] * (2 * DIFF_GROUP)
                        + [pltpu.VMEM((DIFF_T, 2 * LANES), F32)] * (2 * DIFF_GROUP)),
        compiler_params=_cparams(("arbitrary", "arbitrary")),
        name="diff_attn",
    )(slopes, lam, qkv, qkv, qkv, subln_g)


def _sb_attn_kernel(q_ref, k_ref, v_ref, o_ref, *scratch):
    n_heads = 2 * SB_PAIRS
    tail_refs = scratch[:n_heads]
    acc_refs = scratch[n_heads:]

    def pair_lanes(h):
        return slice((h // 2) * LANES, (h // 2 + 1) * LANES)

    row = lax.broadcasted_iota(jnp.int32, (SB_T, SB_T), 0)
    col = lax.broadcasted_iota(jnp.int32, (SB_T, SB_T), 1)
    strict = col < row
    neg_from = jnp.where(row >= col, -1.0, 0.0).astype(BF16)

    def scores(j, h, qhead, mask):
        z = _qk(qhead, k_ref[pl.ds(j * SB_T, SB_T), pair_lanes(h)])
        sp = jnp.maximum(z, 0.0) + jnp.log2(1.0 + jnp.exp2(-jnp.abs(z)))
        if mask is not None:
            sp = jnp.where(mask, sp, 0.0)
        return z, sp

    def weighted(j, h, z, sp, tail, mask):
        log_a = (z + jnp.dot(sp.astype(BF16), neg_from, preferred_element_type=F32)
                 + _lane_tile(tail, SB_T // LANES))
        a = jnp.exp2(log_a)
        if mask is not None:
            a = jnp.where(mask, a, 0.0)
        return jnp.dot(a.astype(BF16), v_ref[pl.ds(j * SB_T, SB_T), pair_lanes(h)],
                       preferred_element_type=F32)

    def row_sum(sp):
        return jnp.broadcast_to(jnp.sum(sp, axis=-1, keepdims=True), (SB_T, LANES))

    zero = jnp.zeros((SB_T, LANES), F32)
    lane = lax.broadcasted_iota(jnp.int32, (SB_T, LANES), 1)

    def query_heads(qi):
        q = q_ref[pl.ds(qi * SB_T, SB_T), :]
        return [_half_masked(q[:, pair_lanes(h)], h % 2 == 1) for h in range(n_heads)]

    def finish(qi):
        for h in range(0, n_heads, 2):
            o_ref[pl.ds(qi * SB_T, SB_T), pair_lanes(h)] = jnp.where(
                lane < LANES // 2, acc_refs[h][...], acc_refs[h + 1][...]).astype(BF16)

    qh = query_heads(0)
    for h in range(n_heads):
        z, sp = scores(0, h, qh[h], strict)
        acc_refs[h][...] = weighted(0, h, z, sp, zero, strict)
    finish(0)

    def query_tile(qi, _):
        qh = query_heads(qi)
        zs = [(scores(qi, h, qh[h], strict), scores(qi - 1, h, qh[h], None))
              for h in range(n_heads)]
        for h in range(n_heads):
            (z0, sp0), (z1, sp1) = zs[h]
            tail0 = -row_sum(sp0)
            acc_refs[h][...] = (weighted(qi, h, z0, sp0, zero, strict)
                                + weighted(qi - 1, h, z1, sp1, tail0, None))
            tail_refs[h][...] = tail0 - row_sum(sp1)

        def cond(state):
            j, live = state
            return (j >= 0) & (functools.reduce(jnp.maximum, live) > EXP2_UNDERFLOW)

        def body(state):
            j, live = state

            def advance(h):
                z, sp = scores(j, h, qh[h], None)
                tail = tail_refs[h][...]
                acc_refs[h][...] += weighted(j, h, z, sp, tail, None)
                tail = tail - row_sum(sp)
                tail_refs[h][...] = tail
                return jnp.max(tail)

            live = tuple(
                lax.cond(live[h] > EXP2_UNDERFLOW, functools.partial(advance, h),
                         functools.partial(lambda done: done, live[h]))
                for h in range(n_heads))
            return j - 1, live

        live = tuple(jnp.max(t[...]) for t in tail_refs)
        lax.while_loop(cond, body, (qi - 2, live))
        finish(qi)
        return 0

    lax.fori_loop(1, SEQ // SB_T, query_tile, 0)


def _sb_attn(qkv):
    groups = SB_HEADS // (2 * SB_PAIRS)
    width = SB_PAIRS * LANES
    return pl.pallas_call(
        _sb_attn_kernel,
        grid=(BATCH, groups),
        in_specs=[
            pl.BlockSpec((SEQ, width), lambda b, h: (b, h)),
            pl.BlockSpec((SEQ, width), lambda b, h: (b, groups + h)),
            pl.BlockSpec((SEQ, width), lambda b, h: (b, 2 * groups + h)),
        ],
        out_specs=pl.BlockSpec((SEQ, width), lambda b, h: (b, h)),
        out_shape=jax.ShapeDtypeStruct((N_TOK, D_MODEL), BF16),
        scratch_shapes=[pltpu.VMEM((SB_T, LANES), F32)] * (4 * SB_PAIRS),
        compiler_params=_cparams(("arbitrary", "arbitrary")),
        name="sb_attn",
    )(qkv, qkv, qkv)


def _out_router_kernel(o_ref, x_ref, mod_ref, g_ref, w_ref, rw_ref, rb_ref,
                       x1_ref, h2_ref, idr_ref, wts_ref, cnt_ref, tri_ref, base_ref):
    i = pl.program_id(0)

    @pl.when(i == 0)
    def _():
        r = lax.broadcasted_iota(jnp.int32, (OUT_TM, OUT_TM), 0)
        c = lax.broadcasted_iota(jnp.int32, (OUT_TM, OUT_TM), 1)
        tri_ref[...] = jnp.where(c < r, 1.0, 0.0).astype(BF16)
        base_ref[...] = jnp.zeros_like(base_ref)

    m = mod_ref[0]
    y = jnp.dot(o_ref[...], w_ref[...], preferred_element_type=F32)
    x1 = x_ref[...] + m[2:3] * y
    x1_ref[...] = x1
    h2 = _modulated_norm(x1, g_ref[...], m[4:5], m[3:4])
    h2_ref[...] = lax.bitcast_convert_type(_pack_bf16_pairs(h2), jnp.int32)

    logits = jnp.dot(h2.astype(BF16), rw_ref[...], preferred_element_type=F32) + rb_ref[...]
    lane = lax.broadcasted_iota(jnp.int32, logits.shape, 1).astype(F32)
    neg_inf = jnp.float32(-jnp.inf)
    big = jnp.float32(1e9)

    is_group = lane < N_GROUPS
    gl = jnp.where(is_group, logits, neg_inf)
    gmax = jnp.max(gl, axis=-1, keepdims=True)
    gidx = jnp.min(jnp.where(gl == gmax, lane, big), axis=-1, keepdims=True)
    gsum = jnp.sum(jnp.where(is_group, jnp.exp(logits - gmax), 0.0), axis=-1, keepdims=True)
    g_w = 1.0 / gsum

    lo = EXPERT_LANE0 + EXPERTS_PER_GROUP * gidx
    in_group = (lane >= lo) & (lane < lo + EXPERTS_PER_GROUP)
    el = jnp.where(in_group, logits, neg_inf)
    v0 = jnp.max(el, axis=-1, keepdims=True)
    i0 = jnp.min(jnp.where(el == v0, lane, big), axis=-1, keepdims=True)
    el = jnp.where(lane == i0, neg_inf, el)
    v1 = jnp.max(el, axis=-1, keepdims=True)
    i1 = jnp.min(jnp.where(el == v1, lane, big), axis=-1, keepdims=True)
    t = jnp.exp(v1 - v0)
    w0 = g_w / (1.0 + t)
    w1 = g_w * t / (1.0 + t)

    oh0 = jnp.where(lane == i0, 1.0, 0.0)
    oh1 = jnp.where(lane == i1, 1.0, 0.0)
    both = oh0 + oh1
    before = jnp.dot(tri_ref[...], both.astype(BF16), preferred_element_type=F32) + base_ref[...]
    r0 = jnp.sum(before * oh0, axis=-1, keepdims=True)
    r1 = jnp.sum(before * oh1, axis=-1, keepdims=True)
    base_ref[...] = base_ref[...] + jnp.sum(both, axis=0, keepdims=True)
    cnt_ref[...] = base_ref[...]

    e0 = i0 - EXPERT_LANE0
    e1 = i1 - EXPERT_LANE0
    idr = jnp.where(lane == 0, e0, jnp.where(lane == 1, e1, jnp.where(lane == 2, r0, r1)))
    idr_ref[...] = jnp.transpose(idr)[:ROUTE_ROWS].astype(jnp.int32)
    wts_ref[...] = jnp.where(lane == 0, w0, w1)[:, :2]


def _out_router(o, x, mod, g, w, rw, rb):
    tiles_per_batch = SEQ // OUT_TM
    return pl.pallas_call(
        _out_router_kernel,
        grid=(N_TOK // OUT_TM,),
        in_specs=[
            pl.BlockSpec((OUT_TM, D_MODEL), lambda i: (i, 0)),
            pl.BlockSpec((OUT_TM, D_MODEL), lambda i: (i, 0)),
            pl.BlockSpec((1, 6, D_MODEL), lambda i: (i // tiles_per_batch, 0, 0)),
            pl.BlockSpec((1, D_MODEL), lambda i: (0, 0)),
            pl.BlockSpec((D_MODEL, D_MODEL), lambda i: (0, 0)),
            pl.BlockSpec((D_MODEL, LANES), lambda i: (0, 0)),
            pl.BlockSpec((1, LANES), lambda i: (0, 0)),
        ],
        out_specs=[
            pl.BlockSpec((OUT_TM, D_MODEL), lambda i: (i, 0)),
            pl.BlockSpec((OUT_TM, ROW_WORDS), lambda i: (i, 0)),
            pl.BlockSpec((ROUTE_ROWS, OUT_TM), lambda i: (0, i)),
            pl.BlockSpec((OUT_TM, 2), lambda i: (i, 0)),
            pl.BlockSpec((1, LANES), lambda i: (0, 0)),
        ],
        out_shape=[
            jax.ShapeDtypeStruct((N_TOK, D_MODEL), F32),
            jax.ShapeDtypeStruct((N_TOK, ROW_WORDS), jnp.int32),
            jax.ShapeDtypeStruct((ROUTE_ROWS, N_TOK), jnp.int32),
            jax.ShapeDtypeStruct((N_TOK, 2), F32),
            jax.ShapeDtypeStruct((1, LANES), F32),
        ],
        scratch_shapes=[
            pltpu.VMEM((OUT_TM, OUT_TM), BF16),
            pltpu.VMEM((1, LANES), F32),
        ],
        compiler_params=_cparams(("arbitrary",)),
        name="out_router",
    )(o, x, mod, g, w, rw, rb)


def _expert_kernel(ord_ref, active_ref, meta_ref, xs_ref, wg_hbm, wu_hbm, wd_hbm, ys_ref,
                   wg_f32, wu_f32, wd_f32, sems, wg_bf, wu_bf, wd_bf, *, layer):
    i = pl.program_id(0)
    n_used = meta_ref[0]
    n_active = meta_ref[1]

    def fetch(pos, slot):
        e = active_ref[pos]
        return (
            pltpu.make_async_copy(wg_hbm.at[layer, e], wg_f32.at[slot], sems.at[slot, 0]),
            pltpu.make_async_copy(wu_hbm.at[layer, e], wu_f32.at[slot], sems.at[slot, 1]),
            pltpu.make_async_copy(wd_hbm.at[layer, e], wd_f32.at[slot], sems.at[slot, 2]),
        )

    @pl.when(i == 0)
    def _():
        for copy in fetch(0, 0):
            copy.start()

        @pl.when(n_active > 1)
        def _():
            for copy in fetch(1, 1):
                copy.start()

    @pl.when(i < n_used)
    def _():
        pos = ord_ref[i]
        changed = (i == 0) | (pos != ord_ref[jnp.maximum(i - 1, 0)])
        for slot in range(2):
            @pl.when(changed & (pos % 2 == slot))
            def _():
                for copy in fetch(pos, slot):
                    copy.wait()
                wg_bf[...] = wg_f32[slot].astype(BF16)
                wu_bf[...] = wu_f32[slot].astype(BF16)
                wd_bf[...] = wd_f32[slot].astype(BF16)

                @pl.when(pos + 2 < n_active)
                def _():
                    for copy in fetch(pos + 2, slot):
                        copy.start()

        x = _unpack_bf16_pairs(lax.bitcast_convert_type(xs_ref[...], jnp.uint32)).astype(BF16)
        g = jnp.dot(x, wg_bf[...], preferred_element_type=F32)
        u = jnp.dot(x, wu_bf[...], preferred_element_type=F32)
        hid = (g * jax.nn.sigmoid(g)) * u
        y = jnp.dot(hid.astype(BF16), wd_bf[...], preferred_element_type=F32)
        ys_ref[...] = lax.bitcast_convert_type(_pack_bf16_pairs(y), jnp.int32)

    @pl.when(i >= n_used)
    def _():
        ys_ref[...] = jnp.zeros_like(ys_ref)


def _experts(layer, block_pos, active, meta, xs, wg, wu, wd):
    def in_row_map(i, block_pos, active, meta):
        return (jnp.minimum(i, meta[0] - 1), 0)

    def row_map(i, block_pos, active, meta):
        return (i, 0)

    return pl.pallas_call(
        functools.partial(_expert_kernel, layer=layer),
        grid_spec=pltpu.PrefetchScalarGridSpec(
            num_scalar_prefetch=3,
            grid=(N_BLOCKS,),
            in_specs=[
                pl.BlockSpec((MOE_BLK, ROW_WORDS), in_row_map),
                pl.BlockSpec(memory_space=pl.ANY),
                pl.BlockSpec(memory_space=pl.ANY),
                pl.BlockSpec(memory_space=pl.ANY),
            ],
            out_specs=pl.BlockSpec((MOE_BLK, ROW_WORDS), row_map),
            scratch_shapes=[
                pltpu.VMEM((2, D_MODEL, EXPERT_HIDDEN), F32),
                pltpu.VMEM((2, D_MODEL, EXPERT_HIDDEN), F32),
                pltpu.VMEM((2, EXPERT_HIDDEN, D_MODEL), F32),
                pltpu.SemaphoreType.DMA((2, 3)),
                pltpu.VMEM((D_MODEL, EXPERT_HIDDEN), BF16),
                pltpu.VMEM((D_MODEL, EXPERT_HIDDEN), BF16),
                pltpu.VMEM((EXPERT_HIDDEN, D_MODEL), BF16),
            ],
        ),
        out_shape=jax.ShapeDtypeStruct((N_SLOTS, ROW_WORDS), jnp.int32),
        compiler_params=_cparams(("arbitrary",)),
        name="experts",
    )(block_pos, active, meta, xs, wg, wu, wd)


def _sc_gather_rows(table, idx):
    n_rows = idx.shape[0]
    width = table.shape[1]
    workers = SC_CORES * SC_SUBCORES
    per_worker = n_rows // workers
    assert per_worker * workers == n_rows and per_worker % SC_CHUNK == 0
    mesh = plsc.VectorSubcoreMesh(core_axis_name="c", subcore_axis_name="s")

    n_chunks = per_worker // SC_CHUNK

    def body(table_hbm, idx_hbm, out_hbm, idx_v, buf0, buf1, gsem0, gsem1, wsem0, wsem1):
        wid = lax.axis_index("s") * SC_CORES + lax.axis_index("c")
        base = pl.multiple_of(wid * per_worker, SC_CHUNK)
        bufs, gsems, wsems = (buf0, buf1), (gsem0, gsem1), (wsem0, wsem1)
        pltpu.sync_copy(idx_hbm.at[pl.ds(base, per_worker)], idx_v)

        def gather(c):
            rows = idx_v.at[pl.ds(c * SC_CHUNK, SC_CHUNK)]
            return pltpu.async_copy(table_hbm.at[rows], bufs[c % 2], gsems[c % 2])

        def write(c):
            dst = out_hbm.at[pl.ds(base + c * SC_CHUNK, SC_CHUNK)]
            return pltpu.async_copy(bufs[c % 2], dst, wsems[c % 2])

        gathers = {0: gather(0)}
        writes = {}
        for c in range(n_chunks):
            if c + 1 < n_chunks:
                if c >= 1:
                    writes.pop(c - 1).wait()
                gathers[c + 1] = gather(c + 1)
            gathers.pop(c).wait()
            writes[c] = write(c)
        for c in sorted(writes):
            writes[c].wait()

    return pl.kernel(
        body,
        out_type=jax.ShapeDtypeStruct((n_rows, width), table.dtype),
        mesh=mesh,
        scratch_types=[
            pltpu.VMEM((per_worker,), jnp.int32),
            pltpu.VMEM((SC_CHUNK, width), table.dtype),
            pltpu.VMEM((SC_CHUNK, width), table.dtype),
            pltpu.SemaphoreType.DMA,
            pltpu.SemaphoreType.DMA,
            pltpu.SemaphoreType.DMA,
            pltpu.SemaphoreType.DMA,
        ],
        name="sc_gather_rows",
    )(table, idx)


def _sc_slot_tokens(dest_flat):
    n_assign = N_TOK * TOP_K
    lanes = SC_LANES
    assert N_TOK & (N_TOK - 1) == 0
    mesh = plsc.VectorSubcoreMesh(core_axis_name="c", subcore_axis_name="s")

    def body(dest_hbm, out_hbm, dest_v, slot_v):
        wid = lax.axis_index("s") * SC_CORES + lax.axis_index("c")

        @pl.when(wid == 0)
        def _():
            pltpu.sync_copy(dest_hbm, dest_v)
            lane = lax.iota(jnp.int32, lanes)

            def init(i, _):
                start = pl.multiple_of(i * lanes, lanes)
                slot_v[pl.ds(start, lanes)] = (lane + start) & (N_TOK - 1)
                return 0

            def place(i, _):
                start = pl.multiple_of(i * lanes, lanes)
                slots = dest_v[pl.ds(start, lanes)]
                plsc.store_scatter(slot_v, [slots], (lane + start) & (N_TOK - 1))
                return 0

            lax.fori_loop(0, N_SLOTS // lanes, init, 0)
            lax.fori_loop(0, n_assign // lanes, place, 0)
            pltpu.sync_copy(slot_v, out_hbm)

    return pl.kernel(
        body,
        out_type=jax.ShapeDtypeStruct((N_SLOTS,), jnp.int32),
        mesh=mesh,
        scratch_types=[
            pltpu.VMEM((n_assign,), jnp.int32),
            pltpu.VMEM((N_SLOTS,), jnp.int32),
        ],
        compiler_params=pltpu.CompilerParams(needs_layout_passes=False),
        name="sc_slot_tokens",
    )(dest_flat)


def _mix_kernel(x_ref, wts_ref, mod_ref, fg_ref, y0_ref, y1_ref, o_ref):
    out = _expert_mix(x_ref[...], wts_ref[...], mod_ref[0][5:6], y0_ref[...], y1_ref[...])
    ms = jnp.mean(out * out, axis=-1, keepdims=True)
    o_ref[...] = out * lax.rsqrt(ms + RMS_EPS) * fg_ref[...]


def _mix(x1, wts, mod, y2, fg):
    nt = N_TOK // COMB_TM
    tiles_per_batch = SEQ // COMB_TM
    return pl.pallas_call(
        _mix_kernel,
        grid=(nt,),
        in_specs=[
            pl.BlockSpec((COMB_TM, D_MODEL), lambda i: (i, 0)),
            pl.BlockSpec((COMB_TM, TOP_K), lambda i: (i, 0)),
            pl.BlockSpec((1, 6, D_MODEL), lambda i: (i // tiles_per_batch, 0, 0)),
            pl.BlockSpec((1, D_MODEL), lambda i: (0, 0)),
            pl.BlockSpec((COMB_TM, ROW_WORDS), lambda i: (i, 0)),
            pl.BlockSpec((COMB_TM, ROW_WORDS), lambda i: (nt + i, 0)),
        ],
        out_specs=pl.BlockSpec((COMB_TM, D_MODEL), lambda i: (i, 0)),
        out_shape=jax.ShapeDtypeStruct((N_TOK, D_MODEL), F32),
        compiler_params=_cparams(("arbitrary",)),
        name="mix",
    )(x1, wts, mod, fg, y2, y2)


def _routing_tables(idr, cnt):
    counts = cnt[0, EXPERT_LANE0:EXPERT_LANE0 + N_EXPERTS].astype(jnp.int32)
    padded = (counts + MOE_BLK - 1) // MOE_BLK * MOE_BLK
    pad_ends = jnp.cumsum(padded)
    pad_starts = pad_ends - padded
    experts = jnp.arange(N_EXPERTS, dtype=jnp.int32)
    is_expert = idr[None, 0:2, :] == experts[:, None, None]
    dest = jnp.sum(jnp.where(is_expert, pad_starts[:, None, None], 0), axis=0) + idr[2:4, :]
    block_start = jnp.arange(N_BLOCKS, dtype=jnp.int32) * MOE_BLK
    bexp = jnp.minimum(jnp.sum(block_start[:, None] >= pad_ends[None, :], axis=1),
                       N_EXPERTS - 1).astype(jnp.int32)
    owns = padded > 0
    pos_of_expert = jnp.cumsum(owns.astype(jnp.int32)) - 1
    active = jnp.sum(jnp.where(owns[None, :] & (pos_of_expert[None, :] == experts[:, None]),
                               experts[None, :], 0), axis=1).astype(jnp.int32)
    block_pos = jnp.sum(jnp.where(bexp[:, None] == experts[None, :], pos_of_expert[None, :], 0),
                        axis=1).astype(jnp.int32)
    meta = jnp.stack([pad_ends[-1] // MOE_BLK, jnp.sum(owns)]).astype(jnp.int32)
    return dest.astype(jnp.int32), block_pos, active, meta


def _router_weights(w_group, b_group, w_expert, b_expert):
    rw = jnp.zeros((D_MODEL, LANES), F32)
    rw = rw.at[:, :N_GROUPS].set(w_group)
    rw = rw.at[:, EXPERT_LANE0:EXPERT_LANE0 + N_EXPERTS].set(w_expert)
    rb = jnp.zeros((1, LANES), F32)
    rb = rb.at[0, :N_GROUPS].set(b_group)
    rb = rb.at[0, EXPERT_LANE0:EXPERT_LANE0 + N_EXPERTS].set(b_expert.reshape(-1))
    return rw.astype(BF16), rb


def _scaled_qkv_weight(w_in, head_dim):
    scale = jnp.concatenate([jnp.full((D_MODEL,), LOG2_E / math.sqrt(head_dim), F32),
                             jnp.ones((2 * D_MODEL,), F32)])
    return (w_in * scale).astype(BF16)


def kernel(x, c, norm1_g, norm2_g, ada_w, ada_b, diff_w_in, diff_w_out, diff_lambda_q1, diff_lambda_k1, diff_lambda_q2, diff_lambda_k2, diff_subln_g, sb_w_in, sb_w_out, router_group_w, router_group_b, router_expert_w, router_expert_b, expert_w_gate, expert_w_up, expert_w_down, final_norm_g):
    xf = x.reshape(N_TOK, D_MODEL)
    mod_all = _ada(c, ada_w, ada_b)
    slopes = jnp.exp2(-8.0 * jnp.arange(1, DIFF_HEADS + 1, dtype=F32) / DIFF_HEADS)

    pending = None
    for i in range(DEPTH):
        mod = mod_all[i].reshape(BATCH, 6, D_MODEL)
        j = i // 2
        is_diff = i % 2 == 0
        w_in = (_scaled_qkv_weight(diff_w_in[j], DIFF_HEAD_DIM) if is_diff
                else _scaled_qkv_weight(sb_w_in[j], SB_HEAD_DIM))
        g1 = norm1_g[i].reshape(1, D_MODEL)
        if pending is None:
            qkv = _qkv(xf, mod, g1, w_in)
        else:
            xf, qkv = _mix_qkv(*pending, mod, g1, w_in)
        if is_diff:
            lambda_init = 0.8 - 0.6 * math.exp(-0.3 * i)
            lam = (jnp.exp(jnp.sum(diff_lambda_q1[j] * diff_lambda_k1[j]))
                   - jnp.exp(jnp.sum(diff_lambda_q2[j] * diff_lambda_k2[j]))
                   + lambda_init).reshape(1, 1)
            o = _diff_attn(qkv, slopes, lam, diff_subln_g[j].reshape(1, LANES), lambda_init)
            w_out = diff_w_out[j]
        else:
            o = _sb_attn(qkv)
            w_out = sb_w_out[j]
        rw, rb = _router_weights(router_group_w[i], router_group_b[i],
                                 router_expert_w[i], router_expert_b[i])
        x1, h2, idr, wts, cnt = _out_router(o, xf, mod, norm2_g[i].reshape(1, D_MODEL),
                                            w_out.astype(BF16), rw, rb)
        dest, block_pos, active, meta = _routing_tables(idr, cnt)
        dest_flat = dest.reshape(-1)
        xs = _sc_gather_rows(h2, _sc_slot_tokens(dest_flat))
        ys = _experts(i, block_pos, active, meta, xs,
                      expert_w_gate, expert_w_up, expert_w_down)
        y2 = _sc_gather_rows(ys, dest_flat)
        pending = (x1, wts, mod, y2)
    out = _mix(*pending, final_norm_g.reshape(1, D_MODEL))
    return out.reshape(BATCH, SEQ, D_MODEL)
```

```python
import functools
import math

import jax
import jax.numpy as jnp
from jax import lax
from jax.experimental import pallas as pl
from jax.experimental.pallas import tpu as pltpu
from jax.experimental.pallas import tpu_sc as plsc

D_MODEL = 1024
BATCH = 8
SEQ = 2048
DEPTH = 2
N_TOK = BATCH * SEQ

CHUNK = 64
DIFF_HEADS = 8
DIFF_HEAD_DIM = D_MODEL // (2 * DIFF_HEADS)
SB_HEADS = 16
SB_HEAD_DIM = D_MODEL // SB_HEADS
N_GROUPS = 4
EXPERTS_PER_GROUP = 8
N_EXPERTS = N_GROUPS * EXPERTS_PER_GROUP
TOP_K = 2
EXPERT_HIDDEN = D_MODEL // 2
RMS_EPS = 1e-6
SUBLN_EPS = 1e-5

LANES = 128
EXPERT_LANE0 = 32
ROUTE_ROWS = 8
EXP2_UNDERFLOW = -150.0
LOG2_E = math.log2(math.e)

ADA_TN = 1536
QKV_TM = 512
DIFF_T = 512
DIFF_GROUP = 4
SB_T = 256
SB_PAIRS = 4
OUT_TM = 512
MOE_BLK = 512
N_SLOTS = N_TOK * TOP_K + N_EXPERTS * MOE_BLK
N_BLOCKS = N_SLOTS // MOE_BLK
COMB_TM = 512
ROW_WORDS = D_MODEL // 2

SC_CORES = 2
SC_SUBCORES = 16
SC_LANES = 16
SC_CHUNK = 64

VMEM_LIMIT = 56 * 1024 * 1024

F32 = jnp.float32
BF16 = jnp.bfloat16


def _cparams(sem):
    return pltpu.CompilerParams(dimension_semantics=sem, vmem_limit_bytes=VMEM_LIMIT)


def _ada_kernel(c_ref, w_ref, b_ref, o_ref):
    c = c_ref[...]
    cond = c * jax.nn.sigmoid(c)
    o_ref[0] = jnp.dot(cond.astype(BF16), w_ref[0].astype(BF16),
                       preferred_element_type=F32) + b_ref[0]


def _ada(c, ada_w, ada_b):
    six_d = ada_w.shape[-1]
    return pl.pallas_call(
        _ada_kernel,
        grid=(DEPTH, six_d // ADA_TN),
        in_specs=[
            pl.BlockSpec((BATCH, D_MODEL), lambda l, n: (0, 0)),
            pl.BlockSpec((1, D_MODEL, ADA_TN), lambda l, n: (l, 0, n)),
            pl.BlockSpec((1, 1, ADA_TN), lambda l, n: (l, 0, n)),
        ],
        out_specs=pl.BlockSpec((1, BATCH, ADA_TN), lambda l, n: (l, 0, n)),
        out_shape=jax.ShapeDtypeStruct((DEPTH, BATCH, six_d), F32),
        compiler_params=_cparams(("arbitrary", "arbitrary")),
        name="ada",
    )(c, ada_w, ada_b.reshape(DEPTH, 1, six_d))


def _modulated_norm(x, g, scale, shift):
    ms = jnp.mean(x * x, axis=-1, keepdims=True)
    return x * lax.rsqrt(ms + RMS_EPS) * (g * (1.0 + scale)) + shift


def _qkv_kernel(x_ref, mod_ref, g_ref, w_ref, o_ref):
    m = mod_ref[0]
    h = _modulated_norm(x_ref[...], g_ref[...], m[1:2], m[0:1]).astype(BF16)
    for n in range(3):
        cols = slice(n * D_MODEL, (n + 1) * D_MODEL)
        o_ref[:, cols] = jnp.dot(h, w_ref[:, cols], preferred_element_type=F32).astype(BF16)


def _qkv(x, mod, g, w):
    tiles_per_batch = SEQ // QKV_TM
    return pl.pallas_call(
        _qkv_kernel,
        grid=(N_TOK // QKV_TM,),
        in_specs=[
            pl.BlockSpec((QKV_TM, D_MODEL), lambda i: (i, 0)),
            pl.BlockSpec((1, 6, D_MODEL), lambda i: (i // tiles_per_batch, 0, 0)),
            pl.BlockSpec((1, D_MODEL), lambda i: (0, 0)),
            pl.BlockSpec((D_MODEL, 3 * D_MODEL), lambda i: (0, 0)),
        ],
        out_specs=pl.BlockSpec((QKV_TM, 3 * D_MODEL), lambda i: (i, 0)),
        out_shape=jax.ShapeDtypeStruct((N_TOK, 3 * D_MODEL), BF16),
        compiler_params=_cparams(("arbitrary",)),
        name="qkv",
    )(x, mod, g, w)


def _expert_mix(x, wts, gate, y0_words, y1_words):
    y0 = _unpack_bf16_pairs(lax.bitcast_convert_type(y0_words, jnp.uint32))
    y1 = _unpack_bf16_pairs(lax.bitcast_convert_type(y1_words, jnp.uint32))
    return x + gate * (wts[:, 0:1] * y0 + wts[:, 1:2] * y1)


def _mix_qkv_kernel(x1_ref, wts_ref, pmod_ref, y0_ref, y1_ref, mod_ref, g_ref, w_ref,
                    x_ref, o_ref):
    x = _expert_mix(x1_ref[...], wts_ref[...], pmod_ref[0][5:6], y0_ref[...], y1_ref[...])
    x_ref[...] = x
    m = mod_ref[0]
    h = _modulated_norm(x, g_ref[...], m[1:2], m[0:1]).astype(BF16)
    for n in range(3):
        cols = slice(n * D_MODEL, (n + 1) * D_MODEL)
        o_ref[:, cols] = jnp.dot(h, w_ref[:, cols], preferred_element_type=F32).astype(BF16)


def _mix_qkv(x1, wts, prev_mod, y2, mod, g, w):
    nt = N_TOK // QKV_TM
    tiles_per_batch = SEQ // QKV_TM
    return pl.pallas_call(
        _mix_qkv_kernel,
        grid=(nt,),
        in_specs=[
            pl.BlockSpec((QKV_TM, D_MODEL), lambda i: (i, 0)),
            pl.BlockSpec((QKV_TM, TOP_K), lambda i: (i, 0)),
            pl.BlockSpec((1, 6, D_MODEL), lambda i: (i // tiles_per_batch, 0, 0)),
            pl.BlockSpec((QKV_TM, ROW_WORDS), lambda i: (i, 0)),
            pl.BlockSpec((QKV_TM, ROW_WORDS), lambda i: (nt + i, 0)),
            pl.BlockSpec((1, 6, D_MODEL), lambda i: (i // tiles_per_batch, 0, 0)),
            pl.BlockSpec((1, D_MODEL), lambda i: (0, 0)),
            pl.BlockSpec((D_MODEL, 3 * D_MODEL), lambda i: (0, 0)),
        ],
        out_specs=[
            pl.BlockSpec((QKV_TM, D_MODEL), lambda i: (i, 0)),
            pl.BlockSpec((QKV_TM, 3 * D_MODEL), lambda i: (i, 0)),
        ],
        out_shape=[
            jax.ShapeDtypeStruct((N_TOK, D_MODEL), F32),
            jax.ShapeDtypeStruct((N_TOK, 3 * D_MODEL), BF16),
        ],
        compiler_params=_cparams(("arbitrary",)),
        name="mix_qkv",
    )(x1, wts, prev_mod, y2, y2, mod, g, w)


def _half_masked(q, upper):
    lane = lax.broadcasted_iota(jnp.int32, q.shape, 1)
    keep = (lane >= LANES // 2) if upper else (lane < LANES // 2)
    return jnp.where(keep, q, jnp.zeros_like(q))


def _pack_bf16_pairs(x):
    half = x.shape[1] // 2
    hi = lax.bitcast_convert_type(x[:, :half].astype(BF16).astype(F32), jnp.uint32)
    lo = lax.bitcast_convert_type(x[:, half:].astype(BF16).astype(F32), jnp.uint32)
    return hi | (lo >> 16)


def _unpack_bf16_pairs(u):
    hi = lax.bitcast_convert_type(u & jnp.uint32(0xFFFF0000), F32)
    lo = lax.bitcast_convert_type(u << 16, F32)
    return jnp.concatenate([hi, lo], axis=1)


def _lane_tile(x, n):
    return jnp.concatenate([x] * n, axis=1)


def _qk(q, k):
    return lax.dot_general(q, k, (((1,), (1,)), ((), ())), preferred_element_type=F32)


def _diff_attn_kernel(slopes_ref, lam_ref, q_ref, k_ref, v_ref, g_ref, o_ref, *scratch,
                      lambda_init):
    n_chains = 2 * DIFF_GROUP
    m_refs = scratch[:n_chains]
    acc_refs = scratch[n_chains:]
    ones = jnp.ones((DIFF_T, LANES), BF16)

    def head_lanes(c):
        return slice((c // 2) * LANES, (c // 2 + 1) * LANES)

    row = lax.broadcasted_iota(jnp.int32, (DIFF_T, DIFF_T), 0)
    col = lax.broadcasted_iota(jnp.int32, (DIFF_T, DIFF_T), 1)
    allowed = (col // CHUNK) <= (row // CHUNK)
    slopes = [slopes_ref[pl.program_id(1) * DIFF_GROUP + g] * LOG2_E for g in range(DIFF_GROUP)]
    rels = [(row - col).astype(F32) * slope for slope in slopes]
    diag_biases = [jnp.where(allowed, jnp.abs(rel), jnp.inf) for rel in rels]
    out_gain = g_ref[...] * (1.0 - lambda_init)

    def key_block(j, c):
        k = k_ref[pl.ds(j * DIFF_T, DIFF_T), head_lanes(c)]
        v1 = jnp.concatenate([v_ref[pl.ds(j * DIFF_T, DIFF_T), head_lanes(c)], ones], axis=1)
        return k, v1

    def query_tile(qi, _):
        q = q_ref[pl.ds(qi * DIFF_T, DIFF_T), :]
        qm = [_half_masked(q[:, head_lanes(c)], c % 2 == 1) for c in range(n_chains)]

        kv = [key_block(qi, c) for c in range(n_chains)]
        s_maps = [_qk(qm[c], kv[c][0]) - diag_biases[c // 2] for c in range(n_chains)]
        for c in range(n_chains):
            m = jnp.max(s_maps[c], axis=-1, keepdims=True)
            p = jnp.exp2(s_maps[c] - m)
            m_refs[c][...] = jnp.broadcast_to(m, (DIFF_T, LANES))
            acc_refs[c][...] = jnp.dot(p.astype(BF16), kv[c][1], preferred_element_type=F32)

        def body(j, _):
            kv = [key_block(j, c) for c in range(n_chains)]
            ahead = jnp.asarray((qi - j) * DIFF_T).astype(F32)
            s_maps = [_qk(qm[c], kv[c][0]) - rels[c // 2] for c in range(n_chains)]
            for c in range(n_chains):
                shift = slopes[c // 2] * ahead
                m = m_refs[c][...]
                m_new = jnp.maximum(m, jnp.max(s_maps[c], axis=-1, keepdims=True) - shift)
                alpha = jnp.exp2(m - m_new)
                p = jnp.exp2(s_maps[c] - _lane_tile(m_new + shift, DIFF_T // LANES))
                m_refs[c][...] = m_new
                acc_refs[c][...] = _lane_tile(alpha, 2) * acc_refs[c][...] + jnp.dot(
                    p.astype(BF16), kv[c][1], preferred_element_type=F32)
            return 0

        lax.fori_loop(0, qi, body, 0)
        for g in range(DIFF_GROUP):
            a0 = acc_refs[2 * g][...]
            a1 = acc_refs[2 * g + 1][...]
            o = a0[:, :LANES] / a0[:, LANES:] - lam_ref[...] * (a1[:, :LANES] / a1[:, LANES:])
            ms = jnp.mean(o * o, axis=-1, keepdims=True)
            o = o * lax.rsqrt(ms + SUBLN_EPS) * out_gain
            o_ref[pl.ds(qi * DIFF_T, DIFF_T), head_lanes(2 * g)] = o.astype(BF16)
        return 0

    lax.fori_loop(0, SEQ // DIFF_T, query_tile, 0)


def _diff_attn(qkv, slopes, lam, subln_g, lambda_init):
    groups = DIFF_HEADS // DIFF_GROUP
    width = DIFF_GROUP * LANES
    return pl.pallas_call(
        functools.partial(_diff_attn_kernel, lambda_init=lambda_init),
        grid=(BATCH, groups),
        in_specs=[
            pl.BlockSpec(memory_space=pltpu.SMEM),
            pl.BlockSpec((1, 1), lambda b, h: (0, 0)),
            pl.BlockSpec((SEQ, width), lambda b, h: (b, h)),
            pl.BlockSpec((SEQ, width), lambda b, h: (b, groups + h)),
            pl.BlockSpec((SEQ, width), lambda b, h: (b, 2 * groups + h)),
            pl.BlockSpec((1, LANES), lambda b, h: (0, 0)),
        ],
        out_specs=pl.BlockSpec((SEQ, width), lambda b, h: (b, h)),
        out_shape=jax.ShapeDtypeStruct((N_TOK, D_MODEL), BF16),
        scratch_shapes=([pltpu.VMEM((DIFF_T, LANES), F32)] * (2 * DIFF_GROUP)
                        + [pltpu.VMEM((DIFF_T, 2 * LANES), F32)] * (2 * DIFF_GROUP)),
        compiler_params=_cparams(("arbitrary", "arbitrary")),
        name="diff_attn",
    )(slopes, lam, qkv, qkv, qkv, subln_g)


def _sb_attn_kernel(q_ref, k_ref, v_ref, o_ref, *scratch):
    n_heads = 2 * SB_PAIRS
    tail_refs = scratch[:n_heads]
    acc_refs = scratch[n_heads:]

    def pair_lanes(h):
        return slice((h // 2) * LANES, (h // 2 + 1) * LANES)

    row = lax.broadcasted_iota(jnp.int32, (SB_T, SB_T), 0)
    col = lax.broadcasted_iota(jnp.int32, (SB_T, SB_T), 1)
    strict = col < row
    neg_from = jnp.where(row >= col, -1.0, 0.0).astype(BF16)

    def scores(j, h, qhead, mask):
        z = _qk(qhead, k_ref[pl.ds(j * SB_T, SB_T), pair_lanes(h)])
        sp = jnp.maximum(z, 0.0) + jnp.log2(1.0 + jnp.exp2(-jnp.abs(z)))
        if mask is not None:
            sp = jnp.where(mask, sp, 0.0)
        return z, sp

    def weighted(j, h, z, sp, tail, mask):
        log_a = (z + jnp.dot(sp.astype(BF16), neg_from, preferred_element_type=F32)
                 + _lane_tile(tail, SB_T // LANES))
        a = jnp.exp2(log_a)
        if mask is not None:
            a = jnp.where(mask, a, 0.0)
        return jnp.dot(a.astype(BF16), v_ref[pl.ds(j * SB_T, SB_T), pair_lanes(h)],
                       preferred_element_type=F32)

    def row_sum(sp):
        return jnp.broadcast_to(jnp.sum(sp, axis=-1, keepdims=True), (SB_T, LANES))

    zero = jnp.zeros((SB_T, LANES), F32)
    lane = lax.broadcasted_iota(jnp.int32, (SB_T, LANES), 1)

    def query_heads(qi):
        q = q_ref[pl.ds(qi * SB_T, SB_T), :]
        return [_half_masked(q[:, pair_lanes(h)], h % 2 == 1) for h in range(n_heads)]

    def finish(qi):
        for h in range(0, n_heads, 2):
            o_ref[pl.ds(qi * SB_T, SB_T), pair_lanes(h)] = jnp.where(
                lane < LANES // 2, acc_refs[h][...], acc_refs[h + 1][...]).astype(BF16)

    qh = query_heads(0)
    for h in range(n_heads):
        z, sp = scores(0, h, qh[h], strict)
        acc_refs[h][...] = weighted(0, h, z, sp, zero, strict)
    finish(0)

    def query_tile(qi, _):
        qh = query_heads(qi)
        zs = [(scores(qi, h, qh[h], strict), scores(qi - 1, h, qh[h], None))
              for h in range(n_heads)]
        for h in range(n_heads):
            (z0, sp0), (z1, sp1) = zs[h]
            tail0 = -row_sum(sp0)
            acc_refs[h][...] = (weighted(qi, h, z0, sp0, zero, strict)
                                + weighted(qi - 1, h, z1, sp1, tail0, None))
            tail_refs[h][...] = tail0 - row_sum(sp1)

        def cond(state):
            j, live = state
            return (j >= 0) & (functools.reduce(jnp.maximum, live) > EXP2_UNDERFLOW)

        def body(state):
            j, live = state

            def advance(h):
                z, sp = scores(j, h, qh[h], None)
                tail = tail_refs[h][...]
                acc_refs[h][...] += weighted(j, h, z, sp, tail, None)
                tail = tail - row_sum(sp)
                tail_refs[h][...] = tail
                return jnp.max(tail)

            live = tuple(
                lax.cond(live[h] > EXP2_UNDERFLOW, functools.partial(advance, h),
                         functools.partial(lambda done: done, live[h]))
                for h in range(n_heads))
            return j - 1, live

        live = tuple(jnp.max(t[...]) for t in tail_refs)
        lax.while_loop(cond, body, (qi - 2, live))
        finish(qi)
        return 0

    lax.fori_loop(1, SEQ // SB_T, query_tile, 0)


def _sb_attn(qkv):
    groups = SB_HEADS // (2 * SB_PAIRS)
    width = SB_PAIRS * LANES
    return pl.pallas_call(
        _sb_attn_kernel,
        grid=(BATCH, groups),
        in_specs=[
            pl.BlockSpec((SEQ, width), lambda b, h: (b, h)),
            pl.BlockSpec((SEQ, width), lambda b, h: (b, groups + h)),
            pl.BlockSpec((SEQ, width), lambda b, h: (b, 2 * groups + h)),
        ],
        out_specs=pl.BlockSpec((SEQ, width), lambda b, h: (b, h)),
        out_shape=jax.ShapeDtypeStruct((N_TOK, D_MODEL), BF16),
        scratch_shapes=[pltpu.VMEM((SB_T, LANES), F32)] * (4 * SB_PAIRS),
        compiler_params=_cparams(("arbitrary", "arbitrary")),
        name="sb_attn",
    )(qkv, qkv, qkv)


def _out_router_kernel(o_ref, x_ref, mod_ref, g_ref, w_ref, rw_ref, rb_ref,
                       x1_ref, h2_ref, idr_ref, wts_ref, cnt_ref, tri_ref, base_ref):
    i = pl.program_id(0)

    @pl.when(i == 0)
    def _():
        r = lax.broadcasted_iota(jnp.int32, (OUT_TM, OUT_TM), 0)
        c = lax.broadcasted_iota(jnp.int32, (OUT_TM, OUT_TM), 1)
        tri_ref[...] = jnp.where(c < r, 1.0, 0.0).astype(BF16)
        base_ref[...] = jnp.zeros_like(base_ref)

    m = mod_ref[0]
    y = jnp.dot(o_ref[...], w_ref[...], preferred_element_type=F32)
    x1 = x_ref[...] + m[2:3] * y
    x1_ref[...] = x1
    h2 = _modulated_norm(x1, g_ref[...], m[4:5], m[3:4])
    h2_ref[...] = lax.bitcast_convert_type(_pack_bf16_pairs(h2), jnp.int32)

    logits = jnp.dot(h2.astype(BF16), rw_ref[...], preferred_element_type=F32) + rb_ref[...]
    lane = lax.broadcasted_iota(jnp.int32, logits.shape, 1).astype(F32)
    neg_inf = jnp.float32(-jnp.inf)
    big = jnp.float32(1e9)

    is_group = lane < N_GROUPS
    gl = jnp.where(is_group, logits, neg_inf)
    gmax = jnp.max(gl, axis=-1, keepdims=True)
    gidx = jnp.min(jnp.where(gl == gmax, lane, big), axis=-1, keepdims=True)
    gsum = jnp.sum(jnp.where(is_group, jnp.exp(logits - gmax), 0.0), axis=-1, keepdims=True)
    g_w = 1.0 / gsum

    lo = EXPERT_LANE0 + EXPERTS_PER_GROUP * gidx
    in_group = (lane >= lo) & (lane < lo + EXPERTS_PER_GROUP)
    el = jnp.where(in_group, logits, neg_inf)
    v0 = jnp.max(el, axis=-1, keepdims=True)
    i0 = jnp.min(jnp.where(el == v0, lane, big), axis=-1, keepdims=True)
    el = jnp.where(lane == i0, neg_inf, el)
    v1 = jnp.max(el, axis=-1, keepdims=True)
    i1 = jnp.min(jnp.where(el == v1, lane, big), axis=-1, keepdims=True)
    t = jnp.exp(v1 - v0)
    w0 = g_w / (1.0 + t)
    w1 = g_w * t / (1.0 + t)

    oh0 = jnp.where(lane == i0, 1.0, 0.0)
    oh1 = jnp.where(lane == i1, 1.0, 0.0)
    both = oh0 + oh1
    before = jnp.dot(tri_ref[...], both.astype(BF16), preferred_element_type=F32) + base_ref[...]
    r0 = jnp.sum(before * oh0, axis=-1, keepdims=True)
    r1 = jnp.sum(before * oh1, axis=-1, keepdims=True)
    base_ref[...] = base_ref[...] + jnp.sum(both, axis=0, keepdims=True)
    cnt_ref[...] = base_ref[...]

    e0 = i0 - EXPERT_LANE0
    e1 = i1 - EXPERT_LANE0
    idr = jnp.where(lane == 0, e0, jnp.where(lane == 1, e1, jnp.where(lane == 2, r0, r1)))
    idr_ref[...] = jnp.transpose(idr)[:ROUTE_ROWS].astype(jnp.int32)
    wts_ref[...] = jnp.where(lane == 0, w0, w1)[:, :2]


def _out_router(o, x, mod, g, w, rw, rb):
    tiles_per_batch = SEQ // OUT_TM
    return pl.pallas_call(
        _out_router_kernel,
        grid=(N_TOK // OUT_TM,),
        in_specs=[
            pl.BlockSpec((OUT_TM, D_MODEL), lambda i: (i, 0)),
            pl.BlockSpec((OUT_TM, D_MODEL), lambda i: (i, 0)),
            pl.BlockSpec((1, 6, D_MODEL), lambda i: (i // tiles_per_batch, 0, 0)),
            pl.BlockSpec((1, D_MODEL), lambda i: (0, 0)),
            pl.BlockSpec((D_MODEL, D_MODEL), lambda i: (0, 0)),
            pl.BlockSpec((D_MODEL, LANES), lambda i: (0, 0)),
            pl.BlockSpec((1, LANES), lambda i: (0, 0)),
        ],
        out_specs=[
            pl.BlockSpec((OUT_TM, D_MODEL), lambda i: (i, 0)),
            pl.BlockSpec((OUT_TM, ROW_WORDS), lambda i: (i, 0)),
            pl.BlockSpec((ROUTE_ROWS, OUT_TM), lambda i: (0, i)),
            pl.BlockSpec((OUT_TM, 2), lambda i: (i, 0)),
            pl.BlockSpec((1, LANES), lambda i: (0, 0)),
        ],
        out_shape=[
            jax.ShapeDtypeStruct((N_TOK, D_MODEL), F32),
            jax.ShapeDtypeStruct((N_TOK, ROW_WORDS), jnp.int32),
            jax.ShapeDtypeStruct((ROUTE_ROWS, N_TOK), jnp.int32),
            jax.ShapeDtypeStruct((N_TOK, 2), F32),
            jax.ShapeDtypeStruct((1, LANES), F32),
        ],
        scratch_shapes=[
            pltpu.VMEM((OUT_TM, OUT_TM), BF16),
            pltpu.VMEM((1, LANES), F32),
        ],
        compiler_params=_cparams(("arbitrary",)),
        name="out_router",
    )(o, x, mod, g, w, rw, rb)


def _expert_kernel(ord_ref, active_ref, meta_ref, xs_ref, wg_hbm, wu_hbm, wd_hbm, ys_ref,
                   wg_f32, wu_f32, wd_f32, sems, wg_bf, wu_bf, wd_bf, *, layer):
    i = pl.program_id(0)
    n_used = meta_ref[0]
    n_active = meta_ref[1]

    def fetch(pos, slot):
        e = active_ref[pos]
        return (
            pltpu.make_async_copy(wg_hbm.at[layer, e], wg_f32.at[slot], sems.at[slot, 0]),
            pltpu.make_async_copy(wu_hbm.at[layer, e], wu_f32.at[slot], sems.at[slot, 1]),
            pltpu.make_async_copy(wd_hbm.at[layer, e], wd_f32.at[slot], sems.at[slot, 2]),
        )

    @pl.when(i == 0)
    def _():
        for copy in fetch(0, 0):
            copy.start()

        @pl.when(n_active > 1)
        def _():
            for copy in fetch(1, 1):
                copy.start()

    @pl.when(i < n_used)
    def _():
        pos = ord_ref[i]
        changed = (i == 0) | (pos != ord_ref[jnp.maximum(i - 1, 0)])
        for slot in range(2):
            @pl.when(changed & (pos % 2 == slot))
            def _():
                for copy in fetch(pos, slot):
                    copy.wait()
                wg_bf[...] = wg_f32[slot].astype(BF16)
                wu_bf[...] = wu_f32[slot].astype(BF16)
                wd_bf[...] = wd_f32[slot].astype(BF16)

                @pl.when(pos + 2 < n_active)
                def _():
                    for copy in fetch(pos + 2, slot):
                        copy.start()

        x = _unpack_bf16_pairs(lax.bitcast_convert_type(xs_ref[...], jnp.uint32)).astype(BF16)
        g = jnp.dot(x, wg_bf[...], preferred_element_type=F32)
        u = jnp.dot(x, wu_bf[...], preferred_element_type=F32)
        hid = (g * jax.nn.sigmoid(g)) * u
        y = jnp.dot(hid.astype(BF16), wd_bf[...], preferred_element_type=F32)
        ys_ref[...] = lax.bitcast_convert_type(_pack_bf16_pairs(y), jnp.int32)

    @pl.when(i >= n_used)
    def _():
        ys_ref[...] = jnp.zeros_like(ys_ref)


def _experts(layer, block_pos, active, meta, xs, wg, wu, wd):
    def in_row_map(i, block_pos, active, meta):
        return (jnp.minimum(i, meta[0] - 1), 0)

    def row_map(i, block_pos, active, meta):
        return (i, 0)

    return pl.pallas_call(
        functools.partial(_expert_kernel, layer=layer),
        grid_spec=pltpu.PrefetchScalarGridSpec(
            num_scalar_prefetch=3,
            grid=(N_BLOCKS,),
            in_specs=[
                pl.BlockSpec((MOE_BLK, ROW_WORDS), in_row_map),
                pl.BlockSpec(memory_space=pl.ANY),
                pl.BlockSpec(memory_space=pl.ANY),
                pl.BlockSpec(memory_space=pl.ANY),
            ],
            out_specs=pl.BlockSpec((MOE_BLK, ROW_WORDS), row_map),
            scratch_shapes=[
                pltpu.VMEM((2, D_MODEL, EXPERT_HIDDEN), F32),
                pltpu.VMEM((2, D_MODEL, EXPERT_HIDDEN), F32),
                pltpu.VMEM((2, EXPERT_HIDDEN, D_MODEL), F32),
                pltpu.SemaphoreType.DMA((2, 3)),
                pltpu.VMEM((D_MODEL, EXPERT_HIDDEN), BF16),
                pltpu.VMEM((D_MODEL, EXPERT_HIDDEN), BF16),
                pltpu.VMEM((EXPERT_HIDDEN, D_MODEL), BF16),
            ],
        ),
        out_shape=jax.ShapeDtypeStruct((N_SLOTS, ROW_WORDS), jnp.int32),
        compiler_params=_cparams(("arbitrary",)),
        name="experts",
    )(block_pos, active, meta, xs, wg, wu, wd)


def _sc_gather_rows(table, idx):
    n_rows = idx.shape[0]
    width = table.shape[1]
    workers = SC_CORES * SC_SUBCORES
    per_worker = n_rows // workers
    assert per_worker * workers == n_rows and per_worker % SC_CHUNK == 0
    mesh = plsc.VectorSubcoreMesh(core_axis_name="c", subcore_axis_name="s")

    n_chunks = per_worker // SC_CHUNK

    def body(table_hbm, idx_hbm, out_hbm, idx_v, buf0, buf1, gsem0, gsem1, wsem0, wsem1):
        wid = lax.axis_index("s") * SC_CORES + lax.axis_index("c")
        base = pl.multiple_of(wid * per_worker, SC_CHUNK)
        bufs, gsems, wsems = (buf0, buf1), (gsem0, gsem1), (wsem0, wsem1)
        pltpu.sync_copy(idx_hbm.at[pl.ds(base, per_worker)], idx_v)

        def gather(c):
            rows = idx_v.at[pl.ds(c * SC_CHUNK, SC_CHUNK)]
            return pltpu.async_copy(table_hbm.at[rows], bufs[c % 2], gsems[c % 2])

        def write(c):
            dst = out_hbm.at[pl.ds(base + c * SC_CHUNK, SC_CHUNK)]
            return pltpu.async_copy(bufs[c % 2], dst, wsems[c % 2])

        gathers = {0: gather(0)}
        writes = {}
        for c in range(n_chunks):
            if c + 1 < n_chunks:
                if c >= 1:
                    writes.pop(c - 1).wait()
                gathers[c + 1] = gather(c + 1)
            gathers.pop(c).wait()
            writes[c] = write(c)
        for c in sorted(writes):
            writes[c].wait()

    return pl.kernel(
        body,
        out_type=jax.ShapeDtypeStruct((n_rows, width), table.dtype),
        mesh=mesh,
        scratch_types=[
            pltpu.VMEM((per_worker,), jnp.int32),
            pltpu.VMEM((SC_CHUNK, width), table.dtype),
            pltpu.VMEM((SC_CHUNK, width), table.dtype),
            pltpu.SemaphoreType.DMA,
            pltpu.SemaphoreType.DMA,
            pltpu.SemaphoreType.DMA,
            pltpu.SemaphoreType.DMA,
        ],
        name="sc_gather_rows",
    )(table, idx)


def _sc_slot_tokens(dest_flat):
    n_assign = N_TOK * TOP_K
    lanes = SC_LANES
    assert N_TOK & (N_TOK - 1) == 0
    mesh = plsc.VectorSubcoreMesh(core_axis_name="c", subcore_axis_name="s")

    filler = jnp.arange(N_SLOTS, dtype=jnp.int32) & (N_TOK - 1)

    def body(dest_hbm, filler_hbm, out_hbm, dest_v, slot_v):
        wid = lax.axis_index("s") * SC_CORES + lax.axis_index("c")

        @pl.when(wid == 0)
        def _():
            pltpu.sync_copy(dest_hbm, dest_v)
            pltpu.sync_copy(filler_hbm, slot_v)
            lane = lax.iota(jnp.int32, lanes)

            def place(i, _):
                start = pl.multiple_of(i * lanes, lanes)
                slots = dest_v[pl.ds(start, lanes)]
                plsc.store_scatter(slot_v, [slots], (lane + start) & (N_TOK - 1))
                return 0

            lax.fori_loop(0, n_assign // lanes, place, 0)
            pltpu.sync_copy(slot_v, out_hbm)

    return pl.kernel(
        body,
        out_type=jax.ShapeDtypeStruct((N_SLOTS,), jnp.int32),
        mesh=mesh,
        scratch_types=[
            pltpu.VMEM((n_assign,), jnp.int32),
            pltpu.VMEM((N_SLOTS,), jnp.int32),
        ],
        compiler_params=pltpu.CompilerParams(needs_layout_passes=False),
        name="sc_slot_tokens",
    )(dest_flat, filler)


def _mix_kernel(x_ref, wts_ref, mod_ref, fg_ref, y0_ref, y1_ref, o_ref):
    out = _expert_mix(x_ref[...], wts_ref[...], mod_ref[0][5:6], y0_ref[...], y1_ref[...])
    ms = jnp.mean(out * out, axis=-1, keepdims=True)
    o_ref[...] = out * lax.rsqrt(ms + RMS_EPS) * fg_ref[...]


def _mix(x1, wts, mod, y2, fg):
    nt = N_TOK // COMB_TM
    tiles_per_batch = SEQ // COMB_TM
    return pl.pallas_call(
        _mix_kernel,
        grid=(nt,),
        in_specs=[
            pl.BlockSpec((COMB_TM, D_MODEL), lambda i: (i, 0)),
            pl.BlockSpec((COMB_TM, TOP_K), lambda i: (i, 0)),
            pl.BlockSpec((1, 6, D_MODEL), lambda i: (i // tiles_per_batch, 0, 0)),
            pl.BlockSpec((1, D_MODEL), lambda i: (0, 0)),
            pl.BlockSpec((COMB_TM, ROW_WORDS), lambda i: (i, 0)),
            pl.BlockSpec((COMB_TM, ROW_WORDS), lambda i: (nt + i, 0)),
        ],
        out_specs=pl.BlockSpec((COMB_TM, D_MODEL), lambda i: (i, 0)),
        out_shape=jax.ShapeDtypeStruct((N_TOK, D_MODEL), F32),
        compiler_params=_cparams(("arbitrary",)),
        name="mix",
    )(x1, wts, mod, fg, y2, y2)


def _routing_tables(idr, cnt):
    counts = cnt[0, EXPERT_LANE0:EXPERT_LANE0 + N_EXPERTS].astype(jnp.int32)
    padded = (counts + MOE_BLK - 1) // MOE_BLK * MOE_BLK
    pad_ends = jnp.cumsum(padded)
    pad_starts = pad_ends - padded
    experts = jnp.arange(N_EXPERTS, dtype=jnp.int32)
    is_expert = idr[None, 0:2, :] == experts[:, None, None]
    dest = jnp.sum(jnp.where(is_expert, pad_starts[:, None, None], 0), axis=0) + idr[2:4, :]
    block_start = jnp.arange(N_BLOCKS, dtype=jnp.int32) * MOE_BLK
    bexp = jnp.minimum(jnp.sum(block_start[:, None] >= pad_ends[None, :], axis=1),
                       N_EXPERTS - 1).astype(jnp.int32)
    owns = padded > 0
    pos_of_expert = jnp.cumsum(owns.astype(jnp.int32)) - 1
    active = jnp.sum(jnp.where(owns[None, :] & (pos_of_expert[None, :] == experts[:, None]),
                               experts[None, :], 0), axis=1).astype(jnp.int32)
    block_pos = jnp.sum(jnp.where(bexp[:, None] == experts[None, :], pos_of_expert[None, :], 0),
                        axis=1).astype(jnp.int32)
    meta = jnp.stack([pad_ends[-1] // MOE_BLK, jnp.sum(owns)]).astype(jnp.int32)
    return dest.astype(jnp.int32), block_pos, active, meta


def _router_weights(w_group, b_group, w_expert, b_expert):
    rw = jnp.zeros((D_MODEL, LANES), F32)
    rw = rw.at[:, :N_GROUPS].set(w_group)
    rw = rw.at[:, EXPERT_LANE0:EXPERT_LANE0 + N_EXPERTS].set(w_expert)
    rb = jnp.zeros((1, LANES), F32)
    rb = rb.at[0, :N_GROUPS].set(b_group)
    rb = rb.at[0, EXPERT_LANE0:EXPERT_LANE0 + N_EXPERTS].set(b_expert.reshape(-1))
    return rw.astype(BF16), rb


def _scaled_qkv_weight(w_in, head_dim):
    scale = jnp.concatenate([jnp.full((D_MODEL,), LOG2_E / math.sqrt(head_dim), F32),
                             jnp.ones((2 * D_MODEL,), F32)])
    return (w_in * scale).astype(BF16)


def kernel(x, c, norm1_g, norm2_g, ada_w, ada_b, diff_w_in, diff_w_out, diff_lambda_q1, diff_lambda_k1, diff_lambda_q2, diff_lambda_k2, diff_subln_g, sb_w_in, sb_w_out, router_group_w, router_group_b, router_expert_w, router_expert_b, expert_w_gate, expert_w_up, expert_w_down, final_norm_g):
    xf = x.reshape(N_TOK, D_MODEL)
    mod_all = _ada(c, ada_w, ada_b)
    slopes = jnp.exp2(-8.0 * jnp.arange(1, DIFF_HEADS + 1, dtype=F32) / DIFF_HEADS)

    pending = None
    for i in range(DEPTH):
        mod = mod_all[i].reshape(BATCH, 6, D_MODEL)
        j = i // 2
        is_diff = i % 2 == 0
        w_in = (_scaled_qkv_weight(diff_w_in[j], DIFF_HEAD_DIM) if is_diff
                else _scaled_qkv_weight(sb_w_in[j], SB_HEAD_DIM))
        g1 = norm1_g[i].reshape(1, D_MODEL)
        if pending is None:
            qkv = _qkv(xf, mod, g1, w_in)
        else:
            xf, qkv = _mix_qkv(*pending, mod, g1, w_in)
        if is_diff:
            lambda_init = 0.8 - 0.6 * math.exp(-0.3 * i)
            lam = (jnp.exp(jnp.sum(diff_lambda_q1[j] * diff_lambda_k1[j]))
                   - jnp.exp(jnp.sum(diff_lambda_q2[j] * diff_lambda_k2[j]))
                   + lambda_init).reshape(1, 1)
            o = _diff_attn(qkv, slopes, lam, diff_subln_g[j].reshape(1, LANES), lambda_init)
            w_out = diff_w_out[j]
        else:
            o = _sb_attn(qkv)
            w_out = sb_w_out[j]
        rw, rb = _router_weights(router_group_w[i], router_group_b[i],
                                 router_expert_w[i], router_expert_b[i])
        x1, h2, idr, wts, cnt = _out_router(o, xf, mod, norm2_g[i].reshape(1, D_MODEL),
                                            w_out.astype(BF16), rw, rb)
        dest, block_pos, active, meta = _routing_tables(idr, cnt)
        dest_flat = dest.reshape(-1)
        xs = _sc_gather_rows(h2, _sc_slot_tokens(dest_flat))
        ys = _experts(i, block_pos, active, meta, xs,
                      expert_w_gate, expert_w_up, expert_w_down)
        y2 = _sc_gather_rows(ys, dest_flat)
        pending = (x1, wts, mod, y2)
    out = _mix(*pending, final_norm_g.reshape(1, D_MODEL))
    return out.reshape(BATCH, SEQ, D_MODEL)
```

```python
import functools
import math

import jax
import jax.numpy as jnp
from jax import lax
from jax.experimental import pallas as pl
from jax.experimental.pallas import tpu as pltpu
from jax.experimental.pallas import tpu_sc as plsc

D_MODEL = 1024
BATCH = 8
SEQ = 2048
DEPTH = 2
N_TOK = BATCH * SEQ

CHUNK = 64
DIFF_HEADS = 8
DIFF_HEAD_DIM = D_MODEL // (2 * DIFF_HEADS)
SB_HEADS = 16
SB_HEAD_DIM = D_MODEL // SB_HEADS
N_GROUPS = 4
EXPERTS_PER_GROUP = 8
N_EXPERTS = N_GROUPS * EXPERTS_PER_GROUP
TOP_K = 2
EXPERT_HIDDEN = D_MODEL // 2
RMS_EPS = 1e-6
SUBLN_EPS = 1e-5

LANES = 128
EXPERT_LANE0 = 32
ROUTE_ROWS = 8
EXP2_UNDERFLOW = -150.0
LOG2_E = math.log2(math.e)

ADA_TN = 1536
QKV_TM = 512
DIFF_T = 512
DIFF_GROUP = 4
SB_T = 256
SB_PAIRS = 4
OUT_TM = 512
MOE_BLK = 512
N_SLOTS = N_TOK * TOP_K + N_EXPERTS * MOE_BLK
N_BLOCKS = N_SLOTS // MOE_BLK
COMB_TM = 512
ROW_WORDS = D_MODEL // 2

SC_CORES = 2
SC_SUBCORES = 16
SC_LANES = 16
SC_CHUNK = 64

VMEM_LIMIT = 56 * 1024 * 1024

F32 = jnp.float32
BF16 = jnp.bfloat16


def _cparams(sem):
    return pltpu.CompilerParams(dimension_semantics=sem, vmem_limit_bytes=VMEM_LIMIT)


def _ada_kernel(c_ref, w_ref, b_ref, o_ref):
    c = c_ref[...]
    cond = c * jax.nn.sigmoid(c)
    o_ref[0] = jnp.dot(cond.astype(BF16), w_ref[0].astype(BF16),
                       preferred_element_type=F32) + b_ref[0]


def _ada(c, ada_w, ada_b):
    six_d = ada_w.shape[-1]
    return pl.pallas_call(
        _ada_kernel,
        grid=(DEPTH, six_d // ADA_TN),
        in_specs=[
            pl.BlockSpec((BATCH, D_MODEL), lambda l, n: (0, 0)),
            pl.BlockSpec((1, D_MODEL, ADA_TN), lambda l, n: (l, 0, n)),
            pl.BlockSpec((1, 1, ADA_TN), lambda l, n: (l, 0, n)),
        ],
        out_specs=pl.BlockSpec((1, BATCH, ADA_TN), lambda l, n: (l, 0, n)),
        out_shape=jax.ShapeDtypeStruct((DEPTH, BATCH, six_d), F32),
        compiler_params=_cparams(("arbitrary", "arbitrary")),
        name="ada",
    )(c, ada_w, ada_b.reshape(DEPTH, 1, six_d))


def _modulated_norm(x, g, scale, shift):
    ms = jnp.mean(x * x, axis=-1, keepdims=True)
    return x * lax.rsqrt(ms + RMS_EPS) * (g * (1.0 + scale)) + shift


def _qkv_kernel(x_ref, mod_ref, g_ref, w_ref, o_ref):
    m = mod_ref[0]
    h = _modulated_norm(x_ref[...], g_ref[...], m[1:2], m[0:1]).astype(BF16)
    for n in range(3):
        cols = slice(n * D_MODEL, (n + 1) * D_MODEL)
        o_ref[:, cols] = jnp.dot(h, w_ref[:, cols], preferred_element_type=F32).astype(BF16)


def _qkv(x, mod, g, w):
    tiles_per_batch = SEQ // QKV_TM
    return pl.pallas_call(
        _qkv_kernel,
        grid=(N_TOK // QKV_TM,),
        in_specs=[
            pl.BlockSpec((QKV_TM, D_MODEL), lambda i: (i, 0)),
            pl.BlockSpec((1, 6, D_MODEL), lambda i: (i // tiles_per_batch, 0, 0)),
            pl.BlockSpec((1, D_MODEL), lambda i: (0, 0)),
            pl.BlockSpec((D_MODEL, 3 * D_MODEL), lambda i: (0, 0)),
        ],
        out_specs=pl.BlockSpec((QKV_TM, 3 * D_MODEL), lambda i: (i, 0)),
        out_shape=jax.ShapeDtypeStruct((N_TOK, 3 * D_MODEL), BF16),
        compiler_params=_cparams(("arbitrary",)),
        name="qkv",
    )(x, mod, g, w)


def _expert_mix(x, wts, gate, y0_words, y1_words):
    y0 = _unpack_bf16_pairs(lax.bitcast_convert_type(y0_words, jnp.uint32))
    y1 = _unpack_bf16_pairs(lax.bitcast_convert_type(y1_words, jnp.uint32))
    return x + gate * (wts[:, 0:1] * y0 + wts[:, 1:2] * y1)


def _mix_qkv_kernel(x1_ref, wts_ref, pmod_ref, y0_ref, y1_ref, mod_ref, g_ref, w_ref,
                    x_ref, o_ref):
    x = _expert_mix(x1_ref[...], wts_ref[...], pmod_ref[0][5:6], y0_ref[...], y1_ref[...])
    x_ref[...] = x
    m = mod_ref[0]
    h = _modulated_norm(x, g_ref[...], m[1:2], m[0:1]).astype(BF16)
    for n in range(3):
        cols = slice(n * D_MODEL, (n + 1) * D_MODEL)
        o_ref[:, cols] = jnp.dot(h, w_ref[:, cols], preferred_element_type=F32).astype(BF16)


def _mix_qkv(x1, wts, prev_mod, y2, mod, g, w):
    nt = N_TOK // QKV_TM
    tiles_per_batch = SEQ // QKV_TM
    return pl.pallas_call(
        _mix_qkv_kernel,
        grid=(nt,),
        in_specs=[
            pl.BlockSpec((QKV_TM, D_MODEL), lambda i: (i, 0)),
            pl.BlockSpec((QKV_TM, TOP_K), lambda i: (i, 0)),
            pl.BlockSpec((1, 6, D_MODEL), lambda i: (i // tiles_per_batch, 0, 0)),
            pl.BlockSpec((QKV_TM, ROW_WORDS), lambda i: (i, 0)),
            pl.BlockSpec((QKV_TM, ROW_WORDS), lambda i: (nt + i, 0)),
            pl.BlockSpec((1, 6, D_MODEL), lambda i: (i // tiles_per_batch, 0, 0)),
            pl.BlockSpec((1, D_MODEL), lambda i: (0, 0)),
            pl.BlockSpec((D_MODEL, 3 * D_MODEL), lambda i: (0, 0)),
        ],
        out_specs=[
            pl.BlockSpec((QKV_TM, D_MODEL), lambda i: (i, 0)),
            pl.BlockSpec((QKV_TM, 3 * D_MODEL), lambda i: (i, 0)),
        ],
        out_shape=[
            jax.ShapeDtypeStruct((N_TOK, D_MODEL), F32),
            jax.ShapeDtypeStruct((N_TOK, 3 * D_MODEL), BF16),
        ],
        compiler_params=_cparams(("arbitrary",)),
        name="mix_qkv",
    )(x1, wts, prev_mod, y2, y2, mod, g, w)


def _half_masked(q, upper):
    lane = lax.broadcasted_iota(jnp.int32, q.shape, 1)
    keep = (lane >= LANES // 2) if upper else (lane < LANES // 2)
    return jnp.where(keep, q, jnp.zeros_like(q))


def _pack_bf16_pairs(x):
    half = x.shape[1] // 2
    hi = lax.bitcast_convert_type(x[:, :half].astype(BF16).astype(F32), jnp.uint32)
    lo = lax.bitcast_convert_type(x[:, half:].astype(BF16).astype(F32), jnp.uint32)
    return hi | (lo >> 16)


def _unpack_bf16_pairs(u):
    hi = lax.bitcast_convert_type(u & jnp.uint32(0xFFFF0000), F32)
    lo = lax.bitcast_convert_type(u << 16, F32)
    return jnp.concatenate([hi, lo], axis=1)


def _lane_tile(x, n):
    return jnp.concatenate([x] * n, axis=1)


def _qk(q, k):
    return lax.dot_general(q, k, (((1,), (1,)), ((), ())), preferred_element_type=F32)


def _diff_attn_kernel(slopes_ref, lam_ref, q_ref, k_ref, v_ref, g_ref, o_ref, *scratch,
                      lambda_init):
    n_chains = 2 * DIFF_GROUP
    m_refs = scratch[:n_chains]
    acc_refs = scratch[n_chains:]
    ones = jnp.ones((DIFF_T, LANES), BF16)

    def head_lanes(c):
        return slice((c // 2) * LANES, (c // 2 + 1) * LANES)

    row = lax.broadcasted_iota(jnp.int32, (DIFF_T, DIFF_T), 0)
    col = lax.broadcasted_iota(jnp.int32, (DIFF_T, DIFF_T), 1)
    allowed = (col // CHUNK) <= (row // CHUNK)
    slopes = [slopes_ref[pl.program_id(1) * DIFF_GROUP + g] * LOG2_E for g in range(DIFF_GROUP)]
    rels = [(row - col).astype(F32) * slope for slope in slopes]
    diag_biases = [jnp.where(allowed, jnp.abs(rel), jnp.inf) for rel in rels]
    out_gain = g_ref[...] * (1.0 - lambda_init)

    def key_block(j, c):
        k = k_ref[pl.ds(j * DIFF_T, DIFF_T), head_lanes(c)]
        v1 = jnp.concatenate([v_ref[pl.ds(j * DIFF_T, DIFF_T), head_lanes(c)], ones], axis=1)
        return k, v1

    def query_tile(qi, _):
        q = q_ref[pl.ds(qi * DIFF_T, DIFF_T), :]
        qm = [_half_masked(q[:, head_lanes(c)], c % 2 == 1) for c in range(n_chains)]

        kv = [key_block(qi, c) for c in range(n_chains)]
        s_maps = [_qk(qm[c], kv[c][0]) - diag_biases[c // 2] for c in range(n_chains)]
        for c in range(n_chains):
            m = jnp.max(s_maps[c], axis=-1, keepdims=True)
            p = jnp.exp2(s_maps[c] - m)
            m_refs[c][...] = jnp.broadcast_to(m, (DIFF_T, LANES))
            acc_refs[c][...] = jnp.dot(p.astype(BF16), kv[c][1], preferred_element_type=F32)

        def body(j, _):
            kv = [key_block(j, c) for c in range(n_chains)]
            ahead = jnp.asarray((qi - j) * DIFF_T).astype(F32)
            s_maps = [_qk(qm[c], kv[c][0]) - rels[c // 2] for c in range(n_chains)]
            for c in range(n_chains):
                shift = slopes[c // 2] * ahead
                m = m_refs[c][...]
                m_new = jnp.maximum(m, jnp.max(s_maps[c], axis=-1, keepdims=True) - shift)
                alpha = jnp.exp2(m - m_new)
                p = jnp.exp2(s_maps[c] - _lane_tile(m_new + shift, DIFF_T // LANES))
                m_refs[c][...] = m_new
                acc_refs[c][...] = _lane_tile(alpha, 2) * acc_refs[c][...] + jnp.dot(
                    p.astype(BF16), kv[c][1], preferred_element_type=F32)
            return 0

        lax.fori_loop(0, qi, body, 0)
        for g in range(DIFF_GROUP):
            a0 = acc_refs[2 * g][...]
            a1 = acc_refs[2 * g + 1][...]
            o = a0[:, :LANES] / a0[:, LANES:] - lam_ref[...] * (a1[:, :LANES] / a1[:, LANES:])
            ms = jnp.mean(o * o, axis=-1, keepdims=True)
            o = o * lax.rsqrt(ms + SUBLN_EPS) * out_gain
            o_ref[pl.ds(qi * DIFF_T, DIFF_T), head_lanes(2 * g)] = o.astype(BF16)
        return 0

    lax.fori_loop(0, SEQ // DIFF_T, query_tile, 0)


def _diff_attn(qkv, slopes, lam, subln_g, lambda_init):
    groups = DIFF_HEADS // DIFF_GROUP
    width = DIFF_GROUP * LANES
    return pl.pallas_call(
        functools.partial(_diff_attn_kernel, lambda_init=lambda_init),
        grid=(BATCH, groups),
        in_specs=[
            pl.BlockSpec(memory_space=pltpu.SMEM),
            pl.BlockSpec((1, 1), lambda b, h: (0, 0)),
            pl.BlockSpec((SEQ, width), lambda b, h: (b, h)),
            pl.BlockSpec((SEQ, width), lambda b, h: (b, groups + h)),
            pl.BlockSpec((SEQ, width), lambda b, h: (b, 2 * groups + h)),
            pl.BlockSpec((1, LANES), lambda b, h: (0, 0)),
        ],
        out_specs=pl.BlockSpec((SEQ, width), lambda b, h: (b, h)),
        out_shape=jax.ShapeDtypeStruct((N_TOK, D_MODEL), BF16),
        scratch_shapes=([pltpu.VMEM((DIFF_T, LANES), F32)] * (2 * DIFF_GROUP)
                        + [pltpu.VMEM((DIFF_T, 2 * LANES), F32)] * (2 * DIFF_GROUP)),
        compiler_params=_cparams(("arbitrary", "arbitrary")),
        name="diff_attn",
    )(slopes, lam, qkv, qkv, qkv, subln_g)


def _sb_attn_kernel(q_ref, k_ref, v_ref, o_ref, *scratch):
    n_heads = 2 * SB_PAIRS
    tail_refs = scratch[:n_heads]
    acc_refs = scratch[n_heads:]

    def pair_lanes(h):
        return slice((h // 2) * LANES, (h // 2 + 1) * LANES)

    row = lax.broadcasted_iota(jnp.int32, (SB_T, SB_T), 0)
    col = lax.broadcasted_iota(jnp.int32, (SB_T, SB_T), 1)
    strict = col < row
    neg_from = jnp.where(row >= col, -1.0, 0.0).astype(BF16)

    def scores(j, h, qhead, mask):
        z = _qk(qhead, k_ref[pl.ds(j * SB_T, SB_T), pair_lanes(h)])
        sp = jnp.maximum(z, 0.0) + jnp.log2(1.0 + jnp.exp2(-jnp.abs(z)))
        if mask is not None:
            sp = jnp.where(mask, sp, 0.0)
        return z, sp

    def weighted(j, h, z, sp, tail, mask):
        log_a = (z + jnp.dot(sp.astype(BF16), neg_from, preferred_element_type=F32)
                 + _lane_tile(tail, SB_T // LANES))
        a = jnp.exp2(log_a)
        if mask is not None:
            a = jnp.where(mask, a, 0.0)
        return jnp.dot(a.astype(BF16), v_ref[pl.ds(j * SB_T, SB_T), pair_lanes(h)],
                       preferred_element_type=F32)

    def row_sum(sp):
        return jnp.broadcast_to(jnp.sum(sp, axis=-1, keepdims=True), (SB_T, LANES))

    zero = jnp.zeros((SB_T, LANES), F32)
    lane = lax.broadcasted_iota(jnp.int32, (SB_T, LANES), 1)

    def query_heads(qi):
        q = q_ref[pl.ds(qi * SB_T, SB_T), :]
        return [_half_masked(q[:, pair_lanes(h)], h % 2 == 1) for h in range(n_heads)]

    def finish(qi):
        for h in range(0, n_heads, 2):
            o_ref[pl.ds(qi * SB_T, SB_T), pair_lanes(h)] = jnp.where(
                lane < LANES // 2, acc_refs[h][...], acc_refs[h + 1][...]).astype(BF16)

    qh = query_heads(0)
    for h in range(n_heads):
        z, sp = scores(0, h, qh[h], strict)
        acc_refs[h][...] = weighted(0, h, z, sp, zero, strict)
    finish(0)

    def query_tile(qi, _):
        qh = query_heads(qi)
        zs = [(scores(qi, h, qh[h], strict), scores(qi - 1, h, qh[h], None))
              for h in range(n_heads)]
        for h in range(n_heads):
            (z0, sp0), (z1, sp1) = zs[h]
            tail0 = -row_sum(sp0)
            acc_refs[h][...] = (weighted(qi, h, z0, sp0, zero, strict)
                                + weighted(qi - 1, h, z1, sp1, tail0, None))
            tail_refs[h][...] = tail0 - row_sum(sp1)

        def cond(state):
            j, live = state
            return (j >= 0) & (functools.reduce(jnp.maximum, live) > EXP2_UNDERFLOW)

        def body(state):
            j, live = state

            def advance(h):
                z, sp = scores(j, h, qh[h], None)
                tail = tail_refs[h][...]
                acc_refs[h][...] += weighted(j, h, z, sp, tail, None)
                tail = tail - row_sum(sp)
                tail_refs[h][...] = tail
                return jnp.max(tail)

            live = tuple(
                lax.cond(live[h] > EXP2_UNDERFLOW, functools.partial(advance, h),
                         functools.partial(lambda done: done, live[h]))
                for h in range(n_heads))
            return j - 1, live

        live = tuple(jnp.max(t[...]) for t in tail_refs)
        lax.while_loop(cond, body, (qi - 2, live))
        finish(qi)
        return 0

    lax.fori_loop(1, SEQ // SB_T, query_tile, 0)


def _sb_attn(qkv):
    groups = SB_HEADS // (2 * SB_PAIRS)
    width = SB_PAIRS * LANES
    return pl.pallas_call(
        _sb_attn_kernel,
        grid=(BATCH, groups),
        in_specs=[
            pl.BlockSpec((SEQ, width), lambda b, h: (b, h)),
            pl.BlockSpec((SEQ, width), lambda b, h: (b, groups + h)),
            pl.BlockSpec((SEQ, width), lambda b, h: (b, 2 * groups + h)),
        ],
        out_specs=pl.BlockSpec((SEQ, width), lambda b, h: (b, h)),
        out_shape=jax.ShapeDtypeStruct((N_TOK, D_MODEL), BF16),
        scratch_shapes=[pltpu.VMEM((SB_T, LANES), F32)] * (4 * SB_PAIRS),
        compiler_params=_cparams(("arbitrary", "arbitrary")),
        name="sb_attn",
    )(qkv, qkv, qkv)


def _out_router_kernel(o_ref, x_ref, mod_ref, g_ref, w_ref, rw_ref, rb_ref,
                       x1_ref, h2_ref, idr_ref, wts_ref, cnt_ref, tri_ref, base_ref):
    i = pl.program_id(0)

    @pl.when(i == 0)
    def _():
        r = lax.broadcasted_iota(jnp.int32, (OUT_TM, OUT_TM), 0)
        c = lax.broadcasted_iota(jnp.int32, (OUT_TM, OUT_TM), 1)
        tri_ref[...] = jnp.where(c < r, 1.0, 0.0).astype(BF16)
        base_ref[...] = jnp.zeros_like(base_ref)

    m = mod_ref[0]
    y = jnp.dot(o_ref[...], w_ref[...], preferred_element_type=F32)
    x1 = x_ref[...] + m[2:3] * y
    x1_ref[...] = x1
    h2 = _modulated_norm(x1, g_ref[...], m[4:5], m[3:4])
    h2_ref[...] = lax.bitcast_convert_type(_pack_bf16_pairs(h2), jnp.int32)

    logits = jnp.dot(h2.astype(BF16), rw_ref[...], preferred_element_type=F32) + rb_ref[...]
    lane = lax.broadcasted_iota(jnp.int32, logits.shape, 1).astype(F32)
    neg_inf = jnp.float32(-jnp.inf)
    big = jnp.float32(1e9)

    is_group = lane < N_GROUPS
    gl = jnp.where(is_group, logits, neg_inf)
    gmax = jnp.max(gl, axis=-1, keepdims=True)
    gidx = jnp.min(jnp.where(gl == gmax, lane, big), axis=-1, keepdims=True)
    gsum = jnp.sum(jnp.where(is_group, jnp.exp(logits - gmax), 0.0), axis=-1, keepdims=True)
    g_w = 1.0 / gsum

    lo = EXPERT_LANE0 + EXPERTS_PER_GROUP * gidx
    in_group = (lane >= lo) & (lane < lo + EXPERTS_PER_GROUP)
    el = jnp.where(in_group, logits, neg_inf)
    v0 = jnp.max(el, axis=-1, keepdims=True)
    i0 = jnp.min(jnp.where(el == v0, lane, big), axis=-1, keepdims=True)
    el = jnp.where(lane == i0, neg_inf, el)
    v1 = jnp.max(el, axis=-1, keepdims=True)
    i1 = jnp.min(jnp.where(el == v1, lane, big), axis=-1, keepdims=True)
    t = jnp.exp(v1 - v0)
    w0 = g_w / (1.0 + t)
    w1 = g_w * t / (1.0 + t)

    oh0 = jnp.where(lane == i0, 1.0, 0.0)
    oh1 = jnp.where(lane == i1, 1.0, 0.0)
    both = oh0 + oh1
    before = jnp.dot(tri_ref[...], both.astype(BF16), preferred_element_type=F32) + base_ref[...]
    r0 = jnp.sum(before * oh0, axis=-1, keepdims=True)
    r1 = jnp.sum(before * oh1, axis=-1, keepdims=True)
    base_ref[...] = base_ref[...] + jnp.sum(both, axis=0, keepdims=True)
    cnt_ref[...] = base_ref[...]

    e0 = i0 - EXPERT_LANE0
    e1 = i1 - EXPERT_LANE0
    idr = jnp.where(lane == 0, e0, jnp.where(lane == 1, e1, jnp.where(lane == 2, r0, r1)))
    idr_ref[...] = jnp.transpose(idr)[:ROUTE_ROWS].astype(jnp.int32)
    wts_ref[...] = jnp.where(lane == 0, w0, w1)[:, :2]


def _out_router(o, x, mod, g, w, rw, rb):
    tiles_per_batch = SEQ // OUT_TM
    return pl.pallas_call(
        _out_router_kernel,
        grid=(N_TOK // OUT_TM,),
        in_specs=[
            pl.BlockSpec((OUT_TM, D_MODEL), lambda i: (i, 0)),
            pl.BlockSpec((OUT_TM, D_MODEL), lambda i: (i, 0)),
            pl.BlockSpec((1, 6, D_MODEL), lambda i: (i // tiles_per_batch, 0, 0)),
            pl.BlockSpec((1, D_MODEL), lambda i: (0, 0)),
            pl.BlockSpec((D_MODEL, D_MODEL), lambda i: (0, 0)),
            pl.BlockSpec((D_MODEL, LANES), lambda i: (0, 0)),
            pl.BlockSpec((1, LANES), lambda i: (0, 0)),
        ],
        out_specs=[
            pl.BlockSpec((OUT_TM, D_MODEL), lambda i: (i, 0)),
            pl.BlockSpec((OUT_TM, ROW_WORDS), lambda i: (i, 0)),
            pl.BlockSpec((ROUTE_ROWS, OUT_TM), lambda i: (0, i)),
            pl.BlockSpec((OUT_TM, 2), lambda i: (i, 0)),
            pl.BlockSpec((1, LANES), lambda i: (0, 0)),
        ],
        out_shape=[
            jax.ShapeDtypeStruct((N_TOK, D_MODEL), F32),
            jax.ShapeDtypeStruct((N_TOK, ROW_WORDS), jnp.int32),
            jax.ShapeDtypeStruct((ROUTE_ROWS, N_TOK), jnp.int32),
            jax.ShapeDtypeStruct((N_TOK, 2), F32),
            jax.ShapeDtypeStruct((1, LANES), F32),
        ],
        scratch_shapes=[
            pltpu.VMEM((OUT_TM, OUT_TM), BF16),
            pltpu.VMEM((1, LANES), F32),
        ],
        compiler_params=_cparams(("arbitrary",)),
        name="out_router",
    )(o, x, mod, g, w, rw, rb)


def _expert_kernel(ord_ref, active_ref, meta_ref, xs_ref, wg_hbm, wu_hbm, wd_hbm, ys_ref,
                   wg_f32, wu_f32, wd_f32, sems, wg_bf, wu_bf, wd_bf, *, layer):
    i = pl.program_id(0)
    n_used = meta_ref[0]
    n_active = meta_ref[1]

    def fetch(pos, slot):
        e = active_ref[pos]
        return (
            pltpu.make_async_copy(wg_hbm.at[layer, e], wg_f32.at[slot], sems.at[slot, 0]),
            pltpu.make_async_copy(wu_hbm.at[layer, e], wu_f32.at[slot], sems.at[slot, 1]),
            pltpu.make_async_copy(wd_hbm.at[layer, e], wd_f32.at[slot], sems.at[slot, 2]),
        )

    @pl.when(i == 0)
    def _():
        for copy in fetch(0, 0):
            copy.start()

        @pl.when(n_active > 1)
        def _():
            for copy in fetch(1, 1):
                copy.start()

    @pl.when(i < n_used)
    def _():
        pos = ord_ref[i]
        changed = (i == 0) | (pos != ord_ref[jnp.maximum(i - 1, 0)])
        for slot in range(2):
            @pl.when(changed & (pos % 2 == slot))
            def _():
                for copy in fetch(pos, slot):
                    copy.wait()
                wg_bf[...] = wg_f32[slot].astype(BF16)
                wu_bf[...] = wu_f32[slot].astype(BF16)
                wd_bf[...] = wd_f32[slot].astype(BF16)

                @pl.when(pos + 2 < n_active)
                def _():
                    for copy in fetch(pos + 2, slot):
                        copy.start()

        x = _unpack_bf16_pairs(lax.bitcast_convert_type(xs_ref[...], jnp.uint32)).astype(BF16)
        g = jnp.dot(x, wg_bf[...], preferred_element_type=F32)
        u = jnp.dot(x, wu_bf[...], preferred_element_type=F32)
        hid = (g * jax.nn.sigmoid(g)) * u
        y = jnp.dot(hid.astype(BF16), wd_bf[...], preferred_element_type=F32)
        ys_ref[...] = lax.bitcast_convert_type(_pack_bf16_pairs(y), jnp.int32)

    @pl.when(i >= n_used)
    def _():
        ys_ref[...] = jnp.zeros_like(ys_ref)


def _experts(layer, block_pos, active, meta, xs, wg, wu, wd):
    def in_row_map(i, block_pos, active, meta):
        return (jnp.minimum(i, meta[0] - 1), 0)

    def row_map(i, block_pos, active, meta):
        return (i, 0)

    return pl.pallas_call(
        functools.partial(_expert_kernel, layer=layer),
        grid_spec=pltpu.PrefetchScalarGridSpec(
            num_scalar_prefetch=3,
            grid=(N_BLOCKS,),
            in_specs=[
                pl.BlockSpec((MOE_BLK, ROW_WORDS), in_row_map),
                pl.BlockSpec(memory_space=pl.ANY),
                pl.BlockSpec(memory_space=pl.ANY),
                pl.BlockSpec(memory_space=pl.ANY),
            ],
            out_specs=pl.BlockSpec((MOE_BLK, ROW_WORDS), row_map),
            scratch_shapes=[
                pltpu.VMEM((2, D_MODEL, EXPERT_HIDDEN), F32),
                pltpu.VMEM((2, D_MODEL, EXPERT_HIDDEN), F32),
                pltpu.VMEM((2, EXPERT_HIDDEN, D_MODEL), F32),
                pltpu.SemaphoreType.DMA((2, 3)),
                pltpu.VMEM((D_MODEL, EXPERT_HIDDEN), BF16),
                pltpu.VMEM((D_MODEL, EXPERT_HIDDEN), BF16),
                pltpu.VMEM((EXPERT_HIDDEN, D_MODEL), BF16),
            ],
        ),
        out_shape=jax.ShapeDtypeStruct((N_SLOTS, ROW_WORDS), jnp.int32),
        compiler_params=_cparams(("arbitrary",)),
        name="experts",
    )(block_pos, active, meta, xs, wg, wu, wd)


def _sc_gather_rows(table, idx):
    n_rows = idx.shape[0]
    width = table.shape[1]
    workers = SC_CORES * SC_SUBCORES
    per_worker = n_rows // workers
    assert per_worker * workers == n_rows and per_worker % SC_CHUNK == 0
    mesh = plsc.VectorSubcoreMesh(core_axis_name="c", subcore_axis_name="s")

    n_chunks = per_worker // SC_CHUNK

    def body(table_hbm, idx_hbm, out_hbm, idx_v, buf0, buf1, gsem0, gsem1, wsem0, wsem1):
        wid = lax.axis_index("s") * SC_CORES + lax.axis_index("c")
        base = pl.multiple_of(wid * per_worker, SC_CHUNK)
        bufs, gsems, wsems = (buf0, buf1), (gsem0, gsem1), (wsem0, wsem1)
        pltpu.sync_copy(idx_hbm.at[pl.ds(base, per_worker)], idx_v)

        def gather(c):
            rows = idx_v.at[pl.ds(c * SC_CHUNK, SC_CHUNK)]
            return pltpu.async_copy(table_hbm.at[rows], bufs[c % 2], gsems[c % 2])

        def write(c):
            dst = out_hbm.at[pl.ds(base + c * SC_CHUNK, SC_CHUNK)]
            return pltpu.async_copy(bufs[c % 2], dst, wsems[c % 2])

        gathers = {0: gather(0)}
        writes = {}
        for c in range(n_chunks):
            if c + 1 < n_chunks:
                if c >= 1:
                    writes.pop(c - 1).wait()
                gathers[c + 1] = gather(c + 1)
            gathers.pop(c).wait()
            writes[c] = write(c)
        for c in sorted(writes):
            writes[c].wait()

    return pl.kernel(
        body,
        out_type=jax.ShapeDtypeStruct((n_rows, width), table.dtype),
        mesh=mesh,
        scratch_types=[
            pltpu.VMEM((per_worker,), jnp.int32),
            pltpu.VMEM((SC_CHUNK, width), table.dtype),
            pltpu.VMEM((SC_CHUNK, width), table.dtype),
            pltpu.SemaphoreType.DMA,
            pltpu.SemaphoreType.DMA,
            pltpu.SemaphoreType.DMA,
            pltpu.SemaphoreType.DMA,
        ],
        name="sc_gather_rows",
    )(table, idx)


def _sc_slot_tokens(dest_flat):
    n_assign = N_TOK * TOP_K
    lanes = SC_LANES
    assert N_TOK & (N_TOK - 1) == 0
    mesh = plsc.VectorSubcoreMesh(core_axis_name="c", subcore_axis_name="s")

    filler = jnp.arange(N_SLOTS, dtype=jnp.int32) & (N_TOK - 1)

    def body(dest_hbm, filler_hbm, out_hbm, dest_v, slot_v):
        wid = lax.axis_index("s") * SC_CORES + lax.axis_index("c")

        @pl.when(wid == 0)
        def _():
            pltpu.sync_copy(dest_hbm, dest_v)
            pltpu.sync_copy(filler_hbm, slot_v)
            lane = lax.iota(jnp.int32, lanes)

            def place(i, _):
                start = pl.multiple_of(i * lanes, lanes)
                slots = dest_v[pl.ds(start, lanes)]
                plsc.store_scatter(slot_v, [slots], (lane + start) & (N_TOK - 1))
                return 0

            lax.fori_loop(0, n_assign // lanes, place, 0, unroll=8)
            pltpu.sync_copy(slot_v, out_hbm)

    return pl.kernel(
        body,
        out_type=jax.ShapeDtypeStruct((N_SLOTS,), jnp.int32),
        mesh=mesh,
        scratch_types=[
            pltpu.VMEM((n_assign,), jnp.int32),
            pltpu.VMEM((N_SLOTS,), jnp.int32),
        ],
        compiler_params=pltpu.CompilerParams(needs_layout_passes=False),
        name="sc_slot_tokens",
    )(dest_flat, filler)


def _mix_kernel(x_ref, wts_ref, mod_ref, fg_ref, y0_ref, y1_ref, o_ref):
    out = _expert_mix(x_ref[...], wts_ref[...], mod_ref[0][5:6], y0_ref[...], y1_ref[...])
    ms = jnp.mean(out * out, axis=-1, keepdims=True)
    o_ref[...] = out * lax.rsqrt(ms + RMS_EPS) * fg_ref[...]


def _mix(x1, wts, mod, y2, fg):
    nt = N_TOK // COMB_TM
    tiles_per_batch = SEQ // COMB_TM
    return pl.pallas_call(
        _mix_kernel,
        grid=(nt,),
        in_specs=[
            pl.BlockSpec((COMB_TM, D_MODEL), lambda i: (i, 0)),
            pl.BlockSpec((COMB_TM, TOP_K), lambda i: (i, 0)),
            pl.BlockSpec((1, 6, D_MODEL), lambda i: (i // tiles_per_batch, 0, 0)),
            pl.BlockSpec((1, D_MODEL), lambda i: (0, 0)),
            pl.BlockSpec((COMB_TM, ROW_WORDS), lambda i: (i, 0)),
            pl.BlockSpec((COMB_TM, ROW_WORDS), lambda i: (nt + i, 0)),
        ],
        out_specs=pl.BlockSpec((COMB_TM, D_MODEL), lambda i: (i, 0)),
        out_shape=jax.ShapeDtypeStruct((N_TOK, D_MODEL), F32),
        compiler_params=_cparams(("arbitrary",)),
        name="mix",
    )(x1, wts, mod, fg, y2, y2)


def _routing_tables(idr, cnt):
    counts = cnt[0, EXPERT_LANE0:EXPERT_LANE0 + N_EXPERTS].astype(jnp.int32)
    padded = (counts + MOE_BLK - 1) // MOE_BLK * MOE_BLK
    pad_ends = jnp.cumsum(padded)
    pad_starts = pad_ends - padded
    experts = jnp.arange(N_EXPERTS, dtype=jnp.int32)
    is_expert = idr[None, 0:2, :] == experts[:, None, None]
    dest = jnp.sum(jnp.where(is_expert, pad_starts[:, None, None], 0), axis=0) + idr[2:4, :]
    block_start = jnp.arange(N_BLOCKS, dtype=jnp.int32) * MOE_BLK
    bexp = jnp.minimum(jnp.sum(block_start[:, None] >= pad_ends[None, :], axis=1),
                       N_EXPERTS - 1).astype(jnp.int32)
    owns = padded > 0
    pos_of_expert = jnp.cumsum(owns.astype(jnp.int32)) - 1
    active = jnp.sum(jnp.where(owns[None, :] & (pos_of_expert[None, :] == experts[:, None]),
                               experts[None, :], 0), axis=1).astype(jnp.int32)
    block_pos = jnp.sum(jnp.where(bexp[:, None] == experts[None, :], pos_of_expert[None, :], 0),
                        axis=1).astype(jnp.int32)
    meta = jnp.stack([pad_ends[-1] // MOE_BLK, jnp.sum(owns)]).astype(jnp.int32)
    return dest.astype(jnp.int32), block_pos, active, meta


def _router_weights(w_group, b_group, w_expert, b_expert):
    rw = jnp.zeros((D_MODEL, LANES), F32)
    rw = rw.at[:, :N_GROUPS].set(w_group)
    rw = rw.at[:, EXPERT_LANE0:EXPERT_LANE0 + N_EXPERTS].set(w_expert)
    rb = jnp.zeros((1, LANES), F32)
    rb = rb.at[0, :N_GROUPS].set(b_group)
    rb = rb.at[0, EXPERT_LANE0:EXPERT_LANE0 + N_EXPERTS].set(b_expert.reshape(-1))
    return rw.astype(BF16), rb


def _scaled_qkv_weight(w_in, head_dim):
    scale = jnp.concatenate([jnp.full((D_MODEL,), LOG2_E / math.sqrt(head_dim), F32),
                             jnp.ones((2 * D_MODEL,), F32)])
    return (w_in * scale).astype(BF16)


def kernel(x, c, norm1_g, norm2_g, ada_w, ada_b, diff_w_in, diff_w_out, diff_lambda_q1, diff_lambda_k1, diff_lambda_q2, diff_lambda_k2, diff_subln_g, sb_w_in, sb_w_out, router_group_w, router_group_b, router_expert_w, router_expert_b, expert_w_gate, expert_w_up, expert_w_down, final_norm_g):
    xf = x.reshape(N_TOK, D_MODEL)
    mod_all = _ada(c, ada_w, ada_b)
    slopes = jnp.exp2(-8.0 * jnp.arange(1, DIFF_HEADS + 1, dtype=F32) / DIFF_HEADS)

    pending = None
    for i in range(DEPTH):
        mod = mod_all[i].reshape(BATCH, 6, D_MODEL)
        j = i // 2
        is_diff = i % 2 == 0
        w_in = (_scaled_qkv_weight(diff_w_in[j], DIFF_HEAD_DIM) if is_diff
                else _scaled_qkv_weight(sb_w_in[j], SB_HEAD_DIM))
        g1 = norm1_g[i].reshape(1, D_MODEL)
        if pending is None:
            qkv = _qkv(xf, mod, g1, w_in)
        else:
            xf, qkv = _mix_qkv(*pending, mod, g1, w_in)
        if is_diff:
            lambda_init = 0.8 - 0.6 * math.exp(-0.3 * i)
            lam = (jnp.exp(jnp.sum(diff_lambda_q1[j] * diff_lambda_k1[j]))
                   - jnp.exp(jnp.sum(diff_lambda_q2[j] * diff_lambda_k2[j]))
                   + lambda_init).reshape(1, 1)
            o = _diff_attn(qkv, slopes, lam, diff_subln_g[j].reshape(1, LANES), lambda_init)
            w_out = diff_w_out[j]
        else:
            o = _sb_attn(qkv)
            w_out = sb_w_out[j]
        rw, rb = _router_weights(router_group_w[i], router_group_b[i],
                                 router_expert_w[i], router_expert_b[i])
        x1, h2, idr, wts, cnt = _out_router(o, xf, mod, norm2_g[i].reshape(1, D_MODEL),
                                            w_out.astype(BF16), rw, rb)
        dest, block_pos, active, meta = _routing_tables(idr, cnt)
        dest_flat = dest.reshape(-1)
        xs = _sc_gather_rows(h2, _sc_slot_tokens(dest_flat))
        ys = _experts(i, block_pos, active, meta, xs,
                      expert_w_gate, expert_w_up, expert_w_down)
        y2 = _sc_gather_rows(ys, dest_flat)
        pending = (x1, wts, mod, y2)
    out = _mix(*pending, final_norm_g.reshape(1, D_MODEL))
    return out.reshape(BATCH, SEQ, D_MODEL)
```

```python
import functools
import math

import jax
import jax.numpy as jnp
from jax import lax
from jax.experimental import pallas as pl
from jax.experimental.pallas import tpu as pltpu
from jax.experimental.pallas import tpu_sc as plsc

D_MODEL = 1024
BATCH = 8
SEQ = 2048
DEPTH = 2
N_TOK = BATCH * SEQ

CHUNK = 64
DIFF_HEADS = 8
DIFF_HEAD_DIM = D_MODEL // (2 * DIFF_HEADS)
SB_HEADS = 16
SB_HEAD_DIM = D_MODEL // SB_HEADS
N_GROUPS = 4
EXPERTS_PER_GROUP = 8
N_EXPERTS = N_GROUPS * EXPERTS_PER_GROUP
TOP_K = 2
EXPERT_HIDDEN = D_MODEL // 2
RMS_EPS = 1e-6
SUBLN_EPS = 1e-5

LANES = 128
EXPERT_LANE0 = 32
ROUTE_ROWS = 8
EXP2_UNDERFLOW = -150.0
LOG2_E = math.log2(math.e)

ADA_TN = 1536
QKV_TM = 512
DIFF_T = 512
DIFF_GROUP = 4
SB_T = 256
SB_PAIRS = 4
OUT_TM = 1024
MOE_BLK = 512
N_SLOTS = N_TOK * TOP_K + N_EXPERTS * MOE_BLK
N_BLOCKS = N_SLOTS // MOE_BLK
COMB_TM = 512
ROW_WORDS = D_MODEL // 2

SC_CORES = 2
SC_SUBCORES = 16
SC_LANES = 16
SC_CHUNK = 64

VMEM_LIMIT = 56 * 1024 * 1024

F32 = jnp.float32
BF16 = jnp.bfloat16


def _cparams(sem):
    return pltpu.CompilerParams(dimension_semantics=sem, vmem_limit_bytes=VMEM_LIMIT)


def _ada_kernel(c_ref, w_ref, b_ref, o_ref):
    c = c_ref[...]
    cond = c * jax.nn.sigmoid(c)
    o_ref[0] = jnp.dot(cond.astype(BF16), w_ref[0].astype(BF16),
                       preferred_element_type=F32) + b_ref[0]


def _ada(c, ada_w, ada_b):
    six_d = ada_w.shape[-1]
    return pl.pallas_call(
        _ada_kernel,
        grid=(DEPTH, six_d // ADA_TN),
        in_specs=[
            pl.BlockSpec((BATCH, D_MODEL), lambda l, n: (0, 0)),
            pl.BlockSpec((1, D_MODEL, ADA_TN), lambda l, n: (l, 0, n)),
            pl.BlockSpec((1, 1, ADA_TN), lambda l, n: (l, 0, n)),
        ],
        out_specs=pl.BlockSpec((1, BATCH, ADA_TN), lambda l, n: (l, 0, n)),
        out_shape=jax.ShapeDtypeStruct((DEPTH, BATCH, six_d), F32),
        compiler_params=_cparams(("arbitrary", "arbitrary")),
        name="ada",
    )(c, ada_w, ada_b.reshape(DEPTH, 1, six_d))


def _modulated_norm(x, g, scale, shift):
    ms = jnp.mean(x * x, axis=-1, keepdims=True)
    return x * lax.rsqrt(ms + RMS_EPS) * (g * (1.0 + scale)) + shift


def _qkv_kernel(x_ref, mod_ref, g_ref, w_ref, o_ref):
    m = mod_ref[0]
    h = _modulated_norm(x_ref[...], g_ref[...], m[1:2], m[0:1]).astype(BF16)
    for n in range(3):
        cols = slice(n * D_MODEL, (n + 1) * D_MODEL)
        o_ref[:, cols] = jnp.dot(h, w_ref[:, cols], preferred_element_type=F32).astype(BF16)


def _qkv(x, mod, g, w):
    tiles_per_batch = SEQ // QKV_TM
    return pl.pallas_call(
        _qkv_kernel,
        grid=(N_TOK // QKV_TM,),
        in_specs=[
            pl.BlockSpec((QKV_TM, D_MODEL), lambda i: (i, 0)),
            pl.BlockSpec((1, 6, D_MODEL), lambda i: (i // tiles_per_batch, 0, 0)),
            pl.BlockSpec((1, D_MODEL), lambda i: (0, 0)),
            pl.BlockSpec((D_MODEL, 3 * D_MODEL), lambda i: (0, 0)),
        ],
        out_specs=pl.BlockSpec((QKV_TM, 3 * D_MODEL), lambda i: (i, 0)),
        out_shape=jax.ShapeDtypeStruct((N_TOK, 3 * D_MODEL), BF16),
        compiler_params=_cparams(("arbitrary",)),
        name="qkv",
    )(x, mod, g, w)


def _expert_mix(x, wts, gate, y0_words, y1_words):
    y0 = _unpack_bf16_pairs(lax.bitcast_convert_type(y0_words, jnp.uint32))
    y1 = _unpack_bf16_pairs(lax.bitcast_convert_type(y1_words, jnp.uint32))
    return x + gate * (wts[:, 0:1] * y0 + wts[:, 1:2] * y1)


def _mix_qkv_kernel(x1_ref, wts_ref, pmod_ref, y0_ref, y1_ref, mod_ref, g_ref, w_ref,
                    x_ref, o_ref):
    x = _expert_mix(x1_ref[...], wts_ref[...], pmod_ref[0][5:6], y0_ref[...], y1_ref[...])
    x_ref[...] = x
    m = mod_ref[0]
    h = _modulated_norm(x, g_ref[...], m[1:2], m[0:1]).astype(BF16)
    for n in range(3):
        cols = slice(n * D_MODEL, (n + 1) * D_MODEL)
        o_ref[:, cols] = jnp.dot(h, w_ref[:, cols], preferred_element_type=F32).astype(BF16)


def _mix_qkv(x1, wts, prev_mod, y2, mod, g, w):
    nt = N_TOK // QKV_TM
    tiles_per_batch = SEQ // QKV_TM
    return pl.pallas_call(
        _mix_qkv_kernel,
        grid=(nt,),
        in_specs=[
            pl.BlockSpec((QKV_TM, D_MODEL), lambda i: (i, 0)),
            pl.BlockSpec((QKV_TM, TOP_K), lambda i: (i, 0)),
            pl.BlockSpec((1, 6, D_MODEL), lambda i: (i // tiles_per_batch, 0, 0)),
            pl.BlockSpec((QKV_TM, ROW_WORDS), lambda i: (i, 0)),
            pl.BlockSpec((QKV_TM, ROW_WORDS), lambda i: (nt + i, 0)),
            pl.BlockSpec((1, 6, D_MODEL), lambda i: (i // tiles_per_batch, 0, 0)),
            pl.BlockSpec((1, D_MODEL), lambda i: (0, 0)),
            pl.BlockSpec((D_MODEL, 3 * D_MODEL), lambda i: (0, 0)),
        ],
        out_specs=[
            pl.BlockSpec((QKV_TM, D_MODEL), lambda i: (i, 0)),
            pl.BlockSpec((QKV_TM, 3 * D_MODEL), lambda i: (i, 0)),
        ],
        out_shape=[
            jax.ShapeDtypeStruct((N_TOK, D_MODEL), F32),
            jax.ShapeDtypeStruct((N_TOK, 3 * D_MODEL), BF16),
        ],
        compiler_params=_cparams(("arbitrary",)),
        name="mix_qkv",
    )(x1, wts, prev_mod, y2, y2, mod, g, w)


def _half_masked(q, upper):
    lane = lax.broadcasted_iota(jnp.int32, q.shape, 1)
    keep = (lane >= LANES // 2) if upper else (lane < LANES // 2)
    return jnp.where(keep, q, jnp.zeros_like(q))


def _pack_bf16_pairs(x):
    half = x.shape[1] // 2
    hi = lax.bitcast_convert_type(x[:, :half].astype(BF16).astype(F32), jnp.uint32)
    lo = lax.bitcast_convert_type(x[:, half:].astype(BF16).astype(F32), jnp.uint32)
    return hi | (lo >> 16)


def _unpack_bf16_pairs(u):
    hi = lax.bitcast_convert_type(u & jnp.uint32(0xFFFF0000), F32)
    lo = lax.bitcast_convert_type(u << 16, F32)
    return jnp.concatenate([hi, lo], axis=1)


def _lane_tile(x, n):
    return jnp.concatenate([x] * n, axis=1)


def _qk(q, k):
    return lax.dot_general(q, k, (((1,), (1,)), ((), ())), preferred_element_type=F32)


def _diff_attn_kernel(slopes_ref, lam_ref, q_ref, k_ref, v_ref, g_ref, o_ref, *scratch,
                      lambda_init):
    n_chains = 2 * DIFF_GROUP
    m_refs = scratch[:n_chains]
    acc_refs = scratch[n_chains:]
    ones = jnp.ones((DIFF_T, LANES), BF16)

    def head_lanes(c):
        return slice((c // 2) * LANES, (c // 2 + 1) * LANES)

    row = lax.broadcasted_iota(jnp.int32, (DIFF_T, DIFF_T), 0)
    col = lax.broadcasted_iota(jnp.int32, (DIFF_T, DIFF_T), 1)
    allowed = (col // CHUNK) <= (row // CHUNK)
    slopes = [slopes_ref[pl.program_id(1) * DIFF_GROUP + g] * LOG2_E for g in range(DIFF_GROUP)]
    rels = [(row - col).astype(F32) * slope for slope in slopes]
    diag_biases = [jnp.where(allowed, jnp.abs(rel), jnp.inf) for rel in rels]
    out_gain = g_ref[...] * (1.0 - lambda_init)

    def key_block(j, c):
        k = k_ref[pl.ds(j * DIFF_T, DIFF_T), head_lanes(c)]
        v1 = jnp.concatenate([v_ref[pl.ds(j * DIFF_T, DIFF_T), head_lanes(c)], ones], axis=1)
        return k, v1

    def query_tile(qi, _):
        q = q_ref[pl.ds(qi * DIFF_T, DIFF_T), :]
        qm = [_half_masked(q[:, head_lanes(c)], c % 2 == 1) for c in range(n_chains)]

        kv = [key_block(qi, c) for c in range(n_chains)]
        s_maps = [_qk(qm[c], kv[c][0]) - diag_biases[c // 2] for c in range(n_chains)]
        for c in range(n_chains):
            m = jnp.max(s_maps[c], axis=-1, keepdims=True)
            p = jnp.exp2(s_maps[c] - m)
            m_refs[c][...] = jnp.broadcast_to(m, (DIFF_T, LANES))
            acc_refs[c][...] = jnp.dot(p.astype(BF16), kv[c][1], preferred_element_type=F32)

        def body(j, _):
            kv = [key_block(j, c) for c in range(n_chains)]
            ahead = jnp.asarray((qi - j) * DIFF_T).astype(F32)
            s_maps = [_qk(qm[c], kv[c][0]) - rels[c // 2] for c in range(n_chains)]
            for c in range(n_chains):
                shift = slopes[c // 2] * ahead
                m = m_refs[c][...]
                m_new = jnp.maximum(m, jnp.max(s_maps[c], axis=-1, keepdims=True) - shift)
                alpha = jnp.exp2(m - m_new)
                p = jnp.exp2(s_maps[c] - _lane_tile(m_new + shift, DIFF_T // LANES))
                m_refs[c][...] = m_new
                acc_refs[c][...] = _lane_tile(alpha, 2) * acc_refs[c][...] + jnp.dot(
                    p.astype(BF16), kv[c][1], preferred_element_type=F32)
            return 0

        lax.fori_loop(0, qi, body, 0)
        for g in range(DIFF_GROUP):
            a0 = acc_refs[2 * g][...]
            a1 = acc_refs[2 * g + 1][...]
            o = a0[:, :LANES] / a0[:, LANES:] - lam_ref[...] * (a1[:, :LANES] / a1[:, LANES:])
            ms = jnp.mean(o * o, axis=-1, keepdims=True)
            o = o * lax.rsqrt(ms + SUBLN_EPS) * out_gain
            o_ref[pl.ds(qi * DIFF_T, DIFF_T), head_lanes(2 * g)] = o.astype(BF16)
        return 0

    lax.fori_loop(0, SEQ // DIFF_T, query_tile, 0)


def _diff_attn(qkv, slopes, lam, subln_g, lambda_init):
    groups = DIFF_HEADS // DIFF_GROUP
    width = DIFF_GROUP * LANES
    return pl.pallas_call(
        functools.partial(_diff_attn_kernel, lambda_init=lambda_init),
        grid=(BATCH, groups),
        in_specs=[
            pl.BlockSpec(memory_space=pltpu.SMEM),
            pl.BlockSpec((1, 1), lambda b, h: (0, 0)),
            pl.BlockSpec((SEQ, width), lambda b, h: (b, h)),
            pl.BlockSpec((SEQ, width), lambda b, h: (b, groups + h)),
            pl.BlockSpec((SEQ, width), lambda b, h: (b, 2 * groups + h)),
            pl.BlockSpec((1, LANES), lambda b, h: (0, 0)),
        ],
        out_specs=pl.BlockSpec((SEQ, width), lambda b, h: (b, h)),
        out_shape=jax.ShapeDtypeStruct((N_TOK, D_MODEL), BF16),
        scratch_shapes=([pltpu.VMEM((DIFF_T, LANES), F32)] * (2 * DIFF_GROUP)
                        + [pltpu.VMEM((DIFF_T, 2 * LANES), F32)] * (2 * DIFF_GROUP)),
        compiler_params=_cparams(("arbitrary", "arbitrary")),
        name="diff_attn",
    )(slopes, lam, qkv, qkv, qkv, subln_g)


def _sb_attn_kernel(q_ref, k_ref, v_ref, o_ref, *scratch):
    n_heads = 2 * SB_PAIRS
    tail_refs = scratch[:n_heads]
    acc_refs = scratch[n_heads:]

    def pair_lanes(h):
        return slice((h // 2) * LANES, (h // 2 + 1) * LANES)

    row = lax.broadcasted_iota(jnp.int32, (SB_T, SB_T), 0)
    col = lax.broadcasted_iota(jnp.int32, (SB_T, SB_T), 1)
    strict = col < row
    neg_from = jnp.where(row >= col, -1.0, 0.0).astype(BF16)

    def scores(j, h, qhead, mask):
        z = _qk(qhead, k_ref[pl.ds(j * SB_T, SB_T), pair_lanes(h)])
        sp = jnp.maximum(z, 0.0) + jnp.log2(1.0 + jnp.exp2(-jnp.abs(z)))
        if mask is not None:
            sp = jnp.where(mask, sp, 0.0)
        return z, sp

    def weighted(j, h, z, sp, tail, mask):
        log_a = (z + jnp.dot(sp.astype(BF16), neg_from, preferred_element_type=F32)
                 + _lane_tile(tail, SB_T // LANES))
        a = jnp.exp2(log_a)
        if mask is not None:
            a = jnp.where(mask, a, 0.0)
        return jnp.dot(a.astype(BF16), v_ref[pl.ds(j * SB_T, SB_T), pair_lanes(h)],
                       preferred_element_type=F32)

    def row_sum(sp):
        return jnp.broadcast_to(jnp.sum(sp, axis=-1, keepdims=True), (SB_T, LANES))

    zero = jnp.zeros((SB_T, LANES), F32)
    lane = lax.broadcasted_iota(jnp.int32, (SB_T, LANES), 1)

    def query_heads(qi):
        q = q_ref[pl.ds(qi * SB_T, SB_T), :]
        return [_half_masked(q[:, pair_lanes(h)], h % 2 == 1) for h in range(n_heads)]

    def finish(qi):
        for h in range(0, n_heads, 2):
            o_ref[pl.ds(qi * SB_T, SB_T), pair_lanes(h)] = jnp.where(
                lane < LANES // 2, acc_refs[h][...], acc_refs[h + 1][...]).astype(BF16)

    qh = query_heads(0)
    for h in range(n_heads):
        z, sp = scores(0, h, qh[h], strict)
        acc_refs[h][...] = weighted(0, h, z, sp, zero, strict)
    finish(0)

    def query_tile(qi, _):
        qh = query_heads(qi)
        zs = [(scores(qi, h, qh[h], strict), scores(qi - 1, h, qh[h], None))
              for h in range(n_heads)]
        for h in range(n_heads):
            (z0, sp0), (z1, sp1) = zs[h]
            tail0 = -row_sum(sp0)
            acc_refs[h][...] = (weighted(qi, h, z0, sp0, zero, strict)
                                + weighted(qi - 1, h, z1, sp1, tail0, None))
            tail_refs[h][...] = tail0 - row_sum(sp1)

        def cond(state):
            j, live = state
            return (j >= 0) & (functools.reduce(jnp.maximum, live) > EXP2_UNDERFLOW)

        def body(state):
            j, live = state

            def advance(h):
                z, sp = scores(j, h, qh[h], None)
                tail = tail_refs[h][...]
                acc_refs[h][...] += weighted(j, h, z, sp, tail, None)
                tail = tail - row_sum(sp)
                tail_refs[h][...] = tail
                return jnp.max(tail)

            live = tuple(
                lax.cond(live[h] > EXP2_UNDERFLOW, functools.partial(advance, h),
                         functools.partial(lambda done: done, live[h]))
                for h in range(n_heads))
            return j - 1, live

        live = tuple(jnp.max(t[...]) for t in tail_refs)
        lax.while_loop(cond, body, (qi - 2, live))
        finish(qi)
        return 0

    lax.fori_loop(1, SEQ // SB_T, query_tile, 0)


def _sb_attn(qkv):
    groups = SB_HEADS // (2 * SB_PAIRS)
    width = SB_PAIRS * LANES
    return pl.pallas_call(
        _sb_attn_kernel,
        grid=(BATCH, groups),
        in_specs=[
            pl.BlockSpec((SEQ, width), lambda b, h: (b, h)),
            pl.BlockSpec((SEQ, width), lambda b, h: (b, groups + h)),
            pl.BlockSpec((SEQ, width), lambda b, h: (b, 2 * groups + h)),
        ],
        out_specs=pl.BlockSpec((SEQ, width), lambda b, h: (b, h)),
        out_shape=jax.ShapeDtypeStruct((N_TOK, D_MODEL), BF16),
        scratch_shapes=[pltpu.VMEM((SB_T, LANES), F32)] * (4 * SB_PAIRS),
        compiler_params=_cparams(("arbitrary", "arbitrary")),
        name="sb_attn",
    )(qkv, qkv, qkv)


def _out_router_kernel(o_ref, x_ref, mod_ref, g_ref, w_ref, rw_ref, rb_ref,
                       x1_ref, h2_ref, idr_ref, wts_ref, cnt_ref, tri_ref, base_ref):
    i = pl.program_id(0)

    @pl.when(i == 0)
    def _():
        r = lax.broadcasted_iota(jnp.int32, (OUT_TM, OUT_TM), 0)
        c = lax.broadcasted_iota(jnp.int32, (OUT_TM, OUT_TM), 1)
        tri_ref[...] = jnp.where(c < r, 1.0, 0.0).astype(BF16)
        base_ref[...] = jnp.zeros_like(base_ref)

    m = mod_ref[0]
    y = jnp.dot(o_ref[...], w_ref[...], preferred_element_type=F32)
    x1 = x_ref[...] + m[2:3] * y
    x1_ref[...] = x1
    h2 = _modulated_norm(x1, g_ref[...], m[4:5], m[3:4])
    h2_ref[...] = lax.bitcast_convert_type(_pack_bf16_pairs(h2), jnp.int32)

    logits = jnp.dot(h2.astype(BF16), rw_ref[...], preferred_element_type=F32) + rb_ref[...]
    lane = lax.broadcasted_iota(jnp.int32, logits.shape, 1).astype(F32)
    neg_inf = jnp.float32(-jnp.inf)
    big = jnp.float32(1e9)

    is_group = lane < N_GROUPS
    gl = jnp.where(is_group, logits, neg_inf)
    gmax = jnp.max(gl, axis=-1, keepdims=True)
    gidx = jnp.min(jnp.where(gl == gmax, lane, big), axis=-1, keepdims=True)
    gsum = jnp.sum(jnp.where(is_group, jnp.exp(logits - gmax), 0.0), axis=-1, keepdims=True)
    g_w = 1.0 / gsum

    lo = EXPERT_LANE0 + EXPERTS_PER_GROUP * gidx
    in_group = (lane >= lo) & (lane < lo + EXPERTS_PER_GROUP)
    el = jnp.where(in_group, logits, neg_inf)
    v0 = jnp.max(el, axis=-1, keepdims=True)
    i0 = jnp.min(jnp.where(el == v0, lane, big), axis=-1, keepdims=True)
    el = jnp.where(lane == i0, neg_inf, el)
    v1 = jnp.max(el, axis=-1, keepdims=True)
    i1 = jnp.min(jnp.where(el == v1, lane, big), axis=-1, keepdims=True)
    t = jnp.exp(v1 - v0)
    w0 = g_w / (1.0 + t)
    w1 = g_w * t / (1.0 + t)

    oh0 = jnp.where(lane == i0, 1.0, 0.0)
    oh1 = jnp.where(lane == i1, 1.0, 0.0)
    both = oh0 + oh1
    before = jnp.dot(tri_ref[...], both.astype(BF16), preferred_element_type=F32) + base_ref[...]
    r0 = jnp.sum(before * oh0, axis=-1, keepdims=True)
    r1 = jnp.sum(before * oh1, axis=-1, keepdims=True)
    base_ref[...] = base_ref[...] + jnp.sum(both, axis=0, keepdims=True)
    cnt_ref[...] = base_ref[...]

    e0 = i0 - EXPERT_LANE0
    e1 = i1 - EXPERT_LANE0
    idr = jnp.where(lane == 0, e0, jnp.where(lane == 1, e1, jnp.where(lane == 2, r0, r1)))
    idr_ref[...] = jnp.transpose(idr)[:ROUTE_ROWS].astype(jnp.int32)
    wts_ref[...] = jnp.where(lane == 0, w0, w1)[:, :2]


def _out_router(o, x, mod, g, w, rw, rb):
    tiles_per_batch = SEQ // OUT_TM
    return pl.pallas_call(
        _out_router_kernel,
        grid=(N_TOK // OUT_TM,),
        in_specs=[
            pl.BlockSpec((OUT_TM, D_MODEL), lambda i: (i, 0)),
            pl.BlockSpec((OUT_TM, D_MODEL), lambda i: (i, 0)),
            pl.BlockSpec((1, 6, D_MODEL), lambda i: (i // tiles_per_batch, 0, 0)),
            pl.BlockSpec((1, D_MODEL), lambda i: (0, 0)),
            pl.BlockSpec((D_MODEL, D_MODEL), lambda i: (0, 0)),
            pl.BlockSpec((D_MODEL, LANES), lambda i: (0, 0)),
            pl.BlockSpec((1, LANES), lambda i: (0, 0)),
        ],
        out_specs=[
            pl.BlockSpec((OUT_TM, D_MODEL), lambda i: (i, 0)),
            pl.BlockSpec((OUT_TM, ROW_WORDS), lambda i: (i, 0)),
            pl.BlockSpec((ROUTE_ROWS, OUT_TM), lambda i: (0, i)),
            pl.BlockSpec((OUT_TM, 2), lambda i: (i, 0)),
            pl.BlockSpec((1, LANES), lambda i: (0, 0)),
        ],
        out_shape=[
            jax.ShapeDtypeStruct((N_TOK, D_MODEL), F32),
            jax.ShapeDtypeStruct((N_TOK, ROW_WORDS), jnp.int32),
            jax.ShapeDtypeStruct((ROUTE_ROWS, N_TOK), jnp.int32),
            jax.ShapeDtypeStruct((N_TOK, 2), F32),
            jax.ShapeDtypeStruct((1, LANES), F32),
        ],
        scratch_shapes=[
            pltpu.VMEM((OUT_TM, OUT_TM), BF16),
            pltpu.VMEM((1, LANES), F32),
        ],
        compiler_params=_cparams(("arbitrary",)),
        name="out_router",
    )(o, x, mod, g, w, rw, rb)


def _expert_kernel(ord_ref, active_ref, meta_ref, xs_ref, wg_hbm, wu_hbm, wd_hbm, ys_ref,
                   wg_f32, wu_f32, wd_f32, sems, wg_bf, wu_bf, wd_bf, *, layer):
    i = pl.program_id(0)
    n_used = meta_ref[0]
    n_active = meta_ref[1]

    def fetch(pos, slot):
        e = active_ref[pos]
        return (
            pltpu.make_async_copy(wg_hbm.at[layer, e], wg_f32.at[slot], sems.at[slot, 0]),
            pltpu.make_async_copy(wu_hbm.at[layer, e], wu_f32.at[slot], sems.at[slot, 1]),
            pltpu.make_async_copy(wd_hbm.at[layer, e], wd_f32.at[slot], sems.at[slot, 2]),
        )

    @pl.when(i == 0)
    def _():
        for copy in fetch(0, 0):
            copy.start()

        @pl.when(n_active > 1)
        def _():
            for copy in fetch(1, 1):
                copy.start()

    @pl.when(i < n_used)
    def _():
        pos = ord_ref[i]
        changed = (i == 0) | (pos != ord_ref[jnp.maximum(i - 1, 0)])
        for slot in range(2):
            @pl.when(changed & (pos % 2 == slot))
            def _():
                for copy in fetch(pos, slot):
                    copy.wait()
                wg_bf[...] = wg_f32[slot].astype(BF16)
                wu_bf[...] = wu_f32[slot].astype(BF16)
                wd_bf[...] = wd_f32[slot].astype(BF16)

                @pl.when(pos + 2 < n_active)
                def _():
                    for copy in fetch(pos + 2, slot):
                        copy.start()

        x = _unpack_bf16_pairs(lax.bitcast_convert_type(xs_ref[...], jnp.uint32)).astype(BF16)
        g = jnp.dot(x, wg_bf[...], preferred_element_type=F32)
        u = jnp.dot(x, wu_bf[...], preferred_element_type=F32)
        hid = (g * jax.nn.sigmoid(g)) * u
        y = jnp.dot(hid.astype(BF16), wd_bf[...], preferred_element_type=F32)
        ys_ref[...] = lax.bitcast_convert_type(_pack_bf16_pairs(y), jnp.int32)

    @pl.when(i >= n_used)
    def _():
        ys_ref[...] = jnp.zeros_like(ys_ref)


def _experts(layer, block_pos, active, meta, xs, wg, wu, wd):
    def in_row_map(i, block_pos, active, meta):
        return (jnp.minimum(i, meta[0] - 1), 0)

    def row_map(i, block_pos, active, meta):
        return (i, 0)

    return pl.pallas_call(
        functools.partial(_expert_kernel, layer=layer),
        grid_spec=pltpu.PrefetchScalarGridSpec(
            num_scalar_prefetch=3,
            grid=(N_BLOCKS,),
            in_specs=[
                pl.BlockSpec((MOE_BLK, ROW_WORDS), in_row_map),
                pl.BlockSpec(memory_space=pl.ANY),
                pl.BlockSpec(memory_space=pl.ANY),
                pl.BlockSpec(memory_space=pl.ANY),
            ],
            out_specs=pl.BlockSpec((MOE_BLK, ROW_WORDS), row_map),
            scratch_shapes=[
                pltpu.VMEM((2, D_MODEL, EXPERT_HIDDEN), F32),
                pltpu.VMEM((2, D_MODEL, EXPERT_HIDDEN), F32),
                pltpu.VMEM((2, EXPERT_HIDDEN, D_MODEL), F32),
                pltpu.SemaphoreType.DMA((2, 3)),
                pltpu.VMEM((D_MODEL, EXPERT_HIDDEN), BF16),
                pltpu.VMEM((D_MODEL, EXPERT_HIDDEN), BF16),
                pltpu.VMEM((EXPERT_HIDDEN, D_MODEL), BF16),
            ],
        ),
        out_shape=jax.ShapeDtypeStruct((N_SLOTS, ROW_WORDS), jnp.int32),
        compiler_params=_cparams(("arbitrary",)),
        name="experts",
    )(block_pos, active, meta, xs, wg, wu, wd)


def _sc_gather_rows(table, idx):
    n_rows = idx.shape[0]
    width = table.shape[1]
    workers = SC_CORES * SC_SUBCORES
    per_worker = n_rows // workers
    assert per_worker * workers == n_rows and per_worker % SC_CHUNK == 0
    mesh = plsc.VectorSubcoreMesh(core_axis_name="c", subcore_axis_name="s")

    n_chunks = per_worker // SC_CHUNK

    def body(table_hbm, idx_hbm, out_hbm, idx_v, buf0, buf1, gsem0, gsem1, wsem0, wsem1):
        wid = lax.axis_index("s") * SC_CORES + lax.axis_index("c")
        base = pl.multiple_of(wid * per_worker, SC_CHUNK)
        bufs, gsems, wsems = (buf0, buf1), (gsem0, gsem1), (wsem0, wsem1)
        pltpu.sync_copy(idx_hbm.at[pl.ds(base, per_worker)], idx_v)

        def gather(c):
            rows = idx_v.at[pl.ds(c * SC_CHUNK, SC_CHUNK)]
            return pltpu.async_copy(table_hbm.at[rows], bufs[c % 2], gsems[c % 2])

        def write(c):
            dst = out_hbm.at[pl.ds(base + c * SC_CHUNK, SC_CHUNK)]
            return pltpu.async_copy(bufs[c % 2], dst, wsems[c % 2])

        gathers = {0: gather(0)}
        writes = {}
        for c in range(n_chunks):
            if c + 1 < n_chunks:
                if c >= 1:
                    writes.pop(c - 1).wait()
                gathers[c + 1] = gather(c + 1)
            gathers.pop(c).wait()
            writes[c] = write(c)
        for c in sorted(writes):
            writes[c].wait()

    return pl.kernel(
        body,
        out_type=jax.ShapeDtypeStruct((n_rows, width), table.dtype),
        mesh=mesh,
        scratch_types=[
            pltpu.VMEM((per_worker,), jnp.int32),
            pltpu.VMEM((SC_CHUNK, width), table.dtype),
            pltpu.VMEM((SC_CHUNK, width), table.dtype),
            pltpu.SemaphoreType.DMA,
            pltpu.SemaphoreType.DMA,
            pltpu.SemaphoreType.DMA,
            pltpu.SemaphoreType.DMA,
        ],
        name="sc_gather_rows",
    )(table, idx)


def _sc_slot_tokens(dest_flat):
    n_assign = N_TOK * TOP_K
    lanes = SC_LANES
    assert N_TOK & (N_TOK - 1) == 0
    mesh = plsc.VectorSubcoreMesh(core_axis_name="c", subcore_axis_name="s")

    filler = jnp.arange(N_SLOTS, dtype=jnp.int32) & (N_TOK - 1)

    def body(dest_hbm, filler_hbm, out_hbm, dest_v, slot_v):
        wid = lax.axis_index("s") * SC_CORES + lax.axis_index("c")

        @pl.when(wid == 0)
        def _():
            pltpu.sync_copy(dest_hbm, dest_v)
            pltpu.sync_copy(filler_hbm, slot_v)
            lane = lax.iota(jnp.int32, lanes)

            def place(i, _):
                start = pl.multiple_of(i * lanes, lanes)
                slots = dest_v[pl.ds(start, lanes)]
                plsc.store_scatter(slot_v, [slots], (lane + start) & (N_TOK - 1))
                return 0

            lax.fori_loop(0, n_assign // lanes, place, 0, unroll=8)
            pltpu.sync_copy(slot_v, out_hbm)

    return pl.kernel(
        body,
        out_type=jax.ShapeDtypeStruct((N_SLOTS,), jnp.int32),
        mesh=mesh,
        scratch_types=[
            pltpu.VMEM((n_assign,), jnp.int32),
            pltpu.VMEM((N_SLOTS,), jnp.int32),
        ],
        compiler_params=pltpu.CompilerParams(needs_layout_passes=False),
        name="sc_slot_tokens",
    )(dest_flat, filler)


def _mix_kernel(x_ref, wts_ref, mod_ref, fg_ref, y0_ref, y1_ref, o_ref):
    out = _expert_mix(x_ref[...], wts_ref[...], mod_ref[0][5:6], y0_ref[...], y1_ref[...])
    ms = jnp.mean(out * out, axis=-1, keepdims=True)
    o_ref[...] = out * lax.rsqrt(ms + RMS_EPS) * fg_ref[...]


def _mix(x1, wts, mod, y2, fg):
    nt = N_TOK // COMB_TM
    tiles_per_batch = SEQ // COMB_TM
    return pl.pallas_call(
        _mix_kernel,
        grid=(nt,),
        in_specs=[
            pl.BlockSpec((COMB_TM, D_MODEL), lambda i: (i, 0)),
            pl.BlockSpec((COMB_TM, TOP_K), lambda i: (i, 0)),
            pl.BlockSpec((1, 6, D_MODEL), lambda i: (i // tiles_per_batch, 0, 0)),
            pl.BlockSpec((1, D_MODEL), lambda i: (0, 0)),
            pl.BlockSpec((COMB_TM, ROW_WORDS), lambda i: (i, 0)),
            pl.BlockSpec((COMB_TM, ROW_WORDS), lambda i: (nt + i, 0)),
        ],
        out_specs=pl.BlockSpec((COMB_TM, D_MODEL), lambda i: (i, 0)),
        out_shape=jax.ShapeDtypeStruct((N_TOK, D_MODEL), F32),
        compiler_params=_cparams(("arbitrary",)),
        name="mix",
    )(x1, wts, mod, fg, y2, y2)


def _routing_tables(idr, cnt):
    counts = cnt[0, EXPERT_LANE0:EXPERT_LANE0 + N_EXPERTS].astype(jnp.int32)
    padded = (counts + MOE_BLK - 1) // MOE_BLK * MOE_BLK
    pad_ends = jnp.cumsum(padded)
    pad_starts = pad_ends - padded
    experts = jnp.arange(N_EXPERTS, dtype=jnp.int32)
    is_expert = idr[None, 0:2, :] == experts[:, None, None]
    dest = jnp.sum(jnp.where(is_expert, pad_starts[:, None, None], 0), axis=0) + idr[2:4, :]
    block_start = jnp.arange(N_BLOCKS, dtype=jnp.int32) * MOE_BLK
    bexp = jnp.minimum(jnp.sum(block_start[:, None] >= pad_ends[None, :], axis=1),
                       N_EXPERTS - 1).astype(jnp.int32)
    owns = padded > 0
    pos_of_expert = jnp.cumsum(owns.astype(jnp.int32)) - 1
    active = jnp.sum(jnp.where(owns[None, :] & (pos_of_expert[None, :] == experts[:, None]),
                               experts[None, :], 0), axis=1).astype(jnp.int32)
    block_pos = jnp.sum(jnp.where(bexp[:, None] == experts[None, :], pos_of_expert[None, :], 0),
                        axis=1).astype(jnp.int32)
    meta = jnp.stack([pad_ends[-1] // MOE_BLK, jnp.sum(owns)]).astype(jnp.int32)
    return dest.astype(jnp.int32), block_pos, active, meta


def _router_weights(w_group, b_group, w_expert, b_expert):
    rw = jnp.zeros((D_MODEL, LANES), F32)
    rw = rw.at[:, :N_GROUPS].set(w_group)
    rw = rw.at[:, EXPERT_LANE0:EXPERT_LANE0 + N_EXPERTS].set(w_expert)
    rb = jnp.zeros((1, LANES), F32)
    rb = rb.at[0, :N_GROUPS].set(b_group)
    rb = rb.at[0, EXPERT_LANE0:EXPERT_LANE0 + N_EXPERTS].set(b_expert.reshape(-1))
    return rw.astype(BF16), rb


def _scaled_qkv_weight(w_in, head_dim):
    scale = jnp.concatenate([jnp.full((D_MODEL,), LOG2_E / math.sqrt(head_dim), F32),
                             jnp.ones((2 * D_MODEL,), F32)])
    return (w_in * scale).astype(BF16)


def kernel(x, c, norm1_g, norm2_g, ada_w, ada_b, diff_w_in, diff_w_out, diff_lambda_q1, diff_lambda_k1, diff_lambda_q2, diff_lambda_k2, diff_subln_g, sb_w_in, sb_w_out, router_group_w, router_group_b, router_expert_w, router_expert_b, expert_w_gate, expert_w_up, expert_w_down, final_norm_g):
    xf = x.reshape(N_TOK, D_MODEL)
    mod_all = _ada(c, ada_w, ada_b)
    slopes = jnp.exp2(-8.0 * jnp.arange(1, DIFF_HEADS + 1, dtype=F32) / DIFF_HEADS)

    pending = None
    for i in range(DEPTH):
        mod = mod_all[i].reshape(BATCH, 6, D_MODEL)
        j = i // 2
        is_diff = i % 2 == 0
        w_in = (_scaled_qkv_weight(diff_w_in[j], DIFF_HEAD_DIM) if is_diff
                else _scaled_qkv_weight(sb_w_in[j], SB_HEAD_DIM))
        g1 = norm1_g[i].reshape(1, D_MODEL)
        if pending is None:
            qkv = _qkv(xf, mod, g1, w_in)
        else:
            xf, qkv = _mix_qkv(*pending, mod, g1, w_in)
        if is_diff:
            lambda_init = 0.8 - 0.6 * math.exp(-0.3 * i)
            lam = (jnp.exp(jnp.sum(diff_lambda_q1[j] * diff_lambda_k1[j]))
                   - jnp.exp(jnp.sum(diff_lambda_q2[j] * diff_lambda_k2[j]))
                   + lambda_init).reshape(1, 1)
            o = _diff_attn(qkv, slopes, lam, diff_subln_g[j].reshape(1, LANES), lambda_init)
            w_out = diff_w_out[j]
        else:
            o = _sb_attn(qkv)
            w_out = sb_w_out[j]
        rw, rb = _router_weights(router_group_w[i], router_group_b[i],
                                 router_expert_w[i], router_expert_b[i])
        x1, h2, idr, wts, cnt = _out_router(o, xf, mod, norm2_g[i].reshape(1, D_MODEL),
                                            w_out.astype(BF16), rw, rb)
        dest, block_pos, active, meta = _routing_tables(idr, cnt)
        dest_flat = dest.reshape(-1)
        xs = _sc_gather_rows(h2, _sc_slot_tokens(dest_flat))
        ys = _experts(i, block_pos, active, meta, xs,
                      expert_w_gate, expert_w_up, expert_w_down)
        y2 = _sc_gather_rows(ys, dest_flat)
        pending = (x1, wts, mod, y2)
    out = _mix(*pending, final_norm_g.reshape(1, D_MODEL))
    return out.reshape(BATCH, SEQ, D_MODEL)
```
